```python
import math
import jax, jax.numpy as jnp
from jax import lax
import numpy as np

D_MODEL = 2048
BATCH = 8
SEQ = 8192
DEPTH = 2

CHUNK = 64
N_META = 16

DN_ALPHA = (2 * DEPTH) ** 0.25
DN_BETA = (8 * DEPTH) ** -0.25
LN_EPS = 1e-5
RMS_EPS = 1e-6

S5_WIDTH = D_MODEL // 2
S5_GROUP = 16
S5_GROUPS = S5_WIDTH // S5_GROUP
S5_STATE = 64

MLA_HEADS = D_MODEL // 256
MLA_NOPE = 128
MLA_ROPE = 64
MLA_V = 128
MLA_Q_RANK = D_MODEL // 4
MLA_KV_RANK = D_MODEL // 8
ROPE_BASE = 10000.0
Q_BLOCK = 128

L0_IN = S5_WIDTH + MLA_Q_RANK + MLA_KV_RANK + MLA_ROPE
L0_MIX = S5_WIDTH + MLA_HEADS * MLA_V

SSD_INNER = 2 * D_MODEL
SSD_HEAD_DIM = 64
SSD_HEADS = SSD_INNER // SSD_HEAD_DIM
SSD_GROUPS = 8
SSD_HPG = SSD_HEADS // SSD_GROUPS
SSD_STATE = 128
SSD_CONV = 4
SSD_BLOCK = 128
SSD_CONV_DIM = SSD_INNER + 2 * SSD_GROUPS * SSD_STATE
L1_IN = SSD_INNER + SSD_CONV_DIM + SSD_HEADS

FFN_HIDDEN = -(-(8 * D_MODEL) // (3 * 256)) * 256

kernel_name = "hybrid_s5_mla_ssd_deepnorm_encoder"


def layer_norm(x, g, b):
    xf = x.astype(jnp.float32)
    mu = jnp.mean(xf, axis=-1, keepdims=True)
    var = jnp.mean(jnp.square(xf - mu), axis=-1, keepdims=True)
    y = (xf - mu) * lax.rsqrt(var + LN_EPS) * g.astype(jnp.float32) + b.astype(jnp.float32)
    return y.astype(x.dtype)


def rms_norm(x, g):
    xf = x.astype(jnp.float32)
    y = xf * lax.rsqrt(jnp.mean(jnp.square(xf), axis=-1, keepdims=True) + RMS_EPS)
    return (y * g.astype(jnp.float32)).astype(x.dtype)


def chunk_ids(n):
    p = jnp.arange(n)
    return jnp.where(p < N_META, 0, 1 + (p - N_META) // CHUNK)


def rope_tables(n):
    pos = jnp.arange(n, dtype=jnp.float32)
    inv = ROPE_BASE ** (-jnp.arange(0, MLA_ROPE, 2, dtype=jnp.float32) / MLA_ROPE)
    ang = pos[:, None] * inv[None, :]
    return jnp.cos(ang), jnp.sin(ang)


def apply_rope(x, cos, sin):
    half = x.shape[-1] // 2
    x1, x2 = x[..., :half], x[..., half:]
    cos = cos.astype(x.dtype)
    sin = sin.astype(x.dtype)
    return jnp.concatenate([x1 * cos - x2 * sin, x2 * cos + x1 * sin], axis=-1)


def _complex_affine_combine(e1, e2):
    a1r, a1i, b1r, b1i = e1
    a2r, a2i, b2r, b2i = e2
    ar = a2r * a1r - a2i * a1i
    ai = a2r * a1i + a2i * a1r
    br = a2r * b1r - a2i * b1i + b2r
    bi = a2r * b1i + a2i * b1r + b2i
    return ar, ai, br, bi


def s5_mixer(u, log_dt, a_re, a_im, b_re, b_im, c_re, c_im, d, w_glu):
    bsz, n, _ = u.shape
    ug = u.reshape(bsz, n, S5_GROUPS, S5_GROUP)
    dt = jnp.exp(log_dt)[:, None]
    mag = jnp.exp(dt * a_re)
    ab_re = mag * jnp.cos(dt * a_im)
    ab_im = mag * jnp.sin(dt * a_im)
    den = a_re * a_re + a_im * a_im
    nr = ab_re - 1.0
    f_re = (nr * a_re + ab_im * a_im) / den
    f_im = (ab_im * a_re - nr * a_im) / den
    bb_re = f_re[..., None] * b_re - f_im[..., None] * b_im
    bb_im = f_re[..., None] * b_im + f_im[..., None] * b_re
    bu_re = jnp.einsum('blgj,gpj->lbgp', ug, bb_re)
    bu_im = jnp.einsum('blgj,gpj->lbgp', ug, bb_im)
    a_seq_re = jnp.broadcast_to(ab_re, (n, 1) + ab_re.shape)
    a_seq_im = jnp.broadcast_to(ab_im, (n, 1) + ab_im.shape)
    _, _, s_re, s_im = lax.associative_scan(
        _complex_affine_combine, (a_seq_re, a_seq_im, bu_re, bu_im), axis=0)
    y = (jnp.einsum('lbgp,gjp->blgj', s_re, c_re)
         - jnp.einsum('lbgp,gjp->blgj', s_im, c_im))
    y = y.reshape(bsz, n, S5_WIDTH) + d * u
    g = jax.nn.gelu(y)
    return g * jax.nn.sigmoid(g @ w_glu)


def mla_mixer(q_lat, kv_lat, k_rope_raw, q_norm, w_uq, kv_norm, w_ukv, cos, sin, cid):
    bsz, n, _ = q_lat.shape
    q = (rms_norm(q_lat, q_norm) @ w_uq).reshape(bsz, n, MLA_HEADS, MLA_NOPE + MLA_ROPE)
    q_nope = q[..., :MLA_NOPE]
    q_rope = apply_rope(q[..., MLA_NOPE:], cos[:, None, :], sin[:, None, :])
    kv = (rms_norm(kv_lat, kv_norm) @ w_ukv).reshape(bsz, n, MLA_HEADS, MLA_NOPE + MLA_V)
    k_nope = kv[..., :MLA_NOPE]
    v = kv[..., MLA_NOPE:]
    k_rope = apply_rope(k_rope_raw, cos, sin)
    n_pad = -(-n // Q_BLOCK) * Q_BLOCK
    nb = n_pad // Q_BLOCK
    pad = ((0, 0), (0, n_pad - n), (0, 0), (0, 0))
    qn_b = jnp.pad(q_nope, pad).reshape(bsz, nb, Q_BLOCK, MLA_HEADS, MLA_NOPE).transpose(1, 0, 2, 3, 4)
    qr_b = jnp.pad(q_rope, pad).reshape(bsz, nb, Q_BLOCK, MLA_HEADS, MLA_ROPE).transpose(1, 0, 2, 3, 4)
    cid_q = chunk_ids(n_pad).reshape(nb, Q_BLOCK)
    scale = (MLA_NOPE + MLA_ROPE) ** -0.5

    def attend(blk):
        qn, qr, cq = blk
        s = (jnp.einsum('bqhd,bkhd->bhqk', qn, k_nope)
             + jnp.einsum('bqhd,bkd->bhqk', qr, k_rope))
        s = s.astype(jnp.float32) * scale
        visible = cid[None, :] <= cq[:, None]
        s = jnp.where(visible, s, -jnp.inf)
        p = jax.nn.softmax(s, axis=-1).astype(v.dtype)
        return jnp.einsum('bhqk,bkhd->bqhd', p, v)

    o = lax.map(attend, (qn_b, qr_b, cid_q))
    o = o.transpose(1, 0, 2, 3, 4).reshape(bsz, n_pad, MLA_HEADS * MLA_V)
    return o[:, :n]


def s5_mla_mixer(h, cos, sin, cid, w_in, s5_log_dt, s5_a_re, s5_a_im, s5_b_re, s5_b_im,
                 s5_c_re, s5_c_im, s5_d, s5_w_glu, mla_q_norm, mla_w_uq, mla_kv_norm,
                 mla_w_ukv, w_out):
    proj = h @ w_in
    o1 = S5_WIDTH
    o2 = o1 + MLA_Q_RANK
    o3 = o2 + MLA_KV_RANK
    a_out = s5_mixer(proj[..., :o1], s5_log_dt, s5_a_re, s5_a_im, s5_b_re, s5_b_im,
                     s5_c_re, s5_c_im, s5_d, s5_w_glu)
    b_out = mla_mixer(proj[..., o1:o2], proj[..., o2:o3], proj[..., o3:], mla_q_norm,
                      mla_w_uq, mla_kv_norm, mla_w_ukv, cos, sin, cid)
    return jnp.concatenate([a_out, b_out], axis=-1) @ w_out


def causal_depthwise_conv(x, w, b):
    k = w.shape[0]
    y = lax.conv_general_dilated(x, w[:, None, :], window_strides=(1,), padding=((k - 1, 0),),
                                 dimension_numbers=('NWC', 'WIO', 'NWC'),
                                 feature_group_count=x.shape[-1])
    return y + b


def ssd_scan(x, dt, a, bm, cm):
    bsz, n, g, r, p = x.shape
    q = SSD_BLOCK
    nc = n // q
    x = x.reshape(bsz, nc, q, g, r, p)
    dt = dt.reshape(bsz, nc, q, g, r)
    bm = bm.reshape(bsz, nc, q, g, -1)
    cm = cm.reshape(bsz, nc, q, g, -1)
    xdt = x * dt[..., None]
    da = jnp.moveaxis(dt * a, 2, -1)
    cs = jnp.cumsum(da, axis=-1)
    tri = jnp.tril(jnp.ones((q, q), dtype=bool))
    seg = cs[..., :, None] - cs[..., None, :]
    decay = jnp.exp(jnp.where(tri, seg, -jnp.inf))
    cb = jnp.einsum('bclgn,bcsgn->bcgls', cm, bm)
    y_diag = jnp.einsum('bcgls,bcgrls,bcsgrp->bclgrp', cb, decay, xdt)
    decay_states = jnp.exp(cs[..., -1:] - cs)
    states = jnp.einsum('bclgn,bcgrl,bclgrp->bcgrpn', bm, decay_states, xdt)
    chunk_decay = jnp.exp(cs[..., -1])

    def carry_step(hs, inp):
        s_c, a_c = inp
        return hs * a_c[..., None, None] + s_c, hs

    h0 = jnp.zeros_like(states[:, 0])
    _, prev = lax.scan(carry_step, h0, (jnp.moveaxis(states, 1, 0), jnp.moveaxis(chunk_decay, 1, 0)))
    y_off = jnp.einsum('bclgn,cbgrpn,bcgrl->bclgrp', cm, prev, jnp.exp(cs))
    return (y_diag + y_off).reshape(bsz, n, g, r, p)


def mamba2_mixer(h, w_in, conv_w, conv_b, dt_bias, a_log, d, norm_g, w_out):
    bsz, n, _ = h.shape
    zxbcdt = h @ w_in
    z = zxbcdt[..., :SSD_INNER]
    xbc = zxbcdt[..., SSD_INNER:SSD_INNER + SSD_CONV_DIM]
    dt_raw = zxbcdt[..., SSD_INNER + SSD_CONV_DIM:]
    xbc = jax.nn.silu(causal_depthwise_conv(xbc, conv_w, conv_b))
    gn = SSD_GROUPS * SSD_STATE
    xs = xbc[..., :SSD_INNER].reshape(bsz, n, SSD_GROUPS, SSD_HPG, SSD_HEAD_DIM)
    bm = xbc[..., SSD_INNER:SSD_INNER + gn].reshape(bsz, n, SSD_GROUPS, SSD_STATE)
    cm = xbc[..., SSD_INNER + gn:].reshape(bsz, n, SSD_GROUPS, SSD_STATE)
    dt = jax.nn.softplus(dt_raw + dt_bias).reshape(bsz, n, SSD_GROUPS, SSD_HPG)
    a = -jnp.exp(a_log).reshape(SSD_GROUPS, SSD_HPG)
    n_pad = -(-n // SSD_BLOCK) * SSD_BLOCK
    tp = n_pad - n
    y = ssd_scan(jnp.pad(xs, ((0, 0), (0, tp), (0, 0), (0, 0), (0, 0))),
                 jnp.pad(dt, ((0, 0), (0, tp), (0, 0), (0, 0))), a,
                 jnp.pad(bm, ((0, 0), (0, tp), (0, 0), (0, 0))),
                 jnp.pad(cm, ((0, 0), (0, tp), (0, 0), (0, 0))))[:, :n]
    y = y + d.reshape(SSD_GROUPS, SSD_HPG)[..., None] * xs
    y = y.reshape(bsz, n, SSD_INNER) * jax.nn.silu(z)
    y = rms_norm(y.reshape(bsz, n, SSD_GROUPS, SSD_INNER // SSD_GROUPS),
                 norm_g.reshape(SSD_GROUPS, SSD_INNER // SSD_GROUPS)).reshape(bsz, n, SSD_INNER)
    return y @ w_out


def swiglu_ffn(h, w_gate, w_up, w_down):
    return (jax.nn.silu(h @ w_gate) * (h @ w_up)) @ w_down


def _fwd_setup_inputs(seed: int = 0) -> dict:
    key = jax.random.key(seed)
    ks = iter(jax.random.split(key, 64))

    def nrm(shape, scale):
        return jax.random.normal(next(ks), shape, jnp.float32) * scale

    def uni(shape, lo, hi):
        return jax.random.uniform(next(ks), shape, jnp.float32, lo, hi)

    d = D_MODEL
    f = FFN_HIDDEN
    G, P, J = S5_GROUPS, S5_STATE, S5_GROUP
    inp = {}
    inp['x'] = nrm((BATCH, SEQ, d), 1.0)
    inp['meta_tokens'] = nrm((N_META, d), 1.0)
    inp['l0_w_in'] = nrm((d, L0_IN), d ** -0.5)
    inp['l0_s5_log_dt'] = uni((G,), math.log(0.001), math.log(0.1))
    inp['l0_s5_a_re'] = -0.5 + nrm((G, P), 0.01)
    inp['l0_s5_a_im'] = jnp.pi * jnp.arange(P, dtype=jnp.float32)[None, :] + nrm((G, P), 0.01)
    inp['l0_s5_b_re'] = nrm((G, P, J), (2 * J) ** -0.5)
    inp['l0_s5_b_im'] = nrm((G, P, J), (2 * J) ** -0.5)
    inp['l0_s5_c_re'] = nrm((G, J, P), 0.5 ** 0.5)
    inp['l0_s5_c_im'] = nrm((G, J, P), 0.5 ** 0.5)
    inp['l0_s5_d'] = nrm((S5_WIDTH,), 1.0)
    inp['l0_s5_w_glu'] = nrm((S5_WIDTH, S5_WIDTH), S5_WIDTH ** -0.5)
    inp['l0_mla_q_norm'] = 1.0 + nrm((MLA_Q_RANK,), 0.02)
    inp['l0_mla_w_uq'] = nrm((MLA_Q_RANK, MLA_HEADS * (MLA_NOPE + MLA_ROPE)), MLA_Q_RANK ** -0.5)
    inp['l0_mla_kv_norm'] = 1.0 + nrm((MLA_KV_RANK,), 0.02)
    inp['l0_mla_w_ukv'] = nrm((MLA_KV_RANK, MLA_HEADS * (MLA_NOPE + MLA_V)), MLA_KV_RANK ** -0.5)
    inp['l0_w_out'] = nrm((L0_MIX, d), L0_MIX ** -0.5 * DN_BETA)
    inp['l0_ln1_g'] = 1.0 + nrm((d,), 0.02)
    inp['l0_ln1_b'] = nrm((d,), 0.02)
    inp['l0_ffn_w_gate'] = nrm((d, f), d ** -0.5)
    inp['l0_ffn_w_up'] = nrm((d, f), d ** -0.5 * DN_BETA)
    inp['l0_ffn_w_down'] = nrm((f, d), f ** -0.5 * DN_BETA)
    inp['l0_ln2_g'] = 1.0 + nrm((d,), 0.02)
    inp['l0_ln2_b'] = nrm((d,), 0.02)
    inp['l1_w_in'] = nrm((d, L1_IN), d ** -0.5)
    inp['l1_conv_w'] = nrm((SSD_CONV, SSD_CONV_DIM), SSD_CONV ** -0.5)
    inp['l1_conv_b'] = nrm((SSD_CONV_DIM,), 0.02)
    dt0 = jnp.exp(uni((SSD_HEADS,), math.log(0.001), math.log(0.1)))
    inp['l1_dt_bias'] = dt0 + jnp.log(-jnp.expm1(-dt0))
    inp['l1_a_log'] = jnp.log(uni((SSD_HEADS,), 1.0, 16.0))
    inp['l1_d'] = 1.0 + nrm((SSD_HEADS,), 0.02)
    inp['l1_norm_g'] = 1.0 + nrm((SSD_INNER,), 0.02)
    inp['l1_w_out'] = nrm((SSD_INNER, d), SSD_INNER ** -0.5 * DN_BETA)
    inp['l1_ln1_g'] = 1.0 + nrm((d,), 0.02)
    inp['l1_ln1_b'] = nrm((d,), 0.02)
    inp['l1_ffn_w_gate'] = nrm((d, f), d ** -0.5)
    inp['l1_ffn_w_up'] = nrm((d, f), d ** -0.5 * DN_BETA)
    inp['l1_ffn_w_down'] = nrm((f, d), f ** -0.5 * DN_BETA)
    inp['l1_ln2_g'] = 1.0 + nrm((d,), 0.02)
    inp['l1_ln2_b'] = nrm((d,), 0.02)
    return inp


def _fwd_reference(x, meta_tokens,
              l0_w_in, l0_s5_log_dt, l0_s5_a_re, l0_s5_a_im, l0_s5_b_re, l0_s5_b_im,
              l0_s5_c_re, l0_s5_c_im, l0_s5_d, l0_s5_w_glu, l0_mla_q_norm, l0_mla_w_uq,
              l0_mla_kv_norm, l0_mla_w_ukv, l0_w_out, l0_ln1_g, l0_ln1_b,
              l0_ffn_w_gate, l0_ffn_w_up, l0_ffn_w_down, l0_ln2_g, l0_ln2_b,
              l1_w_in, l1_conv_w, l1_conv_b, l1_dt_bias, l1_a_log, l1_d, l1_norm_g,
              l1_w_out, l1_ln1_g, l1_ln1_b, l1_ffn_w_gate, l1_ffn_w_up, l1_ffn_w_down,
              l1_ln2_g, l1_ln2_b):
    bsz = x.shape[0]
    meta = jnp.broadcast_to(meta_tokens[None].astype(x.dtype), (bsz, N_META, D_MODEL))
    h = jnp.concatenate([meta, x], axis=1)
    n = h.shape[1]
    cos, sin = rope_tables(n)
    cid = chunk_ids(n)

    mixers = [
        lambda t: s5_mla_mixer(t, cos, sin, cid, l0_w_in, l0_s5_log_dt, l0_s5_a_re, l0_s5_a_im,
                               l0_s5_b_re, l0_s5_b_im, l0_s5_c_re, l0_s5_c_im, l0_s5_d,
                               l0_s5_w_glu, l0_mla_q_norm, l0_mla_w_uq, l0_mla_kv_norm,
                               l0_mla_w_ukv, l0_w_out),
        lambda t: mamba2_mixer(t, l1_w_in, l1_conv_w, l1_conv_b, l1_dt_bias, l1_a_log, l1_d,
                               l1_norm_g, l1_w_out),
    ]
    post = [
        (l0_ln1_g, l0_ln1_b, l0_ffn_w_gate, l0_ffn_w_up, l0_ffn_w_down, l0_ln2_g, l0_ln2_b),
        (l1_ln1_g, l1_ln1_b, l1_ffn_w_gate, l1_ffn_w_up, l1_ffn_w_down, l1_ln2_g, l1_ln2_b),
    ]
    for i in range(DEPTH):
        ln1_g, ln1_b, w_gate, w_up, w_down, ln2_g, ln2_b = post[i]
        h = layer_norm(DN_ALPHA * h + mixers[i](h), ln1_g, ln1_b)
        h = layer_norm(DN_ALPHA * h + swiglu_ffn(h, w_gate, w_up, w_down), ln2_g, ln2_b)
    return h[:, N_META:]


import jax as _jax
import jax.numpy as _jnp

TWIN_FORMAT = 'train_step'
FWD_PARAMS = ['x', 'meta_tokens', 'l0_w_in', 'l0_s5_log_dt', 'l0_s5_a_re', 'l0_s5_a_im', 'l0_s5_b_re', 'l0_s5_b_im', 'l0_s5_c_re', 'l0_s5_c_im', 'l0_s5_d', 'l0_s5_w_glu', 'l0_mla_q_norm', 'l0_mla_w_uq', 'l0_mla_kv_norm', 'l0_mla_w_ukv', 'l0_w_out', 'l0_ln1_g', 'l0_ln1_b', 'l0_ffn_w_gate', 'l0_ffn_w_up', 'l0_ffn_w_down', 'l0_ln2_g', 'l0_ln2_b', 'l1_w_in', 'l1_conv_w', 'l1_conv_b', 'l1_dt_bias', 'l1_a_log', 'l1_d', 'l1_norm_g', 'l1_w_out', 'l1_ln1_g', 'l1_ln1_b', 'l1_ffn_w_gate', 'l1_ffn_w_up', 'l1_ffn_w_down', 'l1_ln2_g', 'l1_ln2_b']
TWIN_WEIGHTS = ['meta_tokens', 'l0_w_in', 'l0_s5_log_dt', 'l0_s5_a_re', 'l0_s5_a_im', 'l0_s5_b_re', 'l0_s5_b_im', 'l0_s5_c_re', 'l0_s5_c_im', 'l0_s5_d', 'l0_s5_w_glu', 'l0_mla_q_norm', 'l0_mla_w_uq', 'l0_mla_kv_norm', 'l0_mla_w_ukv', 'l0_w_out', 'l0_ln1_g', 'l0_ln1_b', 'l0_ffn_w_gate', 'l0_ffn_w_up', 'l0_ffn_w_down', 'l0_ln2_g', 'l0_ln2_b', 'l1_w_in', 'l1_conv_w', 'l1_conv_b', 'l1_dt_bias', 'l1_a_log', 'l1_d', 'l1_norm_g', 'l1_w_out', 'l1_ln1_g', 'l1_ln1_b', 'l1_ffn_w_gate', 'l1_ffn_w_up', 'l1_ffn_w_down', 'l1_ln2_g', 'l1_ln2_b']
TWIN_DIFF_INPUT = 'x'
TWIN_INPUTS = ['x', 'meta_tokens', 'l0_w_in', 'l0_s5_log_dt', 'l0_s5_a_re', 'l0_s5_a_im', 'l0_s5_b_re', 'l0_s5_b_im', 'l0_s5_c_re', 'l0_s5_c_im', 'l0_s5_d', 'l0_s5_w_glu', 'l0_mla_q_norm', 'l0_mla_w_uq', 'l0_mla_kv_norm', 'l0_mla_w_ukv', 'l0_w_out', 'l0_ln1_g', 'l0_ln1_b', 'l0_ffn_w_gate', 'l0_ffn_w_up', 'l0_ffn_w_down', 'l0_ln2_g', 'l0_ln2_b', 'l1_w_in', 'l1_conv_w', 'l1_conv_b', 'l1_dt_bias', 'l1_a_log', 'l1_d', 'l1_norm_g', 'l1_w_out', 'l1_ln1_g', 'l1_ln1_b', 'l1_ffn_w_gate', 'l1_ffn_w_up', 'l1_ffn_w_down', 'l1_ln2_g', 'l1_ln2_b', 'loss_target', 'm_meta_tokens', 'm_l0_w_in', 'm_l0_s5_log_dt', 'm_l0_s5_a_re', 'm_l0_s5_a_im', 'm_l0_s5_b_re', 'm_l0_s5_b_im', 'm_l0_s5_c_re', 'm_l0_s5_c_im', 'm_l0_s5_d', 'm_l0_s5_w_glu', 'm_l0_mla_q_norm', 'm_l0_mla_w_uq', 'm_l0_mla_kv_norm', 'm_l0_mla_w_ukv', 'm_l0_w_out', 'm_l0_ln1_g', 'm_l0_ln1_b', 'm_l0_ffn_w_gate', 'm_l0_ffn_w_up', 'm_l0_ffn_w_down', 'm_l0_ln2_g', 'm_l0_ln2_b', 'm_l1_w_in', 'm_l1_conv_w', 'm_l1_conv_b', 'm_l1_dt_bias', 'm_l1_a_log', 'm_l1_d', 'm_l1_norm_g', 'm_l1_w_out', 'm_l1_ln1_g', 'm_l1_ln1_b', 'm_l1_ffn_w_gate', 'm_l1_ffn_w_up', 'm_l1_ffn_w_down', 'm_l1_ln2_g', 'm_l1_ln2_b', 'v_meta_tokens', 'v_l0_w_in', 'v_l0_s5_log_dt', 'v_l0_s5_a_re', 'v_l0_s5_a_im', 'v_l0_s5_b_re', 'v_l0_s5_b_im', 'v_l0_s5_c_re', 'v_l0_s5_c_im', 'v_l0_s5_d', 'v_l0_s5_w_glu', 'v_l0_mla_q_norm', 'v_l0_mla_w_uq', 'v_l0_mla_kv_norm', 'v_l0_mla_w_ukv', 'v_l0_w_out', 'v_l0_ln1_g', 'v_l0_ln1_b', 'v_l0_ffn_w_gate', 'v_l0_ffn_w_up', 'v_l0_ffn_w_down', 'v_l0_ln2_g', 'v_l0_ln2_b', 'v_l1_w_in', 'v_l1_conv_w', 'v_l1_conv_b', 'v_l1_dt_bias', 'v_l1_a_log', 'v_l1_d', 'v_l1_norm_g', 'v_l1_w_out', 'v_l1_ln1_g', 'v_l1_ln1_b', 'v_l1_ffn_w_gate', 'v_l1_ffn_w_up', 'v_l1_ffn_w_down', 'v_l1_ln2_g', 'v_l1_ln2_b']
TWIN_OUTPUTS = ['loss', 'grad_x', 'grad_meta_tokens', 'grad_l0_w_in', 'grad_l0_s5_log_dt', 'grad_l0_s5_a_re', 'grad_l0_s5_a_im', 'grad_l0_s5_b_re', 'grad_l0_s5_b_im', 'grad_l0_s5_c_re', 'grad_l0_s5_c_im', 'grad_l0_s5_d', 'grad_l0_s5_w_glu', 'grad_l0_mla_q_norm', 'grad_l0_mla_w_uq', 'grad_l0_mla_kv_norm', 'grad_l0_mla_w_ukv', 'grad_l0_w_out', 'grad_l0_ln1_g', 'grad_l0_ln1_b', 'grad_l0_ffn_w_gate', 'grad_l0_ffn_w_up', 'grad_l0_ffn_w_down', 'grad_l0_ln2_g', 'grad_l0_ln2_b', 'grad_l1_w_in', 'grad_l1_conv_w', 'grad_l1_conv_b', 'grad_l1_dt_bias', 'grad_l1_a_log', 'grad_l1_d', 'grad_l1_norm_g', 'grad_l1_w_out', 'grad_l1_ln1_g', 'grad_l1_ln1_b', 'grad_l1_ffn_w_gate', 'grad_l1_ffn_w_up', 'grad_l1_ffn_w_down', 'grad_l1_ln2_g', 'grad_l1_ln2_b', 'delta_meta_tokens', 'delta_l0_w_in', 'delta_l0_s5_log_dt', 'delta_l0_s5_a_re', 'delta_l0_s5_a_im', 'delta_l0_s5_b_re', 'delta_l0_s5_b_im', 'delta_l0_s5_c_re', 'delta_l0_s5_c_im', 'delta_l0_s5_d', 'delta_l0_s5_w_glu', 'delta_l0_mla_q_norm', 'delta_l0_mla_w_uq', 'delta_l0_mla_kv_norm', 'delta_l0_mla_w_ukv', 'delta_l0_w_out', 'delta_l0_ln1_g', 'delta_l0_ln1_b', 'delta_l0_ffn_w_gate', 'delta_l0_ffn_w_up', 'delta_l0_ffn_w_down', 'delta_l0_ln2_g', 'delta_l0_ln2_b', 'delta_l1_w_in', 'delta_l1_conv_w', 'delta_l1_conv_b', 'delta_l1_dt_bias', 'delta_l1_a_log', 'delta_l1_d', 'delta_l1_norm_g', 'delta_l1_w_out', 'delta_l1_ln1_g', 'delta_l1_ln1_b', 'delta_l1_ffn_w_gate', 'delta_l1_ffn_w_up', 'delta_l1_ffn_w_down', 'delta_l1_ln2_g', 'delta_l1_ln2_b', 'new_m_meta_tokens', 'new_m_l0_w_in', 'new_m_l0_s5_log_dt', 'new_m_l0_s5_a_re', 'new_m_l0_s5_a_im', 'new_m_l0_s5_b_re', 'new_m_l0_s5_b_im', 'new_m_l0_s5_c_re', 'new_m_l0_s5_c_im', 'new_m_l0_s5_d', 'new_m_l0_s5_w_glu', 'new_m_l0_mla_q_norm', 'new_m_l0_mla_w_uq', 'new_m_l0_mla_kv_norm', 'new_m_l0_mla_w_ukv', 'new_m_l0_w_out', 'new_m_l0_ln1_g', 'new_m_l0_ln1_b', 'new_m_l0_ffn_w_gate', 'new_m_l0_ffn_w_up', 'new_m_l0_ffn_w_down', 'new_m_l0_ln2_g', 'new_m_l0_ln2_b', 'new_m_l1_w_in', 'new_m_l1_conv_w', 'new_m_l1_conv_b', 'new_m_l1_dt_bias', 'new_m_l1_a_log', 'new_m_l1_d', 'new_m_l1_norm_g', 'new_m_l1_w_out', 'new_m_l1_ln1_g', 'new_m_l1_ln1_b', 'new_m_l1_ffn_w_gate', 'new_m_l1_ffn_w_up', 'new_m_l1_ffn_w_down', 'new_m_l1_ln2_g', 'new_m_l1_ln2_b', 'new_v_meta_tokens', 'new_v_l0_w_in', 'new_v_l0_s5_log_dt', 'new_v_l0_s5_a_re', 'new_v_l0_s5_a_im', 'new_v_l0_s5_b_re', 'new_v_l0_s5_b_im', 'new_v_l0_s5_c_re', 'new_v_l0_s5_c_im', 'new_v_l0_s5_d', 'new_v_l0_s5_w_glu', 'new_v_l0_mla_q_norm', 'new_v_l0_mla_w_uq', 'new_v_l0_mla_kv_norm', 'new_v_l0_mla_w_ukv', 'new_v_l0_w_out', 'new_v_l0_ln1_g', 'new_v_l0_ln1_b', 'new_v_l0_ffn_w_gate', 'new_v_l0_ffn_w_up', 'new_v_l0_ffn_w_down', 'new_v_l0_ln2_g', 'new_v_l0_ln2_b', 'new_v_l1_w_in', 'new_v_l1_conv_w', 'new_v_l1_conv_b', 'new_v_l1_dt_bias', 'new_v_l1_a_log', 'new_v_l1_d', 'new_v_l1_norm_g', 'new_v_l1_w_out', 'new_v_l1_ln1_g', 'new_v_l1_ln1_b', 'new_v_l1_ffn_w_gate', 'new_v_l1_ffn_w_up', 'new_v_l1_ffn_w_down', 'new_v_l1_ln2_g', 'new_v_l1_ln2_b']
TWIN_LEAF_KINDS = {'loss': 'loss', 'grad_x': 'grad_x', 'grad_meta_tokens': 'grad_w', 'grad_l0_w_in': 'grad_w', 'grad_l0_s5_log_dt': 'grad_w', 'grad_l0_s5_a_re': 'grad_w', 'grad_l0_s5_a_im': 'grad_w', 'grad_l0_s5_b_re': 'grad_w', 'grad_l0_s5_b_im': 'grad_w', 'grad_l0_s5_c_re': 'grad_w', 'grad_l0_s5_c_im': 'grad_w', 'grad_l0_s5_d': 'grad_w', 'grad_l0_s5_w_glu': 'grad_w', 'grad_l0_mla_q_norm': 'grad_w', 'grad_l0_mla_w_uq': 'grad_w', 'grad_l0_mla_kv_norm': 'grad_w', 'grad_l0_mla_w_ukv': 'grad_w', 'grad_l0_w_out': 'grad_w', 'grad_l0_ln1_g': 'grad_w', 'grad_l0_ln1_b': 'grad_w', 'grad_l0_ffn_w_gate': 'grad_w', 'grad_l0_ffn_w_up': 'grad_w', 'grad_l0_ffn_w_down': 'grad_w', 'grad_l0_ln2_g': 'grad_w', 'grad_l0_ln2_b': 'grad_w', 'grad_l1_w_in': 'grad_w', 'grad_l1_conv_w': 'grad_w', 'grad_l1_conv_b': 'grad_w', 'grad_l1_dt_bias': 'grad_w', 'grad_l1_a_log': 'grad_w', 'grad_l1_d': 'grad_w', 'grad_l1_norm_g': 'grad_w', 'grad_l1_w_out': 'grad_w', 'grad_l1_ln1_g': 'grad_w', 'grad_l1_ln1_b': 'grad_w', 'grad_l1_ffn_w_gate': 'grad_w', 'grad_l1_ffn_w_up': 'grad_w', 'grad_l1_ffn_w_down': 'grad_w', 'grad_l1_ln2_g': 'grad_w', 'grad_l1_ln2_b': 'grad_w', 'delta_meta_tokens': 'delta_w', 'delta_l0_w_in': 'delta_w', 'delta_l0_s5_log_dt': 'delta_w', 'delta_l0_s5_a_re': 'delta_w', 'delta_l0_s5_a_im': 'delta_w', 'delta_l0_s5_b_re': 'delta_w', 'delta_l0_s5_b_im': 'delta_w', 'delta_l0_s5_c_re': 'delta_w', 'delta_l0_s5_c_im': 'delta_w', 'delta_l0_s5_d': 'delta_w', 'delta_l0_s5_w_glu': 'delta_w', 'delta_l0_mla_q_norm': 'delta_w', 'delta_l0_mla_w_uq': 'delta_w', 'delta_l0_mla_kv_norm': 'delta_w', 'delta_l0_mla_w_ukv': 'delta_w', 'delta_l0_w_out': 'delta_w', 'delta_l0_ln1_g': 'delta_w', 'delta_l0_ln1_b': 'delta_w', 'delta_l0_ffn_w_gate': 'delta_w', 'delta_l0_ffn_w_up': 'delta_w', 'delta_l0_ffn_w_down': 'delta_w', 'delta_l0_ln2_g': 'delta_w', 'delta_l0_ln2_b': 'delta_w', 'delta_l1_w_in': 'delta_w', 'delta_l1_conv_w': 'delta_w', 'delta_l1_conv_b': 'delta_w', 'delta_l1_dt_bias': 'delta_w', 'delta_l1_a_log': 'delta_w', 'delta_l1_d': 'delta_w', 'delta_l1_norm_g': 'delta_w', 'delta_l1_w_out': 'delta_w', 'delta_l1_ln1_g': 'delta_w', 'delta_l1_ln1_b': 'delta_w', 'delta_l1_ffn_w_gate': 'delta_w', 'delta_l1_ffn_w_up': 'delta_w', 'delta_l1_ffn_w_down': 'delta_w', 'delta_l1_ln2_g': 'delta_w', 'delta_l1_ln2_b': 'delta_w', 'new_m_meta_tokens': 'new_m', 'new_m_l0_w_in': 'new_m', 'new_m_l0_s5_log_dt': 'new_m', 'new_m_l0_s5_a_re': 'new_m', 'new_m_l0_s5_a_im': 'new_m', 'new_m_l0_s5_b_re': 'new_m', 'new_m_l0_s5_b_im': 'new_m', 'new_m_l0_s5_c_re': 'new_m', 'new_m_l0_s5_c_im': 'new_m', 'new_m_l0_s5_d': 'new_m', 'new_m_l0_s5_w_glu': 'new_m', 'new_m_l0_mla_q_norm': 'new_m', 'new_m_l0_mla_w_uq': 'new_m', 'new_m_l0_mla_kv_norm': 'new_m', 'new_m_l0_mla_w_ukv': 'new_m', 'new_m_l0_w_out': 'new_m', 'new_m_l0_ln1_g': 'new_m', 'new_m_l0_ln1_b': 'new_m', 'new_m_l0_ffn_w_gate': 'new_m', 'new_m_l0_ffn_w_up': 'new_m', 'new_m_l0_ffn_w_down': 'new_m', 'new_m_l0_ln2_g': 'new_m', 'new_m_l0_ln2_b': 'new_m', 'new_m_l1_w_in': 'new_m', 'new_m_l1_conv_w': 'new_m', 'new_m_l1_conv_b': 'new_m', 'new_m_l1_dt_bias': 'new_m', 'new_m_l1_a_log': 'new_m', 'new_m_l1_d': 'new_m', 'new_m_l1_norm_g': 'new_m', 'new_m_l1_w_out': 'new_m', 'new_m_l1_ln1_g': 'new_m', 'new_m_l1_ln1_b': 'new_m', 'new_m_l1_ffn_w_gate': 'new_m', 'new_m_l1_ffn_w_up': 'new_m', 'new_m_l1_ffn_w_down': 'new_m', 'new_m_l1_ln2_g': 'new_m', 'new_m_l1_ln2_b': 'new_m', 'new_v_meta_tokens': 'new_v', 'new_v_l0_w_in': 'new_v', 'new_v_l0_s5_log_dt': 'new_v', 'new_v_l0_s5_a_re': 'new_v', 'new_v_l0_s5_a_im': 'new_v', 'new_v_l0_s5_b_re': 'new_v', 'new_v_l0_s5_b_im': 'new_v', 'new_v_l0_s5_c_re': 'new_v', 'new_v_l0_s5_c_im': 'new_v', 'new_v_l0_s5_d': 'new_v', 'new_v_l0_s5_w_glu': 'new_v', 'new_v_l0_mla_q_norm': 'new_v', 'new_v_l0_mla_w_uq': 'new_v', 'new_v_l0_mla_kv_norm': 'new_v', 'new_v_l0_mla_w_ukv': 'new_v', 'new_v_l0_w_out': 'new_v', 'new_v_l0_ln1_g': 'new_v', 'new_v_l0_ln1_b': 'new_v', 'new_v_l0_ffn_w_gate': 'new_v', 'new_v_l0_ffn_w_up': 'new_v', 'new_v_l0_ffn_w_down': 'new_v', 'new_v_l0_ln2_g': 'new_v', 'new_v_l0_ln2_b': 'new_v', 'new_v_l1_w_in': 'new_v', 'new_v_l1_conv_w': 'new_v', 'new_v_l1_conv_b': 'new_v', 'new_v_l1_dt_bias': 'new_v', 'new_v_l1_a_log': 'new_v', 'new_v_l1_d': 'new_v', 'new_v_l1_norm_g': 'new_v', 'new_v_l1_w_out': 'new_v', 'new_v_l1_ln1_g': 'new_v', 'new_v_l1_ln1_b': 'new_v', 'new_v_l1_ffn_w_gate': 'new_v', 'new_v_l1_ffn_w_up': 'new_v', 'new_v_l1_ffn_w_down': 'new_v', 'new_v_l1_ln2_g': 'new_v', 'new_v_l1_ln2_b': 'new_v'}


def _forward(args):
    return _fwd_reference(*[args[k] for k in FWD_PARAMS])


def _output_shape():
    def fwd():
        inp = _fwd_setup_inputs(0)
        return _fwd_reference(*[inp[k] for k in FWD_PARAMS])
    out = _jax.eval_shape(fwd)
    return out.shape, out.dtype

N_MICROBATCH = 1
ADAM_LR = 0.001
ADAM_B1 = 0.9
ADAM_B2 = 0.999
ADAM_EPS = 1e-08
ADAM_WD = 0.01
ADAM_STEP = 10
PER_EXAMPLE_BATCH_AXIS = {'x': 0, 'loss_target': 0}
SHARED_INPUTS = []
_WEIGHT_DTYPES = {'meta_tokens': _jnp.float32, 'l0_w_in': _jnp.float32, 'l0_s5_log_dt': _jnp.float32, 'l0_s5_a_re': _jnp.float32, 'l0_s5_a_im': _jnp.float32, 'l0_s5_b_re': _jnp.float32, 'l0_s5_b_im': _jnp.float32, 'l0_s5_c_re': _jnp.float32, 'l0_s5_c_im': _jnp.float32, 'l0_s5_d': _jnp.float32, 'l0_s5_w_glu': _jnp.float32, 'l0_mla_q_norm': _jnp.float32, 'l0_mla_w_uq': _jnp.float32, 'l0_mla_kv_norm': _jnp.float32, 'l0_mla_w_ukv': _jnp.float32, 'l0_w_out': _jnp.float32, 'l0_ln1_g': _jnp.float32, 'l0_ln1_b': _jnp.float32, 'l0_ffn_w_gate': _jnp.float32, 'l0_ffn_w_up': _jnp.float32, 'l0_ffn_w_down': _jnp.float32, 'l0_ln2_g': _jnp.float32, 'l0_ln2_b': _jnp.float32, 'l1_w_in': _jnp.float32, 'l1_conv_w': _jnp.float32, 'l1_conv_b': _jnp.float32, 'l1_dt_bias': _jnp.float32, 'l1_a_log': _jnp.float32, 'l1_d': _jnp.float32, 'l1_norm_g': _jnp.float32, 'l1_w_out': _jnp.float32, 'l1_ln1_g': _jnp.float32, 'l1_ln1_b': _jnp.float32, 'l1_ffn_w_gate': _jnp.float32, 'l1_ffn_w_up': _jnp.float32, 'l1_ffn_w_down': _jnp.float32, 'l1_ln2_g': _jnp.float32, 'l1_ln2_b': _jnp.float32}
MOMENT_SCALE = {'meta_tokens': 1.319151e-03, 'l0_w_in': 1.701520e-02, 'l0_s5_log_dt': 3.554743e+00, 'l0_s5_a_re': 7.832410e-03, 'l0_s5_a_im': 8.436372e-03, 'l0_s5_b_re': 4.893180e-03, 'l0_s5_b_im': 5.027943e-03, 'l0_s5_c_re': 1.246657e-03, 'l0_s5_c_im': 1.236757e-03, 'l0_s5_d': 2.592951e-02, 'l0_s5_w_glu': 7.182760e-03, 'l0_mla_q_norm': 9.298492e-03, 'l0_mla_w_uq': 5.416801e-03, 'l0_mla_kv_norm': 2.030247e-02, 'l0_mla_w_ukv': 6.433287e-03, 'l0_w_out': 3.962141e-02, 'l0_ln1_g': 9.390916e-01, 'l0_ln1_b': 4.337930e-01, 'l0_ffn_w_gate': 8.734777e-03, 'l0_ffn_w_up': 1.692333e-02, 'l0_ffn_w_down': 2.812096e-02, 'l0_ln2_g': 9.582068e-01, 'l0_ln2_b': 4.392766e-01, 'l1_w_in': 2.766355e-02, 'l1_conv_w': 2.633073e-02, 'l1_conv_b': 4.771247e-02, 'l1_dt_bias': 5.516344e-02, 'l1_a_log': 1.673653e-01, 'l1_d': 1.810989e-01, 'l1_norm_g': 3.460725e-02, 'l1_w_out': 9.531732e-02, 'l1_ln1_g': 1.136183e+00, 'l1_ln1_b': 4.858763e-01, 'l1_ffn_w_gate': 8.198642e-03, 'l1_ffn_w_up': 1.589096e-02, 'l1_ffn_w_down': 2.642354e-02, 'l1_ln2_g': 3.201910e+01, 'l1_ln2_b': 2.707419e+00}


def _to_microbatches(a, axis):
    t = _jnp.moveaxis(a, axis, 0)
    t = t.reshape((N_MICROBATCH, t.shape[0] // N_MICROBATCH) + t.shape[1:])
    return _jnp.moveaxis(t, 1, axis + 1)


def setup_inputs(seed: int = 0) -> dict:
    inp = _fwd_setup_inputs(seed)
    key = _jax.random.fold_in(_jax.random.key(seed), 7919)
    shape, _ = _output_shape()
    out = dict(inp)
    out["loss_target"] = _jax.random.normal(_jax.random.fold_in(key, 0), shape, _jnp.float32)
    for i, name in enumerate(TWIN_WEIGHTS):
        w = inp[name].astype(_jnp.float32)
        if MOMENT_SCALE is None:
            s = _jnp.sqrt(_jnp.mean(_jnp.square(w)) + 1e-30)
        else:
            s = MOMENT_SCALE[name]
        km, kv = _jax.random.split(_jax.random.fold_in(key, i + 1))
        out[name] = w
        out["m_" + name] = s * _jax.random.normal(km, w.shape, _jnp.float32)
        out["v_" + name] = (s * s) * _jax.random.uniform(kv, w.shape, _jnp.float32, 0.5, 1.5)
    if N_MICROBATCH > 1:
        for name, axis in PER_EXAMPLE_BATCH_AXIS.items():
            out[name] = _to_microbatches(out[name], axis)
    return {'x': out['x'], 'meta_tokens': out['meta_tokens'], 'l0_w_in': out['l0_w_in'], 'l0_s5_log_dt': out['l0_s5_log_dt'], 'l0_s5_a_re': out['l0_s5_a_re'], 'l0_s5_a_im': out['l0_s5_a_im'], 'l0_s5_b_re': out['l0_s5_b_re'], 'l0_s5_b_im': out['l0_s5_b_im'], 'l0_s5_c_re': out['l0_s5_c_re'], 'l0_s5_c_im': out['l0_s5_c_im'], 'l0_s5_d': out['l0_s5_d'], 'l0_s5_w_glu': out['l0_s5_w_glu'], 'l0_mla_q_norm': out['l0_mla_q_norm'], 'l0_mla_w_uq': out['l0_mla_w_uq'], 'l0_mla_kv_norm': out['l0_mla_kv_norm'], 'l0_mla_w_ukv': out['l0_mla_w_ukv'], 'l0_w_out': out['l0_w_out'], 'l0_ln1_g': out['l0_ln1_g'], 'l0_ln1_b': out['l0_ln1_b'], 'l0_ffn_w_gate': out['l0_ffn_w_gate'], 'l0_ffn_w_up': out['l0_ffn_w_up'], 'l0_ffn_w_down': out['l0_ffn_w_down'], 'l0_ln2_g': out['l0_ln2_g'], 'l0_ln2_b': out['l0_ln2_b'], 'l1_w_in': out['l1_w_in'], 'l1_conv_w': out['l1_conv_w'], 'l1_conv_b': out['l1_conv_b'], 'l1_dt_bias': out['l1_dt_bias'], 'l1_a_log': out['l1_a_log'], 'l1_d': out['l1_d'], 'l1_norm_g': out['l1_norm_g'], 'l1_w_out': out['l1_w_out'], 'l1_ln1_g': out['l1_ln1_g'], 'l1_ln1_b': out['l1_ln1_b'], 'l1_ffn_w_gate': out['l1_ffn_w_gate'], 'l1_ffn_w_up': out['l1_ffn_w_up'], 'l1_ffn_w_down': out['l1_ffn_w_down'], 'l1_ln2_g': out['l1_ln2_g'], 'l1_ln2_b': out['l1_ln2_b'], 'loss_target': out['loss_target'], 'm_meta_tokens': out['m_meta_tokens'], 'm_l0_w_in': out['m_l0_w_in'], 'm_l0_s5_log_dt': out['m_l0_s5_log_dt'], 'm_l0_s5_a_re': out['m_l0_s5_a_re'], 'm_l0_s5_a_im': out['m_l0_s5_a_im'], 'm_l0_s5_b_re': out['m_l0_s5_b_re'], 'm_l0_s5_b_im': out['m_l0_s5_b_im'], 'm_l0_s5_c_re': out['m_l0_s5_c_re'], 'm_l0_s5_c_im': out['m_l0_s5_c_im'], 'm_l0_s5_d': out['m_l0_s5_d'], 'm_l0_s5_w_glu': out['m_l0_s5_w_glu'], 'm_l0_mla_q_norm': out['m_l0_mla_q_norm'], 'm_l0_mla_w_uq': out['m_l0_mla_w_uq'], 'm_l0_mla_kv_norm': out['m_l0_mla_kv_norm'], 'm_l0_mla_w_ukv': out['m_l0_mla_w_ukv'], 'm_l0_w_out': out['m_l0_w_out'], 'm_l0_ln1_g': out['m_l0_ln1_g'], 'm_l0_ln1_b': out['m_l0_ln1_b'], 'm_l0_ffn_w_gate': out['m_l0_ffn_w_gate'], 'm_l0_ffn_w_up': out['m_l0_ffn_w_up'], 'm_l0_ffn_w_down': out['m_l0_ffn_w_down'], 'm_l0_ln2_g': out['m_l0_ln2_g'], 'm_l0_ln2_b': out['m_l0_ln2_b'], 'm_l1_w_in': out['m_l1_w_in'], 'm_l1_conv_w': out['m_l1_conv_w'], 'm_l1_conv_b': out['m_l1_conv_b'], 'm_l1_dt_bias': out['m_l1_dt_bias'], 'm_l1_a_log': out['m_l1_a_log'], 'm_l1_d': out['m_l1_d'], 'm_l1_norm_g': out['m_l1_norm_g'], 'm_l1_w_out': out['m_l1_w_out'], 'm_l1_ln1_g': out['m_l1_ln1_g'], 'm_l1_ln1_b': out['m_l1_ln1_b'], 'm_l1_ffn_w_gate': out['m_l1_ffn_w_gate'], 'm_l1_ffn_w_up': out['m_l1_ffn_w_up'], 'm_l1_ffn_w_down': out['m_l1_ffn_w_down'], 'm_l1_ln2_g': out['m_l1_ln2_g'], 'm_l1_ln2_b': out['m_l1_ln2_b'], 'v_meta_tokens': out['v_meta_tokens'], 'v_l0_w_in': out['v_l0_w_in'], 'v_l0_s5_log_dt': out['v_l0_s5_log_dt'], 'v_l0_s5_a_re': out['v_l0_s5_a_re'], 'v_l0_s5_a_im': out['v_l0_s5_a_im'], 'v_l0_s5_b_re': out['v_l0_s5_b_re'], 'v_l0_s5_b_im': out['v_l0_s5_b_im'], 'v_l0_s5_c_re': out['v_l0_s5_c_re'], 'v_l0_s5_c_im': out['v_l0_s5_c_im'], 'v_l0_s5_d': out['v_l0_s5_d'], 'v_l0_s5_w_glu': out['v_l0_s5_w_glu'], 'v_l0_mla_q_norm': out['v_l0_mla_q_norm'], 'v_l0_mla_w_uq': out['v_l0_mla_w_uq'], 'v_l0_mla_kv_norm': out['v_l0_mla_kv_norm'], 'v_l0_mla_w_ukv': out['v_l0_mla_w_ukv'], 'v_l0_w_out': out['v_l0_w_out'], 'v_l0_ln1_g': out['v_l0_ln1_g'], 'v_l0_ln1_b': out['v_l0_ln1_b'], 'v_l0_ffn_w_gate': out['v_l0_ffn_w_gate'], 'v_l0_ffn_w_up': out['v_l0_ffn_w_up'], 'v_l0_ffn_w_down': out['v_l0_ffn_w_down'], 'v_l0_ln2_g': out['v_l0_ln2_g'], 'v_l0_ln2_b': out['v_l0_ln2_b'], 'v_l1_w_in': out['v_l1_w_in'], 'v_l1_conv_w': out['v_l1_conv_w'], 'v_l1_conv_b': out['v_l1_conv_b'], 'v_l1_dt_bias': out['v_l1_dt_bias'], 'v_l1_a_log': out['v_l1_a_log'], 'v_l1_d': out['v_l1_d'], 'v_l1_norm_g': out['v_l1_norm_g'], 'v_l1_w_out': out['v_l1_w_out'], 'v_l1_ln1_g': out['v_l1_ln1_g'], 'v_l1_ln1_b': out['v_l1_ln1_b'], 'v_l1_ffn_w_gate': out['v_l1_ffn_w_gate'], 'v_l1_ffn_w_up': out['v_l1_ffn_w_up'], 'v_l1_ffn_w_down': out['v_l1_ffn_w_down'], 'v_l1_ln2_g': out['v_l1_ln2_g'], 'v_l1_ln2_b': out['v_l1_ln2_b']}


def _loss(weights, diff, rest, loss_target):
    with _jax.named_scope("forward"):
        args = {**rest, TWIN_DIFF_INPUT: diff, **{k: w.astype(_WEIGHT_DTYPES[k]) for k, w in weights.items()}}
        y = _forward(args)
    with _jax.named_scope("loss_head"):
        err = _jnp.square(y.astype(_jnp.float32) - loss_target)
        return 0.5 * _jnp.sum(_jnp.mean(err, axis=-1)) if err.ndim else 0.5 * err


def _adamw(w, g, m, v):
    m = ADAM_B1 * m + (1.0 - ADAM_B1) * g
    v = ADAM_B2 * v + (1.0 - ADAM_B2) * _jnp.square(g)
    m_hat = m / (1.0 - ADAM_B1 ** ADAM_STEP)
    v_hat = v / (1.0 - ADAM_B2 ** ADAM_STEP)
    delta = -ADAM_LR * (m_hat / (_jnp.sqrt(v_hat) + ADAM_EPS) + ADAM_WD * w)
    return delta, m, v


def reference(x, meta_tokens, l0_w_in, l0_s5_log_dt, l0_s5_a_re, l0_s5_a_im, l0_s5_b_re, l0_s5_b_im, l0_s5_c_re, l0_s5_c_im, l0_s5_d, l0_s5_w_glu, l0_mla_q_norm, l0_mla_w_uq, l0_mla_kv_norm, l0_mla_w_ukv, l0_w_out, l0_ln1_g, l0_ln1_b, l0_ffn_w_gate, l0_ffn_w_up, l0_ffn_w_down, l0_ln2_g, l0_ln2_b, l1_w_in, l1_conv_w, l1_conv_b, l1_dt_bias, l1_a_log, l1_d, l1_norm_g, l1_w_out, l1_ln1_g, l1_ln1_b, l1_ffn_w_gate, l1_ffn_w_up, l1_ffn_w_down, l1_ln2_g, l1_ln2_b, loss_target, m_meta_tokens, m_l0_w_in, m_l0_s5_log_dt, m_l0_s5_a_re, m_l0_s5_a_im, m_l0_s5_b_re, m_l0_s5_b_im, m_l0_s5_c_re, m_l0_s5_c_im, m_l0_s5_d, m_l0_s5_w_glu, m_l0_mla_q_norm, m_l0_mla_w_uq, m_l0_mla_kv_norm, m_l0_mla_w_ukv, m_l0_w_out, m_l0_ln1_g, m_l0_ln1_b, m_l0_ffn_w_gate, m_l0_ffn_w_up, m_l0_ffn_w_down, m_l0_ln2_g, m_l0_ln2_b, m_l1_w_in, m_l1_conv_w, m_l1_conv_b, m_l1_dt_bias, m_l1_a_log, m_l1_d, m_l1_norm_g, m_l1_w_out, m_l1_ln1_g, m_l1_ln1_b, m_l1_ffn_w_gate, m_l1_ffn_w_up, m_l1_ffn_w_down, m_l1_ln2_g, m_l1_ln2_b, v_meta_tokens, v_l0_w_in, v_l0_s5_log_dt, v_l0_s5_a_re, v_l0_s5_a_im, v_l0_s5_b_re, v_l0_s5_b_im, v_l0_s5_c_re, v_l0_s5_c_im, v_l0_s5_d, v_l0_s5_w_glu, v_l0_mla_q_norm, v_l0_mla_w_uq, v_l0_mla_kv_norm, v_l0_mla_w_ukv, v_l0_w_out, v_l0_ln1_g, v_l0_ln1_b, v_l0_ffn_w_gate, v_l0_ffn_w_up, v_l0_ffn_w_down, v_l0_ln2_g, v_l0_ln2_b, v_l1_w_in, v_l1_conv_w, v_l1_conv_b, v_l1_dt_bias, v_l1_a_log, v_l1_d, v_l1_norm_g, v_l1_w_out, v_l1_ln1_g, v_l1_ln1_b, v_l1_ffn_w_gate, v_l1_ffn_w_up, v_l1_ffn_w_down, v_l1_ln2_g, v_l1_ln2_b):
    given = dict(x=x, meta_tokens=meta_tokens, l0_w_in=l0_w_in, l0_s5_log_dt=l0_s5_log_dt, l0_s5_a_re=l0_s5_a_re, l0_s5_a_im=l0_s5_a_im, l0_s5_b_re=l0_s5_b_re, l0_s5_b_im=l0_s5_b_im, l0_s5_c_re=l0_s5_c_re, l0_s5_c_im=l0_s5_c_im, l0_s5_d=l0_s5_d, l0_s5_w_glu=l0_s5_w_glu, l0_mla_q_norm=l0_mla_q_norm, l0_mla_w_uq=l0_mla_w_uq, l0_mla_kv_norm=l0_mla_kv_norm, l0_mla_w_ukv=l0_mla_w_ukv, l0_w_out=l0_w_out, l0_ln1_g=l0_ln1_g, l0_ln1_b=l0_ln1_b, l0_ffn_w_gate=l0_ffn_w_gate, l0_ffn_w_up=l0_ffn_w_up, l0_ffn_w_down=l0_ffn_w_down, l0_ln2_g=l0_ln2_g, l0_ln2_b=l0_ln2_b, l1_w_in=l1_w_in, l1_conv_w=l1_conv_w, l1_conv_b=l1_conv_b, l1_dt_bias=l1_dt_bias, l1_a_log=l1_a_log, l1_d=l1_d, l1_norm_g=l1_norm_g, l1_w_out=l1_w_out, l1_ln1_g=l1_ln1_g, l1_ln1_b=l1_ln1_b, l1_ffn_w_gate=l1_ffn_w_gate, l1_ffn_w_up=l1_ffn_w_up, l1_ffn_w_down=l1_ffn_w_down, l1_ln2_g=l1_ln2_g, l1_ln2_b=l1_ln2_b, loss_target=loss_target, m_meta_tokens=m_meta_tokens, m_l0_w_in=m_l0_w_in, m_l0_s5_log_dt=m_l0_s5_log_dt, m_l0_s5_a_re=m_l0_s5_a_re, m_l0_s5_a_im=m_l0_s5_a_im, m_l0_s5_b_re=m_l0_s5_b_re, m_l0_s5_b_im=m_l0_s5_b_im, m_l0_s5_c_re=m_l0_s5_c_re, m_l0_s5_c_im=m_l0_s5_c_im, m_l0_s5_d=m_l0_s5_d, m_l0_s5_w_glu=m_l0_s5_w_glu, m_l0_mla_q_norm=m_l0_mla_q_norm, m_l0_mla_w_uq=m_l0_mla_w_uq, m_l0_mla_kv_norm=m_l0_mla_kv_norm, m_l0_mla_w_ukv=m_l0_mla_w_ukv, m_l0_w_out=m_l0_w_out, m_l0_ln1_g=m_l0_ln1_g, m_l0_ln1_b=m_l0_ln1_b, m_l0_ffn_w_gate=m_l0_ffn_w_gate, m_l0_ffn_w_up=m_l0_ffn_w_up, m_l0_ffn_w_down=m_l0_ffn_w_down, m_l0_ln2_g=m_l0_ln2_g, m_l0_ln2_b=m_l0_ln2_b, m_l1_w_in=m_l1_w_in, m_l1_conv_w=m_l1_conv_w, m_l1_conv_b=m_l1_conv_b, m_l1_dt_bias=m_l1_dt_bias, m_l1_a_log=m_l1_a_log, m_l1_d=m_l1_d, m_l1_norm_g=m_l1_norm_g, m_l1_w_out=m_l1_w_out, m_l1_ln1_g=m_l1_ln1_g, m_l1_ln1_b=m_l1_ln1_b, m_l1_ffn_w_gate=m_l1_ffn_w_gate, m_l1_ffn_w_up=m_l1_ffn_w_up, m_l1_ffn_w_down=m_l1_ffn_w_down, m_l1_ln2_g=m_l1_ln2_g, m_l1_ln2_b=m_l1_ln2_b, v_meta_tokens=v_meta_tokens, v_l0_w_in=v_l0_w_in, v_l0_s5_log_dt=v_l0_s5_log_dt, v_l0_s5_a_re=v_l0_s5_a_re, v_l0_s5_a_im=v_l0_s5_a_im, v_l0_s5_b_re=v_l0_s5_b_re, v_l0_s5_b_im=v_l0_s5_b_im, v_l0_s5_c_re=v_l0_s5_c_re, v_l0_s5_c_im=v_l0_s5_c_im, v_l0_s5_d=v_l0_s5_d, v_l0_s5_w_glu=v_l0_s5_w_glu, v_l0_mla_q_norm=v_l0_mla_q_norm, v_l0_mla_w_uq=v_l0_mla_w_uq, v_l0_mla_kv_norm=v_l0_mla_kv_norm, v_l0_mla_w_ukv=v_l0_mla_w_ukv, v_l0_w_out=v_l0_w_out, v_l0_ln1_g=v_l0_ln1_g, v_l0_ln1_b=v_l0_ln1_b, v_l0_ffn_w_gate=v_l0_ffn_w_gate, v_l0_ffn_w_up=v_l0_ffn_w_up, v_l0_ffn_w_down=v_l0_ffn_w_down, v_l0_ln2_g=v_l0_ln2_g, v_l0_ln2_b=v_l0_ln2_b, v_l1_w_in=v_l1_w_in, v_l1_conv_w=v_l1_conv_w, v_l1_conv_b=v_l1_conv_b, v_l1_dt_bias=v_l1_dt_bias, v_l1_a_log=v_l1_a_log, v_l1_d=v_l1_d, v_l1_norm_g=v_l1_norm_g, v_l1_w_out=v_l1_w_out, v_l1_ln1_g=v_l1_ln1_g, v_l1_ln1_b=v_l1_ln1_b, v_l1_ffn_w_gate=v_l1_ffn_w_gate, v_l1_ffn_w_up=v_l1_ffn_w_up, v_l1_ffn_w_down=v_l1_ffn_w_down, v_l1_ln2_g=v_l1_ln2_g, v_l1_ln2_b=v_l1_ln2_b)
    weights = {n: given[n] for n in TWIN_WEIGHTS}
    shared = {n: given[n] for n in SHARED_INPUTS}
    per_example = {n: given[n] for n in ['x']}
    grad_fn = _jax.value_and_grad(_loss, argnums=(0, 1))

    def one_microbatch(ex, loss_target):
        ex = dict(ex)
        diff = ex.pop(TWIN_DIFF_INPUT)
        return grad_fn(weights, diff, {**shared, **ex}, loss_target)

    if N_MICROBATCH == 1:
        loss, (grad_w, grad_x) = one_microbatch(per_example, given["loss_target"])
    else:
        def body(carry, xs):
            loss_sum, grad_sum = carry
            l_k, (gw_k, gx_k) = one_microbatch(xs[0], xs[1])
            with _jax.named_scope("update"):
                return (loss_sum + l_k, _jax.tree.map(_jnp.add, grad_sum, gw_k)), gx_k

        init = (_jnp.zeros((), _jnp.float32), _jax.tree.map(_jnp.zeros_like, weights))
        (loss, grad_w), grad_x = _jax.lax.scan(body, init, (per_example, given["loss_target"]))
    with _jax.named_scope("update"):
        delta_w, new_m, new_v = {}, {}, {}
        for n in TWIN_WEIGHTS:
            delta_w[n], new_m[n], new_v[n] = _adamw(weights[n], grad_w[n], given["m_" + n], given["v_" + n])
    return (loss, grad_x, *[grad_w[n] for n in TWIN_WEIGHTS], *[delta_w[n] for n in TWIN_WEIGHTS],
            *[new_m[n] for n in TWIN_WEIGHTS], *[new_v[n] for n in TWIN_WEIGHTS])
```

```python
import functools
import math

import numpy as np
import jax
import jax.numpy as jnp
from jax import lax
from jax.experimental import pallas as pl
from jax.experimental.pallas import tpu as pltpu

F32 = jnp.float32
BF16 = jnp.bfloat16
HI = lax.Precision.HIGHEST
MESH = pl.DeviceIdType.MESH

LANES = 128
SUBLANES = 8
VMEM_LIMIT_BYTES = 56 * 1024 * 1024

N_META = 16
CHUNK = 64
DEPTH = 2
DN_ALPHA = (2 * DEPTH) ** 0.25
LN_EPS = 1e-5
RMS_EPS = 1e-6
ROPE_BASE = 10000.0
S5_GROUP = 16
S5_STATE = 64
S5_GPB = 8
MLA_NOPE = 128
MLA_ROPE = 64
MLA_V = 128
MLA_QW = 256
ATT_SHIFT = 48
SSD_HEAD_DIM = 64
SSD_GROUPS = 8
SSD_STATE = 128
SSD_CONV = 4
SSD_BLOCK = 128
ADAM_LR = 0.001
ADAM_B1 = 0.9
ADAM_B2 = 0.999
ADAM_EPS = 1e-08
ADAM_WD = 0.01
ADAM_STEP = 10
NEG = -1e30


class _Cfg:
    def __init__(self, d_model, seq, row_tile, att_tile, scan_tiles, small_row_tile):
        d = d_model
        self.tr = small_row_tile
        self.d = d
        self.seq = seq
        self.n = seq + N_META
        lp = -(-(self.n + ATT_SHIFT) // row_tile) * row_tile
        self.lp = lp
        self.tm = row_tile
        self.tq = att_tile
        self.scan_tiles = scan_tiles
        self.s5_w = d // 2
        self.s5_g = self.s5_w // S5_GROUP
        self.s5_nb = self.s5_g // S5_GPB
        self.s5_c = self.s5_g * S5_STATE
        self.heads = d // 256
        self.q_rank = d // 4
        self.kv_rank = d // 8
        self.l0_in = self.s5_w + self.q_rank + self.kv_rank + MLA_ROPE
        self.l0_mix = self.s5_w + self.heads * MLA_V
        self.ssd_inner = 2 * d
        self.ssd_heads = self.ssd_inner // SSD_HEAD_DIM
        self.hpg = self.ssd_heads // SSD_GROUPS
        self.gw = self.hpg * SSD_HEAD_DIM
        self.gn = SSD_GROUPS * SSD_STATE
        self.conv_dim = self.ssd_inner + 2 * self.gn
        self.l1_in = self.ssd_inner + self.conv_dim + self.ssd_heads
        self.l1_inp = -(-self.l1_in // LANES) * LANES
        self.ffn = -(-(8 * d) // (3 * 256)) * 256
        assert lp % att_tile == 0 and lp % SSD_BLOCK == 0 and lp % (8 * scan_tiles) == 0


FULL = _Cfg(2048, 8192, 640, 640, 4, 160)


def _cparams(n_grid):
    return pltpu.CompilerParams(dimension_semantics=("arbitrary",) * n_grid,
                                vmem_limit_bytes=VMEM_LIMIT_BYTES)


def _div_tile(n, target, unit=LANES):
    if n <= target:
        return n
    best = None
    for t in range(unit, target + 1, unit):
        if n % t == 0:
            best = t
    return n if best is None else best


def _tile_call(fn, grid, ins, in_specs, outs, out_specs, name, acc=(), acc_axis=0):
    n_in = len(ins)
    n_out = len(outs)
    acc = tuple(acc)

    def body(*refs):
        vals = fn(*[r[...] for r in refs[:n_in]])
        if not isinstance(vals, (tuple, list)):
            vals = (vals,)
        for k in range(n_out):
            r = refs[n_in + k]
            v = vals[k].astype(r.dtype)
            if k in acc:
                first = pl.program_id(acc_axis) == 0

                @pl.when(first)
                def _(r=r, v=v):
                    r[...] = v

                @pl.when(jnp.logical_not(first))
                def _(r=r, v=v):
                    r[...] += v
            else:
                r[...] = v

    res = pl.pallas_call(
        body, out_shape=[jax.ShapeDtypeStruct(s, d) for s, d in outs], grid=grid,
        in_specs=in_specs, out_specs=out_specs, name=name, compiler_params=_cparams(len(grid)),
    )(*ins)
    return res


def _rows(tm, c):
    return pl.BlockSpec((tm, c), lambda i: (i, 0))


def _whole(shape):
    nd = len(shape)
    return pl.BlockSpec(shape, lambda *a: (0,) * nd)


def _rowwise(fn, rows, params, outs, accs, tm, name):
    lp = rows[0].shape[0]
    n_row_out = len(outs)
    res = _tile_call(
        fn, (lp // tm,), list(rows) + list(params),
        [_rows(tm, r.shape[1]) for r in rows] + [_whole(p.shape) for p in params],
        [((lp, c), dt) for c, dt in outs] + [(s, F32) for s in accs],
        [_rows(tm, c) for c, _ in outs] + [_whole(s) for s in accs],
        name, acc=range(n_row_out, n_row_out + len(accs)))
    return res


_DIMS = {"nn": (((1,), (0,)), ((), ())), "nt": (((1,), (1,)), ((), ())), "tn": (((0,), (0,)), ((), ()))}


def _dot(a, b, mode="nn", precision=None):
    return lax.dot_general(a, b, _DIMS[mode], preferred_element_type=F32, precision=precision)


def _bdot(a, b, mode="nn"):
    return _dot(a.astype(BF16), b.astype(BF16), mode)


def _mm(a, b, mode, name, out_dtype=F32, tm_t=640, tn_t=1536, tk_t=2048):
    if mode == "nn":
        (m, k), (k2, n) = a.shape, b.shape
    elif mode == "nt":
        (m, k), (n, k2) = a.shape, b.shape
    else:
        (k, m), (k2, n) = a.shape, b.shape
        tm_t, tk_t = 1024, 1024
    assert k == k2, (name, a.shape, b.shape)
    tm, tn, tk = _div_tile(m, tm_t), _div_tile(n, tn_t), _div_tile(k, tk_t)
    nk = k // tk
    a_spec = {"nn": pl.BlockSpec((tm, tk), lambda i, j, kk: (i, kk)),
              "nt": pl.BlockSpec((tm, tk), lambda i, j, kk: (i, kk)),
              "tn": pl.BlockSpec((tk, tm), lambda i, j, kk: (kk, i))}[mode]
    b_spec = {"nn": pl.BlockSpec((tk, tn), lambda i, j, kk: (kk, j)),
              "nt": pl.BlockSpec((tn, tk), lambda i, j, kk: (j, kk)),
              "tn": pl.BlockSpec((tk, tn), lambda i, j, kk: (kk, j))}[mode]

    def body(a_ref, b_ref, o_ref, acc_ref):
        part = _bdot(a_ref[...], b_ref[...], mode)
        if nk == 1:
            o_ref[...] = part.astype(o_ref.dtype)
        else:
            kk = pl.program_id(2)

            @pl.when(kk == 0)
            def _():
                acc_ref[...] = part

            @pl.when(kk > 0)
            def _():
                acc_ref[...] += part

            @pl.when(kk == nk - 1)
            def _():
                o_ref[...] = acc_ref[...].astype(o_ref.dtype)

    return pl.pallas_call(
        body, out_shape=jax.ShapeDtypeStruct((m, n), out_dtype), grid=(m // tm, n // tn, nk),
        in_specs=[a_spec, b_spec], out_specs=pl.BlockSpec((tm, tn), lambda i, j, kk: (i, j)),
        scratch_shapes=[pltpu.VMEM((tm, tn) if nk > 1 else (SUBLANES, LANES), F32)],
        name=name, compiler_params=_cparams(3))(a, b)


def _layer_norm(r, g, b):
    mu = jnp.mean(r, axis=-1, keepdims=True)
    xc = r - mu
    var = jnp.mean(xc * xc, axis=-1, keepdims=True)
    return xc * lax.rsqrt(var + LN_EPS) * g + b


def _rms(x, g):
    return x * lax.rsqrt(jnp.mean(x * x, axis=-1, keepdims=True) + RMS_EPS) * g


def _sigmoid(x):
    return 1.0 / (1.0 + jnp.exp(-x))


def _silu(x):
    return x * _sigmoid(x)


def _gelu(x):
    return 0.5 * x * (1.0 + jnp.tanh(0.7978845608028654 * (x + 0.044715 * x * x * x)))


def _softplus(x):
    return jnp.maximum(x, 0.0) + jnp.log(1.0 + jnp.exp(-jnp.abs(x)))


def _ln_fwd(h, mo, g, b, cfg, name):
    def fn(h, mo, g, b):
        return _layer_norm(DN_ALPHA * h + mo, g, b)
    return _rowwise(fn, [h, mo], [g, b], [(cfg.d, F32)], [], cfg.tr, name)[0]


def _ln_bwd(h, mo, g, b, douts, scales, cfg, name):
    def fn(h, mo, *rest):
        ds, (g, b) = rest[:-2], rest[-2:]
        dy = ds[0] * scales[0]
        for t, s in zip(ds[1:], scales[1:]):
            dy = dy + t * s
        _, vjp = jax.vjp(_layer_norm, DN_ALPHA * h + mo, g, b)
        dr, dg, db = vjp(dy)
        return dr, dg, db
    d = cfg.d
    return _rowwise(fn, [h, mo] + list(douts), [g, b], [(d, F32)], [(1, d), (1, d)], cfg.tr, name)


def _ffn_act(gu, cfg, name):
    f = cfg.ffn
    cb = _div_tile(f, 1536)
    nf = f // cb
    lp = gu.shape[0]
    tm = cfg.tr

    def fn(gate, up):
        return _silu(gate) * up
    return _tile_call(fn, (lp // tm, nf), [gu, gu],
                      [pl.BlockSpec((tm, cb), lambda i, j: (i, j)),
                       pl.BlockSpec((tm, cb), lambda i, j: (i, j + nf))],
                      [((lp, f), F32)], [pl.BlockSpec((tm, cb), lambda i, j: (i, j))], name)[0]


def _ffn_act_bwd(gu, dact, cfg, name):
    f = cfg.ffn
    cb = _div_tile(f, 1536)
    nf = f // cb
    lp = gu.shape[0]
    tm = cfg.tr

    def fn(gate, up, da):
        sg = _sigmoid(gate)
        dgate = da * up * sg * (1.0 + gate * (1.0 - sg))
        dup = da * gate * sg
        return jnp.where(pl.program_id(1) < nf, dgate, dup)
    return _tile_call(fn, (lp // tm, 2 * nf), [gu, gu, dact],
                      [pl.BlockSpec((tm, cb), lambda i, j: (i, j % nf)),
                       pl.BlockSpec((tm, cb), lambda i, j: (i, j % nf + nf)),
                       pl.BlockSpec((tm, cb), lambda i, j: (i, j % nf))],
                      [((lp, 2 * f), F32)], [pl.BlockSpec((tm, cb), lambda i, j: (i, j))], name)[0]


def _ffn_fwd(h, w_gu, w_down, cfg, tag):
    gu = _mm(h, w_gu, "nn", tag + "_gu")
    act = _ffn_act(gu, cfg, tag + "_act")
    fo = _mm(act, w_down, "nn", tag + "_down")
    return fo, (gu, act)


def _ffn_bwd(h, saved, dfo, w_gu, w_down, cfg, tag):
    gu, act = saved
    dact = _mm(dfo, w_down, "nt", tag + "_dact")
    d_wdown = _mm(act, dfo, "tn", tag + "_dwdown")
    dgu = _ffn_act_bwd(gu, dact, cfg, tag + "_dgu")
    dh = _mm(dgu, w_gu, "nt", tag + "_dh")
    d_wgu = _mm(h, dgu, "tn", tag + "_dwgu")
    return dh, d_wgu, d_wdown


def _small_call(fn, ins, outs, name):
    return _tile_call(fn, (1,), ins, [_whole(x.shape) for x in ins], [(s, F32) for s in outs],
                      [_whole(s) for s in outs], name)


def _perm(x):
    lp, c = x.shape
    return x.reshape(SUBLANES, lp // SUBLANES, c).transpose(1, 0, 2).reshape(lp, c)


def _unperm(x):
    lp, c = x.shape
    return x.reshape(lp // SUBLANES, SUBLANES, c).transpose(1, 0, 2).reshape(lp, c)


def _s5_disc(log_dt, a_re, a_im):
    dt = jnp.exp(log_dt)
    mag = jnp.exp(dt * a_re)
    ab_re = mag * jnp.cos(dt * a_im)
    ab_im = mag * jnp.sin(dt * a_im)
    den = a_re * a_re + a_im * a_im
    nr = ab_re - 1.0
    f_re = (nr * a_re + ab_im * a_im) / den
    f_im = (ab_im * a_re - nr * a_im) / den
    return ab_re, ab_im, f_re, f_im


def _s5_bbar(f_re, f_im, b_re, b_im):
    return f_re * b_re - f_im * b_im, f_re * b_im + f_im * b_re


def _bd_from(w, cfg):
    g, p, j = w.shape
    w4 = w.reshape(cfg.s5_nb, S5_GPB, p, j)
    eye = jnp.eye(S5_GPB, dtype=w.dtype)
    return jnp.einsum("bgpj,gh->bgjhp", w4, eye).reshape(cfg.s5_nb, S5_GPB * j, S5_GPB * p)


def _bd_to(blocks, cfg, p, j):
    b5 = blocks.reshape(cfg.s5_nb, S5_GPB, j, S5_GPB, p)
    eye = jnp.eye(S5_GPB, dtype=blocks.dtype)
    return jnp.einsum("bgjhp,gh->bgpj", b5, eye).reshape(cfg.s5_g, p, j)


def _bd_split(x, w1, w2, cfg, name):
    nb, ci, co = w1.shape
    lp, tm = x.shape[0], cfg.tm

    def fn(x, w1, w2):
        xb = x.astype(BF16)
        return _dot(xb, w1[0].astype(BF16)), _dot(xb, w2[0].astype(BF16))
    wspec = pl.BlockSpec((1, ci, co), lambda i, j: (j, 0, 0))
    ospec = pl.BlockSpec((tm, co), lambda i, j: (i, j))
    return _tile_call(fn, (lp // tm, nb), [x, w1, w2],
                      [pl.BlockSpec((tm, ci), lambda i, j: (i, j)), wspec, wspec],
                      [((lp, nb * co), F32)] * 2, [ospec, ospec], name)


def _bd_join(x1, x2, w1, w2, extra, scale, cfg, name):
    nb, ci, co = w1.shape
    lp, tm = x1.shape[0], cfg.tm

    def fn(x1, x2, w1, w2, e, s):
        return _bdot(x1, w1[0]) + _bdot(x2, w2[0]) + e * s
    xspec = pl.BlockSpec((tm, ci), lambda i, j: (i, j))
    wspec = pl.BlockSpec((1, ci, co), lambda i, j: (j, 0, 0))
    ospec = pl.BlockSpec((tm, co), lambda i, j: (i, j))
    return _tile_call(fn, (lp // tm, nb), [x1, x2, w1, w2, extra, scale],
                      [xspec, xspec, wspec, wspec, ospec, pl.BlockSpec((1, co), lambda i, j: (0, j))],
                      [((lp, nb * co), F32)], [ospec], name)[0]


def _bd_tn(a, b, nb, cfg, name):
    lp, tk = a.shape[0], cfg.tm
    ca, cb = a.shape[1] // nb, b.shape[1] // nb

    def fn(a, b):
        return _bdot(a, b, "tn")[None]
    return _tile_call(fn, (nb, lp // tk), [a, b],
                      [pl.BlockSpec((tk, ca), lambda j, k: (k, j)), pl.BlockSpec((tk, cb), lambda j, k: (k, j))],
                      [((nb, ca, cb), F32)], [pl.BlockSpec((1, ca, cb), lambda j, k: (j, 0, 0))],
                      name, acc=(0,), acc_axis=1)[0]


def _s5_scan(bu_re, bu_im, ab_re, ab_im, cfg, reverse, tag):
    lp, c = bu_re.shape
    nt = cfg.scan_tiles
    rows = lp // nt
    steps = rows // SUBLANES
    cb = _div_tile(c, 512)
    tmap = (lambda j, t: (nt - 1 - t, j)) if reverse else (lambda j, t: (t, j))
    row_spec = pl.BlockSpec((rows, cb), tmap)
    par_spec = pl.BlockSpec((1, cb), lambda j, t: (0, j))
    st_spec = pl.BlockSpec((SUBLANES, cb), lambda j, t: (0, j))
    grid = (c // cb, nt)
    full = jax.ShapeDtypeStruct((lp, c), F32)
    small = jax.ShapeDtypeStruct((SUBLANES, c), F32)

    def offset(k):
        kk = steps - 1 - k if reverse else k
        return pl.multiple_of(kk * SUBLANES, SUBLANES)

    def local_body(bre, bim, ar_ref, ai_ref, sre, sim, fre, fim, pre, pim, st):
        t = pl.program_id(1)

        @pl.when(t == 0)
        def _():
            zero = jnp.zeros((SUBLANES, cb), F32)
            st[0] = zero
            st[1] = zero
            st[2] = zero + 1.0
            st[3] = zero
        ar = jnp.broadcast_to(ar_ref[...], (SUBLANES, cb))
        ai = jnp.broadcast_to(ai_ref[...], (SUBLANES, cb))

        def step(k, carry):
            s_r, s_i, p_r, p_i = carry
            off = offset(k)
            n_r = ar * s_r - ai * s_i + bre[pl.ds(off, SUBLANES), :]
            n_i = ar * s_i + ai * s_r + bim[pl.ds(off, SUBLANES), :]
            sre[pl.ds(off, SUBLANES), :] = n_r
            sim[pl.ds(off, SUBLANES), :] = n_i
            return n_r, n_i, ar * p_r - ai * p_i, ar * p_i + ai * p_r
        s_r, s_i, p_r, p_i = lax.fori_loop(0, steps, step, (st[0], st[1], st[2], st[3]))
        st[0] = s_r
        st[1] = s_i
        st[2] = p_r
        st[3] = p_i

        @pl.when(t == nt - 1)
        def _():
            fre[...] = s_r
            fim[...] = s_i
            pre[...] = p_r
            pim[...] = p_i

    loc_re, loc_im, f_re, f_im, pn_re, pn_im = pl.pallas_call(
        local_body, out_shape=[full, full, small, small, small, small], grid=grid,
        in_specs=[row_spec, row_spec, par_spec, par_spec],
        out_specs=[row_spec, row_spec, st_spec, st_spec, st_spec, st_spec],
        scratch_shapes=[pltpu.VMEM((4, SUBLANES, cb), F32)],
        name=tag + "_local", compiler_params=_cparams(2))(bu_re, bu_im, ab_re, ab_im)

    def fix_body(lre, lim, fre, fim, pre, pim, ar_ref, ai_ref, sre, sim, st):
        t = pl.program_id(1)
        ar = jnp.broadcast_to(ar_ref[...], (SUBLANES, cb))
        ai = jnp.broadcast_to(ai_ref[...], (SUBLANES, cb))

        @pl.when(t == 0)
        def _():
            f_r, f_i = fre[...], fim[...]
            n_r, n_i = pre[0:1, :], pim[0:1, :]
            row = lax.broadcasted_iota(jnp.int32, (SUBLANES, cb), 0)
            c_r = jnp.zeros((1, cb), F32)
            c_i = jnp.zeros((1, cb), F32)
            car_r = jnp.zeros((SUBLANES, cb), F32)
            car_i = jnp.zeros((SUBLANES, cb), F32)
            order = range(SUBLANES - 2, -1, -1) if reverse else range(1, SUBLANES)
            for i in order:
                src = i + 1 if reverse else i - 1
                c_r, c_i = (n_r * c_r - n_i * c_i + f_r[src:src + 1, :],
                            n_r * c_i + n_i * c_r + f_i[src:src + 1, :])
                car_r = jnp.where(row == i, c_r, car_r)
                car_i = jnp.where(row == i, c_i, car_i)
            st[0] = car_r
            st[1] = car_i
            st[2] = ar
            st[3] = ai
        car_r = st[0]
        car_i = st[1]

        def step(k, carry):
            p_r, p_i = carry
            off = offset(k)
            sre[pl.ds(off, SUBLANES), :] = lre[pl.ds(off, SUBLANES), :] + p_r * car_r - p_i * car_i
            sim[pl.ds(off, SUBLANES), :] = lim[pl.ds(off, SUBLANES), :] + p_r * car_i + p_i * car_r
            return ar * p_r - ai * p_i, ar * p_i + ai * p_r
        p_r, p_i = lax.fori_loop(0, steps, step, (st[2], st[3]))
        st[2] = p_r
        st[3] = p_i

    return pl.pallas_call(
        fix_body, out_shape=[full, full], grid=grid,
        in_specs=[row_spec, row_spec, st_spec, st_spec, st_spec, st_spec, par_spec, par_spec],
        out_specs=[row_spec, row_spec], scratch_shapes=[pltpu.VMEM((4, SUBLANES, cb), F32)],
        name=tag + "_fix", compiler_params=_cparams(2))(loc_re, loc_im, f_re, f_im, pn_re, pn_im, ab_re, ab_im)


def _swap(x):
    return jnp.swapaxes(x, -1, -2)


def _s5_prep(w, cfg):
    g, p, j = cfg.s5_g, S5_STATE, S5_GROUP
    gp = g * p
    log_dt = w["l0_s5_log_dt"].reshape(g, 1)
    ab_re, ab_im, f_re, f_im = _small_call(_s5_disc, [log_dt, w["l0_s5_a_re"], w["l0_s5_a_im"]],
                                           [(g, p)] * 4, "s5_disc")
    b_re2 = w["l0_s5_b_re"].transpose(2, 0, 1).reshape(j, gp)
    b_im2 = w["l0_s5_b_im"].transpose(2, 0, 1).reshape(j, gp)
    f_re1, f_im1 = f_re.reshape(1, gp), f_im.reshape(1, gp)
    bb_re2, bb_im2 = _small_call(_s5_bbar, [f_re1, f_im1, b_re2, b_im2], [(j, gp)] * 2, "s5_bbar")
    bb_re = _bd_from(bb_re2.reshape(j, g, p).transpose(1, 2, 0), cfg).astype(BF16)
    bb_im = _bd_from(bb_im2.reshape(j, g, p).transpose(1, 2, 0), cfg).astype(BF16)
    c_re_t = _bd_from(w["l0_s5_c_re"].transpose(0, 2, 1), cfg).astype(BF16)
    c_imn_t = _bd_from(-w["l0_s5_c_im"].transpose(0, 2, 1), cfg).astype(BF16)
    return dict(log_dt=log_dt, f_re1=f_re1, f_im1=f_im1, b_re2=b_re2, b_im2=b_im2,
                ab_re=ab_re.reshape(1, gp), ab_im=ab_im.reshape(1, gp),
                bb_re=bb_re, bb_im=bb_im, bb_re_t=_swap(bb_re), bb_im_t=_swap(bb_im),
                c_re=_swap(c_re_t), c_imn=_swap(c_imn_t), c_re_t=c_re_t, c_imn_t=c_imn_t,
                d=w["l0_s5_d"].reshape(1, cfg.s5_w))


def _s5_fwd(u, prm, w_glu, cfg):
    tm = cfg.tr
    up = _perm(u)
    bu_re, bu_im = _bd_split(up, prm["bb_re"], prm["bb_im"], cfg, "s5_bu")
    s_re, s_im = _s5_scan(bu_re, bu_im, prm["ab_re"], prm["ab_im"], cfg, False, "s5_scan")
    y = _bd_join(s_re, s_im, prm["c_re"], prm["c_imn"], up, prm["d"], cfg, "s5_y")
    g = _rowwise(_gelu, [y], [], [(cfg.s5_w, F32)], [], tm, "s5_gelu")[0]
    z = _mm(g, w_glu, "nn", "s5_glu_mm")
    a_out = _rowwise(lambda g, z: g * _sigmoid(z), [g, z], [], [(cfg.s5_w, F32)], [], tm, "s5_glu")[0]
    return _unperm(a_out), (up, s_re, s_im, y, g, z)


def _s5_bwd(d_a_out, saved, prm, w, w_glu, cfg):
    up, s_re, s_im, y, g, z = saved
    tm, sw, nb = cfg.tr, cfg.s5_w, cfg.s5_nb
    gs, p, j = cfg.s5_g, S5_STATE, S5_GROUP
    gp = gs * p
    dap = _perm(d_a_out)

    def glu_bwd(da, g, z):
        sg = _sigmoid(z)
        return da * sg, da * g * sg * (1.0 - sg)
    dg1, dz = _rowwise(glu_bwd, [dap, g, z], [], [(sw, F32)] * 2, [], tm, "s5_glu_bwd")
    d_wglu = _mm(g, dz, "tn", "s5_dwglu")
    dg2 = _mm(dz, w_glu, "nt", "s5_dg2")

    def gelu_bwd(dg1, dg2, y, up, d):
        _, vjp = jax.vjp(_gelu, y)
        dy = vjp(dg1 + dg2)[0]
        return dy, dy * d, jnp.sum(dy * up, axis=0, keepdims=True)
    dy, dup_direct, dd = _rowwise(gelu_bwd, [dg1, dg2, y, up], [prm["d"]], [(sw, F32)] * 2, [(1, sw)], tm,
                                  "s5_gelu_bwd")
    ds_re, ds_im = _bd_split(dy, prm["c_re_t"], prm["c_imn_t"], cfg, "s5_ds")
    dc_re_t = _bd_tn(dy, s_re, nb, cfg, "s5_dcre")
    dc_imn_t = _bd_tn(dy, s_im, nb, cfg, "s5_dcim")
    g_re, g_im = _s5_scan(ds_re, ds_im, prm["ab_re"], -prm["ab_im"], cfg, True, "s5_adj")

    def prev(s):
        last = s[-SUBLANES:]
        first = jnp.concatenate([jnp.zeros((1, s.shape[1]), s.dtype), last[:-1]], axis=0)
        return jnp.concatenate([first, s[:-SUBLANES]], axis=0)
    sp_re, sp_im = prev(s_re), prev(s_im)
    cb = _div_tile(gp, 512)
    spec = pl.BlockSpec((tm, cb), lambda jj, i: (i, jj))
    aspec = pl.BlockSpec((1, cb), lambda jj, i: (0, jj))

    def dab(g_r, g_i, p_r, p_i):
        return (jnp.sum(g_r * p_r + g_i * p_i, axis=0, keepdims=True),
                jnp.sum(g_i * p_r - g_r * p_i, axis=0, keepdims=True))
    dab_re, dab_im = _tile_call(dab, (gp // cb, cfg.lp // tm), [g_re, g_im, sp_re, sp_im], [spec] * 4,
                                [((1, gp), F32)] * 2, [aspec] * 2, "s5_dab", acc=(0, 1), acc_axis=1)
    no_scale = jnp.ones((1, sw), F32)
    dup = _bd_join(g_re, g_im, prm["bb_re_t"], prm["bb_im_t"], dup_direct, no_scale, cfg, "s5_dup")
    dbb_re_blk = _bd_tn(up, g_re, nb, cfg, "s5_dbbre")
    dbb_im_blk = _bd_tn(up, g_im, nb, cfg, "s5_dbbim")

    def to2(blk):
        return _bd_to(blk, cfg, p, j).transpose(2, 0, 1).reshape(j, gp)

    def bbar_bwd(f_re, f_im, b_re, b_im, dr, di):
        _, vjp = jax.vjp(_s5_bbar, f_re, f_im, b_re, b_im)
        return vjp((dr, di))
    df_re, df_im, db_re2, db_im2 = _small_call(
        bbar_bwd, [prm["f_re1"], prm["f_im1"], prm["b_re2"], prm["b_im2"], to2(dbb_re_blk), to2(dbb_im_blk)],
        [(1, gp), (1, gp), (j, gp), (j, gp)], "s5_bbar_bwd")

    def disc_bwd(log_dt, a_re, a_im, d1, d2, d3, d4):
        _, vjp = jax.vjp(_s5_disc, log_dt, a_re, a_im)
        return vjp((d1, d2, d3, d4))
    dlog_dt, da_re, da_im = _small_call(
        disc_bwd, [prm["log_dt"], w["l0_s5_a_re"], w["l0_s5_a_im"], dab_re.reshape(gs, p), dab_im.reshape(gs, p),
                   df_re.reshape(gs, p), df_im.reshape(gs, p)], [(gs, 1), (gs, p), (gs, p)], "s5_disc_bwd")
    grads = {
        "l0_s5_log_dt": dlog_dt.reshape(gs), "l0_s5_a_re": da_re, "l0_s5_a_im": da_im,
        "l0_s5_b_re": db_re2.reshape(j, gs, p).transpose(1, 2, 0),
        "l0_s5_b_im": db_im2.reshape(j, gs, p).transpose(1, 2, 0),
        "l0_s5_c_re": _bd_to(dc_re_t, cfg, p, j).transpose(0, 2, 1),
        "l0_s5_c_im": -_bd_to(dc_imn_t, cfg, p, j).transpose(0, 2, 1),
        "l0_s5_d": dd.reshape(sw), "l0_s5_w_glu": d_wglu,
    }
    return _unperm(dup), grads


def _shift(x):
    return jnp.concatenate([jnp.zeros((ATT_SHIFT, x.shape[1]), x.dtype), x[:-ATT_SHIFT]], axis=0)


def _unshift(x):
    return jnp.concatenate([x[ATT_SHIFT:], jnp.zeros((ATT_SHIFT, x.shape[1]), x.dtype)], axis=0)


def _rope_tables(cfg):
    pos = (jnp.arange(cfg.lp) - ATT_SHIFT).astype(F32)
    inv = ROPE_BASE ** (-jnp.arange(0, MLA_ROPE, 2, dtype=F32) / MLA_ROPE)
    ang = pos[:, None] * inv[None, :]
    cos, sin = jnp.cos(ang), jnp.sin(ang)
    z = jnp.zeros((cfg.lp, LANES - MLA_ROPE), F32)
    return jnp.concatenate([cos, cos, z], axis=1), jnp.concatenate([-sin, sin, z], axis=1)


def _swap_halves(x):
    half = MLA_ROPE // 2
    lane = lax.broadcasted_iota(jnp.int32, x.shape, 1)
    left = pltpu.roll(x, LANES - half, axis=1)
    right = pltpu.roll(x, half, axis=1)
    return jnp.where(lane < half, left, jnp.where(lane < 2 * half, right, 0.0))


def _rope(x, cosp, sinp):
    return x * cosp + _swap_halves(x) * sinp


def _rope_t(dy, cosp, sinp):
    return dy * cosp + _swap_halves(dy * sinp)


def _visible(i, j, t):
    row = i * t + lax.broadcasted_iota(jnp.int32, (t, t), 0)
    col = j * t + lax.broadcasted_iota(jnp.int32, (t, t), 1)
    return jnp.logical_and(col // CHUNK <= row // CHUNK, col >= ATT_SHIFT)


def _flash_fwd(q, kv, kr, cfg):
    lp, t, nh = cfg.lp, cfg.tq, cfg.heads
    n = lp // t
    scale = (MLA_NOPE + MLA_ROPE) ** -0.5

    def body(q_ref, kv_ref, kr_ref, o_ref, lse_ref, m_s, l_s, acc_s):
        i, j = pl.program_id(1), pl.program_id(2)

        @pl.when(j == 0)
        def _():
            m_s[...] = jnp.full((t, 1), NEG, F32)
            l_s[...] = jnp.zeros((t, 1), F32)
            acc_s[...] = jnp.zeros((t, MLA_V), F32)

        @pl.when(j <= i)
        def _():
            s = (_dot(q_ref[:, :MLA_NOPE], kv_ref[:, :MLA_NOPE], "nt")
                 + _dot(q_ref[:, MLA_NOPE:], kr_ref[...], "nt")) * scale
            s = jnp.where(_visible(i, j, t), s, NEG)
            m_old = m_s[...]
            m_new = jnp.maximum(m_old, jnp.max(s, axis=1, keepdims=True))
            alpha = jnp.exp(m_old - m_new)
            p = jnp.exp(s - m_new)
            l_s[...] = alpha * l_s[...] + jnp.sum(p, axis=1, keepdims=True)
            acc_s[...] = alpha * acc_s[...] + _dot(p.astype(BF16), kv_ref[:, MLA_NOPE:])
            m_s[...] = m_new

        @pl.when(j == i)
        def _():
            o_ref[...] = acc_s[...] / l_s[...]
            lse_ref[...] = jnp.broadcast_to(m_s[...] + jnp.log(l_s[...]), (t, MLA_V))

    return pl.pallas_call(
        body, out_shape=[jax.ShapeDtypeStruct((lp, nh * MLA_V), F32)] * 2, grid=(nh, n, n),
        in_specs=[pl.BlockSpec((t, MLA_QW), lambda h, i, j: (i, h)),
                  pl.BlockSpec((t, MLA_QW), lambda h, i, j: (jnp.minimum(i, j), h)),
                  pl.BlockSpec((t, LANES), lambda h, i, j: (jnp.minimum(i, j), 0))],
        out_specs=[pl.BlockSpec((t, MLA_V), lambda h, i, j: (i, h))] * 2,
        scratch_shapes=[pltpu.VMEM((t, 1), F32), pltpu.VMEM((t, 1), F32), pltpu.VMEM((t, MLA_V), F32)],
        name="mla_flash_fwd", compiler_params=_cparams(3))(q, kv, kr)


def _flash_bwd(q, kv, kr, o, lse, do, cfg):
    lp, t, nh = cfg.lp, cfg.tq, cfg.heads
    n = lp // t
    scale = (MLA_NOPE + MLA_ROPE) ** -0.5

    def body(q_ref, kv_ref, kr_ref, o_ref, lse_ref, do_ref, dq_ref, dkv_ref, dkr_ref, dkv_s, dkr_s):
        j, i = pl.program_id(1), pl.program_id(2)

        @pl.when(jnp.logical_and(j == 0, i == 0))
        def _():
            dq_ref[...] = jnp.zeros((lp, MLA_QW), F32)

        @pl.when(i == j)
        def _():
            dkv_s[...] = jnp.zeros((t, MLA_QW), F32)
            dkr_s[...] = jnp.zeros((t, LANES), F32)

        @pl.when(i >= j)
        def _():
            qn, qr = q_ref[:, :MLA_NOPE], q_ref[:, MLA_NOPE:]
            kn, v = kv_ref[:, :MLA_NOPE], kv_ref[:, MLA_NOPE:]
            krv = kr_ref[...]
            s = (_dot(qn, kn, "nt") + _dot(qr, krv, "nt")) * scale
            p = jnp.where(_visible(i, j, t), jnp.exp(s - lse_ref[:, :1]), 0.0)
            dov = do_ref[...]
            dob = dov.astype(BF16)
            dp = _dot(dob, v, "nt")
            delta = jnp.sum(dov * o_ref[...], axis=1, keepdims=True)
            ds = (p * (dp - delta) * scale).astype(BF16)
            dkv_s[:, MLA_NOPE:] += _dot(p.astype(BF16), dob, "tn")
            dkv_s[:, :MLA_NOPE] += _dot(ds, qn, "tn")
            dkr_s[...] += _dot(ds, qr, "tn")
            off = pl.multiple_of(i * t, t)
            dq_ref[pl.ds(off, t), :MLA_NOPE] += _dot(ds, kn)
            dq_ref[pl.ds(off, t), MLA_NOPE:] += _dot(ds, krv)

        @pl.when(i == n - 1)
        def _():
            dkv_ref[...] = dkv_s[...]
            dkr_ref[0] = dkr_s[...]

    qspec = pl.BlockSpec((t, MLA_QW), lambda h, j, i: (jnp.maximum(i, j), h))
    ospec = pl.BlockSpec((t, MLA_V), lambda h, j, i: (jnp.maximum(i, j), h))
    return pl.pallas_call(
        body, out_shape=[jax.ShapeDtypeStruct((lp, nh * MLA_QW), F32), jax.ShapeDtypeStruct((lp, nh * MLA_QW), F32),
                         jax.ShapeDtypeStruct((nh, lp, LANES), F32)], grid=(nh, n, n),
        in_specs=[qspec, pl.BlockSpec((t, MLA_QW), lambda h, j, i: (j, h)),
                  pl.BlockSpec((t, LANES), lambda h, j, i: (j, 0)), ospec, ospec, ospec],
        out_specs=[pl.BlockSpec((lp, MLA_QW), lambda h, j, i: (0, h)),
                   pl.BlockSpec((t, MLA_QW), lambda h, j, i: (j, h)),
                   pl.BlockSpec((1, t, LANES), lambda h, j, i: (h, j, 0))],
        scratch_shapes=[pltpu.VMEM((t, MLA_QW), F32), pltpu.VMEM((t, LANES), F32)],
        name="mla_flash_bwd", compiler_params=_cparams(3))(q, kv, kr, o, lse, do)


def _pad_heads(w, nh, width):
    r = w.shape[0]
    w3 = w.reshape(r, nh, width)
    return jnp.pad(w3, ((0, 0), (0, 0), (0, MLA_QW - width))).reshape(r, nh * MLA_QW)


def _mla_fwd(q_lat, kv_lat, k_rope_raw, wq, w_uq_p, w_ukv, cfg):
    tm, nh = cfg.tr, cfg.heads
    ql, kl = _shift(q_lat), _shift(kv_lat)
    kr_raw = jnp.pad(_shift(k_rope_raw), ((0, 0), (0, LANES - MLA_ROPE)))
    cosp, sinp = _rope_tables(cfg)
    qg, kg = wq["l0_mla_q_norm"].reshape(1, -1), wq["l0_mla_kv_norm"].reshape(1, -1)
    qn, kvn = _rowwise(lambda a, b, g1, g2: (_rms(a, g1), _rms(b, g2)), [ql, kl], [qg, kg],
                       [(cfg.q_rank, F32), (cfg.kv_rank, F32)], [], tm, "mla_norm")
    q0 = _mm(qn, w_uq_p, "nn", "mla_q")
    kv = _mm(kvn, w_ukv, "nn", "mla_kv", out_dtype=BF16)

    def rope_fn(q0, kr, cosp, sinp):
        parts = []
        for h in range(nh):
            parts.append(q0[:, h * MLA_QW:h * MLA_QW + MLA_NOPE])
            parts.append(_rope(q0[:, h * MLA_QW + MLA_NOPE:(h + 1) * MLA_QW], cosp, sinp))
        return jnp.concatenate(parts, axis=1), _rope(kr, cosp, sinp)
    q, kr = _rowwise(rope_fn, [q0, kr_raw, cosp, sinp], [], [(nh * MLA_QW, BF16), (LANES, BF16)], [], tm,
                     "mla_rope")
    o, lse = _flash_fwd(q, kv, kr, cfg)
    return _unshift(o), (ql, kl, qn, kvn, q, kv, kr, o, lse, cosp, sinp)


def _mla_bwd(d_b_out, saved, wq, w_uq_p, w_ukv, cfg):
    ql, kl, qn, kvn, q, kv, kr, o, lse, cosp, sinp = saved
    tm, nh, lp = cfg.tr, cfg.heads, cfg.lp
    dq, dkv, dkr_h = _flash_bwd(q, kv, kr, o, lse, _shift(d_b_out), cfg)

    def rope_bwd(dq, dkr_h, cosp, sinp):
        parts = []
        for h in range(nh):
            parts.append(dq[:, h * MLA_QW:h * MLA_QW + MLA_NOPE])
            parts.append(_rope_t(dq[:, h * MLA_QW + MLA_NOPE:(h + 1) * MLA_QW], cosp, sinp))
        dkr = dkr_h[0]
        for h in range(1, nh):
            dkr = dkr + dkr_h[h]
        return jnp.concatenate(parts, axis=1), _rope_t(dkr, cosp, sinp)
    dq0, dkr_raw = _tile_call(
        rope_bwd, (lp // tm,), [dq, dkr_h, cosp, sinp],
        [_rows(tm, nh * MLA_QW), pl.BlockSpec((nh, tm, LANES), lambda i: (0, i, 0)), _rows(tm, LANES),
         _rows(tm, LANES)],
        [((lp, nh * MLA_QW), F32), ((lp, LANES), F32)], [_rows(tm, nh * MLA_QW), _rows(tm, LANES)], "mla_rope_bwd")
    d_wuq_p = _mm(qn, dq0, "tn", "mla_dwuq")
    dqn = _mm(dq0, w_uq_p, "nt", "mla_dqn")
    d_wukv = _mm(kvn, dkv, "tn", "mla_dwukv")
    dkvn = _mm(dkv, w_ukv, "nt", "mla_dkvn")
    qg, kg = wq["l0_mla_q_norm"].reshape(1, -1), wq["l0_mla_kv_norm"].reshape(1, -1)

    def norm_bwd(ql, kl, dqn, dkvn, g1, g2):
        _, vjp1 = jax.vjp(_rms, ql, g1)
        _, vjp2 = jax.vjp(_rms, kl, g2)
        dql, dg1 = vjp1(dqn)
        dkl, dg2 = vjp2(dkvn)
        return dql, dkl, dg1, dg2
    dql, dkl, dg1, dg2 = _rowwise(norm_bwd, [ql, kl, dqn, dkvn], [qg, kg],
                                  [(cfg.q_rank, F32), (cfg.kv_rank, F32)], [(1, cfg.q_rank), (1, cfg.kv_rank)], tm,
                                  "mla_norm_bwd")
    width = MLA_NOPE + MLA_ROPE
    d_wuq = d_wuq_p.reshape(cfg.q_rank, nh, MLA_QW)[:, :, :width].reshape(cfg.q_rank, nh * width)
    grads = {"l0_mla_q_norm": dg1.reshape(-1), "l0_mla_kv_norm": dg2.reshape(-1), "l0_mla_w_uq": d_wuq,
             "l0_mla_w_ukv": d_wukv}
    return _unshift(dql), _unshift(dkl), _unshift(dkr_raw[:, :MLA_ROPE]), grads


def _conv_taps(x, halo, first):
    halo = jnp.where(first, 0.0, halo)
    row8 = lax.broadcasted_iota(jnp.int32, halo.shape, 0)
    taps = []
    for s in range(SSD_CONV - 1, 0, -1):
        r = pltpu.roll(x, s, axis=0)
        top = jnp.where(row8 < s, pltpu.roll(halo, s, axis=0), r[:SUBLANES])
        taps.append(jnp.concatenate([top, r[SUBLANES:]], axis=0))
    taps.append(x)
    return taps


def _conv_specs(cfg, lp):
    tm = cfg.tr
    cb = _div_tile(math.gcd(cfg.ssd_inner, cfg.conv_dim), 2048)
    off = cfg.ssd_inner // cb
    per = tm // SUBLANES
    nrow = lp // tm
    main = pl.BlockSpec((tm, cb), lambda i, j: (i, j + off))
    before = pl.BlockSpec((SUBLANES, cb), lambda i, j: (jnp.maximum(i * per - 1, 0), j + off))
    own = pl.BlockSpec((tm, cb), lambda i, j: (i, j))
    after = pl.BlockSpec((SUBLANES, cb), lambda i, j: (jnp.minimum((i + 1) * per, nrow * per - 1), j))
    par = lambda r: pl.BlockSpec((r, cb), lambda i, j: (0, j))
    return tm, cb, nrow, main, before, own, after, par


def _conv_fwd(zx, conv_w, conv_b, cfg):
    lp = zx.shape[0]
    tm, cb, nrow, main, before, own, after, par = _conv_specs(cfg, lp)

    def fn(x, halo, w, b):
        taps = _conv_taps(x, halo, pl.program_id(0) == 0)
        pre = b
        for k in range(SSD_CONV):
            pre = pre + taps[k] * w[k:k + 1, :]
        return _silu(pre)
    return _tile_call(fn, (nrow, cfg.conv_dim // cb), [zx, zx, conv_w, conv_b],
                      [main, before, par(SSD_CONV), par(1)], [((lp, cfg.conv_dim), F32)], [own], "ssd_conv")[0]


def _conv_bwd(zx, conv_w, conv_b, dact, cfg):
    lp = zx.shape[0]
    tm, cb, nrow, main, before, own, after, par = _conv_specs(cfg, lp)
    ncb = cfg.conv_dim // cb

    def fn1(x, halo, w, b, da):
        taps = _conv_taps(x, halo, pl.program_id(1) == 0)
        pre = b
        for k in range(SSD_CONV):
            pre = pre + taps[k] * w[k:k + 1, :]
        sg = _sigmoid(pre)
        dpre = da * sg * (1.0 + pre * (1.0 - sg))
        row8 = lax.broadcasted_iota(jnp.int32, (SUBLANES, cb), 0)
        dw = jnp.zeros((SUBLANES, cb), F32)
        for k in range(SSD_CONV):
            dw = jnp.where(row8 == k, jnp.sum(dpre * taps[k], axis=0, keepdims=True), dw)
        return dpre, dw, jnp.sum(dpre, axis=0, keepdims=True)
    sw = lambda spec: pl.BlockSpec(spec.block_shape, lambda j, i, f=spec.index_map: f(i, j))
    dpre, dw, db = _tile_call(
        fn1, (ncb, nrow), [zx, zx, conv_w, conv_b, dact], [sw(main), sw(before), sw(par(SSD_CONV)), sw(par(1)), sw(own)],
        [((lp, cfg.conv_dim), F32), ((SUBLANES, cfg.conv_dim), F32), ((1, cfg.conv_dim), F32)],
        [sw(own), sw(par(SUBLANES)), sw(par(1))], "ssd_conv_bwd1", acc=(1, 2), acc_axis=1)

    def fn2(dp, nxt, w):
        nxt = jnp.where(pl.program_id(0) == nrow - 1, 0.0, nxt)
        row8 = lax.broadcasted_iota(jnp.int32, nxt.shape, 0)
        dx = dp * w[SSD_CONV - 1:SSD_CONV, :]
        for s in range(1, SSD_CONV):
            r = pltpu.roll(dp, tm - s, axis=0)
            bot = jnp.where(row8 >= SUBLANES - s, pltpu.roll(nxt, SUBLANES - s, axis=0), r[tm - SUBLANES:])
            up = jnp.concatenate([r[:tm - SUBLANES], bot], axis=0)
            dx = dx + up * w[SSD_CONV - 1 - s:SSD_CONV - s, :]
        return dx
    dx = _tile_call(fn2, (nrow, ncb), [dpre, dpre, conv_w], [own, after, par(SSD_CONV)],
                    [((lp, cfg.conv_dim), F32)], [own], "ssd_conv_bwd2")[0]
    return dx, dw, db


def _ssd_common(x_ref, b_ref, c_ref, dt_ref, dtt_ref, ar_ref, ac_ref, h):
    q = SSD_BLOCK
    x, bm, cm = x_ref[...], b_ref[...], c_ref[...]
    dt, dtt = dt_ref[0], dtt_ref[0]
    row = lax.broadcasted_iota(jnp.int32, (q, q), 0)
    col = lax.broadcasted_iota(jnp.int32, (q, q), 1)
    tri = row >= col
    cs = _dot(tri.astype(F32), dt * ar_ref[0], precision=HI)
    cst = _dot(dtt * ac_ref[0], (row <= col).astype(F32), precision=HI)
    g = _bdot(cm, bm, "nt")
    ch = _bdot(cm, h)
    return x, bm, cm, dt, tri, cs, cst, g, ch


def _ssd_specs(cfg, rev):
    q, n, gw, hpg = SSD_BLOCK, SSD_STATE, cfg.gw, cfg.hpg
    nc = cfg.lp // q
    cc = (lambda c: nc - 1 - c) if rev else (lambda c: c)
    boff = cfg.ssd_inner // n
    return dict(
        x=pl.BlockSpec((q, gw), lambda g, c: (cc(c), g)),
        b=pl.BlockSpec((q, n), lambda g, c: (cc(c), boff + g)),
        c=pl.BlockSpec((q, n), lambda g, c: (cc(c), boff + SSD_GROUPS + g)),
        bc_out=pl.BlockSpec((q, n), lambda g, c: (cc(c), g)),
        dt=pl.BlockSpec((1, q, hpg), lambda g, c: (g, cc(c), 0)),
        dtt=pl.BlockSpec((1, hpg, q), lambda g, c: (g, 0, cc(c))),
        ar=pl.BlockSpec((1, 1, hpg), lambda g, c: (g, 0, 0)),
        ac=pl.BlockSpec((1, hpg, 1), lambda g, c: (g, 0, 0)),
        h=pl.BlockSpec((1, n, gw), lambda g, c: (cc(c), 0, g)))


def _ssd_fwd(xbc, dt_g, dtt_g, a_row, a_col, cfg):
    q, n, gw, hpg, lp = SSD_BLOCK, SSD_STATE, cfg.gw, cfg.hpg, cfg.lp
    nc = lp // q
    hd = SSD_HEAD_DIM
    sp = _ssd_specs(cfg, False)

    def body(x_ref, b_ref, c_ref, dt_ref, dtt_ref, ar_ref, ac_ref, y_ref, hp_ref, h_s, xw_s):
        @pl.when(pl.program_id(1) == 0)
        def _():
            h_s[...] = jnp.zeros((n, gw), F32)
        h = h_s[...]
        hp_ref[0] = h
        x, bm, cm, dt, tri, cs, cst, g, ch = _ssd_common(x_ref, b_ref, c_ref, dt_ref, dtt_ref, ar_ref, ac_ref, h)
        lane = lax.broadcasted_iota(jnp.int32, (1, gw), 1) // hd
        dec = jnp.zeros((1, gw), F32)
        for r in range(hpg):
            sl = slice(r * hd, (r + 1) * hd)
            csr = cs[:, r:r + 1]
            lm = jnp.exp(jnp.where(tri, csr - cst[r:r + 1, :], NEG))
            xdt = x[:, sl] * dt[:, r:r + 1]
            last = cs[q - 1:q, r:r + 1]
            y_ref[:, sl] = _bdot(g * lm, xdt) + jnp.exp(csr) * ch[:, sl]
            xw_s[:, sl] = xdt * jnp.exp(last - csr)
            dec = jnp.where(lane == r, jnp.exp(last), dec)
        h_s[...] = h * dec + _bdot(bm, xw_s[...], "tn")

    return pl.pallas_call(
        body, out_shape=[jax.ShapeDtypeStruct((lp, cfg.ssd_inner), F32),
                         jax.ShapeDtypeStruct((nc, n, cfg.ssd_inner), F32)],
        grid=(SSD_GROUPS, nc),
        in_specs=[sp["x"], sp["b"], sp["c"], sp["dt"], sp["dtt"], sp["ar"], sp["ac"]],
        out_specs=[sp["x"], sp["h"]],
        scratch_shapes=[pltpu.VMEM((n, gw), F32), pltpu.VMEM((q, gw), F32)],
        name="ssd_scan", compiler_params=_cparams(2))(xbc, xbc, xbc, dt_g, dtt_g, a_row, a_col)


def _ssd_bwd(xbc, dt_g, dtt_g, a_row, a_col, hprev, dy, dx_gate, cfg):
    q, n, gw, hpg, lp = SSD_BLOCK, SSD_STATE, cfg.gw, cfg.hpg, cfg.lp
    nc = lp // q
    hd = SSD_HEAD_DIM
    sp = _ssd_specs(cfg, True)

    def body(x_ref, b_ref, c_ref, dt_ref, dtt_ref, ar_ref, ac_ref, hp_ref, dy_ref, dxg_ref,
             dx_ref, db_ref, dc_ref, ddt_ref, da_ref, dh_s, xw_s, dye_s, colp_s, rowp_s, xs_s):
        @pl.when(pl.program_id(1) == 0)
        def _():
            dh_s[...] = jnp.zeros((n, gw), F32)
            da_ref[...] = jnp.zeros((1, 1, hpg), F32)
        h = hp_ref[0]
        dhn = dh_s[...]
        dy = dy_ref[...]
        x, bm, cm, dt, tri, cs, cst, g, ch = _ssd_common(x_ref, b_ref, c_ref, dt_ref, dtt_ref, ar_ref, ac_ref, h)
        bd = _bdot(bm, dhn)
        hh = jnp.sum(dhn * h, axis=0, keepdims=True)
        lane = lax.broadcasted_iota(jnp.int32, (1, gw), 1) // hd
        is_last = lax.broadcasted_iota(jnp.int32, (q, 1), 0) == q - 1
        dec = jnp.zeros((1, gw), F32)
        dg = jnp.zeros((q, q), F32)
        for r in range(hpg):
            sl = slice(r * hd, (r + 1) * hd)
            csr = cs[:, r:r + 1]
            lm = jnp.exp(jnp.where(tri, csr - cst[r:r + 1, :], NEG))
            dtr = dt[:, r:r + 1]
            xr = x[:, sl]
            xdt = xr * dtr
            last = cs[q - 1:q, r:r + 1]
            e = jnp.exp(csr)
            wv = jnp.exp(last - csr)
            elast = jnp.exp(last)
            m = g * lm
            dyr = dy[:, sl]
            dxdt = _bdot(m, dyr, "tn") + wv * bd[:, sl]
            dm = _bdot(dyr, xdt, "nt")
            dg = dg + dm * lm
            z = dm * m
            de = jnp.sum(dyr * ch[:, sl], axis=1, keepdims=True)
            dw = jnp.sum(xdt * bd[:, sl], axis=1, keepdims=True)
            extra = (jnp.sum(dw * wv, axis=0, keepdims=True)
                     + elast * jnp.sum(hh[:, sl], axis=1, keepdims=True))
            colp_s[:, r:r + 1] = (jnp.sum(z, axis=1, keepdims=True) + de * e - dw * wv
                                  + jnp.where(is_last, extra, 0.0))
            rowp_s[r:r + 1, :] = jnp.sum(z, axis=0, keepdims=True)
            xs_s[:, r:r + 1] = jnp.sum(dxdt * xr, axis=1, keepdims=True)
            dx_ref[:, sl] = dxdt * dtr + dxg_ref[:, sl]
            xw_s[:, sl] = xdt * wv
            dye_s[:, sl] = dyr * e
            dec = jnp.where(lane == r, elast, dec)
        eye = (lax.broadcasted_iota(jnp.int32, (hpg, hpg), 0)
               == lax.broadcasted_iota(jnp.int32, (hpg, hpg), 1)).astype(F32)
        dcs = colp_s[...] - _dot(rowp_s[...], eye, "tn", precision=HI)
        row = lax.broadcasted_iota(jnp.int32, (q, q), 0)
        col = lax.broadcasted_iota(jnp.int32, (q, q), 1)
        dda = _dot((row <= col).astype(F32), dcs, precision=HI)
        ddt_ref[0] = dda * ar_ref[0] + xs_s[...]
        da_ref[0] += jnp.sum(dda * dt, axis=0, keepdims=True)
        dc_ref[...] = _bdot(dg, bm) + _bdot(dye_s[...], h, "nt")
        db_ref[...] = _bdot(dg, cm, "tn") + _bdot(xw_s[...], dhn, "nt")
        dh_s[...] = dhn * dec + _bdot(cm, dye_s[...], "tn")

    return pl.pallas_call(
        body, out_shape=[jax.ShapeDtypeStruct((lp, cfg.ssd_inner), F32), jax.ShapeDtypeStruct((lp, cfg.gn), F32),
                         jax.ShapeDtypeStruct((lp, cfg.gn), F32), jax.ShapeDtypeStruct((SSD_GROUPS, lp, hpg), F32),
                         jax.ShapeDtypeStruct((SSD_GROUPS, 1, hpg), F32)],
        grid=(SSD_GROUPS, nc),
        in_specs=[sp["x"], sp["b"], sp["c"], sp["dt"], sp["dtt"], sp["ar"], sp["ac"], sp["h"], sp["x"], sp["x"]],
        out_specs=[sp["x"], sp["bc_out"], sp["bc_out"], sp["dt"], sp["ar"]],
        scratch_shapes=[pltpu.VMEM((n, gw), F32), pltpu.VMEM((q, gw), F32), pltpu.VMEM((q, gw), F32),
                        pltpu.VMEM((q, hpg), F32), pltpu.VMEM((hpg, q), F32), pltpu.VMEM((q, hpg), F32)],
        name="ssd_scan_bwd", compiler_params=_cparams(2))(xbc, xbc, xbc, dt_g, dtt_g, a_row, a_col, hprev, dy, dx_gate)


def _gate_fn(y, xs, z, dexp, ng):
    return _rms((y + dexp * xs) * _silu(z), ng)


def _gate_specs(cfg):
    tm, gw = cfg.tm, cfg.gw
    blk = pl.BlockSpec((tm, gw), lambda g, i: (i, g))
    par = pl.BlockSpec((1, gw), lambda g, i: (0, g))
    return blk, par


def _mamba_fwd(h, w, w_in_p, w_out, conv_w, cfg):
    lp, tm, nh, hpg, inner = cfg.lp, cfg.tm, cfg.ssd_heads, cfg.hpg, cfg.ssd_inner
    zx = _mm(h, w_in_p, "nn", "l1_in")
    conv_b = w["l1_conv_b"].reshape(1, -1)
    xbc = _conv_fwd(zx, conv_w, conv_b, cfg)
    dt_raw = zx[:, inner + cfg.conv_dim:inner + cfg.conv_dim + nh]
    dt_bias = w["l1_dt_bias"].reshape(1, nh)
    a_log = w["l1_a_log"].reshape(1, nh)
    dt = _rowwise(lambda r, b: _softplus(r + b), [dt_raw], [dt_bias], [(nh, F32)], [], tm, "ssd_dt")[0]
    a = _small_call(lambda al: -jnp.exp(al), [a_log], [(1, nh)], "ssd_a")[0]
    dt_g = dt.reshape(lp, SSD_GROUPS, hpg).transpose(1, 0, 2)
    dtt_g = dt_g.transpose(0, 2, 1)
    a_row, a_col = a.reshape(SSD_GROUPS, 1, hpg), a.reshape(SSD_GROUPS, hpg, 1)
    y, hprev = _ssd_fwd(xbc, dt_g, dtt_g, a_row, a_col, cfg)
    dexp = jnp.repeat(w["l1_d"], SSD_HEAD_DIM).reshape(1, inner)
    ng = w["l1_norm_g"].reshape(1, inner)
    blk, par = _gate_specs(cfg)
    yn = _tile_call(_gate_fn, (SSD_GROUPS, lp // tm), [y, xbc, zx, dexp, ng], [blk, blk, blk, par, par],
                    [((lp, inner), F32)], [blk], "ssd_gate")[0]
    mo = _mm(yn, w_out, "nn", "l1_out")
    return mo, (zx, xbc, dt_raw, dt_g, dtt_g, a, a_row, a_col, y, hprev, dexp, ng, yn)


def _mamba_bwd(h, saved, dmo, w, w_in_p, w_out, conv_w, cfg):
    zx, xbc, dt_raw, dt_g, dtt_g, a, a_row, a_col, y, hprev, dexp, ng, yn = saved
    lp, tm, nh, hpg, inner = cfg.lp, cfg.tm, cfg.ssd_heads, cfg.hpg, cfg.ssd_inner
    d_wout = _mm(yn, dmo, "tn", "l1_dwout")
    dyn = _mm(dmo, w_out, "nt", "l1_dyn")
    blk, par = _gate_specs(cfg)

    def gate_bwd(y, xs, z, dexp, ng, dyn):
        _, vjp = jax.vjp(_gate_fn, y, xs, z, dexp, ng)
        return vjp(dyn)
    dy, dxs_gate, dz, ddexp, dng = _tile_call(
        gate_bwd, (SSD_GROUPS, lp // tm), [y, xbc, zx, dexp, ng, dyn], [blk, blk, blk, par, par, blk],
        [((lp, inner), F32)] * 3 + [((1, inner), F32)] * 2, [blk, blk, blk, par, par], "ssd_gate_bwd",
        acc=(3, 4), acc_axis=1)
    dxs, dbm, dcm, ddt_g, da_g = _ssd_bwd(xbc, dt_g, dtt_g, a_row, a_col, hprev, dy, dxs_gate, cfg)
    conv_b = w["l1_conv_b"].reshape(1, -1)
    dxbc, dconv_w, dconv_b = _conv_bwd(zx, conv_w, conv_b, jnp.concatenate([dxs, dbm, dcm], axis=1), cfg)
    ddt = ddt_g.transpose(1, 0, 2).reshape(lp, nh)
    dt_bias = w["l1_dt_bias"].reshape(1, nh)

    def dt_bwd(ddt, r, b):
        d = ddt * _sigmoid(r + b)
        return d, jnp.sum(d, axis=0, keepdims=True)
    ddt_raw, ddt_bias = _rowwise(dt_bwd, [ddt, dt_raw], [dt_bias], [(nh, F32)], [(1, nh)], tm, "ssd_dt_bwd")
    da_log, dd = _small_call(lambda da, a, dde: (da * a, jnp.sum(dde, axis=1, keepdims=True)),
                             [da_g.reshape(1, nh), a, ddexp.reshape(nh, SSD_HEAD_DIM)], [(1, nh), (nh, 1)],
                             "ssd_small_bwd")
    pad = jnp.zeros((lp, cfg.l1_inp - cfg.l1_in), F32)
    dzx = jnp.concatenate([dz, dxbc, ddt_raw, pad], axis=1)
    d_win_p = _mm(h, dzx, "tn", "l1_dwin")
    dh = _mm(dzx, w_in_p, "nt", "l1_dh")
    grads = {"l1_w_in": d_win_p[:, :cfg.l1_in], "l1_conv_w": dconv_w[:SSD_CONV], "l1_conv_b": dconv_b.reshape(-1),
             "l1_dt_bias": ddt_bias.reshape(-1), "l1_a_log": da_log.reshape(-1), "l1_d": dd.reshape(-1),
             "l1_norm_g": dng.reshape(-1), "l1_w_out": d_wout}
    return dh, grads


def _derived_weights(wf, cfg):
    w = dict(wf)
    w["l0_mla_w_uq_p"] = _pad_heads(wf["l0_mla_w_uq"], cfg.heads, MLA_NOPE + MLA_ROPE)
    w["l1_w_in_p"] = jnp.pad(wf["l1_w_in"], ((0, 0), (0, cfg.l1_inp - cfg.l1_in)))
    for l in ("l0", "l1"):
        w[l + "_w_gu"] = jnp.concatenate([wf[l + "_ffn_w_gate"], wf[l + "_ffn_w_up"]], axis=1)
    return w


def _local_step(x, target, w, cfg):
    lp, n, d, tm, sw = cfg.lp, cfg.n, cfg.d, cfg.tr, cfg.s5_w
    row = lambda name: w[name].reshape(1, -1)
    h0 = jnp.concatenate([w["meta_tokens"], x, jnp.zeros((lp - n, d), F32)], axis=0)
    proj = _mm(h0, w["l0_w_in"], "nn", "l0_in")
    o1, o2, o3 = sw, sw + cfg.q_rank, sw + cfg.q_rank + cfg.kv_rank
    prm = _s5_prep(w, cfg)
    a_out, s5_saved = _s5_fwd(proj[:, :o1], prm, w["l0_s5_w_glu"], cfg)
    b_out, mla_saved = _mla_fwd(proj[:, o1:o2], proj[:, o2:o3], proj[:, o3:], w, w["l0_mla_w_uq_p"],
                                w["l0_mla_w_ukv"], cfg)
    mix = jnp.concatenate([a_out, b_out], axis=1)
    mo0 = _mm(mix, w["l0_w_out"], "nn", "l0_out")
    h1 = _ln_fwd(h0, mo0, row("l0_ln1_g"), row("l0_ln1_b"), cfg, "l0_ln1")
    fo0, ffn0 = _ffn_fwd(h1, w["l0_w_gu"], w["l0_ffn_w_down"], cfg, "l0_ffn")
    h2 = _ln_fwd(h1, fo0, row("l0_ln2_g"), row("l0_ln2_b"), cfg, "l0_ln2")
    mo1, mam = _mamba_fwd(h2, w, w["l1_w_in_p"], w["l1_w_out"], w["l1_conv_w"], cfg)
    h3 = _ln_fwd(h2, mo1, row("l1_ln1_g"), row("l1_ln1_b"), cfg, "l1_ln1")
    fo1, ffn1 = _ffn_fwd(h3, w["l1_w_gu"], w["l1_ffn_w_down"], cfg, "l1_ffn")
    h4 = _ln_fwd(h3, fo1, row("l1_ln2_g"), row("l1_ln2_b"), cfg, "l1_ln2")
    tgt = jnp.concatenate([jnp.zeros((N_META, d), F32), target, jnp.zeros((lp - n, d), F32)], axis=0)

    def loss_fn(y, t):
        r = pl.program_id(0) * tm + lax.broadcasted_iota(jnp.int32, (tm, 1), 0)
        diff = jnp.where(jnp.logical_and(r >= N_META, r < n), y - t, 0.0)
        return diff * (1.0 / d), jnp.sum(diff * diff, axis=0, keepdims=True) * (0.5 / d)
    dh4, loss_lanes = _rowwise(loss_fn, [h4, tgt], [], [(d, F32)], [(1, d)], tm, "loss")
    grads = {}
    dr4, dg, db = _ln_bwd(h3, fo1, row("l1_ln2_g"), row("l1_ln2_b"), [dh4], [1.0], cfg, "l1_ln2_bwd")
    grads["l1_ln2_g"], grads["l1_ln2_b"] = dg.reshape(-1), db.reshape(-1)
    dh3, d_wgu, grads["l1_ffn_w_down"] = _ffn_bwd(h3, ffn1, dr4, w["l1_w_gu"], w["l1_ffn_w_down"], cfg, "l1_ffn")
    grads["l1_ffn_w_gate"], grads["l1_ffn_w_up"] = d_wgu[:, :cfg.ffn], d_wgu[:, cfg.ffn:]
    dr3, dg, db = _ln_bwd(h2, mo1, row("l1_ln1_g"), row("l1_ln1_b"), [dr4, dh3], [DN_ALPHA, 1.0], cfg, "l1_ln1_bwd")
    grads["l1_ln1_g"], grads["l1_ln1_b"] = dg.reshape(-1), db.reshape(-1)
    dh2, mg = _mamba_bwd(h2, mam, dr3, w, w["l1_w_in_p"], w["l1_w_out"], w["l1_conv_w"], cfg)
    grads.update(mg)
    dr2, dg, db = _ln_bwd(h1, fo0, row("l0_ln2_g"), row("l0_ln2_b"), [dr3, dh2], [DN_ALPHA, 1.0], cfg, "l0_ln2_bwd")
    grads["l0_ln2_g"], grads["l0_ln2_b"] = dg.reshape(-1), db.reshape(-1)
    dh1, d_wgu, grads["l0_ffn_w_down"] = _ffn_bwd(h1, ffn0, dr2, w["l0_w_gu"], w["l0_ffn_w_down"], cfg, "l0_ffn")
    grads["l0_ffn_w_gate"], grads["l0_ffn_w_up"] = d_wgu[:, :cfg.ffn], d_wgu[:, cfg.ffn:]
    dr1, dg, db = _ln_bwd(h0, mo0, row("l0_ln1_g"), row("l0_ln1_b"), [dr2, dh1], [DN_ALPHA, 1.0], cfg, "l0_ln1_bwd")
    grads["l0_ln1_g"], grads["l0_ln1_b"] = dg.reshape(-1), db.reshape(-1)
    grads["l0_w_out"] = _mm(mix, dr1, "tn", "l0_dwout")
    dmix = _mm(dr1, w["l0_w_out"], "nt", "l0_dmix")
    du, sg = _s5_bwd(dmix[:, :sw], s5_saved, prm, w, w["l0_s5_w_glu"], cfg)
    dql, dkl, dkr, ag = _mla_bwd(dmix[:, sw:], mla_saved, w, w["l0_mla_w_uq_p"], w["l0_mla_w_ukv"], cfg)
    grads.update(sg)
    grads.update(ag)
    dproj = jnp.concatenate([du, dql, dkl, dkr], axis=1)
    grads["l0_w_in"] = _mm(h0, dproj, "tn", "l0_dwin")
    dh0m = _mm(dproj, w["l0_w_in"], "nt", "l0_dh")
    dh0 = _rowwise(lambda a, b: DN_ALPHA * a + b, [dr1, dh0m], [], [(d, F32)], [], tm, "l0_dh0")[0]
    grads["meta_tokens"] = dh0[:N_META]
    return loss_lanes, dh0[N_META:n], grads


FLAT_W = 1024
N_SLOTS = 4
HBM_SPEC = pl.BlockSpec(memory_space=pltpu.HBM)


def _place():
    x, y, c = lax.axis_index("x"), lax.axis_index("y"), lax.axis_index("c")
    chips = [(1 - x, y), (x, 1 - y), (1 - x, 1 - y)]
    return x, y, c, chips


def _remote(src, dst, ssem, rsem, dev):
    return pltpu.make_async_remote_copy(src_ref=src, dst_ref=dst, send_sem=ssem, recv_sem=rsem, device_id=dev,
                                        device_id_type=MESH)


def _gather_shards(shard):
    def body(src, out, send_sems, recv_sems, fsend, frecv, lsem):
        x, y, c, chips = _place()
        me = 2 * x + y
        sib = (x, y, 1 - c)
        local = pltpu.make_async_copy(src, out.at[me], lsem)
        local.start()
        sends = []
        for j, (cx, cy) in enumerate(chips):
            cp = _remote(src.at[c], out.at[me, c], send_sems.at[j], recv_sems.at[j], (cx, cy, c))
            cp.start()
            sends.append(cp)
        for j, (cx, cy) in enumerate(chips):
            slot = 2 * cx + cy
            _remote(src.at[c], out.at[slot, c], send_sems.at[j], recv_sems.at[j], (cx, cy, c)).wait_recv()
            cp = _remote(out.at[slot, c], out.at[slot, c], fsend.at[j], frecv.at[j], sib)
            cp.start()
            sends.append(cp)
        for j, (cx, cy) in enumerate(chips):
            slot = 2 * cx + cy
            _remote(out.at[slot, c], out.at[slot, 1 - c], fsend.at[j], frecv.at[j], sib).wait_recv()
        for cp in sends:
            cp.wait_send()
        local.wait()

    return pl.pallas_call(
        body, out_shape=jax.ShapeDtypeStruct((N_SLOTS,) + shard.shape, shard.dtype),
        in_specs=[HBM_SPEC], out_specs=HBM_SPEC,
        scratch_shapes=[pltpu.SemaphoreType.DMA((3,)), pltpu.SemaphoreType.DMA((3,)), pltpu.SemaphoreType.DMA((3,)),
                        pltpu.SemaphoreType.DMA((3,)), pltpu.SemaphoreType.DMA],
        name="gather_shards")(shard)


def _exchange_slots(g):
    def body(src, out, send_sems, recv_sems):
        x, y, c, chips = _place()
        cps = []
        for j, (cx, cy) in enumerate(chips):
            cp = _remote(src.at[2 * cx + cy], out.at[j], send_sems.at[j], recv_sems.at[j], (cx, cy, c))
            cp.start()
            cps.append(cp)
        for cp in cps:
            cp.wait()

    return pl.pallas_call(
        body, out_shape=jax.ShapeDtypeStruct((3,) + g.shape[1:], g.dtype), in_specs=[HBM_SPEC], out_specs=HBM_SPEC,
        scratch_shapes=[pltpu.SemaphoreType.DMA((3,)), pltpu.SemaphoreType.DMA((3,))], name="exchange_slots")(g)


def _exchange_sibling(p):
    def body(src, out, ssem, rsem):
        x, y, c, _ = _place()
        cp = _remote(src, out, ssem, rsem, (x, y, 1 - c))
        cp.start()
        cp.wait()

    return pl.pallas_call(
        body, out_shape=jax.ShapeDtypeStruct(p.shape, p.dtype), in_specs=[HBM_SPEC], out_specs=HBM_SPEC,
        scratch_shapes=[pltpu.SemaphoreType.DMA, pltpu.SemaphoreType.DMA], name="exchange_sibling")(p)


def _gather_all(v):
    flips = [(fx, fy, fc) for fx in (0, 1) for fy in (0, 1) for fc in (0, 1)][1:]

    def body(src, out, send_sems, recv_sems, lsem):
        x, y, c, _ = _place()
        local = pltpu.make_async_copy(src, out.at[4 * x + 2 * y + c], lsem)
        local.start()
        cps = []
        for k, (fx, fy, fc) in enumerate(flips):
            px, py, pc = (1 - x if fx else x), (1 - y if fy else y), (1 - c if fc else c)
            cp = _remote(src, out.at[4 * x + 2 * y + c], send_sems.at[k], recv_sems.at[k], (px, py, pc))
            cp.start()
            cps.append(cp)
        for cp in cps:
            cp.wait()
        local.wait()

    return pl.pallas_call(
        body, out_shape=jax.ShapeDtypeStruct((8,) + v.shape, v.dtype), in_specs=[HBM_SPEC], out_specs=HBM_SPEC,
        scratch_shapes=[pltpu.SemaphoreType.DMA((7,)), pltpu.SemaphoreType.DMA((7,)), pltpu.SemaphoreType.DMA],
        name="gather_all")(v)


def _flat_rows(n_elems, row_unit):
    return -(-n_elems // (FLAT_W * row_unit)) * row_unit


def _to_flat(pieces, rows):
    flat = jnp.concatenate([p.reshape(-1) for p in pieces])
    return jnp.pad(flat, (0, rows * FLAT_W - flat.shape[0])).reshape(rows, FLAT_W)


def _sum_parts(parts, name, tm=512):
    rows = parts[0].shape[1]
    tm = _div_tile(rows, tm, SUBLANES)

    def fn(*ps):
        acc = None
        for p in ps:
            for k in range(p.shape[0]):
                acc = p[k].astype(F32) if acc is None else acc + p[k].astype(F32)
        return acc
    return _tile_call(fn, (rows // tm,), parts,
                      [pl.BlockSpec((p.shape[0], tm, FLAT_W), lambda i: (0, i, 0)) for p in parts],
                      [((rows, FLAT_W), F32)], [_rows(tm, FLAT_W)], name)[0]


def _adamw(gparts, w, m, v, name, tm=512):
    rows = w.shape[0]
    tm = _div_tile(rows, tm, SUBLANES)
    ng = len(gparts)

    def fn(*a):
        g = a[0]
        for t in a[1:ng]:
            g = g + t
        w, m, v = a[ng:]
        m = ADAM_B1 * m + (1.0 - ADAM_B1) * g
        v = ADAM_B2 * v + (1.0 - ADAM_B2) * (g * g)
        m_hat = m / (1.0 - ADAM_B1 ** ADAM_STEP)
        v_hat = v / (1.0 - ADAM_B2 ** ADAM_STEP)
        delta = -ADAM_LR * (m_hat / (jnp.sqrt(v_hat) + ADAM_EPS) + ADAM_WD * w)
        return g, delta, m, v
    ins = list(gparts) + [w, m, v]
    return _tile_call(fn, (rows // tm,), ins, [_rows(tm, FLAT_W)] * len(ins), [((rows, FLAT_W), F32)] * 4,
                      [_rows(tm, FLAT_W)] * 4, name)


SHARDED = (("meta_tokens", 1), ("l0_w_in", 0), ("l0_s5_w_glu", 0), ("l0_mla_w_uq", 1), ("l0_mla_w_ukv", 1),
           ("l0_w_out", 0), ("l0_ffn_w_gate", 1), ("l0_ffn_w_up", 1), ("l0_ffn_w_down", 0), ("l1_w_in", 1),
           ("l1_conv_w", 1), ("l1_w_out", 0), ("l1_ffn_w_gate", 1), ("l1_ffn_w_up", 1), ("l1_ffn_w_down", 0))
GATHER_F32 = ("meta_tokens", "l1_conv_w")
REPLICATED = ("l0_s5_log_dt", "l0_s5_a_re", "l0_s5_a_im", "l0_s5_b_re", "l0_s5_b_im", "l0_s5_c_re", "l0_s5_c_im",
              "l0_s5_d", "l0_mla_q_norm", "l0_mla_kv_norm", "l0_ln1_g", "l0_ln1_b", "l0_ln2_g", "l0_ln2_b",
              "l1_conv_b", "l1_dt_bias", "l1_a_log", "l1_d", "l1_norm_g", "l1_ln1_g", "l1_ln1_b", "l1_ln2_g",
              "l1_ln2_b")
WEIGHTS = ("meta_tokens", "l0_w_in", "l0_s5_log_dt", "l0_s5_a_re", "l0_s5_a_im", "l0_s5_b_re", "l0_s5_b_im",
           "l0_s5_c_re", "l0_s5_c_im", "l0_s5_d", "l0_s5_w_glu", "l0_mla_q_norm", "l0_mla_w_uq", "l0_mla_kv_norm",
           "l0_mla_w_ukv", "l0_w_out", "l0_ln1_g", "l0_ln1_b", "l0_ffn_w_gate", "l0_ffn_w_up", "l0_ffn_w_down",
           "l0_ln2_g", "l0_ln2_b", "l1_w_in", "l1_conv_w", "l1_conv_b", "l1_dt_bias", "l1_a_log", "l1_d",
           "l1_norm_g", "l1_w_out", "l1_ln1_g", "l1_ln1_b", "l1_ffn_w_gate", "l1_ffn_w_up", "l1_ffn_w_down",
           "l1_ln2_g", "l1_ln2_b")
GATHER_ROW_UNIT = 32
FLAT_ROW_UNIT = 512


def _split_flat(flat2d, shapes):
    flat = flat2d.reshape(-1)
    out, off = [], 0
    for s in shapes:
        n = math.prod(s)
        out.append(flat[off:off + n].reshape(s))
        off += n
    return out


def _unshard(slabs, axis, shard_shape):
    r, c = shard_shape
    s = slabs.reshape((N_SLOTS, r, c))
    return s.reshape(N_SLOTS * r, c) if axis == 0 else s.transpose(1, 0, 2).reshape(r, N_SLOTS * c)


def _to_slots(full, axis):
    r, c = full.shape
    if axis == 0:
        return full.reshape(N_SLOTS, -1)
    return full.reshape(r, N_SLOTS, c // N_SLOTS).transpose(1, 0, 2).reshape(N_SLOTS, -1)


def _step(cfg, x, loss_target, ws, ms, vs):
    pieces, layout = [], []
    for name, axis in SHARDED:
        blk = ws[name]
        hi = blk.astype(BF16)
        pieces.append(hi)
        layout.append((name, axis, blk.shape))
        if name in GATHER_F32:
            pieces.append((blk - hi.astype(F32)).astype(BF16))
            layout.append((name, axis, blk.shape))
    rows = _flat_rows(sum(p.size for p in pieces), GATHER_ROW_UNIT)
    gathered = _gather_shards(_to_flat(pieces, rows).reshape(2, rows // 2, FLAT_W))
    slabs = gathered.reshape(N_SLOTS, rows * FLAT_W)
    wf, off = {}, 0
    for name, axis, shape in layout:
        n = math.prod(shape)
        full = _unshard(slabs[:, off:off + n], axis, shape)
        off += n
        if name in GATHER_F32:
            wf[name] = wf[name] + full.astype(F32) if name in wf else full.astype(F32)
        else:
            wf[name] = full
    for name in REPLICATED:
        wf[name] = ws[name]
    loss_lanes, grad_x, grads = _local_step(x[0], loss_target[0], _derived_weights(wf, cfg), cfg)
    shard_shapes = [ws[name].shape for name, _ in SHARDED]
    grows = _flat_rows(sum(math.prod(s) for s in shard_shapes), FLAT_ROW_UNIT)
    slots = jnp.concatenate([_to_slots(grads[name], axis) for name, axis in SHARDED], axis=1)
    slots = jnp.pad(slots, ((0, 0), (0, grows * FLAT_W - slots.shape[1]))).reshape(N_SLOTS, grows, FLAT_W)
    others = _exchange_slots(slots.astype(BF16))
    me = 2 * lax.axis_index("x") + lax.axis_index("y")
    mine = lax.dynamic_slice_in_dim(slots, me, 1, axis=0)
    part = _sum_parts([mine, others], "sum_slots")
    part_sib = _exchange_sibling(part)
    flat = lambda d: _to_flat([d[name] for name, _ in SHARDED], grows)
    outs = _adamw([part, part_sib], flat(ws), flat(ms), flat(vs), "adamw_sharded")
    res = {}
    for kind, arr in zip(("grad", "delta", "new_m", "new_v"), outs):
        for (name, _), val in zip(SHARDED, _split_flat(arr, shard_shapes)):
            res[kind + "_" + name] = val
    rep_shapes = [(1, cfg.d)] + [ws[name].shape for name in REPLICATED]
    srows = _flat_rows(sum(math.prod(s) for s in rep_shapes), SUBLANES)
    small = _to_flat([loss_lanes] + [grads[name] for name in REPLICATED], srows)
    total = _sum_parts([_gather_all(small)], "sum_small", tm=srows)
    zero = jnp.zeros((1, cfg.d), F32)
    flat = lambda d: _to_flat([zero] + [d[name] for name in REPLICATED], srows)
    outs = _adamw([total], flat(ws), flat(ms), flat(vs), "adamw_replicated", tm=srows)
    for kind, arr in zip(("grad", "delta", "new_m", "new_v"), outs):
        vals = _split_flat(arr, rep_shapes)
        for name, val in zip(REPLICATED, vals[1:]):
            res[kind + "_" + name] = val
    loss = _small_call(lambda t: jnp.sum(t, axis=1, keepdims=True), [_split_flat(total, rep_shapes)[0]], [(1, 1)],
                       "loss_sum")[0].reshape(())
    ordered = [res[kind + "_" + name] for kind in ("grad", "delta", "new_m", "new_v") for name in WEIGHTS]
    return (loss, grad_x[None]) + tuple(ordered)


def kernel(x, meta_tokens, l0_w_in, l0_s5_log_dt, l0_s5_a_re, l0_s5_a_im, l0_s5_b_re, l0_s5_b_im, l0_s5_c_re,
           l0_s5_c_im, l0_s5_d, l0_s5_w_glu, l0_mla_q_norm, l0_mla_w_uq, l0_mla_kv_norm, l0_mla_w_ukv, l0_w_out,
           l0_ln1_g, l0_ln1_b, l0_ffn_w_gate, l0_ffn_w_up, l0_ffn_w_down, l0_ln2_g, l0_ln2_b, l1_w_in, l1_conv_w,
           l1_conv_b, l1_dt_bias, l1_a_log, l1_d, l1_norm_g, l1_w_out, l1_ln1_g, l1_ln1_b, l1_ffn_w_gate,
           l1_ffn_w_up, l1_ffn_w_down, l1_ln2_g, l1_ln2_b, loss_target, m_meta_tokens, m_l0_w_in, m_l0_s5_log_dt,
           m_l0_s5_a_re, m_l0_s5_a_im, m_l0_s5_b_re, m_l0_s5_b_im, m_l0_s5_c_re, m_l0_s5_c_im, m_l0_s5_d,
           m_l0_s5_w_glu, m_l0_mla_q_norm, m_l0_mla_w_uq, m_l0_mla_kv_norm, m_l0_mla_w_ukv, m_l0_w_out, m_l0_ln1_g,
           m_l0_ln1_b, m_l0_ffn_w_gate, m_l0_ffn_w_up, m_l0_ffn_w_down, m_l0_ln2_g, m_l0_ln2_b, m_l1_w_in,
           m_l1_conv_w, m_l1_conv_b, m_l1_dt_bias, m_l1_a_log, m_l1_d, m_l1_norm_g, m_l1_w_out, m_l1_ln1_g,
           m_l1_ln1_b, m_l1_ffn_w_gate, m_l1_ffn_w_up, m_l1_ffn_w_down, m_l1_ln2_g, m_l1_ln2_b, v_meta_tokens,
           v_l0_w_in, v_l0_s5_log_dt, v_l0_s5_a_re, v_l0_s5_a_im, v_l0_s5_b_re, v_l0_s5_b_im, v_l0_s5_c_re,
           v_l0_s5_c_im, v_l0_s5_d, v_l0_s5_w_glu, v_l0_mla_q_norm, v_l0_mla_w_uq, v_l0_mla_kv_norm,
           v_l0_mla_w_ukv, v_l0_w_out, v_l0_ln1_g, v_l0_ln1_b, v_l0_ffn_w_gate, v_l0_ffn_w_up, v_l0_ffn_w_down,
           v_l0_ln2_g, v_l0_ln2_b, v_l1_w_in, v_l1_conv_w, v_l1_conv_b, v_l1_dt_bias, v_l1_a_log, v_l1_d,
           v_l1_norm_g, v_l1_w_out, v_l1_ln1_g, v_l1_ln1_b, v_l1_ffn_w_gate, v_l1_ffn_w_up, v_l1_ffn_w_down,
           v_l1_ln2_g, v_l1_ln2_b):
    given = dict(locals())
    ws = {name: given[name] for name in WEIGHTS}
    ms = {name: given["m_" + name] for name in WEIGHTS}
    vs = {name: given["v_" + name] for name in WEIGHTS}
    return _step(FULL, x, loss_target, ws, ms, vs)
```

```python
import functools
import math

import numpy as np
import jax
import jax.numpy as jnp
from jax import lax
from jax.experimental import pallas as pl
from jax.experimental.pallas import tpu as pltpu

F32 = jnp.float32
BF16 = jnp.bfloat16
HI = lax.Precision.HIGHEST
MESH = pl.DeviceIdType.MESH

LANES = 128
SUBLANES = 8
VMEM_LIMIT_BYTES = 56 * 1024 * 1024

N_META = 16
CHUNK = 64
DEPTH = 2
DN_ALPHA = (2 * DEPTH) ** 0.25
LN_EPS = 1e-5
RMS_EPS = 1e-6
ROPE_BASE = 10000.0
S5_GROUP = 16
S5_STATE = 64
S5_GPB = 8
MLA_NOPE = 128
MLA_ROPE = 64
MLA_V = 128
MLA_QW = 256
ATT_SHIFT = 48
SSD_HEAD_DIM = 64
SSD_GROUPS = 8
SSD_STATE = 128
SSD_CONV = 4
SSD_BLOCK = 128
ADAM_LR = 0.001
ADAM_B1 = 0.9
ADAM_B2 = 0.999
ADAM_EPS = 1e-08
ADAM_WD = 0.01
ADAM_STEP = 10
NEG = -1e30


class _Cfg:
    def __init__(self, d_model, seq, row_tile, att_tile, scan_tiles, small_row_tile):
        d = d_model
        self.tr = small_row_tile
        self.d = d
        self.seq = seq
        self.n = seq + N_META
        lp = -(-(self.n + ATT_SHIFT) // row_tile) * row_tile
        self.lp = lp
        self.tm = row_tile
        self.tq = att_tile
        self.scan_tiles = scan_tiles
        self.s5_w = d // 2
        self.s5_g = self.s5_w // S5_GROUP
        self.s5_nb = self.s5_g // S5_GPB
        self.s5_c = self.s5_g * S5_STATE
        self.heads = d // 256
        self.q_rank = d // 4
        self.kv_rank = d // 8
        self.l0_in = self.s5_w + self.q_rank + self.kv_rank + MLA_ROPE
        self.l0_mix = self.s5_w + self.heads * MLA_V
        self.ssd_inner = 2 * d
        self.ssd_heads = self.ssd_inner // SSD_HEAD_DIM
        self.hpg = self.ssd_heads // SSD_GROUPS
        self.gw = self.hpg * SSD_HEAD_DIM
        self.gn = SSD_GROUPS * SSD_STATE
        self.conv_dim = self.ssd_inner + 2 * self.gn
        self.l1_in = self.ssd_inner + self.conv_dim + self.ssd_heads
        self.l1_inp = -(-self.l1_in // LANES) * LANES
        self.ffn = -(-(8 * d) // (3 * 256)) * 256
        assert lp % att_tile == 0 and lp % SSD_BLOCK == 0 and lp % (8 * scan_tiles) == 0


FULL = _Cfg(2048, 8192, 640, 640, 4, 160)


def _cparams(n_grid):
    return pltpu.CompilerParams(dimension_semantics=("arbitrary",) * n_grid,
                                vmem_limit_bytes=VMEM_LIMIT_BYTES)


def _div_tile(n, target, unit=LANES):
    if n <= target:
        return n
    best = None
    for t in range(unit, target + 1, unit):
        if n % t == 0:
            best = t
    return n if best is None else best


ANY_SPEC = pl.BlockSpec(memory_space=pl.ANY)


def _tile_call(fn, grid, ins, in_specs, outs, out_specs, name, acc=(), acc_axis=0, fill=None):
    n_in = len(ins)
    n_out = len(outs)
    acc = tuple(acc)
    aliases = {}
    if fill is not None:
        aliases = {n_in: fill[1]}
        ins = list(ins) + [fill[0]]
        in_specs = list(in_specs) + [ANY_SPEC]

    def body(*refs):
        refs = refs[:n_in] + refs[len(ins):]
        vals = fn(*[r[...] for r in refs[:n_in]])
        if not isinstance(vals, (tuple, list)):
            vals = (vals,)
        for k in range(n_out):
            r = refs[n_in + k]
            v = vals[k].astype(r.dtype)
            if k in acc:
                first = pl.program_id(acc_axis) == 0

                @pl.when(first)
                def _(r=r, v=v):
                    r[...] = v

                @pl.when(jnp.logical_not(first))
                def _(r=r, v=v):
                    r[...] += v
            else:
                r[...] = v

    res = pl.pallas_call(
        body, out_shape=[jax.ShapeDtypeStruct(s, d) for s, d in outs], grid=grid,
        in_specs=in_specs, out_specs=out_specs, name=name, compiler_params=_cparams(len(grid)),
        input_output_aliases=aliases,
    )(*ins)
    return res


def _rows(tm, c):
    return pl.BlockSpec((tm, c), lambda i: (i, 0))


def _whole(shape):
    nd = len(shape)
    return pl.BlockSpec(shape, lambda *a: (0,) * nd)


def _rowwise(fn, rows, params, outs, accs, tm, name):
    lp = rows[0].shape[0]
    n_row_out = len(outs)
    res = _tile_call(
        fn, (lp // tm,), list(rows) + list(params),
        [_rows(tm, r.shape[1]) for r in rows] + [_whole(p.shape) for p in params],
        [((lp, c), dt) for c, dt in outs] + [(s, F32) for s in accs],
        [_rows(tm, c) for c, _ in outs] + [_whole(s) for s in accs],
        name, acc=range(n_row_out, n_row_out + len(accs)))
    return res


_DIMS = {"nn": (((1,), (0,)), ((), ())), "nt": (((1,), (1,)), ((), ())), "tn": (((0,), (0,)), ((), ()))}


def _dot(a, b, mode="nn", precision=None):
    return lax.dot_general(a, b, _DIMS[mode], preferred_element_type=F32, precision=precision)


def _bdot(a, b, mode="nn"):
    return _dot(a.astype(BF16), b.astype(BF16), mode)


def _mm(a, b, mode, name, out_dtype=F32, out_slabs=None, b_col0=0, b_cols=None, tm_t=640, tn_t=1536, tk_t=2048):
    slab_b = b.ndim == 3
    if mode == "nn":
        m, k = a.shape
        k2, n, unit_n = (b.shape[1], b.shape[0] * b.shape[2], b.shape[2]) if slab_b else (b.shape[0], b.shape[1], b.shape[1])
        unit_k = k
    elif mode == "nt":
        m, k = a.shape
        n, k2, unit_k = (b.shape[1], b.shape[0] * b.shape[2], b.shape[2]) if slab_b else (b.shape[0], b.shape[1], b.shape[1])
        unit_n = n
    else:
        (k, m), k2 = a.shape, b.shape[0]
        n = b.shape[1] if b_cols is None else b_cols
        unit_n, unit_k = n, k
        tm_t, tk_t = 1024, 1024
    if out_slabs:
        unit_n = n // out_slabs
    assert k == k2, (name, a.shape, b.shape)
    tm, tn, tk = _div_tile(m, tm_t), _div_tile(unit_n, tn_t), _div_tile(unit_k, tk_t)
    nk = k // tk
    nps, kps = unit_n // tn, unit_k // tk
    c0 = b_col0 // tn
    assert b_col0 % tn == 0
    a_spec = {"nn": pl.BlockSpec((tm, tk), lambda i, j, kk: (i, kk)),
              "nt": pl.BlockSpec((tm, tk), lambda i, j, kk: (i, kk)),
              "tn": pl.BlockSpec((tk, tm), lambda i, j, kk: (kk, i))}[mode]
    if slab_b:
        b_spec = {"nn": pl.BlockSpec((None, tk, tn), lambda i, j, kk: (j // nps, kk, j % nps)),
                  "nt": pl.BlockSpec((None, tn, tk), lambda i, j, kk: (kk // kps, j, kk % kps))}[mode]
    else:
        b_spec = {"nn": pl.BlockSpec((tk, tn), lambda i, j, kk: (kk, j)),
                  "nt": pl.BlockSpec((tn, tk), lambda i, j, kk: (j, kk)),
                  "tn": pl.BlockSpec((tk, tn), lambda i, j, kk: (kk, j + c0))}[mode]
    if out_slabs:
        out_shape = jax.ShapeDtypeStruct((out_slabs, m, unit_n), out_dtype)
        out_spec = pl.BlockSpec((None, tm, tn), lambda i, j, kk: (j // nps, i, j % nps))
    else:
        out_shape = jax.ShapeDtypeStruct((m, n), out_dtype)
        out_spec = pl.BlockSpec((tm, tn), lambda i, j, kk: (i, j))

    def body(a_ref, b_ref, o_ref, acc_ref):
        part = _bdot(a_ref[...], b_ref[...], mode)
        if nk == 1:
            o_ref[...] = part.astype(o_ref.dtype)
        else:
            kk = pl.program_id(2)

            @pl.when(kk == 0)
            def _():
                acc_ref[...] = part

            @pl.when(kk > 0)
            def _():
                acc_ref[...] += part

            @pl.when(kk == nk - 1)
            def _():
                o_ref[...] = acc_ref[...].astype(o_ref.dtype)

    return pl.pallas_call(
        body, out_shape=out_shape, grid=(m // tm, n // tn, nk),
        in_specs=[a_spec, b_spec], out_specs=out_spec,
        scratch_shapes=[pltpu.VMEM((tm, tn) if nk > 1 else (SUBLANES, LANES), F32)],
        name=name, compiler_params=_cparams(3))(a, b)


def _layer_norm(r, g, b):
    mu = jnp.mean(r, axis=-1, keepdims=True)
    xc = r - mu
    var = jnp.mean(xc * xc, axis=-1, keepdims=True)
    return xc * lax.rsqrt(var + LN_EPS) * g + b


def _rms(x, g):
    return x * lax.rsqrt(jnp.mean(x * x, axis=-1, keepdims=True) + RMS_EPS) * g


def _sigmoid(x):
    return 1.0 / (1.0 + jnp.exp(-x))


def _silu(x):
    return x * _sigmoid(x)


def _gelu(x):
    return 0.5 * x * (1.0 + jnp.tanh(0.7978845608028654 * (x + 0.044715 * x * x * x)))


def _softplus(x):
    return jnp.maximum(x, 0.0) + jnp.log(1.0 + jnp.exp(-jnp.abs(x)))


def _ln_fwd(h, mo, g, b, cfg, name):
    def fn(h, mo, g, b):
        return _layer_norm(DN_ALPHA * h + mo, g, b)
    return _rowwise(fn, [h, mo], [g, b], [(cfg.d, F32)], [], cfg.tr, name)[0]


def _ln_bwd(h, mo, g, b, douts, scales, cfg, name):
    def fn(h, mo, *rest):
        ds, (g, b) = rest[:-2], rest[-2:]
        dy = ds[0] * scales[0]
        for t, s in zip(ds[1:], scales[1:]):
            dy = dy + t * s
        _, vjp = jax.vjp(_layer_norm, DN_ALPHA * h + mo, g, b)
        dr, dg, db = vjp(dy)
        return dr, dg, db
    d = cfg.d
    return _rowwise(fn, [h, mo] + list(douts), [g, b], [(d, F32)], [(1, d), (1, d)], cfg.tr, name)


def _ffn_act(gu, cfg, name):
    f = cfg.ffn
    cb = _div_tile(f, 1536)
    nf = f // cb
    lp = gu.shape[0]
    tm = cfg.tr

    def fn(gate, up):
        return _silu(gate) * up
    return _tile_call(fn, (lp // tm, nf), [gu, gu],
                      [pl.BlockSpec((tm, cb), lambda i, j: (i, j)),
                       pl.BlockSpec((tm, cb), lambda i, j: (i, j + nf))],
                      [((lp, f), F32)], [pl.BlockSpec((tm, cb), lambda i, j: (i, j))], name)[0]


def _ffn_act_bwd(gu, dact, cfg, name):
    f = cfg.ffn
    cb = _div_tile(f, 1536)
    nf = f // cb
    lp = gu.shape[0]
    tm = cfg.tr

    def fn(gate, up, da):
        sg = _sigmoid(gate)
        dgate = da * up * sg * (1.0 + gate * (1.0 - sg))
        dup = da * gate * sg
        return jnp.where(pl.program_id(1) < nf, dgate, dup)
    return _tile_call(fn, (lp // tm, 2 * nf), [gu, gu, dact],
                      [pl.BlockSpec((tm, cb), lambda i, j: (i, j % nf)),
                       pl.BlockSpec((tm, cb), lambda i, j: (i, j % nf + nf)),
                       pl.BlockSpec((tm, cb), lambda i, j: (i, j % nf))],
                      [((lp, 2 * f), F32)], [pl.BlockSpec((tm, cb), lambda i, j: (i, j))], name)[0]


def _ffn_fwd(h, w_gu, w_down, cfg, tag):
    gu = _mm(h, w_gu, "nn", tag + "_gu")
    act = _ffn_act(gu, cfg, tag + "_act")
    fo = _mm(act, w_down, "nn", tag + "_down")
    return fo, (gu, act)


def _ffn_bwd(h, saved, dfo, w_gu, w_down, cfg, tag):
    gu, act = saved
    dact = _mm(dfo, w_down, "nt", tag + "_dact")
    d_wdown = _mm(act, dfo, "tn", tag + "_dwdown", out_dtype=BF16)
    dgu = _ffn_act_bwd(gu, dact, cfg, tag + "_dgu")
    dh = _mm(dgu, w_gu, "nt", tag + "_dh")
    d_wgu = _mm(h, dgu, "tn", tag + "_dwgu", out_dtype=BF16, out_slabs=w_gu.shape[0])
    return dh, d_wgu, d_wdown


def _small_call(fn, ins, outs, name):
    return _tile_call(fn, (1,), ins, [_whole(x.shape) for x in ins], [(s, F32) for s in outs],
                      [_whole(s) for s in outs], name)


def _perm(x):
    lp, c = x.shape
    return x.reshape(SUBLANES, lp // SUBLANES, c).transpose(1, 0, 2).reshape(lp, c)


def _unperm(x):
    lp, c = x.shape
    return x.reshape(lp // SUBLANES, SUBLANES, c).transpose(1, 0, 2).reshape(lp, c)


def _s5_disc(log_dt, a_re, a_im):
    dt = jnp.exp(log_dt)
    mag = jnp.exp(dt * a_re)
    ab_re = mag * jnp.cos(dt * a_im)
    ab_im = mag * jnp.sin(dt * a_im)
    den = a_re * a_re + a_im * a_im
    nr = ab_re - 1.0
    f_re = (nr * a_re + ab_im * a_im) / den
    f_im = (ab_im * a_re - nr * a_im) / den
    return ab_re, ab_im, f_re, f_im


def _s5_bbar(f_re, f_im, b_re, b_im):
    return f_re * b_re - f_im * b_im, f_re * b_im + f_im * b_re


def _bd_from(w, cfg):
    g, p, j = w.shape
    w4 = w.reshape(cfg.s5_nb, S5_GPB, p, j)
    eye = jnp.eye(S5_GPB, dtype=w.dtype)
    return jnp.einsum("bgpj,gh->bgjhp", w4, eye).reshape(cfg.s5_nb, S5_GPB * j, S5_GPB * p)


def _bd_to(blocks, cfg, p, j):
    b5 = blocks.reshape(cfg.s5_nb, S5_GPB, j, S5_GPB, p)
    eye = jnp.eye(S5_GPB, dtype=blocks.dtype)
    return jnp.einsum("bgjhp,gh->bgpj", b5, eye).reshape(cfg.s5_g, p, j)


def _bd_split(x, w1, w2, cfg, name):
    nb, ci, co = w1.shape
    lp, tm = x.shape[0], cfg.tm

    def fn(x, w1, w2):
        xb = x.astype(BF16)
        return _dot(xb, w1[0].astype(BF16)), _dot(xb, w2[0].astype(BF16))
    wspec = pl.BlockSpec((1, ci, co), lambda i, j: (j, 0, 0))
    ospec = pl.BlockSpec((tm, co), lambda i, j: (i, j))
    return _tile_call(fn, (lp // tm, nb), [x, w1, w2],
                      [pl.BlockSpec((tm, ci), lambda i, j: (i, j)), wspec, wspec],
                      [((lp, nb * co), F32)] * 2, [ospec, ospec], name)


def _bd_join(x1, x2, w1, w2, extra, scale, cfg, name):
    nb, ci, co = w1.shape
    lp, tm = x1.shape[0], cfg.tm

    def fn(x1, x2, w1, w2, e, s):
        return _bdot(x1, w1[0]) + _bdot(x2, w2[0]) + e * s
    xspec = pl.BlockSpec((tm, ci), lambda i, j: (i, j))
    wspec = pl.BlockSpec((1, ci, co), lambda i, j: (j, 0, 0))
    ospec = pl.BlockSpec((tm, co), lambda i, j: (i, j))
    return _tile_call(fn, (lp // tm, nb), [x1, x2, w1, w2, extra, scale],
                      [xspec, xspec, wspec, wspec, ospec, pl.BlockSpec((1, co), lambda i, j: (0, j))],
                      [((lp, nb * co), F32)], [ospec], name)[0]


def _bd_tn(a, b, nb, cfg, name):
    lp, tk = a.shape[0], cfg.tm
    ca, cb = a.shape[1] // nb, b.shape[1] // nb

    def fn(a, b):
        return _bdot(a, b, "tn")[None]
    return _tile_call(fn, (nb, lp // tk), [a, b],
                      [pl.BlockSpec((tk, ca), lambda j, k: (k, j)), pl.BlockSpec((tk, cb), lambda j, k: (k, j))],
                      [((nb, ca, cb), F32)], [pl.BlockSpec((1, ca, cb), lambda j, k: (j, 0, 0))],
                      name, acc=(0,), acc_axis=1)[0]


def _s5_scan(bu_re, bu_im, ab_re, ab_im, cfg, reverse, tag):
    lp, c = bu_re.shape
    nt = cfg.scan_tiles
    rows = lp // nt
    steps = rows // SUBLANES
    cb = _div_tile(c, 512)
    tmap = (lambda j, t: (nt - 1 - t, j)) if reverse else (lambda j, t: (t, j))
    row_spec = pl.BlockSpec((rows, cb), tmap)
    par_spec = pl.BlockSpec((1, cb), lambda j, t: (0, j))
    st_spec = pl.BlockSpec((SUBLANES, cb), lambda j, t: (0, j))
    grid = (c // cb, nt)
    full = jax.ShapeDtypeStruct((lp, c), F32)
    small = jax.ShapeDtypeStruct((SUBLANES, c), F32)

    def offset(k):
        kk = steps - 1 - k if reverse else k
        return pl.multiple_of(kk * SUBLANES, SUBLANES)

    def local_body(bre, bim, ar_ref, ai_ref, sre, sim, fre, fim, pre, pim, st):
        t = pl.program_id(1)

        @pl.when(t == 0)
        def _():
            zero = jnp.zeros((SUBLANES, cb), F32)
            st[0] = zero
            st[1] = zero
            st[2] = zero + 1.0
            st[3] = zero
        ar = jnp.broadcast_to(ar_ref[...], (SUBLANES, cb))
        ai = jnp.broadcast_to(ai_ref[...], (SUBLANES, cb))

        def step(k, carry):
            s_r, s_i, p_r, p_i = carry
            off = offset(k)
            n_r = ar * s_r - ai * s_i + bre[pl.ds(off, SUBLANES), :]
            n_i = ar * s_i + ai * s_r + bim[pl.ds(off, SUBLANES), :]
            sre[pl.ds(off, SUBLANES), :] = n_r
            sim[pl.ds(off, SUBLANES), :] = n_i
            return n_r, n_i, ar * p_r - ai * p_i, ar * p_i + ai * p_r
        s_r, s_i, p_r, p_i = lax.fori_loop(0, steps, step, (st[0], st[1], st[2], st[3]))
        st[0] = s_r
        st[1] = s_i
        st[2] = p_r
        st[3] = p_i

        @pl.when(t == nt - 1)
        def _():
            fre[...] = s_r
            fim[...] = s_i
            pre[...] = p_r
            pim[...] = p_i

    loc_re, loc_im, f_re, f_im, pn_re, pn_im = pl.pallas_call(
        local_body, out_shape=[full, full, small, small, small, small], grid=grid,
        in_specs=[row_spec, row_spec, par_spec, par_spec],
        out_specs=[row_spec, row_spec, st_spec, st_spec, st_spec, st_spec],
        scratch_shapes=[pltpu.VMEM((4, SUBLANES, cb), F32)],
        name=tag + "_local", compiler_params=_cparams(2))(bu_re, bu_im, ab_re, ab_im)

    def fix_body(lre, lim, fre, fim, pre, pim, ar_ref, ai_ref, sre, sim, st):
        t = pl.program_id(1)
        ar = jnp.broadcast_to(ar_ref[...], (SUBLANES, cb))
        ai = jnp.broadcast_to(ai_ref[...], (SUBLANES, cb))

        @pl.when(t == 0)
        def _():
            f_r, f_i = fre[...], fim[...]
            n_r, n_i = pre[0:1, :], pim[0:1, :]
            row = lax.broadcasted_iota(jnp.int32, (SUBLANES, cb), 0)
            c_r = jnp.zeros((1, cb), F32)
            c_i = jnp.zeros((1, cb), F32)
            car_r = jnp.zeros((SUBLANES, cb), F32)
            car_i = jnp.zeros((SUBLANES, cb), F32)
            order = range(SUBLANES - 2, -1, -1) if reverse else range(1, SUBLANES)
            for i in order:
                src = i + 1 if reverse else i - 1
                c_r, c_i = (n_r * c_r - n_i * c_i + f_r[src:src + 1, :],
                            n_r * c_i + n_i * c_r + f_i[src:src + 1, :])
                car_r = jnp.where(row == i, c_r, car_r)
                car_i = jnp.where(row == i, c_i, car_i)
            st[0] = car_r
            st[1] = car_i
            st[2] = ar
            st[3] = ai
        car_r = st[0]
        car_i = st[1]

        def step(k, carry):
            p_r, p_i = carry
            off = offset(k)
            sre[pl.ds(off, SUBLANES), :] = lre[pl.ds(off, SUBLANES), :] + p_r * car_r - p_i * car_i
            sim[pl.ds(off, SUBLANES), :] = lim[pl.ds(off, SUBLANES), :] + p_r * car_i + p_i * car_r
            return ar * p_r - ai * p_i, ar * p_i + ai * p_r
        p_r, p_i = lax.fori_loop(0, steps, step, (st[2], st[3]))
        st[2] = p_r
        st[3] = p_i

    return pl.pallas_call(
        fix_body, out_shape=[full, full], grid=grid,
        in_specs=[row_spec, row_spec, st_spec, st_spec, st_spec, st_spec, par_spec, par_spec],
        out_specs=[row_spec, row_spec], scratch_shapes=[pltpu.VMEM((4, SUBLANES, cb), F32)],
        name=tag + "_fix", compiler_params=_cparams(2))(loc_re, loc_im, f_re, f_im, pn_re, pn_im, ab_re, ab_im)


def _swap(x):
    return jnp.swapaxes(x, -1, -2)


def _s5_prep(w, cfg):
    g, p, j = cfg.s5_g, S5_STATE, S5_GROUP
    gp = g * p
    log_dt = w["l0_s5_log_dt"].reshape(g, 1)
    ab_re, ab_im, f_re, f_im = _small_call(_s5_disc, [log_dt, w["l0_s5_a_re"], w["l0_s5_a_im"]],
                                           [(g, p)] * 4, "s5_disc")
    b_re2 = w["l0_s5_b_re"].transpose(2, 0, 1).reshape(j, gp)
    b_im2 = w["l0_s5_b_im"].transpose(2, 0, 1).reshape(j, gp)
    f_re1, f_im1 = f_re.reshape(1, gp), f_im.reshape(1, gp)
    bb_re2, bb_im2 = _small_call(_s5_bbar, [f_re1, f_im1, b_re2, b_im2], [(j, gp)] * 2, "s5_bbar")
    bb_re = _bd_from(bb_re2.reshape(j, g, p).transpose(1, 2, 0), cfg).astype(BF16)
    bb_im = _bd_from(bb_im2.reshape(j, g, p).transpose(1, 2, 0), cfg).astype(BF16)
    c_re_t = _bd_from(w["l0_s5_c_re"].transpose(0, 2, 1), cfg).astype(BF16)
    c_imn_t = _bd_from(-w["l0_s5_c_im"].transpose(0, 2, 1), cfg).astype(BF16)
    return dict(log_dt=log_dt, f_re1=f_re1, f_im1=f_im1, b_re2=b_re2, b_im2=b_im2,
                ab_re=ab_re.reshape(1, gp), ab_im=ab_im.reshape(1, gp),
                bb_re=bb_re, bb_im=bb_im, bb_re_t=_swap(bb_re), bb_im_t=_swap(bb_im),
                c_re=_swap(c_re_t), c_imn=_swap(c_imn_t), c_re_t=c_re_t, c_imn_t=c_imn_t,
                d=w["l0_s5_d"].reshape(1, cfg.s5_w))


def _s5_fwd(u, prm, w_glu, cfg):
    tm = cfg.tr
    up = _perm(u)
    bu_re, bu_im = _bd_split(up, prm["bb_re"], prm["bb_im"], cfg, "s5_bu")
    s_re, s_im = _s5_scan(bu_re, bu_im, prm["ab_re"], prm["ab_im"], cfg, False, "s5_scan")
    y = _bd_join(s_re, s_im, prm["c_re"], prm["c_imn"], up, prm["d"], cfg, "s5_y")
    g = _rowwise(_gelu, [y], [], [(cfg.s5_w, F32)], [], tm, "s5_gelu")[0]
    z = _mm(g, w_glu, "nn", "s5_glu_mm")
    a_out = _rowwise(lambda g, z: g * _sigmoid(z), [g, z], [], [(cfg.s5_w, F32)], [], tm, "s5_glu")[0]
    return _unperm(a_out), (up, s_re, s_im, y, g, z)


def _s5_bwd(d_a_out, saved, prm, w, w_glu, cfg):
    up, s_re, s_im, y, g, z = saved
    tm, sw, nb = cfg.tr, cfg.s5_w, cfg.s5_nb
    gs, p, j = cfg.s5_g, S5_STATE, S5_GROUP
    gp = gs * p
    dap = _perm(d_a_out)

    def glu_bwd(da, g, z):
        sg = _sigmoid(z)
        return da * sg, da * g * sg * (1.0 - sg)
    dg1, dz = _rowwise(glu_bwd, [dap, g, z], [], [(sw, F32)] * 2, [], tm, "s5_glu_bwd")
    d_wglu = _mm(g, dz, "tn", "s5_dwglu", out_dtype=BF16)
    dg2 = _mm(dz, w_glu, "nt", "s5_dg2")

    def gelu_bwd(dg1, dg2, y, up, d):
        _, vjp = jax.vjp(_gelu, y)
        dy = vjp(dg1 + dg2)[0]
        return dy, dy * d, jnp.sum(dy * up, axis=0, keepdims=True)
    dy, dup_direct, dd = _rowwise(gelu_bwd, [dg1, dg2, y, up], [prm["d"]], [(sw, F32)] * 2, [(1, sw)], tm,
                                  "s5_gelu_bwd")
    ds_re, ds_im = _bd_split(dy, prm["c_re_t"], prm["c_imn_t"], cfg, "s5_ds")
    dc_re_t = _bd_tn(dy, s_re, nb, cfg, "s5_dcre")
    dc_imn_t = _bd_tn(dy, s_im, nb, cfg, "s5_dcim")
    g_re, g_im = _s5_scan(ds_re, ds_im, prm["ab_re"], -prm["ab_im"], cfg, True, "s5_adj")

    cb = _div_tile(gp, 512)
    per = tm // SUBLANES
    spec = pl.BlockSpec((tm, cb), lambda jj, i: (i, jj))
    before = pl.BlockSpec((SUBLANES, cb), lambda jj, i: (jnp.maximum(i * per - 1, 0), jj))
    final = pl.BlockSpec((SUBLANES, cb), lambda jj, i: (cfg.lp // SUBLANES - 1, jj))
    aspec = pl.BlockSpec((1, cb), lambda jj, i: (0, jj))

    def dab(g_r, g_i, s_r, s_i, h_r, h_i, l_r, l_i):
        first = pl.program_id(1) == 0
        row8 = lax.broadcasted_iota(jnp.int32, (SUBLANES, cb), 0)

        def prev(s, h, l):
            wrap = jnp.where(row8 == 0, 0.0, pltpu.roll(l, 1, axis=0))
            return jnp.concatenate([jnp.where(first, wrap, h), s[:tm - SUBLANES]], axis=0)
        p_r, p_i = prev(s_r, h_r, l_r), prev(s_i, h_i, l_i)
        return (jnp.sum(g_r * p_r + g_i * p_i, axis=0, keepdims=True),
                jnp.sum(g_i * p_r - g_r * p_i, axis=0, keepdims=True))
    dab_re, dab_im = _tile_call(dab, (gp // cb, cfg.lp // tm), [g_re, g_im, s_re, s_im, s_re, s_im, s_re, s_im],
                                [spec] * 4 + [before] * 2 + [final] * 2,
                                [((1, gp), F32)] * 2, [aspec] * 2, "s5_dab", acc=(0, 1), acc_axis=1)
    no_scale = jnp.ones((1, sw), F32)
    dup = _bd_join(g_re, g_im, prm["bb_re_t"], prm["bb_im_t"], dup_direct, no_scale, cfg, "s5_dup")
    dbb_re_blk = _bd_tn(up, g_re, nb, cfg, "s5_dbbre")
    dbb_im_blk = _bd_tn(up, g_im, nb, cfg, "s5_dbbim")

    def to2(blk):
        return _bd_to(blk, cfg, p, j).transpose(2, 0, 1).reshape(j, gp)

    def bbar_bwd(f_re, f_im, b_re, b_im, dr, di):
        _, vjp = jax.vjp(_s5_bbar, f_re, f_im, b_re, b_im)
        return vjp((dr, di))
    df_re, df_im, db_re2, db_im2 = _small_call(
        bbar_bwd, [prm["f_re1"], prm["f_im1"], prm["b_re2"], prm["b_im2"], to2(dbb_re_blk), to2(dbb_im_blk)],
        [(1, gp), (1, gp), (j, gp), (j, gp)], "s5_bbar_bwd")

    def disc_bwd(log_dt, a_re, a_im, d1, d2, d3, d4):
        _, vjp = jax.vjp(_s5_disc, log_dt, a_re, a_im)
        return vjp((d1, d2, d3, d4))
    dlog_dt, da_re, da_im = _small_call(
        disc_bwd, [prm["log_dt"], w["l0_s5_a_re"], w["l0_s5_a_im"], dab_re.reshape(gs, p), dab_im.reshape(gs, p),
                   df_re.reshape(gs, p), df_im.reshape(gs, p)], [(gs, 1), (gs, p), (gs, p)], "s5_disc_bwd")
    grads = {
        "l0_s5_log_dt": dlog_dt.reshape(gs), "l0_s5_a_re": da_re, "l0_s5_a_im": da_im,
        "l0_s5_b_re": db_re2.reshape(j, gs, p).transpose(1, 2, 0),
        "l0_s5_b_im": db_im2.reshape(j, gs, p).transpose(1, 2, 0),
        "l0_s5_c_re": _bd_to(dc_re_t, cfg, p, j).transpose(0, 2, 1),
        "l0_s5_c_im": -_bd_to(dc_imn_t, cfg, p, j).transpose(0, 2, 1),
        "l0_s5_d": dd.reshape(sw), "l0_s5_w_glu": d_wglu,
    }
    return _unperm(dup), grads


def _shift(x):
    return jnp.concatenate([jnp.zeros((ATT_SHIFT, x.shape[1]), x.dtype), x[:-ATT_SHIFT]], axis=0)


def _unshift(x):
    return jnp.concatenate([x[ATT_SHIFT:], jnp.zeros((ATT_SHIFT, x.shape[1]), x.dtype)], axis=0)


def _rope_tables(cfg):
    pos = (jnp.arange(cfg.lp) - ATT_SHIFT).astype(F32)
    inv = ROPE_BASE ** (-jnp.arange(0, MLA_ROPE, 2, dtype=F32) / MLA_ROPE)
    ang = pos[:, None] * inv[None, :]
    cos, sin = jnp.cos(ang), jnp.sin(ang)
    z = jnp.zeros((cfg.lp, LANES - MLA_ROPE), F32)
    return jnp.concatenate([cos, cos, z], axis=1), jnp.concatenate([-sin, sin, z], axis=1)


def _swap_halves(x):
    half = MLA_ROPE // 2
    lane = lax.broadcasted_iota(jnp.int32, x.shape, 1)
    left = pltpu.roll(x, LANES - half, axis=1)
    right = pltpu.roll(x, half, axis=1)
    return jnp.where(lane < half, left, jnp.where(lane < 2 * half, right, 0.0))


def _rope(x, cosp, sinp):
    return x * cosp + _swap_halves(x) * sinp


def _rope_t(dy, cosp, sinp):
    return dy * cosp + _swap_halves(dy * sinp)


def _visible(i, j, t):
    row = i * t + lax.broadcasted_iota(jnp.int32, (t, t), 0)
    col = j * t + lax.broadcasted_iota(jnp.int32, (t, t), 1)
    return jnp.logical_and(col // CHUNK <= row // CHUNK, col >= ATT_SHIFT)


def _flash_fwd(q, kv, kr, cfg):
    lp, t, nh = cfg.lp, cfg.tq, cfg.heads
    n = lp // t
    scale = (MLA_NOPE + MLA_ROPE) ** -0.5

    def body(q_ref, kv_ref, kr_ref, o_ref, lse_ref, m_s, l_s, acc_s):
        i, j = pl.program_id(1), pl.program_id(2)

        @pl.when(j == 0)
        def _():
            m_s[...] = jnp.full((t, 1), NEG, F32)
            l_s[...] = jnp.zeros((t, 1), F32)
            acc_s[...] = jnp.zeros((t, MLA_V), F32)

        @pl.when(j <= i)
        def _():
            s = (_dot(q_ref[:, :MLA_NOPE], kv_ref[:, :MLA_NOPE], "nt")
                 + _dot(q_ref[:, MLA_NOPE:], kr_ref[...], "nt")) * scale
            s = jnp.where(_visible(i, j, t), s, NEG)
            m_old = m_s[...]
            m_new = jnp.maximum(m_old, jnp.max(s, axis=1, keepdims=True))
            alpha = jnp.exp(m_old - m_new)
            p = jnp.exp(s - m_new)
            l_s[...] = alpha * l_s[...] + jnp.sum(p, axis=1, keepdims=True)
            acc_s[...] = alpha * acc_s[...] + _dot(p.astype(BF16), kv_ref[:, MLA_NOPE:])
            m_s[...] = m_new

        @pl.when(j == i)
        def _():
            o_ref[...] = acc_s[...] / l_s[...]
            lse_ref[...] = jnp.broadcast_to(m_s[...] + jnp.log(l_s[...]), (t, MLA_V))

    return pl.pallas_call(
        body, out_shape=[jax.ShapeDtypeStruct((lp, nh * MLA_V), F32)] * 2, grid=(nh, n, n),
        in_specs=[pl.BlockSpec((t, MLA_QW), lambda h, i, j: (i, h)),
                  pl.BlockSpec((t, MLA_QW), lambda h, i, j: (jnp.minimum(i, j), h)),
                  pl.BlockSpec((t, LANES), lambda h, i, j: (jnp.minimum(i, j), 0))],
        out_specs=[pl.BlockSpec((t, MLA_V), lambda h, i, j: (i, h))] * 2,
        scratch_shapes=[pltpu.VMEM((t, 1), F32), pltpu.VMEM((t, 1), F32), pltpu.VMEM((t, MLA_V), F32)],
        name="mla_flash_fwd", compiler_params=_cparams(3))(q, kv, kr)


def _flash_bwd(q, kv, kr, o, lse, do, cfg):
    lp, t, nh = cfg.lp, cfg.tq, cfg.heads
    n = lp // t
    scale = (MLA_NOPE + MLA_ROPE) ** -0.5

    def body(q_ref, kv_ref, kr_ref, o_ref, lse_ref, do_ref, dq_ref, dkv_ref, dkr_ref, dkv_s, dkr_s):
        j, i = pl.program_id(1), pl.program_id(2)

        @pl.when(jnp.logical_and(j == 0, i == 0))
        def _():
            dq_ref[...] = jnp.zeros((lp, MLA_QW), F32)

        @pl.when(i == j)
        def _():
            dkv_s[...] = jnp.zeros((t, MLA_QW), F32)
            dkr_s[...] = jnp.zeros((t, LANES), F32)

        @pl.when(i >= j)
        def _():
            qn, qr = q_ref[:, :MLA_NOPE], q_ref[:, MLA_NOPE:]
            kn, v = kv_ref[:, :MLA_NOPE], kv_ref[:, MLA_NOPE:]
            krv = kr_ref[...]
            s = (_dot(qn, kn, "nt") + _dot(qr, krv, "nt")) * scale
            p = jnp.where(_visible(i, j, t), jnp.exp(s - lse_ref[:, :1]), 0.0)
            dov = do_ref[...]
            dob = dov.astype(BF16)
            dp = _dot(dob, v, "nt")
            delta = jnp.sum(dov * o_ref[...], axis=1, keepdims=True)
            ds = (p * (dp - delta) * scale).astype(BF16)
            dkv_s[:, MLA_NOPE:] += _dot(p.astype(BF16), dob, "tn")
            dkv_s[:, :MLA_NOPE] += _dot(ds, qn, "tn")
            dkr_s[...] += _dot(ds, qr, "tn")
            off = pl.multiple_of(i * t, t)
            dq_ref[pl.ds(off, t), :MLA_NOPE] += _dot(ds, kn)
            dq_ref[pl.ds(off, t), MLA_NOPE:] += _dot(ds, krv)

        @pl.when(i == n - 1)
        def _():
            dkv_ref[...] = dkv_s[...]
            dkr_ref[0] = dkr_s[...]

    qspec = pl.BlockSpec((t, MLA_QW), lambda h, j, i: (jnp.maximum(i, j), h))
    ospec = pl.BlockSpec((t, MLA_V), lambda h, j, i: (jnp.maximum(i, j), h))
    return pl.pallas_call(
        body, out_shape=[jax.ShapeDtypeStruct((lp, nh * MLA_QW), F32), jax.ShapeDtypeStruct((lp, nh * MLA_QW), F32),
                         jax.ShapeDtypeStruct((nh, lp, LANES), F32)], grid=(nh, n, n),
        in_specs=[qspec, pl.BlockSpec((t, MLA_QW), lambda h, j, i: (j, h)),
                  pl.BlockSpec((t, LANES), lambda h, j, i: (j, 0)), ospec, ospec, ospec],
        out_specs=[pl.BlockSpec((lp, MLA_QW), lambda h, j, i: (0, h)),
                   pl.BlockSpec((t, MLA_QW), lambda h, j, i: (j, h)),
                   pl.BlockSpec((1, t, LANES), lambda h, j, i: (h, j, 0))],
        scratch_shapes=[pltpu.VMEM((t, MLA_QW), F32), pltpu.VMEM((t, LANES), F32)],
        name="mla_flash_bwd", compiler_params=_cparams(3))(q, kv, kr, o, lse, do)


def _pad_heads(w, nh, width):
    r = w.shape[0]
    w3 = w.reshape(r, nh, width)
    return jnp.pad(w3, ((0, 0), (0, 0), (0, MLA_QW - width))).reshape(r, nh * MLA_QW)


def _mla_fwd(q_lat, kv_lat, k_rope_raw, wq, w_uq_p, w_ukv, cfg):
    tm, nh = cfg.tr, cfg.heads
    ql, kl = _shift(q_lat), _shift(kv_lat)
    kr_raw = jnp.pad(_shift(k_rope_raw), ((0, 0), (0, LANES - MLA_ROPE)))
    cosp, sinp = _rope_tables(cfg)
    qg, kg = wq["l0_mla_q_norm"].reshape(1, -1), wq["l0_mla_kv_norm"].reshape(1, -1)
    qn, kvn = _rowwise(lambda a, b, g1, g2: (_rms(a, g1), _rms(b, g2)), [ql, kl], [qg, kg],
                       [(cfg.q_rank, F32), (cfg.kv_rank, F32)], [], tm, "mla_norm")
    q0 = _mm(qn, w_uq_p, "nn", "mla_q")
    kv = _mm(kvn, w_ukv, "nn", "mla_kv", out_dtype=BF16)

    def rope_fn(q0, kr, cosp, sinp):
        parts = []
        for h in range(nh):
            parts.append(q0[:, h * MLA_QW:h * MLA_QW + MLA_NOPE])
            parts.append(_rope(q0[:, h * MLA_QW + MLA_NOPE:(h + 1) * MLA_QW], cosp, sinp))
        return jnp.concatenate(parts, axis=1), _rope(kr, cosp, sinp)
    q, kr = _rowwise(rope_fn, [q0, kr_raw, cosp, sinp], [], [(nh * MLA_QW, BF16), (LANES, BF16)], [], tm,
                     "mla_rope")
    o, lse = _flash_fwd(q, kv, kr, cfg)
    return _unshift(o), (ql, kl, qn, kvn, q, kv, kr, o, lse, cosp, sinp)


def _mla_bwd(d_b_out, saved, wq, w_uq_p, w_ukv, cfg):
    ql, kl, qn, kvn, q, kv, kr, o, lse, cosp, sinp = saved
    tm, nh, lp = cfg.tr, cfg.heads, cfg.lp
    dq, dkv, dkr_h = _flash_bwd(q, kv, kr, o, lse, _shift(d_b_out), cfg)

    def rope_bwd(dq, dkr_h, cosp, sinp):
        parts = []
        for h in range(nh):
            parts.append(dq[:, h * MLA_QW:h * MLA_QW + MLA_NOPE])
            parts.append(_rope_t(dq[:, h * MLA_QW + MLA_NOPE:(h + 1) * MLA_QW], cosp, sinp))
        dkr = dkr_h[0]
        for h in range(1, nh):
            dkr = dkr + dkr_h[h]
        return jnp.concatenate(parts, axis=1), _rope_t(dkr, cosp, sinp)
    dq0, dkr_raw = _tile_call(
        rope_bwd, (lp // tm,), [dq, dkr_h, cosp, sinp],
        [_rows(tm, nh * MLA_QW), pl.BlockSpec((nh, tm, LANES), lambda i: (0, i, 0)), _rows(tm, LANES),
         _rows(tm, LANES)],
        [((lp, nh * MLA_QW), F32), ((lp, LANES), F32)], [_rows(tm, nh * MLA_QW), _rows(tm, LANES)], "mla_rope_bwd")
    d_wuq_p = _mm(qn, dq0, "tn", "mla_dwuq")
    dqn = _mm(dq0, w_uq_p, "nt", "mla_dqn")
    d_wukv = _mm(kvn, dkv, "tn", "mla_dwukv")
    dkvn = _mm(dkv, w_ukv, "nt", "mla_dkvn")
    qg, kg = wq["l0_mla_q_norm"].reshape(1, -1), wq["l0_mla_kv_norm"].reshape(1, -1)

    def norm_bwd(ql, kl, dqn, dkvn, g1, g2):
        _, vjp1 = jax.vjp(_rms, ql, g1)
        _, vjp2 = jax.vjp(_rms, kl, g2)
        dql, dg1 = vjp1(dqn)
        dkl, dg2 = vjp2(dkvn)
        return dql, dkl, dg1, dg2
    dql, dkl, dg1, dg2 = _rowwise(norm_bwd, [ql, kl, dqn, dkvn], [qg, kg],
                                  [(cfg.q_rank, F32), (cfg.kv_rank, F32)], [(1, cfg.q_rank), (1, cfg.kv_rank)], tm,
                                  "mla_norm_bwd")
    width = MLA_NOPE + MLA_ROPE
    d_wuq = d_wuq_p.reshape(cfg.q_rank, nh, MLA_QW)[:, :, :width].reshape(cfg.q_rank, nh * width)
    grads = {"l0_mla_q_norm": dg1.reshape(-1), "l0_mla_kv_norm": dg2.reshape(-1), "l0_mla_w_uq": d_wuq,
             "l0_mla_w_ukv": d_wukv}
    return _unshift(dql), _unshift(dkl), _unshift(dkr_raw[:, :MLA_ROPE]), grads


def _conv_taps(x, halo, first):
    halo = jnp.where(first, 0.0, halo)
    row8 = lax.broadcasted_iota(jnp.int32, halo.shape, 0)
    taps = []
    for s in range(SSD_CONV - 1, 0, -1):
        r = pltpu.roll(x, s, axis=0)
        top = jnp.where(row8 < s, pltpu.roll(halo, s, axis=0), r[:SUBLANES])
        taps.append(jnp.concatenate([top, r[SUBLANES:]], axis=0))
    taps.append(x)
    return taps


def _conv_specs(cfg, lp):
    tm = cfg.tr
    cb = _div_tile(math.gcd(cfg.ssd_inner, cfg.gn), 1024)
    off = cfg.ssd_inner // cb
    per = tm // SUBLANES
    nrow = lp // tm
    main = pl.BlockSpec((tm, cb), lambda i, j: (i, j + off))
    before = pl.BlockSpec((SUBLANES, cb), lambda i, j: (jnp.maximum(i * per - 1, 0), j + off))
    own = pl.BlockSpec((tm, cb), lambda i, j: (i, j))
    after = pl.BlockSpec((SUBLANES, cb), lambda i, j: (jnp.minimum((i + 1) * per, nrow * per - 1), j))
    par = lambda r: pl.BlockSpec((r, cb), lambda i, j: (0, j))
    return tm, cb, nrow, main, before, own, after, par


def _conv_fwd(zx, conv_w, conv_b, cfg):
    lp = zx.shape[0]
    tm, cb, nrow, main, before, own, after, par = _conv_specs(cfg, lp)

    def fn(x, halo, w, b):
        taps = _conv_taps(x, halo, pl.program_id(0) == 0)
        pre = b
        for k in range(SSD_CONV):
            pre = pre + taps[k] * w[k:k + 1, :]
        return _silu(pre)
    return _tile_call(fn, (nrow, cfg.conv_dim // cb), [zx, zx, conv_w, conv_b],
                      [main, before, par(SSD_CONV), par(1)], [((lp, cfg.conv_dim), F32)], [own], "ssd_conv")[0]


def _conv_bwd(zx, conv_w, conv_b, dxs, dbm, dcm, dzx, cfg):
    lp = zx.shape[0]
    tm, cb, nrow, main, before, own, after, par = _conv_specs(cfg, lp)
    ncb = cfg.conv_dim // cb
    nx, nb = cfg.ssd_inner // cb, cfg.gn // cb
    off = nx

    def fn1(x, halo, w, b, d1, d2, d3):
        j = pl.program_id(0)
        da = jnp.where(j < nx, d1, jnp.where(j < nx + nb, d2, d3))
        taps = _conv_taps(x, halo, pl.program_id(1) == 0)
        pre = b
        for k in range(SSD_CONV):
            pre = pre + taps[k] * w[k:k + 1, :]
        sg = _sigmoid(pre)
        dpre = da * sg * (1.0 + pre * (1.0 - sg))
        row8 = lax.broadcasted_iota(jnp.int32, (SUBLANES, cb), 0)
        dw = jnp.zeros((SUBLANES, cb), F32)
        for k in range(SSD_CONV):
            dw = jnp.where(row8 == k, jnp.sum(dpre * taps[k], axis=0, keepdims=True), dw)
        return dpre, dw, jnp.sum(dpre, axis=0, keepdims=True)
    sw = lambda spec: pl.BlockSpec(spec.block_shape, lambda j, i, f=spec.index_map: f(i, j))
    piece = lambda lo, n: pl.BlockSpec((tm, cb), lambda j, i: (i, jnp.clip(j - lo, 0, n - 1)))
    dpre, dw, db = _tile_call(
        fn1, (ncb, nrow), [zx, zx, conv_w, conv_b, dxs, dbm, dcm],
        [sw(main), sw(before), sw(par(SSD_CONV)), sw(par(1)), piece(0, nx), piece(nx, nb), piece(nx + nb, nb)],
        [((lp, cfg.conv_dim), F32), ((SUBLANES, cfg.conv_dim), F32), ((1, cfg.conv_dim), F32)],
        [sw(own), sw(par(SUBLANES)), sw(par(1))], "ssd_conv_bwd1", acc=(1, 2), acc_axis=1)

    def fn2(dp, nxt, w):
        nxt = jnp.where(pl.program_id(0) == nrow - 1, 0.0, nxt)
        row8 = lax.broadcasted_iota(jnp.int32, nxt.shape, 0)
        dx = dp * w[SSD_CONV - 1:SSD_CONV, :]
        for s in range(1, SSD_CONV):
            r = pltpu.roll(dp, tm - s, axis=0)
            bot = jnp.where(row8 >= SUBLANES - s, pltpu.roll(nxt, SUBLANES - s, axis=0), r[tm - SUBLANES:])
            up = jnp.concatenate([r[:tm - SUBLANES], bot], axis=0)
            dx = dx + up * w[SSD_CONV - 1 - s:SSD_CONV - s, :]
        return dx
    dzx = _tile_call(fn2, (nrow, ncb), [dpre, dpre, conv_w], [own, after, par(SSD_CONV)],
                     [(dzx.shape, F32)], [main], "ssd_conv_bwd2", fill=(dzx, 0))[0]
    return dzx, dw, db


def _ssd_common(x_ref, b_ref, c_ref, dt_ref, dtt_ref, ar_ref, ac_ref, h):
    q = SSD_BLOCK
    x, bm, cm = x_ref[...], b_ref[...], c_ref[...]
    dt, dtt = dt_ref[0], dtt_ref[0]
    row = lax.broadcasted_iota(jnp.int32, (q, q), 0)
    col = lax.broadcasted_iota(jnp.int32, (q, q), 1)
    tri = row >= col
    cs = _dot(tri.astype(F32), dt * ar_ref[0], precision=HI)
    cst = _dot(dtt * ac_ref[0], (row <= col).astype(F32), precision=HI)
    g = _bdot(cm, bm, "nt")
    ch = _bdot(cm, h)
    return x, bm, cm, dt, tri, cs, cst, g, ch


def _ssd_specs(cfg, rev):
    q, n, gw, hpg = SSD_BLOCK, SSD_STATE, cfg.gw, cfg.hpg
    nc = cfg.lp // q
    cc = (lambda c: nc - 1 - c) if rev else (lambda c: c)
    boff = cfg.ssd_inner // n
    return dict(
        x=pl.BlockSpec((q, gw), lambda g, c: (cc(c), g)),
        b=pl.BlockSpec((q, n), lambda g, c: (cc(c), boff + g)),
        c=pl.BlockSpec((q, n), lambda g, c: (cc(c), boff + SSD_GROUPS + g)),
        bc_out=pl.BlockSpec((q, n), lambda g, c: (cc(c), g)),
        dt=pl.BlockSpec((1, q, hpg), lambda g, c: (g, cc(c), 0)),
        dtt=pl.BlockSpec((1, hpg, q), lambda g, c: (g, 0, cc(c))),
        ar=pl.BlockSpec((1, 1, hpg), lambda g, c: (g, 0, 0)),
        ac=pl.BlockSpec((1, hpg, 1), lambda g, c: (g, 0, 0)),
        h=pl.BlockSpec((1, n, gw), lambda g, c: (cc(c), 0, g)))


def _ssd_fwd(xbc, dt_g, dtt_g, a_row, a_col, cfg):
    q, n, gw, hpg, lp = SSD_BLOCK, SSD_STATE, cfg.gw, cfg.hpg, cfg.lp
    nc = lp // q
    hd = SSD_HEAD_DIM
    sp = _ssd_specs(cfg, False)

    def body(x_ref, b_ref, c_ref, dt_ref, dtt_ref, ar_ref, ac_ref, y_ref, hp_ref, h_s, xw_s):
        @pl.when(pl.program_id(1) == 0)
        def _():
            h_s[...] = jnp.zeros((n, gw), F32)
        h = h_s[...]
        hp_ref[0] = h
        x, bm, cm, dt, tri, cs, cst, g, ch = _ssd_common(x_ref, b_ref, c_ref, dt_ref, dtt_ref, ar_ref, ac_ref, h)
        lane = lax.broadcasted_iota(jnp.int32, (1, gw), 1) // hd
        dec = jnp.zeros((1, gw), F32)
        for r in range(hpg):
            sl = slice(r * hd, (r + 1) * hd)
            csr = cs[:, r:r + 1]
            lm = jnp.exp(jnp.where(tri, csr - cst[r:r + 1, :], NEG))
            xdt = x[:, sl] * dt[:, r:r + 1]
            last = cs[q - 1:q, r:r + 1]
            y_ref[:, sl] = _bdot(g * lm, xdt) + jnp.exp(csr) * ch[:, sl]
            xw_s[:, sl] = xdt * jnp.exp(last - csr)
            dec = jnp.where(lane == r, jnp.exp(last), dec)
        h_s[...] = h * dec + _bdot(bm, xw_s[...], "tn")

    return pl.pallas_call(
        body, out_shape=[jax.ShapeDtypeStruct((lp, cfg.ssd_inner), F32),
                         jax.ShapeDtypeStruct((nc, n, cfg.ssd_inner), F32)],
        grid=(SSD_GROUPS, nc),
        in_specs=[sp["x"], sp["b"], sp["c"], sp["dt"], sp["dtt"], sp["ar"], sp["ac"]],
        out_specs=[sp["x"], sp["h"]],
        scratch_shapes=[pltpu.VMEM((n, gw), F32), pltpu.VMEM((q, gw), F32)],
        name="ssd_scan", compiler_params=_cparams(2))(xbc, xbc, xbc, dt_g, dtt_g, a_row, a_col)


def _ssd_bwd(xbc, dt_g, dtt_g, a_row, a_col, hprev, dy, dx_gate, cfg):
    q, n, gw, hpg, lp = SSD_BLOCK, SSD_STATE, cfg.gw, cfg.hpg, cfg.lp
    nc = lp // q
    hd = SSD_HEAD_DIM
    sp = _ssd_specs(cfg, True)

    def body(x_ref, b_ref, c_ref, dt_ref, dtt_ref, ar_ref, ac_ref, hp_ref, dy_ref, dxg_ref,
             dx_ref, db_ref, dc_ref, ddt_ref, da_ref, dh_s, xw_s, dye_s, colp_s, rowp_s, xs_s):
        @pl.when(pl.program_id(1) == 0)
        def _():
            dh_s[...] = jnp.zeros((n, gw), F32)
            da_ref[...] = jnp.zeros((1, 1, hpg), F32)
        h = hp_ref[0]
        dhn = dh_s[...]
        dy = dy_ref[...]
        x, bm, cm, dt, tri, cs, cst, g, ch = _ssd_common(x_ref, b_ref, c_ref, dt_ref, dtt_ref, ar_ref, ac_ref, h)
        bd = _bdot(bm, dhn)
        hh = jnp.sum(dhn * h, axis=0, keepdims=True)
        lane = lax.broadcasted_iota(jnp.int32, (1, gw), 1) // hd
        is_last = lax.broadcasted_iota(jnp.int32, (q, 1), 0) == q - 1
        dec = jnp.zeros((1, gw), F32)
        dg = jnp.zeros((q, q), F32)
        for r in range(hpg):
            sl = slice(r * hd, (r + 1) * hd)
            csr = cs[:, r:r + 1]
            lm = jnp.exp(jnp.where(tri, csr - cst[r:r + 1, :], NEG))
            dtr = dt[:, r:r + 1]
            xr = x[:, sl]
            xdt = xr * dtr
            last = cs[q - 1:q, r:r + 1]
            e = jnp.exp(csr)
            wv = jnp.exp(last - csr)
            elast = jnp.exp(last)
            m = g * lm
            dyr = dy[:, sl]
            dxdt = _bdot(m, dyr, "tn") + wv * bd[:, sl]
            dm = _bdot(dyr, xdt, "nt")
            dg = dg + dm * lm
            z = dm * m
            de = jnp.sum(dyr * ch[:, sl], axis=1, keepdims=True)
            dw = jnp.sum(xdt * bd[:, sl], axis=1, keepdims=True)
            extra = (jnp.sum(dw * wv, axis=0, keepdims=True)
                     + elast * jnp.sum(hh[:, sl], axis=1, keepdims=True))
            colp_s[:, r:r + 1] = (jnp.sum(z, axis=1, keepdims=True) + de * e - dw * wv
                                  + jnp.where(is_last, extra, 0.0))
            rowp_s[r:r + 1, :] = jnp.sum(z, axis=0, keepdims=True)
            xs_s[:, r:r + 1] = jnp.sum(dxdt * xr, axis=1, keepdims=True)
            dx_ref[:, sl] = dxdt * dtr + dxg_ref[:, sl]
            xw_s[:, sl] = xdt * wv
            dye_s[:, sl] = dyr * e
            dec = jnp.where(lane == r, elast, dec)
        eye = (lax.broadcasted_iota(jnp.int32, (hpg, hpg), 0)
               == lax.broadcasted_iota(jnp.int32, (hpg, hpg), 1)).astype(F32)
        dcs = colp_s[...] - _dot(rowp_s[...], eye, "tn", precision=HI)
        row = lax.broadcasted_iota(jnp.int32, (q, q), 0)
        col = lax.broadcasted_iota(jnp.int32, (q, q), 1)
        dda = _dot((row <= col).astype(F32), dcs, precision=HI)
        ddt_ref[0] = dda * ar_ref[0] + xs_s[...]
        da_ref[0] += jnp.sum(dda * dt, axis=0, keepdims=True)
        dc_ref[...] = _bdot(dg, bm) + _bdot(dye_s[...], h, "nt")
        db_ref[...] = _bdot(dg, cm, "tn") + _bdot(xw_s[...], dhn, "nt")
        dh_s[...] = dhn * dec + _bdot(cm, dye_s[...], "tn")

    return pl.pallas_call(
        body, out_shape=[jax.ShapeDtypeStruct((lp, cfg.ssd_inner), F32), jax.ShapeDtypeStruct((lp, cfg.gn), F32),
                         jax.ShapeDtypeStruct((lp, cfg.gn), F32), jax.ShapeDtypeStruct((SSD_GROUPS, lp, hpg), F32),
                         jax.ShapeDtypeStruct((SSD_GROUPS, 1, hpg), F32)],
        grid=(SSD_GROUPS, nc),
        in_specs=[sp["x"], sp["b"], sp["c"], sp["dt"], sp["dtt"], sp["ar"], sp["ac"], sp["h"], sp["x"], sp["x"]],
        out_specs=[sp["x"], sp["bc_out"], sp["bc_out"], sp["dt"], sp["ar"]],
        scratch_shapes=[pltpu.VMEM((n, gw), F32), pltpu.VMEM((q, gw), F32), pltpu.VMEM((q, gw), F32),
                        pltpu.VMEM((q, hpg), F32), pltpu.VMEM((hpg, q), F32), pltpu.VMEM((q, hpg), F32)],
        name="ssd_scan_bwd", compiler_params=_cparams(2))(xbc, xbc, xbc, dt_g, dtt_g, a_row, a_col, hprev, dy, dx_gate)


def _gate_fn(y, xs, z, dexp, ng):
    return _rms((y + dexp * xs) * _silu(z), ng)


def _gate_specs(cfg):
    tm, gw = cfg.tm, cfg.gw
    blk = pl.BlockSpec((tm, gw), lambda g, i: (i, g))
    par = pl.BlockSpec((1, gw), lambda g, i: (0, g))
    return blk, par


def _mamba_fwd(h, w, w_in_t, w_out, conv_w, cfg):
    lp, tm, nh, hpg, inner = cfg.lp, cfg.tm, cfg.ssd_heads, cfg.hpg, cfg.ssd_inner
    zx = _mm(h, w_in_t, "nt", "l1_in")
    conv_b = w["l1_conv_b"].reshape(1, -1)
    xbc = _conv_fwd(zx, conv_w, conv_b, cfg)
    dt_raw = zx[:, inner + cfg.conv_dim:inner + cfg.conv_dim + nh]
    dt_bias = w["l1_dt_bias"].reshape(1, nh)
    a_log = w["l1_a_log"].reshape(1, nh)
    dt = _rowwise(lambda r, b: _softplus(r + b), [dt_raw], [dt_bias], [(nh, F32)], [], tm, "ssd_dt")[0]
    a = _small_call(lambda al: -jnp.exp(al), [a_log], [(1, nh)], "ssd_a")[0]
    dt_g = dt.reshape(lp, SSD_GROUPS, hpg).transpose(1, 0, 2)
    dtt_g = dt_g.transpose(0, 2, 1)
    a_row, a_col = a.reshape(SSD_GROUPS, 1, hpg), a.reshape(SSD_GROUPS, hpg, 1)
    y, hprev = _ssd_fwd(xbc, dt_g, dtt_g, a_row, a_col, cfg)
    dexp = jnp.repeat(w["l1_d"], SSD_HEAD_DIM).reshape(1, inner)
    ng = w["l1_norm_g"].reshape(1, inner)
    blk, par = _gate_specs(cfg)
    yn = _tile_call(_gate_fn, (SSD_GROUPS, lp // tm), [y, xbc, zx, dexp, ng], [blk, blk, blk, par, par],
                    [((lp, inner), F32)], [blk], "ssd_gate")[0]
    mo = _mm(yn, w_out, "nn", "l1_out")
    return mo, (zx, xbc, dt_raw, dt_g, dtt_g, a, a_row, a_col, y, hprev, dexp, ng, yn)


def _mamba_bwd(h, saved, dmo, w, w_in_t, w_out, conv_w, cfg):
    zx, xbc, dt_raw, dt_g, dtt_g, a, a_row, a_col, y, hprev, dexp, ng, yn = saved
    lp, tm, nh, hpg, inner = cfg.lp, cfg.tm, cfg.ssd_heads, cfg.hpg, cfg.ssd_inner
    d_wout = _mm(yn, dmo, "tn", "l1_dwout", out_dtype=BF16)
    dyn = _mm(dmo, w_out, "nt", "l1_dyn")
    blk, par = _gate_specs(cfg)

    def gate_bwd(y, xs, z, dexp, ng, dyn):
        _, vjp = jax.vjp(_gate_fn, y, xs, z, dexp, ng)
        return vjp(dyn)
    dy, dxs_gate, dzx, ddexp, dng = _tile_call(
        gate_bwd, (SSD_GROUPS, lp // tm), [y, xbc, zx, dexp, ng, dyn], [blk, blk, blk, par, par, blk],
        [((lp, inner), F32)] * 2 + [((lp, cfg.l1_inp), F32)] + [((1, inner), F32)] * 2, [blk, blk, blk, par, par],
        "ssd_gate_bwd", acc=(3, 4), acc_axis=1)
    dxs, dbm, dcm, ddt_g, da_g = _ssd_bwd(xbc, dt_g, dtt_g, a_row, a_col, hprev, dy, dxs_gate, cfg)
    conv_b = w["l1_conv_b"].reshape(1, -1)
    dzx, dconv_w, dconv_b = _conv_bwd(zx, conv_w, conv_b, dxs, dbm, dcm, dzx, cfg)
    assert cfg.l1_inp - inner - cfg.conv_dim == LANES
    ddt = jnp.pad(ddt_g.transpose(1, 0, 2).reshape(lp, nh), ((0, 0), (0, LANES - nh)))
    dt_bias = jnp.pad(w["l1_dt_bias"].reshape(1, nh), ((0, 0), (0, LANES - nh)))
    last = (inner + cfg.conv_dim) // LANES
    tail = pl.BlockSpec((tm, LANES), lambda i: (i, last))

    def dt_bwd(ddt, r, b):
        lane = lax.broadcasted_iota(jnp.int32, ddt.shape, 1)
        d = jnp.where(lane < nh, ddt * _sigmoid(r + b), 0.0)
        return d, jnp.sum(d, axis=0, keepdims=True)
    dzx, ddt_bias = _tile_call(dt_bwd, (lp // tm,), [ddt, zx, dt_bias], [_rows(tm, LANES), tail, _whole((1, LANES))],
                               [(dzx.shape, F32), ((1, LANES), F32)], [tail, _whole((1, LANES))], "ssd_dt_bwd",
                               acc=(1,), fill=(dzx, 0))
    ddt_bias = ddt_bias[:, :nh]
    da_log, dd = _small_call(lambda da, a, dde: (da * a, jnp.sum(dde, axis=1, keepdims=True)),
                             [da_g.reshape(1, nh), a, ddexp.reshape(nh, SSD_HEAD_DIM)], [(1, nh), (nh, 1)],
                             "ssd_small_bwd")
    d_win_t = _mm(dzx, h, "tn", "l1_dwin", out_dtype=BF16)
    dh = _mm(dzx, w_in_t, "nn", "l1_dh")
    grads = {"l1_w_in": d_win_t, "l1_conv_w": dconv_w[:SSD_CONV], "l1_conv_b": dconv_b.reshape(-1),
             "l1_dt_bias": ddt_bias.reshape(-1), "l1_a_log": da_log.reshape(-1), "l1_d": dd.reshape(-1),
             "l1_norm_g": dng.reshape(-1), "l1_w_out": d_wout}
    return dh, grads


def _local_step(x, target, w, cfg):
    lp, n, d, tm, sw = cfg.lp, cfg.n, cfg.d, cfg.tr, cfg.s5_w
    row = lambda name: w[name].reshape(1, -1)
    h0 = jnp.concatenate([w["meta_tokens"], x, jnp.zeros((lp - n, d), F32)], axis=0)
    proj = _mm(h0, w["l0_w_in"], "nn", "l0_in")
    o1, o2, o3 = sw, sw + cfg.q_rank, sw + cfg.q_rank + cfg.kv_rank
    prm = _s5_prep(w, cfg)
    a_out, s5_saved = _s5_fwd(proj[:, :o1], prm, w["l0_s5_w_glu"], cfg)
    b_out, mla_saved = _mla_fwd(proj[:, o1:o2], proj[:, o2:o3], proj[:, o3:], w, w["l0_mla_w_uq_p"],
                                w["l0_mla_w_ukv"], cfg)
    mix = jnp.concatenate([a_out, b_out], axis=1)
    mo0 = _mm(mix, w["l0_w_out"], "nn", "l0_out")
    h1 = _ln_fwd(h0, mo0, row("l0_ln1_g"), row("l0_ln1_b"), cfg, "l0_ln1")
    fo0, ffn0 = _ffn_fwd(h1, w["l0_w_gu"], w["l0_ffn_w_down"], cfg, "l0_ffn")
    h2 = _ln_fwd(h1, fo0, row("l0_ln2_g"), row("l0_ln2_b"), cfg, "l0_ln2")
    mo1, mam = _mamba_fwd(h2, w, w["l1_w_in_t"], w["l1_w_out"], w["l1_conv_w"], cfg)
    h3 = _ln_fwd(h2, mo1, row("l1_ln1_g"), row("l1_ln1_b"), cfg, "l1_ln1")
    fo1, ffn1 = _ffn_fwd(h3, w["l1_w_gu"], w["l1_ffn_w_down"], cfg, "l1_ffn")
    h4 = _ln_fwd(h3, fo1, row("l1_ln2_g"), row("l1_ln2_b"), cfg, "l1_ln2")
    tgt = jnp.concatenate([jnp.zeros((N_META, d), F32), target, jnp.zeros((lp - n, d), F32)], axis=0)

    def loss_fn(y, t):
        r = pl.program_id(0) * tm + lax.broadcasted_iota(jnp.int32, (tm, 1), 0)
        diff = jnp.where(jnp.logical_and(r >= N_META, r < n), y - t, 0.0)
        return diff * (1.0 / d), jnp.sum(diff * diff, axis=0, keepdims=True) * (0.5 / d)
    dh4, loss_lanes = _rowwise(loss_fn, [h4, tgt], [], [(d, F32)], [(1, d)], tm, "loss")
    grads = {}
    dr4, dg, db = _ln_bwd(h3, fo1, row("l1_ln2_g"), row("l1_ln2_b"), [dh4], [1.0], cfg, "l1_ln2_bwd")
    grads["l1_ln2_g"], grads["l1_ln2_b"] = dg.reshape(-1), db.reshape(-1)
    dh3, grads["l1_w_gu"], grads["l1_ffn_w_down"] = _ffn_bwd(h3, ffn1, dr4, w["l1_w_gu"], w["l1_ffn_w_down"], cfg,
                                                             "l1_ffn")
    dr3, dg, db = _ln_bwd(h2, mo1, row("l1_ln1_g"), row("l1_ln1_b"), [dr4, dh3], [DN_ALPHA, 1.0], cfg, "l1_ln1_bwd")
    grads["l1_ln1_g"], grads["l1_ln1_b"] = dg.reshape(-1), db.reshape(-1)
    dh2, mg = _mamba_bwd(h2, mam, dr3, w, w["l1_w_in_t"], w["l1_w_out"], w["l1_conv_w"], cfg)
    grads.update(mg)
    dr2, dg, db = _ln_bwd(h1, fo0, row("l0_ln2_g"), row("l0_ln2_b"), [dr3, dh2], [DN_ALPHA, 1.0], cfg, "l0_ln2_bwd")
    grads["l0_ln2_g"], grads["l0_ln2_b"] = dg.reshape(-1), db.reshape(-1)
    dh1, grads["l0_w_gu"], grads["l0_ffn_w_down"] = _ffn_bwd(h1, ffn0, dr2, w["l0_w_gu"], w["l0_ffn_w_down"], cfg,
                                                             "l0_ffn")
    dr1, dg, db = _ln_bwd(h0, mo0, row("l0_ln1_g"), row("l0_ln1_b"), [dr2, dh1], [DN_ALPHA, 1.0], cfg, "l0_ln1_bwd")
    grads["l0_ln1_g"], grads["l0_ln1_b"] = dg.reshape(-1), db.reshape(-1)
    grads["l0_w_out"] = _mm(mix, dr1, "tn", "l0_dwout", out_dtype=BF16)
    dmix = _mm(dr1, w["l0_w_out"], "nt", "l0_dmix")
    du, sg = _s5_bwd(dmix[:, :sw], s5_saved, prm, w, w["l0_s5_w_glu"], cfg)
    dql, dkl, dkr, ag = _mla_bwd(dmix[:, sw:], mla_saved, w, w["l0_mla_w_uq_p"], w["l0_mla_w_ukv"], cfg)
    grads.update(sg)
    grads.update(ag)
    dproj = jnp.concatenate([du, dql, dkl, dkr], axis=1)
    grads["l0_w_in"] = _mm(h0, dproj, "tn", "l0_dwin", out_dtype=BF16)
    dh0m = _mm(dproj, w["l0_w_in"], "nt", "l0_dh")
    dh0 = _rowwise(lambda a, b: DN_ALPHA * a + b, [dr1, dh0m], [], [(d, F32)], [], tm, "l0_dh0")[0]
    grads["meta_tokens"] = dh0[:N_META]
    return loss_lanes, dh0[N_META:n], grads


FLAT_W = 1024
N_SLOTS = 4
HBM_SPEC = pl.BlockSpec(memory_space=pltpu.HBM)


def _place():
    x, y, c = lax.axis_index("x"), lax.axis_index("y"), lax.axis_index("c")
    chips = [(1 - x, y), (x, 1 - y), (1 - x, 1 - y)]
    return x, y, c, chips


def _remote(src, dst, ssem, rsem, dev):
    return pltpu.make_async_remote_copy(src_ref=src, dst_ref=dst, send_sem=ssem, recv_sem=rsem, device_id=dev,
                                        device_id_type=MESH)


def _gather_shards(shards, groups):
    n = len(shards)
    place = {t: (g, row0) for g, members in enumerate(groups) for t, row0 in members}
    out_shapes = []
    for members in groups:
        t0 = members[0][0]
        rows = max(row0 + N_SLOTS * shards[t].shape[0] for t, row0 in members)
        out_shapes.append(jax.ShapeDtypeStruct((rows, shards[t0].shape[1]), shards[t0].dtype))

    def body(*refs):
        srcs, outs = refs[:n], refs[n:n + len(groups)]
        send_sems, recv_sems, fsend, frecv, lsems = refs[n + len(groups):]
        x, y, c, chips = _place()
        me = 2 * x + y
        sib = (x, y, 1 - c)

        def rows_of(t, slot, half):
            r = shards[t].shape[0]
            g, row0 = place[t]
            return outs[g].at[pl.ds(row0 + slot * r + half * (r // 2), r // 2)]
        started = []
        for t in range(n):
            r = shards[t].shape[0]
            g, row0 = place[t]
            cp = pltpu.make_async_copy(srcs[t], outs[g].at[pl.ds(row0 + me * r, r)], lsems.at[t])
            cp.start()
            started.append(cp)
        sends = []
        for t in range(n):
            mine = srcs[t].at[pl.ds(c * (shards[t].shape[0] // 2), shards[t].shape[0] // 2)]
            for j, (cx, cy) in enumerate(chips):
                cp = _remote(mine, rows_of(t, me, c), send_sems.at[3 * t + j], recv_sems.at[3 * t + j], (cx, cy, c))
                cp.start()
                sends.append(cp)
        for t in range(n):
            for j, (cx, cy) in enumerate(chips):
                got = rows_of(t, 2 * cx + cy, c)
                _remote(got, got, send_sems.at[3 * t + j], recv_sems.at[3 * t + j], (cx, cy, c)).wait_recv()
                cp = _remote(got, got, fsend.at[3 * t + j], frecv.at[3 * t + j], sib)
                cp.start()
                sends.append(cp)
        for t in range(n):
            for j, (cx, cy) in enumerate(chips):
                got = rows_of(t, 2 * cx + cy, 1 - c)
                _remote(got, got, fsend.at[3 * t + j], frecv.at[3 * t + j], sib).wait_recv()
        for cp in sends:
            cp.wait_send()
        for cp in started:
            cp.wait()

    sems = pltpu.SemaphoreType.DMA((3 * n,))
    return pl.pallas_call(
        body, out_shape=out_shapes, in_specs=[HBM_SPEC] * n, out_specs=[HBM_SPEC] * len(groups),
        scratch_shapes=[sems, sems, sems, sems, pltpu.SemaphoreType.DMA((n,))], name="gather_shards")(*shards)


def _exchange_slots(items):
    n = len(items)
    arrays = [a for a, _, _ in items]

    def body(*refs):
        srcs, outs = refs[:n], refs[n:2 * n]
        send_sems, recv_sems = refs[2 * n:]
        x, y, c, chips = _place()
        cps = []
        for t, (_, row0, rps) in enumerate(items):
            for j, (cx, cy) in enumerate(chips):
                cp = _remote(srcs[t].at[pl.ds(row0 + (2 * cx + cy) * rps, rps)], outs[t].at[j],
                             send_sems.at[3 * t + j], recv_sems.at[3 * t + j], (cx, cy, c))
                cp.start()
                cps.append(cp)
        for cp in cps:
            cp.wait()

    sems = pltpu.SemaphoreType.DMA((3 * n,))
    return pl.pallas_call(
        body, out_shape=[jax.ShapeDtypeStruct((3, rps, a.shape[1]), a.dtype) for a, _, rps in items],
        in_specs=[HBM_SPEC] * n, out_specs=[HBM_SPEC] * n, scratch_shapes=[sems, sems], name="exchange_slots")(*arrays)


def _exchange_sibling(arrays):
    n = len(arrays)

    def body(*refs):
        srcs, outs = refs[:n], refs[n:2 * n]
        ssems, rsems = refs[2 * n:]
        x, y, c, _ = _place()
        cps = []
        for t in range(n):
            cp = _remote(srcs[t], outs[t], ssems.at[t], rsems.at[t], (x, y, 1 - c))
            cp.start()
            cps.append(cp)
        for cp in cps:
            cp.wait()

    sems = pltpu.SemaphoreType.DMA((n,))
    return pl.pallas_call(
        body, out_shape=[jax.ShapeDtypeStruct(a.shape, a.dtype) for a in arrays], in_specs=[HBM_SPEC] * n,
        out_specs=[HBM_SPEC] * n, scratch_shapes=[sems, sems], name="exchange_sibling")(*arrays)


def _gather_all(v):
    flips = [(fx, fy, fc) for fx in (0, 1) for fy in (0, 1) for fc in (0, 1)][1:]

    def body(src, out, send_sems, recv_sems, lsem):
        x, y, c, _ = _place()
        local = pltpu.make_async_copy(src, out.at[4 * x + 2 * y + c], lsem)
        local.start()
        cps = []
        for k, (fx, fy, fc) in enumerate(flips):
            px, py, pc = (1 - x if fx else x), (1 - y if fy else y), (1 - c if fc else c)
            cp = _remote(src, out.at[4 * x + 2 * y + c], send_sems.at[k], recv_sems.at[k], (px, py, pc))
            cp.start()
            cps.append(cp)
        for cp in cps:
            cp.wait()
        local.wait()

    return pl.pallas_call(
        body, out_shape=jax.ShapeDtypeStruct((8,) + v.shape, v.dtype), in_specs=[HBM_SPEC], out_specs=HBM_SPEC,
        scratch_shapes=[pltpu.SemaphoreType.DMA((7,)), pltpu.SemaphoreType.DMA((7,)), pltpu.SemaphoreType.DMA],
        name="gather_all")(v)


def _flat_rows(n_elems, row_unit):
    return -(-n_elems // (FLAT_W * row_unit)) * row_unit


def _to_flat(pieces, rows):
    flat = jnp.concatenate([p.reshape(-1) for p in pieces])
    return jnp.pad(flat, (0, rows * FLAT_W - flat.shape[0])).reshape(rows, FLAT_W)


def _sum_parts(parts, name, tm=512):
    rows = parts[0].shape[1]
    tm = _div_tile(rows, tm, SUBLANES)

    def fn(*ps):
        acc = None
        for p in ps:
            for k in range(p.shape[0]):
                acc = p[k].astype(F32) if acc is None else acc + p[k].astype(F32)
        return acc
    return _tile_call(fn, (rows // tm,), parts,
                      [pl.BlockSpec((p.shape[0], tm, FLAT_W), lambda i: (0, i, 0)) for p in parts],
                      [((rows, FLAT_W), F32)], [_rows(tm, FLAT_W)], name)[0]


ELEMENTWISE_BLOCK = 1 << 19


def _row_tile(rows, cols, unit):
    return _div_tile(rows, max(unit, ELEMENTWISE_BLOCK // cols), unit)


def _sum_slot(g, row0, rps, others, me, name):
    c = g.shape[1]
    tm = _row_tile(rps, c, 2 * SUBLANES)
    nrb = rps // tm
    assert row0 % tm == 0

    def body(me_ref, g_ref, o_ref, out_ref):
        out_ref[...] = ((g_ref[...].astype(F32) + o_ref[0].astype(F32)) + o_ref[1].astype(F32)) + o_ref[2].astype(F32)

    grid_spec = pltpu.PrefetchScalarGridSpec(
        num_scalar_prefetch=1, grid=(nrb,),
        in_specs=[pl.BlockSpec((tm, c), lambda i, me_ref: (row0 // tm + me_ref[0] * nrb + i, 0)),
                  pl.BlockSpec((3, tm, c), lambda i, me_ref: (0, i, 0))],
        out_specs=pl.BlockSpec((tm, c), lambda i, me_ref: (i, 0)))
    return pl.pallas_call(body, out_shape=jax.ShapeDtypeStruct((rps, c), F32), grid_spec=grid_spec, name=name,
                          compiler_params=_cparams(1))(me, g, others)


def _adamw(gparts, w, m, v, name, tm=None):
    rows, cols = w.shape
    tm = _row_tile(rows, cols, SUBLANES) if tm is None else _div_tile(rows, tm, SUBLANES)
    ng = len(gparts)

    def fn(*a):
        g = a[0]
        for t in a[1:ng]:
            g = g + t
        w, m, v = a[ng:]
        m = ADAM_B1 * m + (1.0 - ADAM_B1) * g
        v = ADAM_B2 * v + (1.0 - ADAM_B2) * (g * g)
        m_hat = m / (1.0 - ADAM_B1 ** ADAM_STEP)
        v_hat = v / (1.0 - ADAM_B2 ** ADAM_STEP)
        delta = -ADAM_LR * (m_hat / (jnp.sqrt(v_hat) + ADAM_EPS) + ADAM_WD * w)
        return g, delta, m, v
    ins = list(gparts) + [w, m, v]
    return _tile_call(fn, (rows // tm,), ins, [_rows(tm, cols)] * len(ins), [((rows, cols), F32)] * 4,
                      [_rows(tm, cols)] * 4, name)


BIG = ("l0_w_in", "l0_s5_w_glu", "l0_mla_w_uq", "l0_mla_w_ukv", "l0_w_out", "l0_ffn_w_gate", "l0_ffn_w_up",
       "l0_ffn_w_down", "l1_w_in", "l1_w_out", "l1_ffn_w_gate", "l1_ffn_w_up", "l1_ffn_w_down")
TINY = ("meta_tokens", "l1_conv_w")
REPLICATED = ("l0_s5_log_dt", "l0_s5_a_re", "l0_s5_a_im", "l0_s5_b_re", "l0_s5_b_im", "l0_s5_c_re", "l0_s5_c_im",
              "l0_s5_d", "l0_mla_q_norm", "l0_mla_kv_norm", "l0_ln1_g", "l0_ln1_b", "l0_ln2_g", "l0_ln2_b",
              "l1_conv_b", "l1_dt_bias", "l1_a_log", "l1_d", "l1_norm_g", "l1_ln1_g", "l1_ln1_b", "l1_ln2_g",
              "l1_ln2_b")
WEIGHTS = ("meta_tokens", "l0_w_in", "l0_s5_log_dt", "l0_s5_a_re", "l0_s5_a_im", "l0_s5_b_re", "l0_s5_b_im",
           "l0_s5_c_re", "l0_s5_c_im", "l0_s5_d", "l0_s5_w_glu", "l0_mla_q_norm", "l0_mla_w_uq", "l0_mla_kv_norm",
           "l0_mla_w_ukv", "l0_w_out", "l0_ln1_g", "l0_ln1_b", "l0_ffn_w_gate", "l0_ffn_w_up", "l0_ffn_w_down",
           "l0_ln2_g", "l0_ln2_b", "l1_w_in", "l1_conv_w", "l1_conv_b", "l1_dt_bias", "l1_a_log", "l1_d",
           "l1_norm_g", "l1_w_out", "l1_ln1_g", "l1_ln1_b", "l1_ffn_w_gate", "l1_ffn_w_up", "l1_ffn_w_down",
           "l1_ln2_g", "l1_ln2_b")
def _split_flat(flat2d, shapes):
    flat = flat2d.reshape(-1)
    out, off = [], 0
    for s in shapes:
        n = math.prod(s)
        out.append(flat[off:off + n].reshape(s))
        off += n
    return out


def _cols_to_slots(full):
    r, c = full.shape
    return full.reshape(r, N_SLOTS, c // N_SLOTS).transpose(1, 0, 2).reshape(N_SLOTS * r, c // N_SLOTS)


def _slots_to_cols(slabs):
    r4, c = slabs.shape
    return slabs.reshape(N_SLOTS, r4 // N_SLOTS, c).transpose(1, 0, 2).reshape(r4 // N_SLOTS, N_SLOTS * c)


def _step(cfg, x, loss_target, ws, ms, vs):
    d, f = cfg.d, cfg.ffn
    me = 2 * lax.axis_index("x") + lax.axis_index("y")
    kinds = ("grad", "delta", "new_m", "new_v")
    shards = [ws[name].astype(BF16) for name in BIG]
    groups, w = [], {}
    for t, name in enumerate(BIG):
        if name.endswith("_ffn_w_up"):
            groups[-1].append((t, N_SLOTS * d))
        else:
            groups.append([(t, 0)])
    names = [BIG[members[0][0]] for members in groups]
    got = dict(zip(names, _gather_shards(shards, groups)))
    for name in ("l0_w_in", "l0_s5_w_glu", "l0_w_out", "l0_ffn_w_down", "l1_w_out", "l1_ffn_w_down"):
        w[name] = got[name]
    w["l0_mla_w_uq_p"] = _pad_heads(_slots_to_cols(got["l0_mla_w_uq"]), cfg.heads, MLA_NOPE + MLA_ROPE)
    w["l0_mla_w_ukv"] = _slots_to_cols(got["l0_mla_w_ukv"])
    for l in ("l0", "l1"):
        w[l + "_w_gu"] = got[l + "_ffn_w_gate"].reshape(2 * N_SLOTS, d, f // N_SLOTS)
    w_in_t = got["l1_w_in"].reshape(N_SLOTS, d, cfg.l1_in // N_SLOTS).transpose(0, 2, 1).reshape(cfg.l1_in, d)
    w["l1_w_in_t"] = jnp.pad(w_in_t, ((0, cfg.l1_inp - cfg.l1_in), (0, 0)))
    tiny_shapes = [ws[name].shape for name in TINY]
    trows = _flat_rows(sum(math.prod(s) for s in tiny_shapes), SUBLANES)
    tiny = _gather_all(_to_flat([ws[name] for name in TINY], trows))[0::2]
    for k, name in enumerate(TINY):
        blocks = jnp.stack([_split_flat(tiny[s], tiny_shapes)[k] for s in range(N_SLOTS)])
        w[name] = blocks.transpose(1, 0, 2).reshape(blocks.shape[1], -1)
    for name in REPLICATED:
        w[name] = ws[name]
    loss_lanes, grad_x, grads = _local_step(x[0], loss_target[0], w, cfg)
    items = {
        "l0_w_in": (grads["l0_w_in"], 0), "l0_s5_w_glu": (grads["l0_s5_w_glu"], 0),
        "l0_mla_w_uq": (_cols_to_slots(grads["l0_mla_w_uq"]).astype(BF16), 0),
        "l0_mla_w_ukv": (_cols_to_slots(grads["l0_mla_w_ukv"]).astype(BF16), 0),
        "l0_w_out": (grads["l0_w_out"], 0), "l0_ffn_w_down": (grads["l0_ffn_w_down"], 0),
        "l1_w_in": (grads["l1_w_in"], 0), "l1_w_out": (grads["l1_w_out"], 0),
        "l1_ffn_w_down": (grads["l1_ffn_w_down"], 0)}
    for l in ("l0", "l1"):
        gu = grads[l + "_w_gu"].reshape(2 * N_SLOTS * d, f // N_SLOTS)
        items[l + "_ffn_w_gate"] = (gu, 0)
        items[l + "_ffn_w_up"] = (gu, N_SLOTS * d)
    rps = {name: ws[name].shape[1 if name == "l1_w_in" else 0] for name in BIG}
    triples = [(items[name][0], items[name][1], rps[name]) for name in BIG]
    others = _exchange_slots(triples)
    me1 = me.reshape(1).astype(jnp.int32)
    parts = [_sum_slot(a, row0, r, o, me1, "sum_" + name) for (a, row0, r), o, name in zip(triples, others, BIG)]
    sibs = _exchange_sibling(parts)
    res = {}
    for name, p, q in zip(BIG, parts, sibs):
        if name == "l1_w_in":
            p, q = p.T, q.T
        for kind, arr in zip(kinds, _adamw([p, q], ws[name], ms[name], vs[name], "adamw_" + name)):
            res[kind + "_" + name] = arr
    rep_shapes = [(1, d)] + [ws[name].shape for name in REPLICATED]
    all_shapes = rep_shapes + [grads[name].shape for name in TINY]
    srows = _flat_rows(sum(math.prod(s) for s in all_shapes), SUBLANES)
    small = _to_flat([loss_lanes] + [grads[name] for name in REPLICATED + TINY], srows)
    total = _sum_parts([_gather_all(small)], "sum_small", tm=srows)
    zero = jnp.zeros((1, d), F32)
    flat = lambda dct: _to_flat([zero] + [dct[name] for name in REPLICATED], srows)
    outs = _adamw([total], flat(ws), flat(ms), flat(vs), "adamw_replicated", tm=srows)
    for kind, arr in zip(kinds, outs):
        vals = _split_flat(arr, rep_shapes)
        for name, val in zip(REPLICATED, vals[1:]):
            res[kind + "_" + name] = val
    for name, g in zip(TINY, _split_flat(total, all_shapes)[len(rep_shapes):]):
        cols = ws[name].shape[1]
        mine = lax.dynamic_slice_in_dim(g, me * cols, cols, axis=1)
        for kind, arr in zip(kinds, _adamw([mine], ws[name], ms[name], vs[name], "adamw_" + name)):
            res[kind + "_" + name] = arr
    loss = _small_call(lambda t: jnp.sum(t, axis=1, keepdims=True), [_split_flat(total, rep_shapes)[0]], [(1, 1)],
                       "loss_sum")[0].reshape(())
    ordered = [res[kind + "_" + name] for kind in ("grad", "delta", "new_m", "new_v") for name in WEIGHTS]
    return (loss, grad_x[None]) + tuple(ordered)


def kernel(x, meta_tokens, l0_w_in, l0_s5_log_dt, l0_s5_a_re, l0_s5_a_im, l0_s5_b_re, l0_s5_b_im, l0_s5_c_re,
           l0_s5_c_im, l0_s5_d, l0_s5_w_glu, l0_mla_q_norm, l0_mla_w_uq, l0_mla_kv_norm, l0_mla_w_ukv, l0_w_out,
           l0_ln1_g, l0_ln1_b, l0_ffn_w_gate, l0_ffn_w_up, l0_ffn_w_down, l0_ln2_g, l0_ln2_b, l1_w_in, l1_conv_w,
           l1_conv_b, l1_dt_bias, l1_a_log, l1_d, l1_norm_g, l1_w_out, l1_ln1_g, l1_ln1_b, l1_ffn_w_gate,
           l1_ffn_w_up, l1_ffn_w_down, l1_ln2_g, l1_ln2_b, loss_target, m_meta_tokens, m_l0_w_in, m_l0_s5_log_dt,
           m_l0_s5_a_re, m_l0_s5_a_im, m_l0_s5_b_re, m_l0_s5_b_im, m_l0_s5_c_re, m_l0_s5_c_im, m_l0_s5_d,
           m_l0_s5_w_glu, m_l0_mla_q_norm, m_l0_mla_w_uq, m_l0_mla_kv_norm, m_l0_mla_w_ukv, m_l0_w_out, m_l0_ln1_g,
           m_l0_ln1_b, m_l0_ffn_w_gate, m_l0_ffn_w_up, m_l0_ffn_w_down, m_l0_ln2_g, m_l0_ln2_b, m_l1_w_in,
           m_l1_conv_w, m_l1_conv_b, m_l1_dt_bias, m_l1_a_log, m_l1_d, m_l1_norm_g, m_l1_w_out, m_l1_ln1_g,
           m_l1_ln1_b, m_l1_ffn_w_gate, m_l1_ffn_w_up, m_l1_ffn_w_down, m_l1_ln2_g, m_l1_ln2_b, v_meta_tokens,
           v_l0_w_in, v_l0_s5_log_dt, v_l0_s5_a_re, v_l0_s5_a_im, v_l0_s5_b_re, v_l0_s5_b_im, v_l0_s5_c_re,
           v_l0_s5_c_im, v_l0_s5_d, v_l0_s5_w_glu, v_l0_mla_q_norm, v_l0_mla_w_uq, v_l0_mla_kv_norm,
           v_l0_mla_w_ukv, v_l0_w_out, v_l0_ln1_g, v_l0_ln1_b, v_l0_ffn_w_gate, v_l0_ffn_w_up, v_l0_ffn_w_down,
           v_l0_ln2_g, v_l0_ln2_b, v_l1_w_in, v_l1_conv_w, v_l1_conv_b, v_l1_dt_bias, v_l1_a_log, v_l1_d,
           v_l1_norm_g, v_l1_w_out, v_l1_ln1_g, v_l1_ln1_b, v_l1_ffn_w_gate, v_l1_ffn_w_up, v_l1_ffn_w_down,
           v_l1_ln2_g, v_l1_ln2_b):
    given = dict(locals())
    ws = {name: given[name] for name in WEIGHTS}
    ms = {name: given["m_" + name] for name in WEIGHTS}
    vs = {name: given["v_" + name] for name in WEIGHTS}
    return _step(FULL, x, loss_target, ws, ms, vs)
```

```python
import functools
import math

import numpy as np
import jax
import jax.numpy as jnp
from jax import lax
from jax.experimental import pallas as pl
from jax.experimental.pallas import tpu as pltpu

F32 = jnp.float32
BF16 = jnp.bfloat16
HI = lax.Precision.HIGHEST
MESH = pl.DeviceIdType.MESH

LANES = 128
SUBLANES = 8
VMEM_LIMIT_BYTES = 56 * 1024 * 1024

N_META = 16
CHUNK = 64
DEPTH = 2
DN_ALPHA = (2 * DEPTH) ** 0.25
LN_EPS = 1e-5
RMS_EPS = 1e-6
ROPE_BASE = 10000.0
S5_GROUP = 16
S5_STATE = 64
S5_GPB = 8
MLA_NOPE = 128
MLA_ROPE = 64
MLA_V = 128
MLA_QW = 256
ATT_SHIFT = 48
SSD_HEAD_DIM = 64
SSD_GROUPS = 8
SSD_STATE = 128
SSD_CONV = 4
SSD_BLOCK = 128
ADAM_LR = 0.001
ADAM_B1 = 0.9
ADAM_B2 = 0.999
ADAM_EPS = 1e-08
ADAM_WD = 0.01
ADAM_STEP = 10
NEG = -1e30


class _Cfg:
    def __init__(self, d_model, seq, row_tile, att_tile, scan_tiles, small_row_tile):
        d = d_model
        self.tr = small_row_tile
        self.d = d
        self.seq = seq
        self.n = seq + N_META
        lp = -(-(self.n + ATT_SHIFT) // row_tile) * row_tile
        self.lp = lp
        self.tm = row_tile
        self.tq = att_tile
        self.scan_tiles = scan_tiles
        self.s5_w = d // 2
        self.s5_g = self.s5_w // S5_GROUP
        self.s5_nb = self.s5_g // S5_GPB
        self.s5_c = self.s5_g * S5_STATE
        self.heads = d // 256
        self.q_rank = d // 4
        self.kv_rank = d // 8
        self.l0_in = self.s5_w + self.q_rank + self.kv_rank + MLA_ROPE
        self.l0_mix = self.s5_w + self.heads * MLA_V
        self.ssd_inner = 2 * d
        self.ssd_heads = self.ssd_inner // SSD_HEAD_DIM
        self.hpg = self.ssd_heads // SSD_GROUPS
        self.gw = self.hpg * SSD_HEAD_DIM
        self.gn = SSD_GROUPS * SSD_STATE
        self.conv_dim = self.ssd_inner + 2 * self.gn
        self.l1_in = self.ssd_inner + self.conv_dim + self.ssd_heads
        self.l1_inp = -(-self.l1_in // LANES) * LANES
        self.ffn = -(-(8 * d) // (3 * 256)) * 256
        assert lp % att_tile == 0 and lp % SSD_BLOCK == 0 and lp % (8 * scan_tiles) == 0


FULL = _Cfg(2048, 8192, 640, 640, 4, 160)


def _cparams(n_grid):
    return pltpu.CompilerParams(dimension_semantics=("arbitrary",) * n_grid,
                                vmem_limit_bytes=VMEM_LIMIT_BYTES)


def _div_tile(n, target, unit=LANES):
    if n <= target:
        return n
    best = None
    for t in range(unit, target + 1, unit):
        if n % t == 0:
            best = t
    return n if best is None else best


ANY_SPEC = pl.BlockSpec(memory_space=pl.ANY)


def _tile_call(fn, grid, ins, in_specs, outs, out_specs, name, acc=(), acc_axis=0, fill=None):
    n_in = len(ins)
    n_out = len(outs)
    acc = tuple(acc)
    aliases = {}
    if fill is not None:
        aliases = {n_in: fill[1]}
        ins = list(ins) + [fill[0]]
        in_specs = list(in_specs) + [ANY_SPEC]

    def body(*refs):
        refs = refs[:n_in] + refs[len(ins):]
        vals = fn(*[r[...] for r in refs[:n_in]])
        if not isinstance(vals, (tuple, list)):
            vals = (vals,)
        for k in range(n_out):
            r = refs[n_in + k]
            v = vals[k].astype(r.dtype)
            if k in acc:
                first = pl.program_id(acc_axis) == 0

                @pl.when(first)
                def _(r=r, v=v):
                    r[...] = v

                @pl.when(jnp.logical_not(first))
                def _(r=r, v=v):
                    r[...] += v
            else:
                r[...] = v

    res = pl.pallas_call(
        body, out_shape=[jax.ShapeDtypeStruct(s, d) for s, d in outs], grid=grid,
        in_specs=in_specs, out_specs=out_specs, name=name, compiler_params=_cparams(len(grid)),
        input_output_aliases=aliases,
    )(*ins)
    return res


def _rows(tm, c):
    return pl.BlockSpec((tm, c), lambda i: (i, 0))


def _whole(shape):
    nd = len(shape)
    return pl.BlockSpec(shape, lambda *a: (0,) * nd)


def _rowwise(fn, rows, params, outs, accs, tm, name):
    lp = rows[0].shape[0]
    n_row_out = len(outs)
    res = _tile_call(
        fn, (lp // tm,), list(rows) + list(params),
        [_rows(tm, r.shape[1]) for r in rows] + [_whole(p.shape) for p in params],
        [((lp, c), dt) for c, dt in outs] + [(s, F32) for s in accs],
        [_rows(tm, c) for c, _ in outs] + [_whole(s) for s in accs],
        name, acc=range(n_row_out, n_row_out + len(accs)))
    return res


_DIMS = {"nn": (((1,), (0,)), ((), ())), "nt": (((1,), (1,)), ((), ())), "tn": (((0,), (0,)), ((), ()))}


def _dot(a, b, mode="nn", precision=None):
    return lax.dot_general(a, b, _DIMS[mode], preferred_element_type=F32, precision=precision)


def _bdot(a, b, mode="nn"):
    return _dot(a.astype(BF16), b.astype(BF16), mode)


def _mm(a, b, mode, name, out_dtype=F32, out_slabs=None, b_col0=0, b_cols=None, tm_t=640, tn_t=1536, tk_t=2048):
    slab_b = b.ndim == 3
    if mode == "nn":
        m, k = a.shape
        k2, n, unit_n = (b.shape[1], b.shape[0] * b.shape[2], b.shape[2]) if slab_b else (b.shape[0], b.shape[1], b.shape[1])
        unit_k = k
    elif mode == "nt":
        m, k = a.shape
        n, k2, unit_k = (b.shape[1], b.shape[0] * b.shape[2], b.shape[2]) if slab_b else (b.shape[0], b.shape[1], b.shape[1])
        unit_n = n
    else:
        (k, m), k2 = a.shape, b.shape[0]
        n = b.shape[1] if b_cols is None else b_cols
        unit_n, unit_k = n, k
        tm_t = max(tm_t, 1024)
        tk_t = 1664 if a.dtype == BF16 and b.dtype == BF16 else 1024
    if out_slabs:
        unit_n = n // out_slabs
    assert k == k2, (name, a.shape, b.shape)
    tm, tn, tk = _div_tile(m, tm_t), _div_tile(unit_n, tn_t), _div_tile(unit_k, tk_t)
    nk = k // tk
    nps, kps = unit_n // tn, unit_k // tk
    c0 = b_col0 // tn
    assert b_col0 % tn == 0
    a_spec = {"nn": pl.BlockSpec((tm, tk), lambda i, j, kk: (i, kk)),
              "nt": pl.BlockSpec((tm, tk), lambda i, j, kk: (i, kk)),
              "tn": pl.BlockSpec((tk, tm), lambda i, j, kk: (kk, i))}[mode]
    if slab_b:
        b_spec = {"nn": pl.BlockSpec((None, tk, tn), lambda i, j, kk: (j // nps, kk, j % nps)),
                  "nt": pl.BlockSpec((None, tn, tk), lambda i, j, kk: (kk // kps, j, kk % kps))}[mode]
    else:
        b_spec = {"nn": pl.BlockSpec((tk, tn), lambda i, j, kk: (kk, j)),
                  "nt": pl.BlockSpec((tn, tk), lambda i, j, kk: (j, kk)),
                  "tn": pl.BlockSpec((tk, tn), lambda i, j, kk: (kk, j + c0))}[mode]
    if out_slabs:
        out_shape = jax.ShapeDtypeStruct((out_slabs, m, unit_n), out_dtype)
        out_spec = pl.BlockSpec((None, tm, tn), lambda i, j, kk: (j // nps, i, j % nps))
    else:
        out_shape = jax.ShapeDtypeStruct((m, n), out_dtype)
        out_spec = pl.BlockSpec((tm, tn), lambda i, j, kk: (i, j))

    def body(a_ref, b_ref, o_ref, acc_ref):
        part = _bdot(a_ref[...], b_ref[...], mode)
        if nk == 1:
            o_ref[...] = part.astype(o_ref.dtype)
        else:
            kk = pl.program_id(2)

            @pl.when(kk == 0)
            def _():
                acc_ref[...] = part

            @pl.when(kk > 0)
            def _():
                acc_ref[...] += part

            @pl.when(kk == nk - 1)
            def _():
                o_ref[...] = acc_ref[...].astype(o_ref.dtype)

    return pl.pallas_call(
        body, out_shape=out_shape, grid=(m // tm, n // tn, nk),
        in_specs=[a_spec, b_spec], out_specs=out_spec,
        scratch_shapes=[pltpu.VMEM((tm, tn) if nk > 1 else (SUBLANES, LANES), F32)],
        name=name, compiler_params=_cparams(3))(a, b)


def _layer_norm(r, g, b):
    mu = jnp.mean(r, axis=-1, keepdims=True)
    xc = r - mu
    var = jnp.mean(xc * xc, axis=-1, keepdims=True)
    return xc * lax.rsqrt(var + LN_EPS) * g + b


def _rms(x, g):
    return x * lax.rsqrt(jnp.mean(x * x, axis=-1, keepdims=True) + RMS_EPS) * g


def _sigmoid(x):
    return 1.0 / (1.0 + jnp.exp(-x))


def _silu(x):
    return x * _sigmoid(x)


def _gelu(x):
    return 0.5 * x * (1.0 + jnp.tanh(0.7978845608028654 * (x + 0.044715 * x * x * x)))


def _softplus(x):
    return jnp.maximum(x, 0.0) + jnp.log(1.0 + jnp.exp(-jnp.abs(x)))


def _ln_fwd(h, mo, g, b, cfg, name):
    def fn(h, mo, g, b):
        y = _layer_norm(DN_ALPHA * h + mo, g, b)
        return y, y
    return _rowwise(fn, [h, mo], [g, b], [(cfg.d, F32), (cfg.d, BF16)], [], cfg.tr, name)


def _ln_bwd(h, mo, g, b, douts, scales, cfg, name):
    def fn(h, mo, *rest):
        ds, (g, b) = rest[:-2], rest[-2:]
        dy = ds[0] * scales[0]
        for t, s in zip(ds[1:], scales[1:]):
            dy = dy + t * s
        _, vjp = jax.vjp(_layer_norm, DN_ALPHA * h + mo, g, b)
        dr, dg, db = vjp(dy)
        return dr, dr, dg, db
    d = cfg.d
    return _rowwise(fn, [h, mo] + list(douts), [g, b], [(d, F32), (d, BF16)], [(1, d), (1, d)], cfg.tr, name)


def _ffn_act(gu, cfg, name):
    f = cfg.ffn
    cb = _div_tile(f, 1536)
    nf = f // cb
    lp = gu.shape[0]
    tm = cfg.tr

    def fn(gate, up):
        return _silu(gate.astype(F32)) * up.astype(F32)
    return _tile_call(fn, (lp // tm, nf), [gu, gu],
                      [pl.BlockSpec((tm, cb), lambda i, j: (i, j)),
                       pl.BlockSpec((tm, cb), lambda i, j: (i, j + nf))],
                      [((lp, f), BF16)], [pl.BlockSpec((tm, cb), lambda i, j: (i, j))], name)[0]


def _ffn_act_bwd(gu, dact, cfg, name):
    f = cfg.ffn
    cb = _div_tile(f, 1536)
    nf = f // cb
    lp = gu.shape[0]
    tm = cfg.tr

    def fn(gate, up, da):
        gate, up, da = gate.astype(F32), up.astype(F32), da.astype(F32)
        sg = _sigmoid(gate)
        dgate = da * up * sg * (1.0 + gate * (1.0 - sg))
        dup = da * gate * sg
        return jnp.where(pl.program_id(1) < nf, dgate, dup)
    return _tile_call(fn, (lp // tm, 2 * nf), [gu, gu, dact],
                      [pl.BlockSpec((tm, cb), lambda i, j: (i, j % nf)),
                       pl.BlockSpec((tm, cb), lambda i, j: (i, j % nf + nf)),
                       pl.BlockSpec((tm, cb), lambda i, j: (i, j % nf))],
                      [((lp, 2 * f), BF16)], [pl.BlockSpec((tm, cb), lambda i, j: (i, j))], name)[0]


def _ffn_fwd(h, w_gu, w_down, cfg, tag):
    gu = _mm(h, w_gu, "nn", tag + "_gu", out_dtype=BF16)
    act = _ffn_act(gu, cfg, tag + "_act")
    fo = _mm(act, w_down, "nn", tag + "_down")
    return fo, (gu, act)


def _ffn_bwd(h, saved, dfo, w_gu, w_down, cfg, tag):
    gu, act = saved
    dact = _mm(dfo, w_down, "nt", tag + "_dact", out_dtype=BF16)
    d_wdown = _mm(act, dfo, "tn", tag + "_dwdown", out_dtype=BF16)
    dgu = _ffn_act_bwd(gu, dact, cfg, tag + "_dgu")
    dh = _mm(dgu, w_gu, "nt", tag + "_dh")
    d_wgu = _mm(h, dgu, "tn", tag + "_dwgu", out_dtype=BF16, out_slabs=w_gu.shape[0])
    return dh, d_wgu, d_wdown


def _small_call(fn, ins, outs, name):
    return _tile_call(fn, (1,), ins, [_whole(x.shape) for x in ins], [(s, F32) for s in outs],
                      [_whole(s) for s in outs], name)


def _perm(x):
    lp, c = x.shape
    return x.reshape(SUBLANES, lp // SUBLANES, c).transpose(1, 0, 2).reshape(lp, c)


def _unperm(x):
    lp, c = x.shape
    return x.reshape(lp // SUBLANES, SUBLANES, c).transpose(1, 0, 2).reshape(lp, c)


def _s5_disc(log_dt, a_re, a_im):
    dt = jnp.exp(log_dt)
    mag = jnp.exp(dt * a_re)
    ab_re = mag * jnp.cos(dt * a_im)
    ab_im = mag * jnp.sin(dt * a_im)
    den = a_re * a_re + a_im * a_im
    nr = ab_re - 1.0
    f_re = (nr * a_re + ab_im * a_im) / den
    f_im = (ab_im * a_re - nr * a_im) / den
    return ab_re, ab_im, f_re, f_im


def _s5_bbar(f_re, f_im, b_re, b_im):
    return f_re * b_re - f_im * b_im, f_re * b_im + f_im * b_re


def _bd_from(w, cfg):
    g, p, j = w.shape
    w4 = w.reshape(cfg.s5_nb, S5_GPB, p, j)
    eye = jnp.eye(S5_GPB, dtype=w.dtype)
    return jnp.einsum("bgpj,gh->bgjhp", w4, eye).reshape(cfg.s5_nb, S5_GPB * j, S5_GPB * p)


def _bd_to(blocks, cfg, p, j):
    b5 = blocks.reshape(cfg.s5_nb, S5_GPB, j, S5_GPB, p)
    eye = jnp.eye(S5_GPB, dtype=blocks.dtype)
    return jnp.einsum("bgjhp,gh->bgpj", b5, eye).reshape(cfg.s5_g, p, j)


def _bd_split(x, w1, w2, cfg, name):
    nb, ci, co = w1.shape
    lp, tm = x.shape[0], cfg.tm

    def fn(x, w1, w2):
        xb = x.astype(BF16)
        return _dot(xb, w1[0].astype(BF16)), _dot(xb, w2[0].astype(BF16))
    wspec = pl.BlockSpec((1, ci, co), lambda i, j: (j, 0, 0))
    ospec = pl.BlockSpec((tm, co), lambda i, j: (i, j))
    return _tile_call(fn, (lp // tm, nb), [x, w1, w2],
                      [pl.BlockSpec((tm, ci), lambda i, j: (i, j)), wspec, wspec],
                      [((lp, nb * co), F32)] * 2, [ospec, ospec], name)


def _bd_join(x1, x2, w1, w2, extra, scale, cfg, name):
    nb, ci, co = w1.shape
    lp, tm = x1.shape[0], cfg.tm

    def fn(x1, x2, w1, w2, e, s):
        return _bdot(x1, w1[0]) + _bdot(x2, w2[0]) + e * s
    xspec = pl.BlockSpec((tm, ci), lambda i, j: (i, j))
    wspec = pl.BlockSpec((1, ci, co), lambda i, j: (j, 0, 0))
    ospec = pl.BlockSpec((tm, co), lambda i, j: (i, j))
    return _tile_call(fn, (lp // tm, nb), [x1, x2, w1, w2, extra, scale],
                      [xspec, xspec, wspec, wspec, ospec, pl.BlockSpec((1, co), lambda i, j: (0, j))],
                      [((lp, nb * co), F32)], [ospec], name)[0]


def _bd_tn(a, b, nb, cfg, name):
    lp, tk = a.shape[0], cfg.tm
    ca, cb = a.shape[1] // nb, b.shape[1] // nb

    def fn(a, b):
        return _bdot(a, b, "tn")[None]
    return _tile_call(fn, (nb, lp // tk), [a, b],
                      [pl.BlockSpec((tk, ca), lambda j, k: (k, j)), pl.BlockSpec((tk, cb), lambda j, k: (k, j))],
                      [((nb, ca, cb), F32)], [pl.BlockSpec((1, ca, cb), lambda j, k: (j, 0, 0))],
                      name, acc=(0,), acc_axis=1)[0]


def _s5_scan(bu_re, bu_im, ab_re, ab_im, cfg, reverse, tag):
    lp, c = bu_re.shape
    nt = cfg.scan_tiles
    rows = lp // nt
    steps = rows // SUBLANES
    cb = _div_tile(c, 512)
    tmap = (lambda j, t: (nt - 1 - t, j)) if reverse else (lambda j, t: (t, j))
    row_spec = pl.BlockSpec((rows, cb), tmap)
    par_spec = pl.BlockSpec((1, cb), lambda j, t: (0, j))
    st_spec = pl.BlockSpec((SUBLANES, cb), lambda j, t: (0, j))
    grid = (c // cb, nt)
    full = jax.ShapeDtypeStruct((lp, c), F32)
    small = jax.ShapeDtypeStruct((SUBLANES, c), F32)

    def offset(k):
        kk = steps - 1 - k if reverse else k
        return pl.multiple_of(kk * SUBLANES, SUBLANES)

    def local_body(bre, bim, ar_ref, ai_ref, sre, sim, fre, fim, pre, pim, st):
        t = pl.program_id(1)

        @pl.when(t == 0)
        def _():
            zero = jnp.zeros((SUBLANES, cb), F32)
            st[0] = zero
            st[1] = zero
            st[2] = zero + 1.0
            st[3] = zero
        ar = jnp.broadcast_to(ar_ref[...], (SUBLANES, cb))
        ai = jnp.broadcast_to(ai_ref[...], (SUBLANES, cb))

        def step(k, carry):
            s_r, s_i, p_r, p_i = carry
            off = offset(k)
            n_r = ar * s_r - ai * s_i + bre[pl.ds(off, SUBLANES), :]
            n_i = ar * s_i + ai * s_r + bim[pl.ds(off, SUBLANES), :]
            sre[pl.ds(off, SUBLANES), :] = n_r
            sim[pl.ds(off, SUBLANES), :] = n_i
            return n_r, n_i, ar * p_r - ai * p_i, ar * p_i + ai * p_r
        s_r, s_i, p_r, p_i = lax.fori_loop(0, steps, step, (st[0], st[1], st[2], st[3]))
        st[0] = s_r
        st[1] = s_i
        st[2] = p_r
        st[3] = p_i

        @pl.when(t == nt - 1)
        def _():
            fre[...] = s_r
            fim[...] = s_i
            pre[...] = p_r
            pim[...] = p_i

    loc_re, loc_im, f_re, f_im, pn_re, pn_im = pl.pallas_call(
        local_body, out_shape=[full, full, small, small, small, small], grid=grid,
        in_specs=[row_spec, row_spec, par_spec, par_spec],
        out_specs=[row_spec, row_spec, st_spec, st_spec, st_spec, st_spec],
        scratch_shapes=[pltpu.VMEM((4, SUBLANES, cb), F32)],
        name=tag + "_local", compiler_params=_cparams(2))(bu_re, bu_im, ab_re, ab_im)

    def fix_body(lre, lim, fre, fim, pre, pim, ar_ref, ai_ref, sre, sim, st):
        t = pl.program_id(1)
        ar = jnp.broadcast_to(ar_ref[...], (SUBLANES, cb))
        ai = jnp.broadcast_to(ai_ref[...], (SUBLANES, cb))

        @pl.when(t == 0)
        def _():
            f_r, f_i = fre[...], fim[...]
            n_r, n_i = pre[0:1, :], pim[0:1, :]
            row = lax.broadcasted_iota(jnp.int32, (SUBLANES, cb), 0)
            c_r = jnp.zeros((1, cb), F32)
            c_i = jnp.zeros((1, cb), F32)
            car_r = jnp.zeros((SUBLANES, cb), F32)
            car_i = jnp.zeros((SUBLANES, cb), F32)
            order = range(SUBLANES - 2, -1, -1) if reverse else range(1, SUBLANES)
            for i in order:
                src = i + 1 if reverse else i - 1
                c_r, c_i = (n_r * c_r - n_i * c_i + f_r[src:src + 1, :],
                            n_r * c_i + n_i * c_r + f_i[src:src + 1, :])
                car_r = jnp.where(row == i, c_r, car_r)
                car_i = jnp.where(row == i, c_i, car_i)
            st[0] = car_r
            st[1] = car_i
            st[2] = ar
            st[3] = ai
        car_r = st[0]
        car_i = st[1]

        def step(k, carry):
            p_r, p_i = carry
            off = offset(k)
            sre[pl.ds(off, SUBLANES), :] = lre[pl.ds(off, SUBLANES), :] + p_r * car_r - p_i * car_i
            sim[pl.ds(off, SUBLANES), :] = lim[pl.ds(off, SUBLANES), :] + p_r * car_i + p_i * car_r
            return ar * p_r - ai * p_i, ar * p_i + ai * p_r
        p_r, p_i = lax.fori_loop(0, steps, step, (st[2], st[3]))
        st[2] = p_r
        st[3] = p_i

    return pl.pallas_call(
        fix_body, out_shape=[full, full], grid=grid,
        in_specs=[row_spec, row_spec, st_spec, st_spec, st_spec, st_spec, par_spec, par_spec],
        out_specs=[row_spec, row_spec], scratch_shapes=[pltpu.VMEM((4, SUBLANES, cb), F32)],
        name=tag + "_fix", compiler_params=_cparams(2))(loc_re, loc_im, f_re, f_im, pn_re, pn_im, ab_re, ab_im)


def _swap(x):
    return jnp.swapaxes(x, -1, -2)


def _s5_prep(w, cfg):
    g, p, j = cfg.s5_g, S5_STATE, S5_GROUP
    gp = g * p
    log_dt = w["l0_s5_log_dt"].reshape(g, 1)
    ab_re, ab_im, f_re, f_im = _small_call(_s5_disc, [log_dt, w["l0_s5_a_re"], w["l0_s5_a_im"]],
                                           [(g, p)] * 4, "s5_disc")
    b_re2 = w["l0_s5_b_re"].transpose(2, 0, 1).reshape(j, gp)
    b_im2 = w["l0_s5_b_im"].transpose(2, 0, 1).reshape(j, gp)
    f_re1, f_im1 = f_re.reshape(1, gp), f_im.reshape(1, gp)
    bb_re2, bb_im2 = _small_call(_s5_bbar, [f_re1, f_im1, b_re2, b_im2], [(j, gp)] * 2, "s5_bbar")
    bb_re = _bd_from(bb_re2.reshape(j, g, p).transpose(1, 2, 0), cfg).astype(BF16)
    bb_im = _bd_from(bb_im2.reshape(j, g, p).transpose(1, 2, 0), cfg).astype(BF16)
    c_re_t = _bd_from(w["l0_s5_c_re"].transpose(0, 2, 1), cfg).astype(BF16)
    c_imn_t = _bd_from(-w["l0_s5_c_im"].transpose(0, 2, 1), cfg).astype(BF16)
    return dict(log_dt=log_dt, f_re1=f_re1, f_im1=f_im1, b_re2=b_re2, b_im2=b_im2,
                ab_re=ab_re.reshape(1, gp), ab_im=ab_im.reshape(1, gp),
                bb_re=bb_re, bb_im=bb_im, bb_re_t=_swap(bb_re), bb_im_t=_swap(bb_im),
                c_re=_swap(c_re_t), c_imn=_swap(c_imn_t), c_re_t=c_re_t, c_imn_t=c_imn_t,
                d=w["l0_s5_d"].reshape(1, cfg.s5_w))


def _s5_fwd(u, prm, w_glu, cfg):
    tm = cfg.tr
    up = _perm(u)
    bu_re, bu_im = _bd_split(up, prm["bb_re"], prm["bb_im"], cfg, "s5_bu")
    s_re, s_im = _s5_scan(bu_re, bu_im, prm["ab_re"], prm["ab_im"], cfg, False, "s5_scan")
    y = _bd_join(s_re, s_im, prm["c_re"], prm["c_imn"], up, prm["d"], cfg, "s5_y")
    g = _rowwise(_gelu, [y], [], [(cfg.s5_w, F32)], [], tm, "s5_gelu")[0]
    z = _mm(g, w_glu, "nn", "s5_glu_mm")
    a_out = _rowwise(lambda g, z: g * _sigmoid(z), [g, z], [], [(cfg.s5_w, F32)], [], tm, "s5_glu")[0]
    return _unperm(a_out), (up, s_re, s_im, y, g, z)


def _s5_bwd(d_a_out, saved, prm, w, w_glu, cfg):
    up, s_re, s_im, y, g, z = saved
    tm, sw, nb = cfg.tr, cfg.s5_w, cfg.s5_nb
    gs, p, j = cfg.s5_g, S5_STATE, S5_GROUP
    gp = gs * p
    dap = _perm(d_a_out)

    def glu_bwd(da, g, z):
        sg = _sigmoid(z)
        return da * sg, da * g * sg * (1.0 - sg)
    dg1, dz = _rowwise(glu_bwd, [dap, g, z], [], [(sw, F32)] * 2, [], tm, "s5_glu_bwd")
    d_wglu = _mm(g, dz, "tn", "s5_dwglu", out_dtype=BF16)
    dg2 = _mm(dz, w_glu, "nt", "s5_dg2")

    def gelu_bwd(dg1, dg2, y, up, d):
        _, vjp = jax.vjp(_gelu, y)
        dy = vjp(dg1 + dg2)[0]
        return dy, dy * d, jnp.sum(dy * up, axis=0, keepdims=True)
    dy, dup_direct, dd = _rowwise(gelu_bwd, [dg1, dg2, y, up], [prm["d"]], [(sw, F32)] * 2, [(1, sw)], tm,
                                  "s5_gelu_bwd")
    ds_re, ds_im = _bd_split(dy, prm["c_re_t"], prm["c_imn_t"], cfg, "s5_ds")
    dc_re_t = _bd_tn(dy, s_re, nb, cfg, "s5_dcre")
    dc_imn_t = _bd_tn(dy, s_im, nb, cfg, "s5_dcim")
    g_re, g_im = _s5_scan(ds_re, ds_im, prm["ab_re"], -prm["ab_im"], cfg, True, "s5_adj")

    cb = _div_tile(gp, 512)
    per = tm // SUBLANES
    spec = pl.BlockSpec((tm, cb), lambda jj, i: (i, jj))
    before = pl.BlockSpec((SUBLANES, cb), lambda jj, i: (jnp.maximum(i * per - 1, 0), jj))
    final = pl.BlockSpec((SUBLANES, cb), lambda jj, i: (cfg.lp // SUBLANES - 1, jj))
    aspec = pl.BlockSpec((1, cb), lambda jj, i: (0, jj))

    def dab(g_r, g_i, s_r, s_i, h_r, h_i, l_r, l_i):
        first = pl.program_id(1) == 0
        row8 = lax.broadcasted_iota(jnp.int32, (SUBLANES, cb), 0)

        def prev(s, h, l):
            wrap = jnp.where(row8 == 0, 0.0, pltpu.roll(l, 1, axis=0))
            return jnp.concatenate([jnp.where(first, wrap, h), s[:tm - SUBLANES]], axis=0)
        p_r, p_i = prev(s_r, h_r, l_r), prev(s_i, h_i, l_i)
        return (jnp.sum(g_r * p_r + g_i * p_i, axis=0, keepdims=True),
                jnp.sum(g_i * p_r - g_r * p_i, axis=0, keepdims=True))
    dab_re, dab_im = _tile_call(dab, (gp // cb, cfg.lp // tm), [g_re, g_im, s_re, s_im, s_re, s_im, s_re, s_im],
                                [spec] * 4 + [before] * 2 + [final] * 2,
                                [((1, gp), F32)] * 2, [aspec] * 2, "s5_dab", acc=(0, 1), acc_axis=1)
    no_scale = jnp.ones((1, sw), F32)
    dup = _bd_join(g_re, g_im, prm["bb_re_t"], prm["bb_im_t"], dup_direct, no_scale, cfg, "s5_dup")
    dbb_re_blk = _bd_tn(up, g_re, nb, cfg, "s5_dbbre")
    dbb_im_blk = _bd_tn(up, g_im, nb, cfg, "s5_dbbim")

    def to2(blk):
        return _bd_to(blk, cfg, p, j).transpose(2, 0, 1).reshape(j, gp)

    def bbar_bwd(f_re, f_im, b_re, b_im, dr, di):
        _, vjp = jax.vjp(_s5_bbar, f_re, f_im, b_re, b_im)
        return vjp((dr, di))
    df_re, df_im, db_re2, db_im2 = _small_call(
        bbar_bwd, [prm["f_re1"], prm["f_im1"], prm["b_re2"], prm["b_im2"], to2(dbb_re_blk), to2(dbb_im_blk)],
        [(1, gp), (1, gp), (j, gp), (j, gp)], "s5_bbar_bwd")

    def disc_bwd(log_dt, a_re, a_im, d1, d2, d3, d4):
        _, vjp = jax.vjp(_s5_disc, log_dt, a_re, a_im)
        return vjp((d1, d2, d3, d4))
    dlog_dt, da_re, da_im = _small_call(
        disc_bwd, [prm["log_dt"], w["l0_s5_a_re"], w["l0_s5_a_im"], dab_re.reshape(gs, p), dab_im.reshape(gs, p),
                   df_re.reshape(gs, p), df_im.reshape(gs, p)], [(gs, 1), (gs, p), (gs, p)], "s5_disc_bwd")
    grads = {
        "l0_s5_log_dt": dlog_dt.reshape(gs), "l0_s5_a_re": da_re, "l0_s5_a_im": da_im,
        "l0_s5_b_re": db_re2.reshape(j, gs, p).transpose(1, 2, 0),
        "l0_s5_b_im": db_im2.reshape(j, gs, p).transpose(1, 2, 0),
        "l0_s5_c_re": _bd_to(dc_re_t, cfg, p, j).transpose(0, 2, 1),
        "l0_s5_c_im": -_bd_to(dc_imn_t, cfg, p, j).transpose(0, 2, 1),
        "l0_s5_d": dd.reshape(sw), "l0_s5_w_glu": d_wglu,
    }
    return _unperm(dup), grads


def _shift(x):
    return jnp.concatenate([jnp.zeros((ATT_SHIFT, x.shape[1]), x.dtype), x[:-ATT_SHIFT]], axis=0)


def _unshift(x):
    return jnp.concatenate([x[ATT_SHIFT:], jnp.zeros((ATT_SHIFT, x.shape[1]), x.dtype)], axis=0)


def _rope_tables(cfg):
    pos = (jnp.arange(cfg.lp) - ATT_SHIFT).astype(F32)
    inv = ROPE_BASE ** (-jnp.arange(0, MLA_ROPE, 2, dtype=F32) / MLA_ROPE)
    ang = pos[:, None] * inv[None, :]
    cos, sin = jnp.cos(ang), jnp.sin(ang)
    z = jnp.zeros((cfg.lp, LANES - MLA_ROPE), F32)
    return jnp.concatenate([cos, cos, z], axis=1), jnp.concatenate([-sin, sin, z], axis=1)


def _swap_halves(x):
    half = MLA_ROPE // 2
    lane = lax.broadcasted_iota(jnp.int32, x.shape, 1)
    left = pltpu.roll(x, LANES - half, axis=1)
    right = pltpu.roll(x, half, axis=1)
    return jnp.where(lane < half, left, jnp.where(lane < 2 * half, right, 0.0))


def _rope(x, cosp, sinp):
    return x * cosp + _swap_halves(x) * sinp


def _rope_t(dy, cosp, sinp):
    return dy * cosp + _swap_halves(dy * sinp)


def _visible(i, j, t):
    row = i * t + lax.broadcasted_iota(jnp.int32, (t, t), 0)
    col = j * t + lax.broadcasted_iota(jnp.int32, (t, t), 1)
    return jnp.logical_and(col // CHUNK <= row // CHUNK, col >= ATT_SHIFT)


class _NoPlan:
    n = n_out = 0
    arrays, out_shapes, scratch = [], [], []


def _side_refs(refs, n_in, n_out, n_scratch, side):
    a = n_in + side.n
    b = a + n_out + side.n_out
    c = b + n_scratch
    own = refs[:n_in] + refs[a:a + n_out] + refs[b:c]
    return own, refs[n_in:a] + refs[a + n_out:b] + refs[c:]


def _flash_fwd(q, kv, kr, cfg, side=None):
    lp, t, nh = cfg.lp, cfg.tq, cfg.heads
    n = lp // t
    scale = (MLA_NOPE + MLA_ROPE) ** -0.5
    side = side or _NoPlan()

    def body(*refs):
        (q_ref, kv_ref, kr_ref, o_ref, lse_ref, m_s, l_s, acc_s), ex = _side_refs(refs, 3, 2, 3, side)
        hh, i, j = pl.program_id(0), pl.program_id(1), pl.program_id(2)
        if side.n:
            at_tile0 = jnp.logical_and(i == 0, j == 0)
            pl.when(jnp.logical_and(hh == 0, at_tile0))(lambda: side.start(ex))
            pl.when(jnp.logical_and(hh == nh // 2, at_tile0))(lambda: side.relay(ex))

        @pl.when(j == 0)
        def _():
            m_s[...] = jnp.full((t, 1), NEG, F32)
            l_s[...] = jnp.zeros((t, 1), F32)
            acc_s[...] = jnp.zeros((t, MLA_V), F32)

        @pl.when(j <= i)
        def _():
            s = (_dot(q_ref[:, :MLA_NOPE], kv_ref[:, :MLA_NOPE], "nt")
                 + _dot(q_ref[:, MLA_NOPE:], kr_ref[...], "nt")) * scale
            s = jnp.where(_visible(i, j, t), s, NEG)
            m_old = m_s[...]
            m_new = jnp.maximum(m_old, jnp.max(s, axis=1, keepdims=True))
            alpha = jnp.exp(m_old - m_new)
            p = jnp.exp(s - m_new)
            l_s[...] = alpha * l_s[...] + jnp.sum(p, axis=1, keepdims=True)
            acc_s[...] = alpha * acc_s[...] + _dot(p.astype(BF16), kv_ref[:, MLA_NOPE:])
            m_s[...] = m_new

        @pl.when(j == i)
        def _():
            o_ref[...] = acc_s[...] / l_s[...]
            lse_ref[...] = jnp.broadcast_to(m_s[...] + jnp.log(l_s[...]), (t, MLA_V))

        if side.n:
            pl.when(jnp.logical_and(hh == nh - 1, jnp.logical_and(i == n - 1, j == n - 1)))(lambda: side.finish(ex))

    res = pl.pallas_call(
        body, out_shape=[jax.ShapeDtypeStruct((lp, nh * MLA_V), F32)] * 2 + side.out_shapes, grid=(nh, n, n),
        in_specs=[pl.BlockSpec((t, MLA_QW), lambda h, i, j: (i, h)),
                  pl.BlockSpec((t, MLA_QW), lambda h, i, j: (jnp.minimum(i, j), h)),
                  pl.BlockSpec((t, LANES), lambda h, i, j: (jnp.minimum(i, j), 0))] + [HBM_SPEC] * side.n,
        out_specs=[pl.BlockSpec((t, MLA_V), lambda h, i, j: (i, h))] * 2 + [HBM_SPEC] * side.n_out,
        scratch_shapes=[pltpu.VMEM((t, 1), F32), pltpu.VMEM((t, 1), F32), pltpu.VMEM((t, MLA_V), F32)] + side.scratch,
        name="mla_flash_fwd", compiler_params=_cparams(3))(q, kv, kr, *side.arrays)
    return res[0], res[1], res[2:]


def _flash_bwd(q, kv, kr, o, lse, do, cfg, side=None):
    lp, t, nh = cfg.lp, cfg.tq, cfg.heads
    n = lp // t
    scale = (MLA_NOPE + MLA_ROPE) ** -0.5
    side = side or _NoPlan()

    def body(*refs):
        own, ex = _side_refs(refs, 6, 3, 2, side)
        q_ref, kv_ref, kr_ref, o_ref, lse_ref, do_ref, dq_ref, dkv_ref, dkr_ref, dkv_s, dkr_s = own
        hh, j, i = pl.program_id(0), pl.program_id(1), pl.program_id(2)
        if side.n:
            pl.when(jnp.logical_and(hh == 0, jnp.logical_and(i == 0, j == 0)))(lambda: side.start(ex))

        @pl.when(jnp.logical_and(j == 0, i == 0))
        def _():
            dq_ref[...] = jnp.zeros((lp, MLA_QW), F32)

        @pl.when(i == j)
        def _():
            dkv_s[...] = jnp.zeros((t, MLA_QW), F32)
            dkr_s[...] = jnp.zeros((t, LANES), F32)

        @pl.when(i >= j)
        def _():
            qn, qr = q_ref[:, :MLA_NOPE], q_ref[:, MLA_NOPE:]
            kn, v = kv_ref[:, :MLA_NOPE], kv_ref[:, MLA_NOPE:]
            krv = kr_ref[...]
            s = (_dot(qn, kn, "nt") + _dot(qr, krv, "nt")) * scale
            p = jnp.where(_visible(i, j, t), jnp.exp(s - lse_ref[:, :1]), 0.0)
            dov = do_ref[...]
            dob = dov.astype(BF16)
            dp = _dot(dob, v, "nt")
            delta = jnp.sum(dov * o_ref[...], axis=1, keepdims=True)
            ds = (p * (dp - delta) * scale).astype(BF16)
            dkv_s[:, MLA_NOPE:] += _dot(p.astype(BF16), dob, "tn")
            dkv_s[:, :MLA_NOPE] += _dot(ds, qn, "tn")
            dkr_s[...] += _dot(ds, qr, "tn")
            off = pl.multiple_of(i * t, t)
            dq_ref[pl.ds(off, t), :MLA_NOPE] += _dot(ds, kn)
            dq_ref[pl.ds(off, t), MLA_NOPE:] += _dot(ds, krv)

        @pl.when(i == n - 1)
        def _():
            dkv_ref[...] = dkv_s[...]
            dkr_ref[0] = dkr_s[...]

        if side.n:
            pl.when(jnp.logical_and(hh == nh - 1, jnp.logical_and(i == n - 1, j == n - 1)))(lambda: side.finish(ex))

    qspec = pl.BlockSpec((t, MLA_QW), lambda h, j, i: (jnp.maximum(i, j), h))
    ospec = pl.BlockSpec((t, MLA_V), lambda h, j, i: (jnp.maximum(i, j), h))
    res = pl.pallas_call(
        body, out_shape=[jax.ShapeDtypeStruct((lp, nh * MLA_QW), F32), jax.ShapeDtypeStruct((lp, nh * MLA_QW), F32),
                         jax.ShapeDtypeStruct((nh, lp, LANES), F32)] + side.out_shapes, grid=(nh, n, n),
        in_specs=[qspec, pl.BlockSpec((t, MLA_QW), lambda h, j, i: (j, h)),
                  pl.BlockSpec((t, LANES), lambda h, j, i: (j, 0)), ospec, ospec, ospec] + [HBM_SPEC] * side.n,
        out_specs=[pl.BlockSpec((lp, MLA_QW), lambda h, j, i: (0, h)),
                   pl.BlockSpec((t, MLA_QW), lambda h, j, i: (j, h)),
                   pl.BlockSpec((1, t, LANES), lambda h, j, i: (h, j, 0))] + [HBM_SPEC] * side.n_out,
        scratch_shapes=[pltpu.VMEM((t, MLA_QW), F32), pltpu.VMEM((t, LANES), F32)] + side.scratch,
        name="mla_flash_bwd", compiler_params=_cparams(3))(q, kv, kr, o, lse, do, *side.arrays)
    return res[0], res[1], res[2], res[3:]


def _pad_heads(w, nh, width):
    r = w.shape[0]
    w3 = w.reshape(r, nh, width)
    return jnp.pad(w3, ((0, 0), (0, 0), (0, MLA_QW - width))).reshape(r, nh * MLA_QW)


def _mla_fwd(q_lat, kv_lat, k_rope_raw, wq, w_uq_p, w_ukv, cfg, side=None):
    tm, nh = cfg.tr, cfg.heads
    ql, kl = _shift(q_lat), _shift(kv_lat)
    kr_raw = jnp.pad(_shift(k_rope_raw), ((0, 0), (0, LANES - MLA_ROPE)))
    cosp, sinp = _rope_tables(cfg)
    qg, kg = wq["l0_mla_q_norm"].reshape(1, -1), wq["l0_mla_kv_norm"].reshape(1, -1)
    qn, kvn = _rowwise(lambda a, b, g1, g2: (_rms(a, g1), _rms(b, g2)), [ql, kl], [qg, kg],
                       [(cfg.q_rank, F32), (cfg.kv_rank, F32)], [], tm, "mla_norm")
    q0 = _mm(qn, w_uq_p, "nn", "mla_q")
    kv = _mm(kvn, w_ukv, "nn", "mla_kv", out_dtype=BF16)

    def rope_fn(q0, kr, cosp, sinp):
        parts = []
        for h in range(nh):
            parts.append(q0[:, h * MLA_QW:h * MLA_QW + MLA_NOPE])
            parts.append(_rope(q0[:, h * MLA_QW + MLA_NOPE:(h + 1) * MLA_QW], cosp, sinp))
        return jnp.concatenate(parts, axis=1), _rope(kr, cosp, sinp)
    q, kr = _rowwise(rope_fn, [q0, kr_raw, cosp, sinp], [], [(nh * MLA_QW, BF16), (LANES, BF16)], [], tm,
                     "mla_rope")
    o, lse, side_out = _flash_fwd(q, kv, kr, cfg, side)
    return _unshift(o), (ql, kl, qn, kvn, q, kv, kr, o, lse, cosp, sinp), side_out


def _mla_bwd(d_b_out, saved, wq, w_uq_p, w_ukv, cfg, side=None):
    ql, kl, qn, kvn, q, kv, kr, o, lse, cosp, sinp = saved
    tm, nh, lp = cfg.tr, cfg.heads, cfg.lp
    dq, dkv, dkr_h, side_out = _flash_bwd(q, kv, kr, o, lse, _shift(d_b_out), cfg, side)

    def rope_bwd(dq, dkr_h, cosp, sinp):
        parts = []
        for h in range(nh):
            parts.append(dq[:, h * MLA_QW:h * MLA_QW + MLA_NOPE])
            parts.append(_rope_t(dq[:, h * MLA_QW + MLA_NOPE:(h + 1) * MLA_QW], cosp, sinp))
        dkr = dkr_h[0]
        for h in range(1, nh):
            dkr = dkr + dkr_h[h]
        return jnp.concatenate(parts, axis=1), _rope_t(dkr, cosp, sinp)
    dq0, dkr_raw = _tile_call(
        rope_bwd, (lp // tm,), [dq, dkr_h, cosp, sinp],
        [_rows(tm, nh * MLA_QW), pl.BlockSpec((nh, tm, LANES), lambda i: (0, i, 0)), _rows(tm, LANES),
         _rows(tm, LANES)],
        [((lp, nh * MLA_QW), F32), ((lp, LANES), F32)], [_rows(tm, nh * MLA_QW), _rows(tm, LANES)], "mla_rope_bwd")
    d_wuq_p = _mm(qn, dq0, "tn", "mla_dwuq")
    dqn = _mm(dq0, w_uq_p, "nt", "mla_dqn")
    d_wukv = _mm(kvn, dkv, "tn", "mla_dwukv")
    dkvn = _mm(dkv, w_ukv, "nt", "mla_dkvn")
    qg, kg = wq["l0_mla_q_norm"].reshape(1, -1), wq["l0_mla_kv_norm"].reshape(1, -1)

    def norm_bwd(ql, kl, dqn, dkvn, g1, g2):
        _, vjp1 = jax.vjp(_rms, ql, g1)
        _, vjp2 = jax.vjp(_rms, kl, g2)
        dql, dg1 = vjp1(dqn)
        dkl, dg2 = vjp2(dkvn)
        return dql, dkl, dg1, dg2
    dql, dkl, dg1, dg2 = _rowwise(norm_bwd, [ql, kl, dqn, dkvn], [qg, kg],
                                  [(cfg.q_rank, F32), (cfg.kv_rank, F32)], [(1, cfg.q_rank), (1, cfg.kv_rank)], tm,
                                  "mla_norm_bwd")
    width = MLA_NOPE + MLA_ROPE
    d_wuq = d_wuq_p.reshape(cfg.q_rank, nh, MLA_QW)[:, :, :width].reshape(cfg.q_rank, nh * width)
    grads = {"l0_mla_q_norm": dg1.reshape(-1), "l0_mla_kv_norm": dg2.reshape(-1), "l0_mla_w_uq": d_wuq,
             "l0_mla_w_ukv": d_wukv}
    return _unshift(dql), _unshift(dkl), _unshift(dkr_raw[:, :MLA_ROPE]), grads, side_out


def _conv_taps(x, halo, first):
    halo = jnp.where(first, 0.0, halo)
    row8 = lax.broadcasted_iota(jnp.int32, halo.shape, 0)
    taps = []
    for s in range(SSD_CONV - 1, 0, -1):
        r = pltpu.roll(x, s, axis=0)
        top = jnp.where(row8 < s, pltpu.roll(halo, s, axis=0), r[:SUBLANES])
        taps.append(jnp.concatenate([top, r[SUBLANES:]], axis=0))
    taps.append(x)
    return taps


def _conv_specs(cfg, lp):
    tm = cfg.tr
    cb = _div_tile(math.gcd(cfg.ssd_inner, cfg.gn), 1024)
    off = cfg.ssd_inner // cb
    per = tm // SUBLANES
    nrow = lp // tm
    main = pl.BlockSpec((tm, cb), lambda i, j: (i, j + off))
    before = pl.BlockSpec((SUBLANES, cb), lambda i, j: (jnp.maximum(i * per - 1, 0), j + off))
    own = pl.BlockSpec((tm, cb), lambda i, j: (i, j))
    after = pl.BlockSpec((SUBLANES, cb), lambda i, j: (jnp.minimum((i + 1) * per, nrow * per - 1), j))
    par = lambda r: pl.BlockSpec((r, cb), lambda i, j: (0, j))
    return tm, cb, nrow, main, before, own, after, par


def _conv_fwd(zx, conv_w, conv_b, cfg):
    lp = zx.shape[0]
    tm, cb, nrow, main, before, own, after, par = _conv_specs(cfg, lp)

    def fn(x, halo, w, b):
        taps = _conv_taps(x, halo, pl.program_id(0) == 0)
        pre = b
        for k in range(SSD_CONV):
            pre = pre + taps[k] * w[k:k + 1, :]
        return _silu(pre)
    return _tile_call(fn, (nrow, cfg.conv_dim // cb), [zx, zx, conv_w, conv_b],
                      [main, before, par(SSD_CONV), par(1)], [((lp, cfg.conv_dim), F32)], [own], "ssd_conv")[0]


def _conv_bwd(zx, conv_w, conv_b, dxs, dbm, dcm, dzx, cfg):
    lp = zx.shape[0]
    tm, cb, nrow, main, before, own, after, par = _conv_specs(cfg, lp)
    ncb = cfg.conv_dim // cb
    nx, nb = cfg.ssd_inner // cb, cfg.gn // cb
    off = nx

    def fn1(x, halo, w, b, d1, d2, d3):
        j = pl.program_id(0)
        da = jnp.where(j < nx, d1, jnp.where(j < nx + nb, d2, d3))
        taps = _conv_taps(x, halo, pl.program_id(1) == 0)
        pre = b
        for k in range(SSD_CONV):
            pre = pre + taps[k] * w[k:k + 1, :]
        sg = _sigmoid(pre)
        dpre = da * sg * (1.0 + pre * (1.0 - sg))
        row8 = lax.broadcasted_iota(jnp.int32, (SUBLANES, cb), 0)
        dw = jnp.zeros((SUBLANES, cb), F32)
        for k in range(SSD_CONV):
            dw = jnp.where(row8 == k, jnp.sum(dpre * taps[k], axis=0, keepdims=True), dw)
        return dpre, dw, jnp.sum(dpre, axis=0, keepdims=True)
    sw = lambda spec: pl.BlockSpec(spec.block_shape, lambda j, i, f=spec.index_map: f(i, j))
    piece = lambda lo, n: pl.BlockSpec((tm, cb), lambda j, i: (i, jnp.clip(j - lo, 0, n - 1)))
    dpre, dw, db = _tile_call(
        fn1, (ncb, nrow), [zx, zx, conv_w, conv_b, dxs, dbm, dcm],
        [sw(main), sw(before), sw(par(SSD_CONV)), sw(par(1)), piece(0, nx), piece(nx, nb), piece(nx + nb, nb)],
        [((lp, cfg.conv_dim), F32), ((SUBLANES, cfg.conv_dim), F32), ((1, cfg.conv_dim), F32)],
        [sw(own), sw(par(SUBLANES)), sw(par(1))], "ssd_conv_bwd1", acc=(1, 2), acc_axis=1)

    def fn2(dp, nxt, w):
        nxt = jnp.where(pl.program_id(0) == nrow - 1, 0.0, nxt)
        row8 = lax.broadcasted_iota(jnp.int32, nxt.shape, 0)
        dx = dp * w[SSD_CONV - 1:SSD_CONV, :]
        for s in range(1, SSD_CONV):
            r = pltpu.roll(dp, tm - s, axis=0)
            bot = jnp.where(row8 >= SUBLANES - s, pltpu.roll(nxt, SUBLANES - s, axis=0), r[tm - SUBLANES:])
            up = jnp.concatenate([r[:tm - SUBLANES], bot], axis=0)
            dx = dx + up * w[SSD_CONV - 1 - s:SSD_CONV - s, :]
        return dx
    dzx = _tile_call(fn2, (nrow, ncb), [dpre, dpre, conv_w], [own, after, par(SSD_CONV)],
                     [(dzx.shape, F32)], [main], "ssd_conv_bwd2", fill=(dzx, 0))[0]
    return dzx, dw, db


def _ssd_common(x_ref, b_ref, c_ref, dt_ref, dtt_ref, ar_ref, ac_ref, h):
    q = SSD_BLOCK
    x, bm, cm = x_ref[...], b_ref[...], c_ref[...]
    dt, dtt = dt_ref[0], dtt_ref[0]
    row = lax.broadcasted_iota(jnp.int32, (q, q), 0)
    col = lax.broadcasted_iota(jnp.int32, (q, q), 1)
    tri = row >= col
    cs = _dot(tri.astype(F32), dt * ar_ref[0], precision=HI)
    cst = _dot(dtt * ac_ref[0], (row <= col).astype(F32), precision=HI)
    g = _bdot(cm, bm, "nt")
    ch = _bdot(cm, h)
    return x, bm, cm, dt, tri, cs, cst, g, ch


def _ssd_specs(cfg, rev):
    q, n, gw, hpg = SSD_BLOCK, SSD_STATE, cfg.gw, cfg.hpg
    nc = cfg.lp // q
    cc = (lambda c: nc - 1 - c) if rev else (lambda c: c)
    boff = cfg.ssd_inner // n
    return dict(
        x=pl.BlockSpec((q, gw), lambda g, c: (cc(c), g)),
        b=pl.BlockSpec((q, n), lambda g, c: (cc(c), boff + g)),
        c=pl.BlockSpec((q, n), lambda g, c: (cc(c), boff + SSD_GROUPS + g)),
        bc_out=pl.BlockSpec((q, n), lambda g, c: (cc(c), g)),
        dt=pl.BlockSpec((1, q, hpg), lambda g, c: (g, cc(c), 0)),
        dtt=pl.BlockSpec((1, hpg, q), lambda g, c: (g, 0, cc(c))),
        ar=pl.BlockSpec((1, 1, hpg), lambda g, c: (g, 0, 0)),
        ac=pl.BlockSpec((1, hpg, 1), lambda g, c: (g, 0, 0)),
        h=pl.BlockSpec((1, n, gw), lambda g, c: (cc(c), 0, g)))


def _ssd_fwd(xbc, dt_g, dtt_g, a_row, a_col, cfg):
    q, n, gw, hpg, lp = SSD_BLOCK, SSD_STATE, cfg.gw, cfg.hpg, cfg.lp
    nc = lp // q
    hd = SSD_HEAD_DIM
    sp = _ssd_specs(cfg, False)

    def body(x_ref, b_ref, c_ref, dt_ref, dtt_ref, ar_ref, ac_ref, y_ref, hp_ref, h_s, xw_s):
        @pl.when(pl.program_id(1) == 0)
        def _():
            h_s[...] = jnp.zeros((n, gw), F32)
        h = h_s[...]
        hp_ref[0] = h
        x, bm, cm, dt, tri, cs, cst, g, ch = _ssd_common(x_ref, b_ref, c_ref, dt_ref, dtt_ref, ar_ref, ac_ref, h)
        lane = lax.broadcasted_iota(jnp.int32, (1, gw), 1) // hd
        dec = jnp.zeros((1, gw), F32)
        for r in range(hpg):
            sl = slice(r * hd, (r + 1) * hd)
            csr = cs[:, r:r + 1]
            lm = jnp.exp(jnp.where(tri, csr - cst[r:r + 1, :], NEG))
            xdt = x[:, sl] * dt[:, r:r + 1]
            last = cs[q - 1:q, r:r + 1]
            y_ref[:, sl] = _bdot(g * lm, xdt) + jnp.exp(csr) * ch[:, sl]
            xw_s[:, sl] = xdt * jnp.exp(last - csr)
            dec = jnp.where(lane == r, jnp.exp(last), dec)
        h_s[...] = h * dec + _bdot(bm, xw_s[...], "tn")

    return pl.pallas_call(
        body, out_shape=[jax.ShapeDtypeStruct((lp, cfg.ssd_inner), F32),
                         jax.ShapeDtypeStruct((nc, n, cfg.ssd_inner), F32)],
        grid=(SSD_GROUPS, nc),
        in_specs=[sp["x"], sp["b"], sp["c"], sp["dt"], sp["dtt"], sp["ar"], sp["ac"]],
        out_specs=[sp["x"], sp["h"]],
        scratch_shapes=[pltpu.VMEM((n, gw), F32), pltpu.VMEM((q, gw), F32)],
        name="ssd_scan", compiler_params=_cparams(2))(xbc, xbc, xbc, dt_g, dtt_g, a_row, a_col)


def _ssd_bwd(xbc, dt_g, dtt_g, a_row, a_col, hprev, dy, dx_gate, cfg):
    q, n, gw, hpg, lp = SSD_BLOCK, SSD_STATE, cfg.gw, cfg.hpg, cfg.lp
    nc = lp // q
    hd = SSD_HEAD_DIM
    sp = _ssd_specs(cfg, True)

    def body(x_ref, b_ref, c_ref, dt_ref, dtt_ref, ar_ref, ac_ref, hp_ref, dy_ref, dxg_ref,
             dx_ref, db_ref, dc_ref, ddt_ref, da_ref, dh_s, xw_s, dye_s, colp_s, rowp_s, xs_s):
        @pl.when(pl.program_id(1) == 0)
        def _():
            dh_s[...] = jnp.zeros((n, gw), F32)
            da_ref[...] = jnp.zeros((1, 1, hpg), F32)
        h = hp_ref[0]
        dhn = dh_s[...]
        dy = dy_ref[...]
        x, bm, cm, dt, tri, cs, cst, g, ch = _ssd_common(x_ref, b_ref, c_ref, dt_ref, dtt_ref, ar_ref, ac_ref, h)
        bd = _bdot(bm, dhn)
        hh = jnp.sum(dhn * h, axis=0, keepdims=True)
        lane = lax.broadcasted_iota(jnp.int32, (1, gw), 1) // hd
        is_last = lax.broadcasted_iota(jnp.int32, (q, 1), 0) == q - 1
        dec = jnp.zeros((1, gw), F32)
        dg = jnp.zeros((q, q), F32)
        for r in range(hpg):
            sl = slice(r * hd, (r + 1) * hd)
            csr = cs[:, r:r + 1]
            lm = jnp.exp(jnp.where(tri, csr - cst[r:r + 1, :], NEG))
            dtr = dt[:, r:r + 1]
            xr = x[:, sl]
            xdt = xr * dtr
            last = cs[q - 1:q, r:r + 1]
            e = jnp.exp(csr)
            wv = jnp.exp(last - csr)
            elast = jnp.exp(last)
            m = g * lm
            dyr = dy[:, sl]
            dxdt = _bdot(m, dyr, "tn") + wv * bd[:, sl]
            dm = _bdot(dyr, xdt, "nt")
            dg = dg + dm * lm
            z = dm * m
            de = jnp.sum(dyr * ch[:, sl], axis=1, keepdims=True)
            dw = jnp.sum(xdt * bd[:, sl], axis=1, keepdims=True)
            extra = (jnp.sum(dw * wv, axis=0, keepdims=True)
                     + elast * jnp.sum(hh[:, sl], axis=1, keepdims=True))
            colp_s[:, r:r + 1] = (jnp.sum(z, axis=1, keepdims=True) + de * e - dw * wv
                                  + jnp.where(is_last, extra, 0.0))
            rowp_s[r:r + 1, :] = jnp.sum(z, axis=0, keepdims=True)
            xs_s[:, r:r + 1] = jnp.sum(dxdt * xr, axis=1, keepdims=True)
            dx_ref[:, sl] = dxdt * dtr + dxg_ref[:, sl]
            xw_s[:, sl] = xdt * wv
            dye_s[:, sl] = dyr * e
            dec = jnp.where(lane == r, elast, dec)
        eye = (lax.broadcasted_iota(jnp.int32, (hpg, hpg), 0)
               == lax.broadcasted_iota(jnp.int32, (hpg, hpg), 1)).astype(F32)
        dcs = colp_s[...] - _dot(rowp_s[...], eye, "tn", precision=HI)
        row = lax.broadcasted_iota(jnp.int32, (q, q), 0)
        col = lax.broadcasted_iota(jnp.int32, (q, q), 1)
        dda = _dot((row <= col).astype(F32), dcs, precision=HI)
        ddt_ref[0] = dda * ar_ref[0] + xs_s[...]
        da_ref[0] += jnp.sum(dda * dt, axis=0, keepdims=True)
        dc_ref[...] = _bdot(dg, bm) + _bdot(dye_s[...], h, "nt")
        db_ref[...] = _bdot(dg, cm, "tn") + _bdot(xw_s[...], dhn, "nt")
        dh_s[...] = dhn * dec + _bdot(cm, dye_s[...], "tn")

    return pl.pallas_call(
        body, out_shape=[jax.ShapeDtypeStruct((lp, cfg.ssd_inner), F32), jax.ShapeDtypeStruct((lp, cfg.gn), F32),
                         jax.ShapeDtypeStruct((lp, cfg.gn), F32), jax.ShapeDtypeStruct((SSD_GROUPS, lp, hpg), F32),
                         jax.ShapeDtypeStruct((SSD_GROUPS, 1, hpg), F32)],
        grid=(SSD_GROUPS, nc),
        in_specs=[sp["x"], sp["b"], sp["c"], sp["dt"], sp["dtt"], sp["ar"], sp["ac"], sp["h"], sp["x"], sp["x"]],
        out_specs=[sp["x"], sp["bc_out"], sp["bc_out"], sp["dt"], sp["ar"]],
        scratch_shapes=[pltpu.VMEM((n, gw), F32), pltpu.VMEM((q, gw), F32), pltpu.VMEM((q, gw), F32),
                        pltpu.VMEM((q, hpg), F32), pltpu.VMEM((hpg, q), F32), pltpu.VMEM((q, hpg), F32)],
        name="ssd_scan_bwd", compiler_params=_cparams(2))(xbc, xbc, xbc, dt_g, dtt_g, a_row, a_col, hprev, dy, dx_gate)


def _gate_fn(y, xs, z, dexp, ng):
    return _rms((y + dexp * xs) * _silu(z), ng)


def _gate_specs(cfg):
    tm, gw = cfg.tm, cfg.gw
    blk = pl.BlockSpec((tm, gw), lambda g, i: (i, g))
    par = pl.BlockSpec((1, gw), lambda g, i: (0, g))
    return blk, par


def _mamba_fwd(h, w, w_in_t, w_out, conv_w, cfg):
    lp, tm, nh, hpg, inner = cfg.lp, cfg.tm, cfg.ssd_heads, cfg.hpg, cfg.ssd_inner
    zx = _mm(h, w_in_t, "nt", "l1_in")
    conv_b = w["l1_conv_b"].reshape(1, -1)
    xbc = _conv_fwd(zx, conv_w, conv_b, cfg)
    dt_raw = zx[:, inner + cfg.conv_dim:inner + cfg.conv_dim + nh]
    dt_bias = w["l1_dt_bias"].reshape(1, nh)
    a_log = w["l1_a_log"].reshape(1, nh)
    dt = _rowwise(lambda r, b: _softplus(r + b), [dt_raw], [dt_bias], [(nh, F32)], [], tm, "ssd_dt")[0]
    a = _small_call(lambda al: -jnp.exp(al), [a_log], [(1, nh)], "ssd_a")[0]
    dt_g = dt.reshape(lp, SSD_GROUPS, hpg).transpose(1, 0, 2)
    dtt_g = dt_g.transpose(0, 2, 1)
    a_row, a_col = a.reshape(SSD_GROUPS, 1, hpg), a.reshape(SSD_GROUPS, hpg, 1)
    y, hprev = _ssd_fwd(xbc, dt_g, dtt_g, a_row, a_col, cfg)
    dexp = jnp.repeat(w["l1_d"], SSD_HEAD_DIM).reshape(1, inner)
    ng = w["l1_norm_g"].reshape(1, inner)
    blk, par = _gate_specs(cfg)
    yn = _tile_call(_gate_fn, (SSD_GROUPS, lp // tm), [y, xbc, zx, dexp, ng], [blk, blk, blk, par, par],
                    [((lp, inner), F32)], [blk], "ssd_gate")[0]
    mo = _mm(yn, w_out, "nn", "l1_out")
    return mo, (zx, xbc, dt_raw, dt_g, dtt_g, a, a_row, a_col, y, hprev, dexp, ng, yn)


def _mamba_bwd(h, saved, dmo, w, w_in_t, w_out, conv_w, cfg):
    zx, xbc, dt_raw, dt_g, dtt_g, a, a_row, a_col, y, hprev, dexp, ng, yn = saved
    lp, tm, nh, hpg, inner = cfg.lp, cfg.tm, cfg.ssd_heads, cfg.hpg, cfg.ssd_inner
    d_wout = _mm(yn, dmo, "tn", "l1_dwout", out_dtype=BF16)
    dyn = _mm(dmo, w_out, "nt", "l1_dyn")
    blk, par = _gate_specs(cfg)

    def gate_bwd(y, xs, z, dexp, ng, dyn):
        _, vjp = jax.vjp(_gate_fn, y, xs, z, dexp, ng)
        return vjp(dyn)
    dy, dxs_gate, dzx, ddexp, dng = _tile_call(
        gate_bwd, (SSD_GROUPS, lp // tm), [y, xbc, zx, dexp, ng, dyn], [blk, blk, blk, par, par, blk],
        [((lp, inner), F32)] * 2 + [((lp, cfg.l1_inp), F32)] + [((1, inner), F32)] * 2, [blk, blk, blk, par, par],
        "ssd_gate_bwd", acc=(3, 4), acc_axis=1)
    dxs, dbm, dcm, ddt_g, da_g = _ssd_bwd(xbc, dt_g, dtt_g, a_row, a_col, hprev, dy, dxs_gate, cfg)
    conv_b = w["l1_conv_b"].reshape(1, -1)
    dzx, dconv_w, dconv_b = _conv_bwd(zx, conv_w, conv_b, dxs, dbm, dcm, dzx, cfg)
    assert cfg.l1_inp - inner - cfg.conv_dim == LANES
    ddt = jnp.pad(ddt_g.transpose(1, 0, 2).reshape(lp, nh), ((0, 0), (0, LANES - nh)))
    dt_bias = jnp.pad(w["l1_dt_bias"].reshape(1, nh), ((0, 0), (0, LANES - nh)))
    last = (inner + cfg.conv_dim) // LANES
    tail = pl.BlockSpec((tm, LANES), lambda i: (i, last))

    def dt_bwd(ddt, r, b):
        lane = lax.broadcasted_iota(jnp.int32, ddt.shape, 1)
        d = jnp.where(lane < nh, ddt * _sigmoid(r + b), 0.0)
        return d, jnp.sum(d, axis=0, keepdims=True)
    dzx, ddt_bias = _tile_call(dt_bwd, (lp // tm,), [ddt, zx, dt_bias], [_rows(tm, LANES), tail, _whole((1, LANES))],
                               [(dzx.shape, F32), ((1, LANES), F32)], [tail, _whole((1, LANES))], "ssd_dt_bwd",
                               acc=(1,), fill=(dzx, 0))
    ddt_bias = ddt_bias[:, :nh]
    da_log, dd = _small_call(lambda da, a, dde: (da * a, jnp.sum(dde, axis=1, keepdims=True)),
                             [da_g.reshape(1, nh), a, ddexp.reshape(nh, SSD_HEAD_DIM)], [(1, nh), (nh, 1)],
                             "ssd_small_bwd")
    d_win_t = _mm(dzx, h, "tn", "l1_dwin", out_dtype=BF16, tm_t=1152)
    dh = _mm(dzx, w_in_t, "nn", "l1_dh")
    grads = {"l1_w_in": d_win_t, "l1_conv_w": dconv_w[:SSD_CONV], "l1_conv_b": dconv_b.reshape(-1),
             "l1_dt_bias": ddt_bias.reshape(-1), "l1_a_log": da_log.reshape(-1), "l1_d": dd.reshape(-1),
             "l1_norm_g": dng.reshape(-1), "l1_w_out": d_wout}
    return dh, grads


def _local_step(x, target, w, cfg, late_weights=None, early_grads=None):
    lp, n, d, tm, sw = cfg.lp, cfg.n, cfg.d, cfg.tr, cfg.s5_w
    row = lambda name: w[name].reshape(1, -1)
    h0 = jnp.concatenate([w["meta_tokens"], x, jnp.zeros((lp - n, d), F32)], axis=0)
    proj = _mm(h0, w["l0_w_in"], "nn", "l0_in")
    o1, o2, o3 = sw, sw + cfg.q_rank, sw + cfg.q_rank + cfg.kv_rank
    prm = _s5_prep(w, cfg)
    a_out, s5_saved = _s5_fwd(proj[:, :o1], prm, w["l0_s5_w_glu"], cfg)
    b_out, mla_saved, arrived = _mla_fwd(proj[:, o1:o2], proj[:, o2:o3], proj[:, o3:], w, w["l0_mla_w_uq_p"],
                                         w["l0_mla_w_ukv"], cfg, late_weights[0] if late_weights else None)
    if late_weights:
        w = dict(w, **late_weights[1](arrived))
    mix = jnp.concatenate([a_out, b_out], axis=1).astype(BF16)
    mo0 = _mm(mix, w["l0_w_out"], "nn", "l0_out")
    h1, h1b = _ln_fwd(h0, mo0, row("l0_ln1_g"), row("l0_ln1_b"), cfg, "l0_ln1")
    fo0, ffn0 = _ffn_fwd(h1b, w["l0_w_gu"], w["l0_ffn_w_down"], cfg, "l0_ffn")
    h2, h2b = _ln_fwd(h1, fo0, row("l0_ln2_g"), row("l0_ln2_b"), cfg, "l0_ln2")
    mo1, mam = _mamba_fwd(h2b, w, w["l1_w_in_t"], w["l1_w_out"], w["l1_conv_w"], cfg)
    h3, h3b = _ln_fwd(h2, mo1, row("l1_ln1_g"), row("l1_ln1_b"), cfg, "l1_ln1")
    fo1, ffn1 = _ffn_fwd(h3b, w["l1_w_gu"], w["l1_ffn_w_down"], cfg, "l1_ffn")
    h4, _ = _ln_fwd(h3, fo1, row("l1_ln2_g"), row("l1_ln2_b"), cfg, "l1_ln2")
    tgt = jnp.concatenate([jnp.zeros((N_META, d), F32), target, jnp.zeros((lp - n, d), F32)], axis=0)

    def loss_fn(y, t):
        r = pl.program_id(0) * tm + lax.broadcasted_iota(jnp.int32, (tm, 1), 0)
        diff = jnp.where(jnp.logical_and(r >= N_META, r < n), y - t, 0.0)
        return diff * (1.0 / d), jnp.sum(diff * diff, axis=0, keepdims=True) * (0.5 / d)
    dh4, loss_lanes = _rowwise(loss_fn, [h4, tgt], [], [(d, F32)], [(1, d)], tm, "loss")
    grads = {}
    dr4, dr4b, dg, db = _ln_bwd(h3, fo1, row("l1_ln2_g"), row("l1_ln2_b"), [dh4], [1.0], cfg, "l1_ln2_bwd")
    grads["l1_ln2_g"], grads["l1_ln2_b"] = dg.reshape(-1), db.reshape(-1)
    dh3, grads["l1_w_gu"], grads["l1_ffn_w_down"] = _ffn_bwd(h3b, ffn1, dr4b, w["l1_w_gu"], w["l1_ffn_w_down"], cfg,
                                                             "l1_ffn")
    dr3, dr3b, dg, db = _ln_bwd(h2, mo1, row("l1_ln1_g"), row("l1_ln1_b"), [dr4, dh3], [DN_ALPHA, 1.0], cfg,
                                "l1_ln1_bwd")
    grads["l1_ln1_g"], grads["l1_ln1_b"] = dg.reshape(-1), db.reshape(-1)
    dh2, mg = _mamba_bwd(h2b, mam, dr3b, w, w["l1_w_in_t"], w["l1_w_out"], w["l1_conv_w"], cfg)
    grads.update(mg)
    dr2, dr2b, dg, db = _ln_bwd(h1, fo0, row("l0_ln2_g"), row("l0_ln2_b"), [dr3, dh2], [DN_ALPHA, 1.0], cfg,
                                "l0_ln2_bwd")
    grads["l0_ln2_g"], grads["l0_ln2_b"] = dg.reshape(-1), db.reshape(-1)
    dh1, grads["l0_w_gu"], grads["l0_ffn_w_down"] = _ffn_bwd(h1b, ffn0, dr2b, w["l0_w_gu"], w["l0_ffn_w_down"], cfg,
                                                             "l0_ffn")
    dr1, dr1b, dg, db = _ln_bwd(h0, mo0, row("l0_ln1_g"), row("l0_ln1_b"), [dr2, dh1], [DN_ALPHA, 1.0], cfg,
                                "l0_ln1_bwd")
    grads["l0_ln1_g"], grads["l0_ln1_b"] = dg.reshape(-1), db.reshape(-1)
    grads["l0_w_out"] = _mm(mix, dr1b, "tn", "l0_dwout", out_dtype=BF16)
    dmix = _mm(dr1b, w["l0_w_out"], "nt", "l0_dmix")
    du, sg = _s5_bwd(dmix[:, :sw], s5_saved, prm, w, w["l0_s5_w_glu"], cfg)
    dql, dkl, dkr, ag, exchanged = _mla_bwd(dmix[:, sw:], mla_saved, w, w["l0_mla_w_uq_p"], w["l0_mla_w_ukv"], cfg,
                                            early_grads(grads) if early_grads else None)
    grads.update(sg)
    grads.update(ag)
    dproj = jnp.concatenate([du, dql, dkl, dkr], axis=1)
    grads["l0_w_in"] = _mm(h0, dproj, "tn", "l0_dwin", out_dtype=BF16)
    dh0m = _mm(dproj, w["l0_w_in"], "nt", "l0_dh")
    dh0 = _rowwise(lambda a, b: DN_ALPHA * a + b, [dr1, dh0m], [], [(d, F32)], [], tm, "l0_dh0")[0]
    grads["meta_tokens"] = dh0[:N_META]
    return loss_lanes, dh0[N_META:n], grads, exchanged


FLAT_W = 1024
N_SLOTS = 4
HBM_SPEC = pl.BlockSpec(memory_space=pltpu.HBM)


def _place():
    x, y, c = lax.axis_index("x"), lax.axis_index("y"), lax.axis_index("c")
    chips = [(1 - x, y), (x, 1 - y), (1 - x, 1 - y)]
    return x, y, c, chips


def _remote(src, dst, ssem, rsem, dev):
    return pltpu.make_async_remote_copy(src_ref=src, dst_ref=dst, send_sem=ssem, recv_sem=rsem, device_id=dev,
                                        device_id_type=MESH)


class _GatherPlan:
    def __init__(self, shards, groups):
        self.arrays = list(shards)
        self.n = n = len(shards)
        self.rows = [s.shape[0] for s in shards]
        self.place = {t: (g, row0) for g, members in enumerate(groups) for t, row0 in members}
        self.n_out = len(groups)
        self.out_shapes = []
        for members in groups:
            t0 = members[0][0]
            rows = max(row0 + N_SLOTS * shards[t].shape[0] for t, row0 in members)
            self.out_shapes.append(jax.ShapeDtypeStruct((rows, shards[t0].shape[1]), shards[t0].dtype))
        sems = pltpu.SemaphoreType.DMA((3 * n,))
        self.scratch = [sems, sems, sems, sems, pltpu.SemaphoreType.DMA((n,))]

    def _copies(self, refs):
        n = self.n
        srcs, outs = refs[:n], refs[n:n + self.n_out]
        send_sems, recv_sems, fsend, frecv, lsems = refs[n + self.n_out:]
        x, y, c, chips = _place()
        me = 2 * x + y
        sib = (x, y, 1 - c)

        def rows_of(t, slot, half):
            r = self.rows[t]
            g, row0 = self.place[t]
            return outs[g].at[pl.ds(row0 + slot * r + half * (r // 2), r // 2)]
        local, send, arrive, relay, arrive_sib = [], [], [], [], []
        for t in range(n):
            r = self.rows[t]
            g, row0 = self.place[t]
            local.append(pltpu.make_async_copy(srcs[t], outs[g].at[pl.ds(row0 + me * r, r)], lsems.at[t]))
            mine = srcs[t].at[pl.ds(c * (r // 2), r // 2)]
            for j, (cx, cy) in enumerate(chips):
                k = 3 * t + j
                send.append(_remote(mine, rows_of(t, me, c), send_sems.at[k], recv_sems.at[k], (cx, cy, c)))
                got = rows_of(t, 2 * cx + cy, c)
                arrive.append(_remote(got, got, send_sems.at[k], recv_sems.at[k], (cx, cy, c)))
                relay.append(_remote(got, got, fsend.at[k], frecv.at[k], sib))
                got_sib = rows_of(t, 2 * cx + cy, 1 - c)
                arrive_sib.append(_remote(got_sib, got_sib, fsend.at[k], frecv.at[k], sib))
        return local, send, arrive, relay, arrive_sib

    def start(self, refs):
        local, send, _, _, _ = self._copies(refs)
        for cp in local + send:
            cp.start()

    def relay(self, refs):
        _, _, arrive, relay, _ = self._copies(refs)
        for a, r in zip(arrive, relay):
            a.wait_recv()
            r.start()

    def finish(self, refs):
        local, send, _, relay, arrive_sib = self._copies(refs)
        for cp in arrive_sib:
            cp.wait_recv()
        for cp in send + relay:
            cp.wait_send()
        for cp in local:
            cp.wait()


class _ExchangePlan:
    def __init__(self, items):
        self.items = items
        self.arrays = [a for a, _, _ in items]
        self.n = self.n_out = n = len(items)
        self.out_shapes = [jax.ShapeDtypeStruct((3, rps, a.shape[1]), a.dtype) for a, _, rps in items]
        sems = pltpu.SemaphoreType.DMA((3 * n,))
        self.scratch = [sems, sems]

    def _copies(self, refs):
        n = self.n
        srcs, outs = refs[:n], refs[n:2 * n]
        send_sems, recv_sems = refs[2 * n:]
        x, y, c, chips = _place()
        cps = []
        for t, (_, row0, rps) in enumerate(self.items):
            for j, (cx, cy) in enumerate(chips):
                cps.append(_remote(srcs[t].at[pl.ds(row0 + (2 * cx + cy) * rps, rps)], outs[t].at[j],
                                   send_sems.at[3 * t + j], recv_sems.at[3 * t + j], (cx, cy, c)))
        return cps

    def start(self, refs):
        for cp in self._copies(refs):
            cp.start()

    def relay(self, refs):
        pass

    def finish(self, refs):
        for cp in self._copies(refs):
            cp.wait()


def _run_plan(plan, name):
    def body(*refs):
        plan.start(refs)
        plan.relay(refs)
        plan.finish(refs)
    return pl.pallas_call(body, out_shape=plan.out_shapes, in_specs=[HBM_SPEC] * plan.n,
                          out_specs=[HBM_SPEC] * plan.n_out, scratch_shapes=plan.scratch, name=name)(*plan.arrays)


def _exchange_sibling(arrays):
    n = len(arrays)

    def body(*refs):
        srcs, outs = refs[:n], refs[n:2 * n]
        ssems, rsems = refs[2 * n:]
        x, y, c, _ = _place()
        cps = []
        for t in range(n):
            cp = _remote(srcs[t], outs[t], ssems.at[t], rsems.at[t], (x, y, 1 - c))
            cp.start()
            cps.append(cp)
        for cp in cps:
            cp.wait()

    sems = pltpu.SemaphoreType.DMA((n,))
    return pl.pallas_call(
        body, out_shape=[jax.ShapeDtypeStruct(a.shape, a.dtype) for a in arrays], in_specs=[HBM_SPEC] * n,
        out_specs=[HBM_SPEC] * n, scratch_shapes=[sems, sems], name="exchange_sibling")(*arrays)


def _gather_all(v):
    flips = [(fx, fy, fc) for fx in (0, 1) for fy in (0, 1) for fc in (0, 1)][1:]

    def body(src, out, send_sems, recv_sems, lsem):
        x, y, c, _ = _place()
        local = pltpu.make_async_copy(src, out.at[4 * x + 2 * y + c], lsem)
        local.start()
        cps = []
        for k, (fx, fy, fc) in enumerate(flips):
            px, py, pc = (1 - x if fx else x), (1 - y if fy else y), (1 - c if fc else c)
            cp = _remote(src, out.at[4 * x + 2 * y + c], send_sems.at[k], recv_sems.at[k], (px, py, pc))
            cp.start()
            cps.append(cp)
        for cp in cps:
            cp.wait()
        local.wait()

    return pl.pallas_call(
        body, out_shape=jax.ShapeDtypeStruct((8,) + v.shape, v.dtype), in_specs=[HBM_SPEC], out_specs=HBM_SPEC,
        scratch_shapes=[pltpu.SemaphoreType.DMA((7,)), pltpu.SemaphoreType.DMA((7,)), pltpu.SemaphoreType.DMA],
        name="gather_all")(v)


def _flat_rows(n_elems, row_unit):
    return -(-n_elems // (FLAT_W * row_unit)) * row_unit


def _to_flat(pieces, rows):
    flat = jnp.concatenate([p.reshape(-1) for p in pieces])
    return jnp.pad(flat, (0, rows * FLAT_W - flat.shape[0])).reshape(rows, FLAT_W)


def _sum_parts(parts, name, tm=512):
    rows = parts[0].shape[1]
    tm = _div_tile(rows, tm, SUBLANES)

    def fn(*ps):
        acc = None
        for p in ps:
            for k in range(p.shape[0]):
                acc = p[k].astype(F32) if acc is None else acc + p[k].astype(F32)
        return acc
    return _tile_call(fn, (rows // tm,), parts,
                      [pl.BlockSpec((p.shape[0], tm, FLAT_W), lambda i: (0, i, 0)) for p in parts],
                      [((rows, FLAT_W), F32)], [_rows(tm, FLAT_W)], name)[0]


ELEMENTWISE_BLOCK = 1 << 19


def _row_tile(rows, cols, unit):
    return _div_tile(rows, max(unit, ELEMENTWISE_BLOCK // cols), unit)


def _sum_slot(g, row0, rps, others, me, name):
    c = g.shape[1]
    tm = _row_tile(rps, c, 2 * SUBLANES)
    nrb = rps // tm
    assert row0 % tm == 0

    def body(me_ref, g_ref, o_ref, out_ref):
        out_ref[...] = ((g_ref[...].astype(F32) + o_ref[0].astype(F32)) + o_ref[1].astype(F32)) + o_ref[2].astype(F32)

    grid_spec = pltpu.PrefetchScalarGridSpec(
        num_scalar_prefetch=1, grid=(nrb,),
        in_specs=[pl.BlockSpec((tm, c), lambda i, me_ref: (row0 // tm + me_ref[0] * nrb + i, 0)),
                  pl.BlockSpec((3, tm, c), lambda i, me_ref: (0, i, 0))],
        out_specs=pl.BlockSpec((tm, c), lambda i, me_ref: (i, 0)))
    return pl.pallas_call(body, out_shape=jax.ShapeDtypeStruct((rps, c), F32), grid_spec=grid_spec, name=name,
                          compiler_params=_cparams(1))(me, g, others)


def _adamw(gparts, w, m, v, name, tm=None):
    rows, cols = w.shape
    tm = _row_tile(rows, cols, SUBLANES) if tm is None else _div_tile(rows, tm, SUBLANES)
    ng = len(gparts)

    def fn(*a):
        g = a[0]
        for t in a[1:ng]:
            g = g + t
        w, m, v = a[ng:]
        m = ADAM_B1 * m + (1.0 - ADAM_B1) * g
        v = ADAM_B2 * v + (1.0 - ADAM_B2) * (g * g)
        m_hat = m / (1.0 - ADAM_B1 ** ADAM_STEP)
        v_hat = v / (1.0 - ADAM_B2 ** ADAM_STEP)
        delta = -ADAM_LR * (m_hat / (jnp.sqrt(v_hat) + ADAM_EPS) + ADAM_WD * w)
        return g, delta, m, v
    ins = list(gparts) + [w, m, v]
    return _tile_call(fn, (rows // tm,), ins, [_rows(tm, cols)] * len(ins), [((rows, cols), F32)] * 4,
                      [_rows(tm, cols)] * 4, name)


FIRST = ("l0_w_in", "l0_s5_w_glu", "l0_mla_w_uq", "l0_mla_w_ukv")
REST = ("l0_w_out", "l0_ffn_w_gate", "l0_ffn_w_up", "l0_ffn_w_down", "l1_w_in", "l1_w_out", "l1_ffn_w_gate",
        "l1_ffn_w_up", "l1_ffn_w_down")
BIG = FIRST + REST
TINY = ("meta_tokens", "l1_conv_w")
REPLICATED = ("l0_s5_log_dt", "l0_s5_a_re", "l0_s5_a_im", "l0_s5_b_re", "l0_s5_b_im", "l0_s5_c_re", "l0_s5_c_im",
              "l0_s5_d", "l0_mla_q_norm", "l0_mla_kv_norm", "l0_ln1_g", "l0_ln1_b", "l0_ln2_g", "l0_ln2_b",
              "l1_conv_b", "l1_dt_bias", "l1_a_log", "l1_d", "l1_norm_g", "l1_ln1_g", "l1_ln1_b", "l1_ln2_g",
              "l1_ln2_b")
WEIGHTS = ("meta_tokens", "l0_w_in", "l0_s5_log_dt", "l0_s5_a_re", "l0_s5_a_im", "l0_s5_b_re", "l0_s5_b_im",
           "l0_s5_c_re", "l0_s5_c_im", "l0_s5_d", "l0_s5_w_glu", "l0_mla_q_norm", "l0_mla_w_uq", "l0_mla_kv_norm",
           "l0_mla_w_ukv", "l0_w_out", "l0_ln1_g", "l0_ln1_b", "l0_ffn_w_gate", "l0_ffn_w_up", "l0_ffn_w_down",
           "l0_ln2_g", "l0_ln2_b", "l1_w_in", "l1_conv_w", "l1_conv_b", "l1_dt_bias", "l1_a_log", "l1_d",
           "l1_norm_g", "l1_w_out", "l1_ln1_g", "l1_ln1_b", "l1_ffn_w_gate", "l1_ffn_w_up", "l1_ffn_w_down",
           "l1_ln2_g", "l1_ln2_b")
def _split_flat(flat2d, shapes):
    flat = flat2d.reshape(-1)
    out, off = [], 0
    for s in shapes:
        n = math.prod(s)
        out.append(flat[off:off + n].reshape(s))
        off += n
    return out


def _cols_to_slots(full):
    r, c = full.shape
    return full.reshape(r, N_SLOTS, c // N_SLOTS).transpose(1, 0, 2).reshape(N_SLOTS * r, c // N_SLOTS)


def _slots_to_cols(slabs):
    r4, c = slabs.shape
    return slabs.reshape(N_SLOTS, r4 // N_SLOTS, c).transpose(1, 0, 2).reshape(r4 // N_SLOTS, N_SLOTS * c)


def _step(cfg, x, loss_target, ws, ms, vs):
    d, f = cfg.d, cfg.ffn
    me = 2 * lax.axis_index("x") + lax.axis_index("y")
    kinds = ("grad", "delta", "new_m", "new_v")
    def gather_plan(names):
        groups = []
        for t, name in enumerate(names):
            if name.endswith("_ffn_w_up"):
                groups[-1].append((t, N_SLOTS * d))
            else:
                groups.append([(t, 0)])
        heads = [names[members[0][0]] for members in groups]
        return _GatherPlan([ws[name].astype(BF16) for name in names], groups), heads
    plan, heads = gather_plan(FIRST)
    got = dict(zip(heads, _run_plan(plan, "gather_first")))
    w = {"l0_w_in": got["l0_w_in"], "l0_s5_w_glu": got["l0_s5_w_glu"],
         "l0_mla_w_uq_p": _pad_heads(_slots_to_cols(got["l0_mla_w_uq"]), cfg.heads, MLA_NOPE + MLA_ROPE),
         "l0_mla_w_ukv": _slots_to_cols(got["l0_mla_w_ukv"])}
    late_plan, late_heads = gather_plan(REST)

    def late_weights(arrived):
        got = dict(zip(late_heads, arrived))
        lw = {name: got[name] for name in ("l0_w_out", "l0_ffn_w_down", "l1_w_out", "l1_ffn_w_down")}
        for l in ("l0", "l1"):
            lw[l + "_w_gu"] = got[l + "_ffn_w_gate"].reshape(2 * N_SLOTS, d, f // N_SLOTS)
        w_in_t = got["l1_w_in"].reshape(N_SLOTS, d, cfg.l1_in // N_SLOTS).transpose(0, 2, 1).reshape(cfg.l1_in, d)
        lw["l1_w_in_t"] = jnp.pad(w_in_t, ((0, cfg.l1_inp - cfg.l1_in), (0, 0)))
        return lw
    tiny_shapes = [ws[name].shape for name in TINY]
    trows = _flat_rows(sum(math.prod(s) for s in tiny_shapes), SUBLANES)
    tiny = _gather_all(_to_flat([ws[name] for name in TINY], trows))[0::2]
    for k, name in enumerate(TINY):
        blocks = jnp.stack([_split_flat(tiny[s], tiny_shapes)[k] for s in range(N_SLOTS)])
        w[name] = blocks.transpose(1, 0, 2).reshape(blocks.shape[1], -1)
    for name in REPLICATED:
        w[name] = ws[name]
    rps = {name: ws[name].shape[1 if name == "l1_w_in" else 0] for name in BIG}

    def triples(grads, names):
        out = []
        for name in names:
            if name.endswith(("_ffn_w_gate", "_ffn_w_up")):
                g = grads[name[:3] + "w_gu"].reshape(2 * N_SLOTS * d, f // N_SLOTS)
                out.append((g, N_SLOTS * d if name.endswith("_up") else 0, rps[name]))
            elif name in ("l0_mla_w_uq", "l0_mla_w_ukv"):
                out.append((_cols_to_slots(grads[name]).astype(BF16), 0, rps[name]))
            else:
                out.append((grads[name], 0, rps[name]))
        return out
    held = {}

    def early_grads(grads):
        held["rest"] = triples(grads, REST)
        return _ExchangePlan(held["rest"])
    loss_lanes, grad_x, grads, others_rest = _local_step(x[0], loss_target[0], w, cfg, (late_plan, late_weights),
                                                         early_grads)
    first = triples(grads, FIRST)
    others = list(_run_plan(_ExchangePlan(first), "exchange_first")) + list(others_rest)
    me1 = me.reshape(1).astype(jnp.int32)
    parts = [_sum_slot(a, row0, r, o, me1, "sum_" + name)
             for (a, row0, r), o, name in zip(first + held["rest"], others, BIG)]
    sibs = _exchange_sibling(parts)
    res = {}
    for name, p, q in zip(BIG, parts, sibs):
        if name == "l1_w_in":
            p, q = p.T, q.T
        for kind, arr in zip(kinds, _adamw([p, q], ws[name], ms[name], vs[name], "adamw_" + name)):
            res[kind + "_" + name] = arr
    rep_shapes = [(1, d)] + [ws[name].shape for name in REPLICATED]
    all_shapes = rep_shapes + [grads[name].shape for name in TINY]
    srows = _flat_rows(sum(math.prod(s) for s in all_shapes), SUBLANES)
    small = _to_flat([loss_lanes] + [grads[name] for name in REPLICATED + TINY], srows)
    total = _sum_parts([_gather_all(small)], "sum_small", tm=srows)
    zero = jnp.zeros((1, d), F32)
    flat = lambda dct: _to_flat([zero] + [dct[name] for name in REPLICATED], srows)
    outs = _adamw([total], flat(ws), flat(ms), flat(vs), "adamw_replicated", tm=srows)
    for kind, arr in zip(kinds, outs):
        vals = _split_flat(arr, rep_shapes)
        for name, val in zip(REPLICATED, vals[1:]):
            res[kind + "_" + name] = val
    for name, g in zip(TINY, _split_flat(total, all_shapes)[len(rep_shapes):]):
        cols = ws[name].shape[1]
        mine = lax.dynamic_slice_in_dim(g, me * cols, cols, axis=1)
        for kind, arr in zip(kinds, _adamw([mine], ws[name], ms[name], vs[name], "adamw_" + name)):
            res[kind + "_" + name] = arr
    loss = _small_call(lambda t: jnp.sum(t, axis=1, keepdims=True), [_split_flat(total, rep_shapes)[0]], [(1, 1)],
                       "loss_sum")[0].reshape(())
    ordered = [res[kind + "_" + name] for kind in ("grad", "delta", "new_m", "new_v") for name in WEIGHTS]
    return (loss, grad_x[None]) + tuple(ordered)


def kernel(x, meta_tokens, l0_w_in, l0_s5_log_dt, l0_s5_a_re, l0_s5_a_im, l0_s5_b_re, l0_s5_b_im, l0_s5_c_re,
           l0_s5_c_im, l0_s5_d, l0_s5_w_glu, l0_mla_q_norm, l0_mla_w_uq, l0_mla_kv_norm, l0_mla_w_ukv, l0_w_out,
           l0_ln1_g, l0_ln1_b, l0_ffn_w_gate, l0_ffn_w_up, l0_ffn_w_down, l0_ln2_g, l0_ln2_b, l1_w_in, l1_conv_w,
           l1_conv_b, l1_dt_bias, l1_a_log, l1_d, l1_norm_g, l1_w_out, l1_ln1_g, l1_ln1_b, l1_ffn_w_gate,
           l1_ffn_w_up, l1_ffn_w_down, l1_ln2_g, l1_ln2_b, loss_target, m_meta_tokens, m_l0_w_in, m_l0_s5_log_dt,
           m_l0_s5_a_re, m_l0_s5_a_im, m_l0_s5_b_re, m_l0_s5_b_im, m_l0_s5_c_re, m_l0_s5_c_im, m_l0_s5_d,
           m_l0_s5_w_glu, m_l0_mla_q_norm, m_l0_mla_w_uq, m_l0_mla_kv_norm, m_l0_mla_w_ukv, m_l0_w_out, m_l0_ln1_g,
           m_l0_ln1_b, m_l0_ffn_w_gate, m_l0_ffn_w_up, m_l0_ffn_w_down, m_l0_ln2_g, m_l0_ln2_b, m_l1_w_in,
           m_l1_conv_w, m_l1_conv_b, m_l1_dt_bias, m_l1_a_log, m_l1_d, m_l1_norm_g, m_l1_w_out, m_l1_ln1_g,
           m_l1_ln1_b, m_l1_ffn_w_gate, m_l1_ffn_w_up, m_l1_ffn_w_down, m_l1_ln2_g, m_l1_ln2_b, v_meta_tokens,
           v_l0_w_in, v_l0_s5_log_dt, v_l0_s5_a_re, v_l0_s5_a_im, v_l0_s5_b_re, v_l0_s5_b_im, v_l0_s5_c_re,
           v_l0_s5_c_im, v_l0_s5_d, v_l0_s5_w_glu, v_l0_mla_q_norm, v_l0_mla_w_uq, v_l0_mla_kv_norm,
           v_l0_mla_w_ukv, v_l0_w_out, v_l0_ln1_g, v_l0_ln1_b, v_l0_ffn_w_gate, v_l0_ffn_w_up, v_l0_ffn_w_down,
           v_l0_ln2_g, v_l0_ln2_b, v_l1_w_in, v_l1_conv_w, v_l1_conv_b, v_l1_dt_bias, v_l1_a_log, v_l1_d,
           v_l1_norm_g, v_l1_w_out, v_l1_ln1_g, v_l1_ln1_b, v_l1_ffn_w_gate, v_l1_ffn_w_up, v_l1_ffn_w_down,
           v_l1_ln2_g, v_l1_ln2_b):
    given = dict(locals())
    ws = {name: given[name] for name in WEIGHTS}
    ms = {name: given["m_" + name] for name in WEIGHTS}
    vs = {name: given["v_" + name] for name in WEIGHTS}
    return _step(FULL, x, loss_target, ws, ms, vs)
```

```python
import functools
import math

import numpy as np
import jax
import jax.numpy as jnp
from jax import lax
from jax.experimental import pallas as pl
from jax.experimental.pallas import tpu as pltpu

F32 = jnp.float32
BF16 = jnp.bfloat16
HI = lax.Precision.HIGHEST
MESH = pl.DeviceIdType.MESH

LANES = 128
SUBLANES = 8
VMEM_LIMIT_BYTES = 56 * 1024 * 1024

N_META = 16
CHUNK = 64
DEPTH = 2
DN_ALPHA = (2 * DEPTH) ** 0.25
LN_EPS = 1e-5
RMS_EPS = 1e-6
ROPE_BASE = 10000.0
S5_GROUP = 16
S5_STATE = 64
S5_GPB = 8
MLA_NOPE = 128
MLA_ROPE = 64
MLA_V = 128
MLA_QW = 256
ATT_SHIFT = 48
SSD_HEAD_DIM = 64
SSD_GROUPS = 8
SSD_STATE = 128
SSD_CONV = 4
SSD_BLOCK = 128
ADAM_LR = 0.001
ADAM_B1 = 0.9
ADAM_B2 = 0.999
ADAM_EPS = 1e-08
ADAM_WD = 0.01
ADAM_STEP = 10
NEG = -1e30


class _Cfg:
    def __init__(self, d_model, seq, row_tile, att_tile, scan_tiles, small_row_tile):
        d = d_model
        self.tr = small_row_tile
        self.d = d
        self.seq = seq
        self.n = seq + N_META
        lp = -(-(self.n + ATT_SHIFT) // row_tile) * row_tile
        self.lp = lp
        self.tm = row_tile
        self.tq = att_tile
        self.scan_tiles = scan_tiles
        self.s5_w = d // 2
        self.s5_g = self.s5_w // S5_GROUP
        self.s5_nb = self.s5_g // S5_GPB
        self.s5_c = self.s5_g * S5_STATE
        self.heads = d // 256
        self.q_rank = d // 4
        self.kv_rank = d // 8
        self.l0_in = self.s5_w + self.q_rank + self.kv_rank + MLA_ROPE
        self.l0_mix = self.s5_w + self.heads * MLA_V
        self.ssd_inner = 2 * d
        self.ssd_heads = self.ssd_inner // SSD_HEAD_DIM
        self.hpg = self.ssd_heads // SSD_GROUPS
        self.gw = self.hpg * SSD_HEAD_DIM
        self.gn = SSD_GROUPS * SSD_STATE
        self.conv_dim = self.ssd_inner + 2 * self.gn
        self.l1_in = self.ssd_inner + self.conv_dim + self.ssd_heads
        self.l1_inp = -(-self.l1_in // LANES) * LANES
        self.ffn = -(-(8 * d) // (3 * 256)) * 256
        assert lp % att_tile == 0 and lp % SSD_BLOCK == 0 and lp % (8 * scan_tiles) == 0


FULL = _Cfg(2048, 8192, 640, 640, 4, 160)


def _cparams(n_grid):
    return pltpu.CompilerParams(dimension_semantics=("arbitrary",) * n_grid,
                                vmem_limit_bytes=VMEM_LIMIT_BYTES)


def _div_tile(n, target, unit=LANES):
    if n <= target:
        return n
    best = None
    for t in range(unit, target + 1, unit):
        if n % t == 0:
            best = t
    return n if best is None else best


ANY_SPEC = pl.BlockSpec(memory_space=pl.ANY)


def _tile_call(fn, grid, ins, in_specs, outs, out_specs, name, acc=(), acc_axis=0, fill=None):
    n_in = len(ins)
    n_out = len(outs)
    acc = tuple(acc)
    aliases = {}
    if fill is not None:
        aliases = {n_in: fill[1]}
        ins = list(ins) + [fill[0]]
        in_specs = list(in_specs) + [ANY_SPEC]

    def body(*refs):
        refs = refs[:n_in] + refs[len(ins):]
        vals = fn(*[r[...] for r in refs[:n_in]])
        if not isinstance(vals, (tuple, list)):
            vals = (vals,)
        for k in range(n_out):
            r = refs[n_in + k]
            v = vals[k].astype(r.dtype)
            if k in acc:
                first = pl.program_id(acc_axis) == 0

                @pl.when(first)
                def _(r=r, v=v):
                    r[...] = v

                @pl.when(jnp.logical_not(first))
                def _(r=r, v=v):
                    r[...] += v
            else:
                r[...] = v

    res = pl.pallas_call(
        body, out_shape=[jax.ShapeDtypeStruct(s, d) for s, d in outs], grid=grid,
        in_specs=in_specs, out_specs=out_specs, name=name, compiler_params=_cparams(len(grid)),
        input_output_aliases=aliases,
    )(*ins)
    return res


def _rows(tm, c):
    return pl.BlockSpec((tm, c), lambda i: (i, 0))


def _whole(shape):
    nd = len(shape)
    return pl.BlockSpec(shape, lambda *a: (0,) * nd)


def _rowwise(fn, rows, params, outs, accs, tm, name):
    lp = rows[0].shape[0]
    n_row_out = len(outs)
    res = _tile_call(
        fn, (lp // tm,), list(rows) + list(params),
        [_rows(tm, r.shape[1]) for r in rows] + [_whole(p.shape) for p in params],
        [((lp, c), dt) for c, dt in outs] + [(s, F32) for s in accs],
        [_rows(tm, c) for c, _ in outs] + [_whole(s) for s in accs],
        name, acc=range(n_row_out, n_row_out + len(accs)))
    return res


_DIMS = {"nn": (((1,), (0,)), ((), ())), "nt": (((1,), (1,)), ((), ())), "tn": (((0,), (0,)), ((), ()))}


def _dot(a, b, mode="nn", precision=None):
    return lax.dot_general(a, b, _DIMS[mode], preferred_element_type=F32, precision=precision)


def _bdot(a, b, mode="nn"):
    return _dot(a.astype(BF16), b.astype(BF16), mode)


def _mm(a, b, mode, name, out_dtype=F32, out_slabs=None, b_col0=0, b_cols=None, tm_t=640, tn_t=1536, tk_t=2048):
    slab_b = b.ndim == 3
    if mode == "nn":
        m, k = a.shape
        k2, n, unit_n = (b.shape[1], b.shape[0] * b.shape[2], b.shape[2]) if slab_b else (b.shape[0], b.shape[1], b.shape[1])
        unit_k = k
    elif mode == "nt":
        m, k = a.shape
        n, k2, unit_k = (b.shape[1], b.shape[0] * b.shape[2], b.shape[2]) if slab_b else (b.shape[0], b.shape[1], b.shape[1])
        unit_n = n
    else:
        (k, m), k2 = a.shape, b.shape[0]
        n = b.shape[1] if b_cols is None else b_cols
        unit_n, unit_k = n, k
        tm_t = max(tm_t, 1024)
        tk_t = 1664 if a.dtype == BF16 and b.dtype == BF16 else 1024
    if out_slabs:
        unit_n = n // out_slabs
    assert k == k2, (name, a.shape, b.shape)
    tm, tn, tk = _div_tile(m, tm_t), _div_tile(unit_n, tn_t), _div_tile(unit_k, tk_t)
    nk = k // tk
    nps, kps = unit_n // tn, unit_k // tk
    c0 = b_col0 // tn
    assert b_col0 % tn == 0
    a_spec = {"nn": pl.BlockSpec((tm, tk), lambda i, j, kk: (i, kk)),
              "nt": pl.BlockSpec((tm, tk), lambda i, j, kk: (i, kk)),
              "tn": pl.BlockSpec((tk, tm), lambda i, j, kk: (kk, i))}[mode]
    if slab_b:
        b_spec = {"nn": pl.BlockSpec((None, tk, tn), lambda i, j, kk: (j // nps, kk, j % nps)),
                  "nt": pl.BlockSpec((None, tn, tk), lambda i, j, kk: (kk // kps, j, kk % kps))}[mode]
    else:
        b_spec = {"nn": pl.BlockSpec((tk, tn), lambda i, j, kk: (kk, j)),
                  "nt": pl.BlockSpec((tn, tk), lambda i, j, kk: (j, kk)),
                  "tn": pl.BlockSpec((tk, tn), lambda i, j, kk: (kk, j + c0))}[mode]
    if out_slabs:
        out_shape = jax.ShapeDtypeStruct((out_slabs, m, unit_n), out_dtype)
        out_spec = pl.BlockSpec((None, tm, tn), lambda i, j, kk: (j // nps, i, j % nps))
    else:
        out_shape = jax.ShapeDtypeStruct((m, n), out_dtype)
        out_spec = pl.BlockSpec((tm, tn), lambda i, j, kk: (i, j))

    def body(a_ref, b_ref, o_ref, acc_ref):
        part = _bdot(a_ref[...], b_ref[...], mode)
        if nk == 1:
            o_ref[...] = part.astype(o_ref.dtype)
        else:
            kk = pl.program_id(2)

            @pl.when(kk == 0)
            def _():
                acc_ref[...] = part

            @pl.when(kk > 0)
            def _():
                acc_ref[...] += part

            @pl.when(kk == nk - 1)
            def _():
                o_ref[...] = acc_ref[...].astype(o_ref.dtype)

    return pl.pallas_call(
        body, out_shape=out_shape, grid=(m // tm, n // tn, nk),
        in_specs=[a_spec, b_spec], out_specs=out_spec,
        scratch_shapes=[pltpu.VMEM((tm, tn) if nk > 1 else (SUBLANES, LANES), F32)],
        name=name, compiler_params=_cparams(3))(a, b)


def _layer_norm(r, g, b):
    mu = jnp.mean(r, axis=-1, keepdims=True)
    xc = r - mu
    var = jnp.mean(xc * xc, axis=-1, keepdims=True)
    return xc * lax.rsqrt(var + LN_EPS) * g + b


def _rms(x, g):
    return x * lax.rsqrt(jnp.mean(x * x, axis=-1, keepdims=True) + RMS_EPS) * g


def _sigmoid(x):
    return 1.0 / (1.0 + jnp.exp(-x))


def _silu(x):
    return x * _sigmoid(x)


def _gelu(x):
    return 0.5 * x * (1.0 + jnp.tanh(0.7978845608028654 * (x + 0.044715 * x * x * x)))


def _softplus(x):
    return jnp.maximum(x, 0.0) + jnp.log(1.0 + jnp.exp(-jnp.abs(x)))


def _ln_fwd(h, mo, g, b, cfg, name):
    def fn(h, mo, g, b):
        y = _layer_norm(DN_ALPHA * h + mo, g, b)
        return y, y
    return _rowwise(fn, [h, mo], [g, b], [(cfg.d, F32), (cfg.d, BF16)], [], cfg.tr, name)


def _ln_bwd(h, mo, g, b, douts, scales, cfg, name):
    def fn(h, mo, *rest):
        ds, (g, b) = rest[:-2], rest[-2:]
        dy = ds[0] * scales[0]
        for t, s in zip(ds[1:], scales[1:]):
            dy = dy + t * s
        _, vjp = jax.vjp(_layer_norm, DN_ALPHA * h + mo, g, b)
        dr, dg, db = vjp(dy)
        return dr, dr, dg, db
    d = cfg.d
    return _rowwise(fn, [h, mo] + list(douts), [g, b], [(d, F32), (d, BF16)], [(1, d), (1, d)], cfg.tr, name)


def _ffn_act(gu, cfg, name):
    f = cfg.ffn
    cb = _div_tile(f, 1536)
    nf = f // cb
    lp = gu.shape[0]
    tm = cfg.tr

    def fn(gate, up):
        return _silu(gate.astype(F32)) * up.astype(F32)
    return _tile_call(fn, (lp // tm, nf), [gu, gu],
                      [pl.BlockSpec((tm, cb), lambda i, j: (i, j)),
                       pl.BlockSpec((tm, cb), lambda i, j: (i, j + nf))],
                      [((lp, f), BF16)], [pl.BlockSpec((tm, cb), lambda i, j: (i, j))], name)[0]


def _ffn_act_bwd(gu, dact, cfg, name):
    f = cfg.ffn
    cb = _div_tile(f, 1536)
    nf = f // cb
    lp = gu.shape[0]
    tm = cfg.tr

    def fn(gate, up, da):
        gate, up, da = gate.astype(F32), up.astype(F32), da.astype(F32)
        sg = _sigmoid(gate)
        dgate = da * up * sg * (1.0 + gate * (1.0 - sg))
        dup = da * gate * sg
        return jnp.where(pl.program_id(1) < nf, dgate, dup)
    return _tile_call(fn, (lp // tm, 2 * nf), [gu, gu, dact],
                      [pl.BlockSpec((tm, cb), lambda i, j: (i, j % nf)),
                       pl.BlockSpec((tm, cb), lambda i, j: (i, j % nf + nf)),
                       pl.BlockSpec((tm, cb), lambda i, j: (i, j % nf))],
                      [((lp, 2 * f), BF16)], [pl.BlockSpec((tm, cb), lambda i, j: (i, j))], name)[0]


def _ffn_fwd(h, w_gu, w_down, cfg, tag):
    gu = _mm(h, w_gu, "nn", tag + "_gu", out_dtype=BF16)
    act = _ffn_act(gu, cfg, tag + "_act")
    fo = _mm(act, w_down, "nn", tag + "_down")
    return fo, (gu, act)


def _ffn_bwd(h, saved, dfo, w_gu, w_down, cfg, tag):
    gu, act = saved
    dact = _mm(dfo, w_down, "nt", tag + "_dact", out_dtype=BF16)
    d_wdown = _mm(act, dfo, "tn", tag + "_dwdown", out_dtype=BF16)
    dgu = _ffn_act_bwd(gu, dact, cfg, tag + "_dgu")
    dh = _mm(dgu, w_gu, "nt", tag + "_dh")
    d_wgu = _mm(h, dgu, "tn", tag + "_dwgu", out_dtype=BF16, out_slabs=w_gu.shape[0])
    return dh, d_wgu, d_wdown


def _small_call(fn, ins, outs, name):
    return _tile_call(fn, (1,), ins, [_whole(x.shape) for x in ins], [(s, F32) for s in outs],
                      [_whole(s) for s in outs], name)


def _perm(x):
    lp, c = x.shape
    return x.reshape(SUBLANES, lp // SUBLANES, c).transpose(1, 0, 2).reshape(lp, c)


def _unperm(x):
    lp, c = x.shape
    return x.reshape(lp // SUBLANES, SUBLANES, c).transpose(1, 0, 2).reshape(lp, c)


def _s5_disc(log_dt, a_re, a_im):
    dt = jnp.exp(log_dt)
    mag = jnp.exp(dt * a_re)
    ab_re = mag * jnp.cos(dt * a_im)
    ab_im = mag * jnp.sin(dt * a_im)
    den = a_re * a_re + a_im * a_im
    nr = ab_re - 1.0
    f_re = (nr * a_re + ab_im * a_im) / den
    f_im = (ab_im * a_re - nr * a_im) / den
    return ab_re, ab_im, f_re, f_im


def _s5_bbar(f_re, f_im, b_re, b_im):
    return f_re * b_re - f_im * b_im, f_re * b_im + f_im * b_re


def _bd_from(w, cfg):
    g, p, j = w.shape
    w4 = w.reshape(cfg.s5_nb, S5_GPB, p, j)
    eye = jnp.eye(S5_GPB, dtype=w.dtype)
    return jnp.einsum("bgpj,gh->bgjhp", w4, eye).reshape(cfg.s5_nb, S5_GPB * j, S5_GPB * p)


def _bd_to(blocks, cfg, p, j):
    b5 = blocks.reshape(cfg.s5_nb, S5_GPB, j, S5_GPB, p)
    eye = jnp.eye(S5_GPB, dtype=blocks.dtype)
    return jnp.einsum("bgjhp,gh->bgpj", b5, eye).reshape(cfg.s5_g, p, j)


def _bd_split(x, w1, w2, cfg, name):
    nb, ci, co = w1.shape
    lp, tm = x.shape[0], cfg.tm

    def fn(x, w1, w2):
        xb = x.astype(BF16)
        return _dot(xb, w1[0].astype(BF16)), _dot(xb, w2[0].astype(BF16))
    wspec = pl.BlockSpec((1, ci, co), lambda i, j: (j, 0, 0))
    ospec = pl.BlockSpec((tm, co), lambda i, j: (i, j))
    return _tile_call(fn, (lp // tm, nb), [x, w1, w2],
                      [pl.BlockSpec((tm, ci), lambda i, j: (i, j)), wspec, wspec],
                      [((lp, nb * co), F32)] * 2, [ospec, ospec], name)


def _bd_join(x1, x2, w1, w2, extra, scale, cfg, name):
    nb, ci, co = w1.shape
    lp, tm = x1.shape[0], cfg.tm

    def fn(x1, x2, w1, w2, e, s):
        return _bdot(x1, w1[0]) + _bdot(x2, w2[0]) + e * s
    xspec = pl.BlockSpec((tm, ci), lambda i, j: (i, j))
    wspec = pl.BlockSpec((1, ci, co), lambda i, j: (j, 0, 0))
    ospec = pl.BlockSpec((tm, co), lambda i, j: (i, j))
    return _tile_call(fn, (lp // tm, nb), [x1, x2, w1, w2, extra, scale],
                      [xspec, xspec, wspec, wspec, ospec, pl.BlockSpec((1, co), lambda i, j: (0, j))],
                      [((lp, nb * co), F32)], [ospec], name)[0]


def _bd_tn(a, b, nb, cfg, name):
    lp, tk = a.shape[0], cfg.tm
    ca, cb = a.shape[1] // nb, b.shape[1] // nb

    def fn(a, b):
        return _bdot(a, b, "tn")[None]
    return _tile_call(fn, (nb, lp // tk), [a, b],
                      [pl.BlockSpec((tk, ca), lambda j, k: (k, j)), pl.BlockSpec((tk, cb), lambda j, k: (k, j))],
                      [((nb, ca, cb), F32)], [pl.BlockSpec((1, ca, cb), lambda j, k: (j, 0, 0))],
                      name, acc=(0,), acc_axis=1)[0]


def _s5_scan(bu_re, bu_im, ab_re, ab_im, cfg, reverse, tag):
    lp, c = bu_re.shape
    nt = cfg.scan_tiles
    rows = lp // nt
    steps = rows // SUBLANES
    cb = _div_tile(c, 512)
    tmap = (lambda j, t: (nt - 1 - t, j)) if reverse else (lambda j, t: (t, j))
    row_spec = pl.BlockSpec((rows, cb), tmap)
    par_spec = pl.BlockSpec((1, cb), lambda j, t: (0, j))
    st_spec = pl.BlockSpec((SUBLANES, cb), lambda j, t: (0, j))
    grid = (c // cb, nt)
    full = jax.ShapeDtypeStruct((lp, c), F32)
    small = jax.ShapeDtypeStruct((SUBLANES, c), F32)

    def offset(k):
        kk = steps - 1 - k if reverse else k
        return pl.multiple_of(kk * SUBLANES, SUBLANES)

    def local_body(bre, bim, ar_ref, ai_ref, sre, sim, fre, fim, pre, pim, st):
        t = pl.program_id(1)

        @pl.when(t == 0)
        def _():
            zero = jnp.zeros((SUBLANES, cb), F32)
            st[0] = zero
            st[1] = zero
            st[2] = zero + 1.0
            st[3] = zero
        ar = jnp.broadcast_to(ar_ref[...], (SUBLANES, cb))
        ai = jnp.broadcast_to(ai_ref[...], (SUBLANES, cb))

        def step(k, carry):
            s_r, s_i, p_r, p_i = carry
            off = offset(k)
            n_r = ar * s_r - ai * s_i + bre[pl.ds(off, SUBLANES), :]
            n_i = ar * s_i + ai * s_r + bim[pl.ds(off, SUBLANES), :]
            sre[pl.ds(off, SUBLANES), :] = n_r
            sim[pl.ds(off, SUBLANES), :] = n_i
            return n_r, n_i, ar * p_r - ai * p_i, ar * p_i + ai * p_r
        s_r, s_i, p_r, p_i = lax.fori_loop(0, steps, step, (st[0], st[1], st[2], st[3]))
        st[0] = s_r
        st[1] = s_i
        st[2] = p_r
        st[3] = p_i

        @pl.when(t == nt - 1)
        def _():
            fre[...] = s_r
            fim[...] = s_i
            pre[...] = p_r
            pim[...] = p_i

    loc_re, loc_im, f_re, f_im, pn_re, pn_im = pl.pallas_call(
        local_body, out_shape=[full, full, small, small, small, small], grid=grid,
        in_specs=[row_spec, row_spec, par_spec, par_spec],
        out_specs=[row_spec, row_spec, st_spec, st_spec, st_spec, st_spec],
        scratch_shapes=[pltpu.VMEM((4, SUBLANES, cb), F32)],
        name=tag + "_local", compiler_params=_cparams(2))(bu_re, bu_im, ab_re, ab_im)

    def fix_body(lre, lim, fre, fim, pre, pim, ar_ref, ai_ref, sre, sim, st):
        t = pl.program_id(1)
        ar = jnp.broadcast_to(ar_ref[...], (SUBLANES, cb))
        ai = jnp.broadcast_to(ai_ref[...], (SUBLANES, cb))

        @pl.when(t == 0)
        def _():
            f_r, f_i = fre[...], fim[...]
            n_r, n_i = pre[0:1, :], pim[0:1, :]
            row = lax.broadcasted_iota(jnp.int32, (SUBLANES, cb), 0)
            c_r = jnp.zeros((1, cb), F32)
            c_i = jnp.zeros((1, cb), F32)
            car_r = jnp.zeros((SUBLANES, cb), F32)
            car_i = jnp.zeros((SUBLANES, cb), F32)
            order = range(SUBLANES - 2, -1, -1) if reverse else range(1, SUBLANES)
            for i in order:
                src = i + 1 if reverse else i - 1
                c_r, c_i = (n_r * c_r - n_i * c_i + f_r[src:src + 1, :],
                            n_r * c_i + n_i * c_r + f_i[src:src + 1, :])
                car_r = jnp.where(row == i, c_r, car_r)
                car_i = jnp.where(row == i, c_i, car_i)
            st[0] = car_r
            st[1] = car_i
            st[2] = ar
            st[3] = ai
        car_r = st[0]
        car_i = st[1]

        def step(k, carry):
            p_r, p_i = carry
            off = offset(k)
            sre[pl.ds(off, SUBLANES), :] = lre[pl.ds(off, SUBLANES), :] + p_r * car_r - p_i * car_i
            sim[pl.ds(off, SUBLANES), :] = lim[pl.ds(off, SUBLANES), :] + p_r * car_i + p_i * car_r
            return ar * p_r - ai * p_i, ar * p_i + ai * p_r
        p_r, p_i = lax.fori_loop(0, steps, step, (st[2], st[3]))
        st[2] = p_r
        st[3] = p_i

    return pl.pallas_call(
        fix_body, out_shape=[full, full], grid=grid,
        in_specs=[row_spec, row_spec, st_spec, st_spec, st_spec, st_spec, par_spec, par_spec],
        out_specs=[row_spec, row_spec], scratch_shapes=[pltpu.VMEM((4, SUBLANES, cb), F32)],
        name=tag + "_fix", compiler_params=_cparams(2))(loc_re, loc_im, f_re, f_im, pn_re, pn_im, ab_re, ab_im)


def _swap(x):
    return jnp.swapaxes(x, -1, -2)


def _s5_prep(w, cfg):
    g, p, j = cfg.s5_g, S5_STATE, S5_GROUP
    gp = g * p
    log_dt = w["l0_s5_log_dt"].reshape(g, 1)
    ab_re, ab_im, f_re, f_im = _small_call(_s5_disc, [log_dt, w["l0_s5_a_re"], w["l0_s5_a_im"]],
                                           [(g, p)] * 4, "s5_disc")
    b_re2 = w["l0_s5_b_re"].transpose(2, 0, 1).reshape(j, gp)
    b_im2 = w["l0_s5_b_im"].transpose(2, 0, 1).reshape(j, gp)
    f_re1, f_im1 = f_re.reshape(1, gp), f_im.reshape(1, gp)
    bb_re2, bb_im2 = _small_call(_s5_bbar, [f_re1, f_im1, b_re2, b_im2], [(j, gp)] * 2, "s5_bbar")
    bb_re = _bd_from(bb_re2.reshape(j, g, p).transpose(1, 2, 0), cfg).astype(BF16)
    bb_im = _bd_from(bb_im2.reshape(j, g, p).transpose(1, 2, 0), cfg).astype(BF16)
    c_re_t = _bd_from(w["l0_s5_c_re"].transpose(0, 2, 1), cfg).astype(BF16)
    c_imn_t = _bd_from(-w["l0_s5_c_im"].transpose(0, 2, 1), cfg).astype(BF16)
    return dict(log_dt=log_dt, f_re1=f_re1, f_im1=f_im1, b_re2=b_re2, b_im2=b_im2,
                ab_re=ab_re.reshape(1, gp), ab_im=ab_im.reshape(1, gp),
                bb_re=bb_re, bb_im=bb_im, bb_re_t=_swap(bb_re), bb_im_t=_swap(bb_im),
                c_re=_swap(c_re_t), c_imn=_swap(c_imn_t), c_re_t=c_re_t, c_imn_t=c_imn_t,
                d=w["l0_s5_d"].reshape(1, cfg.s5_w))


def _s5_fwd(u, prm, w_glu, cfg):
    tm = cfg.tr
    up = _perm(u)
    bu_re, bu_im = _bd_split(up, prm["bb_re"], prm["bb_im"], cfg, "s5_bu")
    s_re, s_im = _s5_scan(bu_re, bu_im, prm["ab_re"], prm["ab_im"], cfg, False, "s5_scan")
    y = _bd_join(s_re, s_im, prm["c_re"], prm["c_imn"], up, prm["d"], cfg, "s5_y")
    g = _rowwise(_gelu, [y], [], [(cfg.s5_w, F32)], [], tm, "s5_gelu")[0]
    z = _mm(g, w_glu, "nn", "s5_glu_mm")
    a_out = _rowwise(lambda g, z: g * _sigmoid(z), [g, z], [], [(cfg.s5_w, F32)], [], tm, "s5_glu")[0]
    return _unperm(a_out), (up, s_re, s_im, y, g, z)


def _s5_bwd(d_a_out, saved, prm, w, w_glu, cfg):
    up, s_re, s_im, y, g, z = saved
    tm, sw, nb = cfg.tr, cfg.s5_w, cfg.s5_nb
    gs, p, j = cfg.s5_g, S5_STATE, S5_GROUP
    gp = gs * p
    dap = _perm(d_a_out)

    def glu_bwd(da, g, z):
        sg = _sigmoid(z)
        return da * sg, da * g * sg * (1.0 - sg)
    dg1, dz = _rowwise(glu_bwd, [dap, g, z], [], [(sw, F32)] * 2, [], tm, "s5_glu_bwd")
    d_wglu = _mm(g, dz, "tn", "s5_dwglu", out_dtype=BF16)
    dg2 = _mm(dz, w_glu, "nt", "s5_dg2")

    def gelu_bwd(dg1, dg2, y, up, d):
        _, vjp = jax.vjp(_gelu, y)
        dy = vjp(dg1 + dg2)[0]
        return dy, dy * d, jnp.sum(dy * up, axis=0, keepdims=True)
    dy, dup_direct, dd = _rowwise(gelu_bwd, [dg1, dg2, y, up], [prm["d"]], [(sw, F32)] * 2, [(1, sw)], tm,
                                  "s5_gelu_bwd")
    ds_re, ds_im = _bd_split(dy, prm["c_re_t"], prm["c_imn_t"], cfg, "s5_ds")
    dc_re_t = _bd_tn(dy, s_re, nb, cfg, "s5_dcre")
    dc_imn_t = _bd_tn(dy, s_im, nb, cfg, "s5_dcim")
    g_re, g_im = _s5_scan(ds_re, ds_im, prm["ab_re"], -prm["ab_im"], cfg, True, "s5_adj")

    cb = _div_tile(gp, 512)
    per = tm // SUBLANES
    spec = pl.BlockSpec((tm, cb), lambda jj, i: (i, jj))
    before = pl.BlockSpec((SUBLANES, cb), lambda jj, i: (jnp.maximum(i * per - 1, 0), jj))
    final = pl.BlockSpec((SUBLANES, cb), lambda jj, i: (cfg.lp // SUBLANES - 1, jj))
    aspec = pl.BlockSpec((1, cb), lambda jj, i: (0, jj))

    def dab(g_r, g_i, s_r, s_i, h_r, h_i, l_r, l_i):
        first = pl.program_id(1) == 0
        row8 = lax.broadcasted_iota(jnp.int32, (SUBLANES, cb), 0)

        def prev(s, h, l):
            wrap = jnp.where(row8 == 0, 0.0, pltpu.roll(l, 1, axis=0))
            return jnp.concatenate([jnp.where(first, wrap, h), s[:tm - SUBLANES]], axis=0)
        p_r, p_i = prev(s_r, h_r, l_r), prev(s_i, h_i, l_i)
        return (jnp.sum(g_r * p_r + g_i * p_i, axis=0, keepdims=True),
                jnp.sum(g_i * p_r - g_r * p_i, axis=0, keepdims=True))
    dab_re, dab_im = _tile_call(dab, (gp // cb, cfg.lp // tm), [g_re, g_im, s_re, s_im, s_re, s_im, s_re, s_im],
                                [spec] * 4 + [before] * 2 + [final] * 2,
                                [((1, gp), F32)] * 2, [aspec] * 2, "s5_dab", acc=(0, 1), acc_axis=1)
    no_scale = jnp.ones((1, sw), F32)
    dup = _bd_join(g_re, g_im, prm["bb_re_t"], prm["bb_im_t"], dup_direct, no_scale, cfg, "s5_dup")
    dbb_re_blk = _bd_tn(up, g_re, nb, cfg, "s5_dbbre")
    dbb_im_blk = _bd_tn(up, g_im, nb, cfg, "s5_dbbim")

    def to2(blk):
        return _bd_to(blk, cfg, p, j).transpose(2, 0, 1).reshape(j, gp)

    def bbar_bwd(f_re, f_im, b_re, b_im, dr, di):
        _, vjp = jax.vjp(_s5_bbar, f_re, f_im, b_re, b_im)
        return vjp((dr, di))
    df_re, df_im, db_re2, db_im2 = _small_call(
        bbar_bwd, [prm["f_re1"], prm["f_im1"], prm["b_re2"], prm["b_im2"], to2(dbb_re_blk), to2(dbb_im_blk)],
        [(1, gp), (1, gp), (j, gp), (j, gp)], "s5_bbar_bwd")

    def disc_bwd(log_dt, a_re, a_im, d1, d2, d3, d4):
        _, vjp = jax.vjp(_s5_disc, log_dt, a_re, a_im)
        return vjp((d1, d2, d3, d4))
    dlog_dt, da_re, da_im = _small_call(
        disc_bwd, [prm["log_dt"], w["l0_s5_a_re"], w["l0_s5_a_im"], dab_re.reshape(gs, p), dab_im.reshape(gs, p),
                   df_re.reshape(gs, p), df_im.reshape(gs, p)], [(gs, 1), (gs, p), (gs, p)], "s5_disc_bwd")
    grads = {
        "l0_s5_log_dt": dlog_dt.reshape(gs), "l0_s5_a_re": da_re, "l0_s5_a_im": da_im,
        "l0_s5_b_re": db_re2.reshape(j, gs, p).transpose(1, 2, 0),
        "l0_s5_b_im": db_im2.reshape(j, gs, p).transpose(1, 2, 0),
        "l0_s5_c_re": _bd_to(dc_re_t, cfg, p, j).transpose(0, 2, 1),
        "l0_s5_c_im": -_bd_to(dc_imn_t, cfg, p, j).transpose(0, 2, 1),
        "l0_s5_d": dd.reshape(sw), "l0_s5_w_glu": d_wglu,
    }
    return _unperm(dup), grads


def _shift(x):
    return jnp.concatenate([jnp.zeros((ATT_SHIFT, x.shape[1]), x.dtype), x[:-ATT_SHIFT]], axis=0)


def _unshift(x):
    return jnp.concatenate([x[ATT_SHIFT:], jnp.zeros((ATT_SHIFT, x.shape[1]), x.dtype)], axis=0)


def _rope_tables(cfg):
    pos = (jnp.arange(cfg.lp) - ATT_SHIFT).astype(F32)
    inv = ROPE_BASE ** (-jnp.arange(0, MLA_ROPE, 2, dtype=F32) / MLA_ROPE)
    ang = pos[:, None] * inv[None, :]
    cos, sin = jnp.cos(ang), jnp.sin(ang)
    z = jnp.zeros((cfg.lp, LANES - MLA_ROPE), F32)
    return jnp.concatenate([cos, cos, z], axis=1), jnp.concatenate([-sin, sin, z], axis=1)


def _swap_halves(x):
    half = MLA_ROPE // 2
    lane = lax.broadcasted_iota(jnp.int32, x.shape, 1)
    left = pltpu.roll(x, LANES - half, axis=1)
    right = pltpu.roll(x, half, axis=1)
    return jnp.where(lane < half, left, jnp.where(lane < 2 * half, right, 0.0))


def _rope(x, cosp, sinp):
    return x * cosp + _swap_halves(x) * sinp


def _rope_t(dy, cosp, sinp):
    return dy * cosp + _swap_halves(dy * sinp)


def _visible(i, j, t):
    row = i * t + lax.broadcasted_iota(jnp.int32, (t, t), 0)
    col = j * t + lax.broadcasted_iota(jnp.int32, (t, t), 1)
    return jnp.logical_and(col // CHUNK <= row // CHUNK, col >= ATT_SHIFT)


class _NoPlan:
    n = n_out = 0
    arrays, out_shapes, scratch = [], [], []


def _side_refs(refs, n_in, n_out, n_scratch, side):
    a = n_in + side.n
    b = a + n_out + side.n_out
    c = b + n_scratch
    own = refs[:n_in] + refs[a:a + n_out] + refs[b:c]
    return own, refs[n_in:a] + refs[a + n_out:b] + refs[c:]


def _needs_mask(i, j):
    return jnp.logical_or(i == j, j == 0)


def _flash_fwd(q, kv, kr, cfg, side=None):
    lp, t, nh = cfg.lp, cfg.tq, cfg.heads
    n = lp // t
    scale = (MLA_NOPE + MLA_ROPE) ** -0.5
    side = side or _NoPlan()

    def body(*refs):
        (q_ref, kv_ref, kr_ref, o_ref, lse_ref, m_s, l_s, acc_s), ex = _side_refs(refs, 3, 2, 3, side)
        hh, i, j = pl.program_id(0), pl.program_id(1), pl.program_id(2)
        if side.n:
            at_tile0 = jnp.logical_and(i == 0, j == 0)
            pl.when(jnp.logical_and(hh == 0, at_tile0))(lambda: side.start(ex))
            pl.when(jnp.logical_and(hh == nh // 2, at_tile0))(lambda: side.relay(ex))

        @pl.when(j == 0)
        def _():
            m_s[...] = jnp.full((t, 1), NEG, F32)
            l_s[...] = jnp.zeros((t, 1), F32)
            acc_s[...] = jnp.zeros((t, MLA_V), F32)

        @pl.when(j <= i)
        def _():
            s = (_dot(q_ref[:, :MLA_NOPE], kv_ref[:, :MLA_NOPE], "nt")
                 + _dot(q_ref[:, MLA_NOPE:], kr_ref[...], "nt")) * scale
            s = lax.cond(_needs_mask(i, j), lambda v: jnp.where(_visible(i, j, t), v, NEG), lambda v: v, s)
            m_old = m_s[...]
            m_new = jnp.maximum(m_old, jnp.max(s, axis=1, keepdims=True))
            alpha = jnp.exp(m_old - m_new)
            p = jnp.exp(s - m_new)
            l_s[...] = alpha * l_s[...] + jnp.sum(p, axis=1, keepdims=True)
            acc_s[...] = alpha * acc_s[...] + _dot(p.astype(BF16), kv_ref[:, MLA_NOPE:])
            m_s[...] = m_new

        @pl.when(j == i)
        def _():
            o_ref[...] = acc_s[...] / l_s[...]
            lse_ref[...] = jnp.broadcast_to(m_s[...] + jnp.log(l_s[...]), (t, MLA_V))

        if side.n:
            pl.when(jnp.logical_and(hh == nh - 1, jnp.logical_and(i == n - 1, j == n - 1)))(lambda: side.finish(ex))

    res = pl.pallas_call(
        body, out_shape=[jax.ShapeDtypeStruct((lp, nh * MLA_V), F32)] * 2 + side.out_shapes, grid=(nh, n, n),
        in_specs=[pl.BlockSpec((t, MLA_QW), lambda h, i, j: (i, h)),
                  pl.BlockSpec((t, MLA_QW), lambda h, i, j: (jnp.minimum(i, j), h)),
                  pl.BlockSpec((t, LANES), lambda h, i, j: (jnp.minimum(i, j), 0))] + [HBM_SPEC] * side.n,
        out_specs=[pl.BlockSpec((t, MLA_V), lambda h, i, j: (i, h))] * 2 + [HBM_SPEC] * side.n_out,
        scratch_shapes=[pltpu.VMEM((t, 1), F32), pltpu.VMEM((t, 1), F32), pltpu.VMEM((t, MLA_V), F32)] + side.scratch,
        name="mla_flash_fwd", compiler_params=_cparams(3))(q, kv, kr, *side.arrays)
    return res[0], res[1], res[2:]


def _flash_bwd(q, kv, kr, o, lse, do, cfg, side=None):
    lp, t, nh = cfg.lp, cfg.tq, cfg.heads
    n = lp // t
    scale = (MLA_NOPE + MLA_ROPE) ** -0.5
    side = side or _NoPlan()

    def body(*refs):
        own, ex = _side_refs(refs, 6, 3, 2, side)
        q_ref, kv_ref, kr_ref, o_ref, lse_ref, do_ref, dq_ref, dkv_ref, dkr_ref, dkv_s, dkr_s = own
        hh, j, i = pl.program_id(0), pl.program_id(1), pl.program_id(2)
        if side.n:
            pl.when(jnp.logical_and(hh == 0, jnp.logical_and(i == 0, j == 0)))(lambda: side.start(ex))

        @pl.when(jnp.logical_and(j == 0, i == 0))
        def _():
            dq_ref[...] = jnp.zeros((lp, MLA_QW), F32)

        @pl.when(i == j)
        def _():
            dkv_s[...] = jnp.zeros((t, MLA_QW), F32)
            dkr_s[...] = jnp.zeros((t, LANES), F32)

        @pl.when(i >= j)
        def _():
            qn, qr = q_ref[:, :MLA_NOPE], q_ref[:, MLA_NOPE:]
            kn, v = kv_ref[:, :MLA_NOPE], kv_ref[:, MLA_NOPE:]
            krv = kr_ref[...]
            s = (_dot(qn, kn, "nt") + _dot(qr, krv, "nt")) * scale
            p = jnp.exp(s - lse_ref[:, :1])
            p = lax.cond(_needs_mask(i, j), lambda v: jnp.where(_visible(i, j, t), v, 0.0), lambda v: v, p)
            dov = do_ref[...]
            dob = dov.astype(BF16)
            dp = _dot(dob, v, "nt")
            delta = jnp.sum(dov * o_ref[...], axis=1, keepdims=True)
            ds = (p * (dp - delta) * scale).astype(BF16)
            dkv_s[:, MLA_NOPE:] += _dot(p.astype(BF16), dob, "tn")
            dkv_s[:, :MLA_NOPE] += _dot(ds, qn, "tn")
            dkr_s[...] += _dot(ds, qr, "tn")
            off = pl.multiple_of(i * t, t)
            dq_ref[pl.ds(off, t), :MLA_NOPE] += _dot(ds, kn)
            dq_ref[pl.ds(off, t), MLA_NOPE:] += _dot(ds, krv)

        @pl.when(i == n - 1)
        def _():
            dkv_ref[...] = dkv_s[...]
            dkr_ref[0] = dkr_s[...]

        if side.n:
            pl.when(jnp.logical_and(hh == nh - 1, jnp.logical_and(i == n - 1, j == n - 1)))(lambda: side.finish(ex))

    qspec = pl.BlockSpec((t, MLA_QW), lambda h, j, i: (jnp.maximum(i, j), h))
    ospec = pl.BlockSpec((t, MLA_V), lambda h, j, i: (jnp.maximum(i, j), h))
    res = pl.pallas_call(
        body, out_shape=[jax.ShapeDtypeStruct((lp, nh * MLA_QW), F32), jax.ShapeDtypeStruct((lp, nh * MLA_QW), F32),
                         jax.ShapeDtypeStruct((nh, lp, LANES), F32)] + side.out_shapes, grid=(nh, n, n),
        in_specs=[qspec, pl.BlockSpec((t, MLA_QW), lambda h, j, i: (j, h)),
                  pl.BlockSpec((t, LANES), lambda h, j, i: (j, 0)), ospec, ospec, ospec] + [HBM_SPEC] * side.n,
        out_specs=[pl.BlockSpec((lp, MLA_QW), lambda h, j, i: (0, h)),
                   pl.BlockSpec((t, MLA_QW), lambda h, j, i: (j, h)),
                   pl.BlockSpec((1, t, LANES), lambda h, j, i: (h, j, 0))] + [HBM_SPEC] * side.n_out,
        scratch_shapes=[pltpu.VMEM((t, MLA_QW), F32), pltpu.VMEM((t, LANES), F32)] + side.scratch,
        name="mla_flash_bwd", compiler_params=_cparams(3))(q, kv, kr, o, lse, do, *side.arrays)
    return res[0], res[1], res[2], res[3:]


def _pad_heads(w, nh, width):
    r = w.shape[0]
    w3 = w.reshape(r, nh, width)
    return jnp.pad(w3, ((0, 0), (0, 0), (0, MLA_QW - width))).reshape(r, nh * MLA_QW)


def _mla_fwd(q_lat, kv_lat, k_rope_raw, wq, w_uq_p, w_ukv, cfg, side=None):
    tm, nh = cfg.tr, cfg.heads
    ql, kl = _shift(q_lat), _shift(kv_lat)
    kr_raw = jnp.pad(_shift(k_rope_raw), ((0, 0), (0, LANES - MLA_ROPE)))
    cosp, sinp = _rope_tables(cfg)
    qg, kg = wq["l0_mla_q_norm"].reshape(1, -1), wq["l0_mla_kv_norm"].reshape(1, -1)
    qn, kvn = _rowwise(lambda a, b, g1, g2: (_rms(a, g1), _rms(b, g2)), [ql, kl], [qg, kg],
                       [(cfg.q_rank, F32), (cfg.kv_rank, F32)], [], tm, "mla_norm")
    q0 = _mm(qn, w_uq_p, "nn", "mla_q")
    kv = _mm(kvn, w_ukv, "nn", "mla_kv", out_dtype=BF16)

    def rope_fn(q0, kr, cosp, sinp):
        parts = []
        for h in range(nh):
            parts.append(q0[:, h * MLA_QW:h * MLA_QW + MLA_NOPE])
            parts.append(_rope(q0[:, h * MLA_QW + MLA_NOPE:(h + 1) * MLA_QW], cosp, sinp))
        return jnp.concatenate(parts, axis=1), _rope(kr, cosp, sinp)
    q, kr = _rowwise(rope_fn, [q0, kr_raw, cosp, sinp], [], [(nh * MLA_QW, BF16), (LANES, BF16)], [], tm,
                     "mla_rope")
    o, lse, side_out = _flash_fwd(q, kv, kr, cfg, side)
    return _unshift(o), (ql, kl, qn, kvn, q, kv, kr, o, lse, cosp, sinp), side_out


def _mla_bwd(d_b_out, saved, wq, w_uq_p, w_ukv, cfg, side=None):
    ql, kl, qn, kvn, q, kv, kr, o, lse, cosp, sinp = saved
    tm, nh, lp = cfg.tr, cfg.heads, cfg.lp
    dq, dkv, dkr_h, side_out = _flash_bwd(q, kv, kr, o, lse, _shift(d_b_out), cfg, side)

    def rope_bwd(dq, dkr_h, cosp, sinp):
        parts = []
        for h in range(nh):
            parts.append(dq[:, h * MLA_QW:h * MLA_QW + MLA_NOPE])
            parts.append(_rope_t(dq[:, h * MLA_QW + MLA_NOPE:(h + 1) * MLA_QW], cosp, sinp))
        dkr = dkr_h[0]
        for h in range(1, nh):
            dkr = dkr + dkr_h[h]
        return jnp.concatenate(parts, axis=1), _rope_t(dkr, cosp, sinp)
    dq0, dkr_raw = _tile_call(
        rope_bwd, (lp // tm,), [dq, dkr_h, cosp, sinp],
        [_rows(tm, nh * MLA_QW), pl.BlockSpec((nh, tm, LANES), lambda i: (0, i, 0)), _rows(tm, LANES),
         _rows(tm, LANES)],
        [((lp, nh * MLA_QW), F32), ((lp, LANES), F32)], [_rows(tm, nh * MLA_QW), _rows(tm, LANES)], "mla_rope_bwd")
    d_wuq_p = _mm(qn, dq0, "tn", "mla_dwuq")
    dqn = _mm(dq0, w_uq_p, "nt", "mla_dqn")
    d_wukv = _mm(kvn, dkv, "tn", "mla_dwukv")
    dkvn = _mm(dkv, w_ukv, "nt", "mla_dkvn")
    qg, kg = wq["l0_mla_q_norm"].reshape(1, -1), wq["l0_mla_kv_norm"].reshape(1, -1)

    def norm_bwd(ql, kl, dqn, dkvn, g1, g2):
        _, vjp1 = jax.vjp(_rms, ql, g1)
        _, vjp2 = jax.vjp(_rms, kl, g2)
        dql, dg1 = vjp1(dqn)
        dkl, dg2 = vjp2(dkvn)
        return dql, dkl, dg1, dg2
    dql, dkl, dg1, dg2 = _rowwise(norm_bwd, [ql, kl, dqn, dkvn], [qg, kg],
                                  [(cfg.q_rank, F32), (cfg.kv_rank, F32)], [(1, cfg.q_rank), (1, cfg.kv_rank)], tm,
                                  "mla_norm_bwd")
    width = MLA_NOPE + MLA_ROPE
    d_wuq = d_wuq_p.reshape(cfg.q_rank, nh, MLA_QW)[:, :, :width].reshape(cfg.q_rank, nh * width)
    grads = {"l0_mla_q_norm": dg1.reshape(-1), "l0_mla_kv_norm": dg2.reshape(-1), "l0_mla_w_uq": d_wuq,
             "l0_mla_w_ukv": d_wukv}
    return _unshift(dql), _unshift(dkl), _unshift(dkr_raw[:, :MLA_ROPE]), grads, side_out


def _conv_taps(x, halo, first):
    halo = jnp.where(first, 0.0, halo)
    row8 = lax.broadcasted_iota(jnp.int32, halo.shape, 0)
    taps = []
    for s in range(SSD_CONV - 1, 0, -1):
        r = pltpu.roll(x, s, axis=0)
        top = jnp.where(row8 < s, pltpu.roll(halo, s, axis=0), r[:SUBLANES])
        taps.append(jnp.concatenate([top, r[SUBLANES:]], axis=0))
    taps.append(x)
    return taps


def _conv_specs(cfg, lp):
    tm = cfg.tr
    cb = _div_tile(math.gcd(cfg.ssd_inner, cfg.gn), 1024)
    off = cfg.ssd_inner // cb
    per = tm // SUBLANES
    nrow = lp // tm
    main = pl.BlockSpec((tm, cb), lambda i, j: (i, j + off))
    before = pl.BlockSpec((SUBLANES, cb), lambda i, j: (jnp.maximum(i * per - 1, 0), j + off))
    own = pl.BlockSpec((tm, cb), lambda i, j: (i, j))
    after = pl.BlockSpec((SUBLANES, cb), lambda i, j: (jnp.minimum((i + 1) * per, nrow * per - 1), j))
    par = lambda r: pl.BlockSpec((r, cb), lambda i, j: (0, j))
    return tm, cb, nrow, main, before, own, after, par


def _conv_fwd(zx, conv_w, conv_b, cfg):
    lp = zx.shape[0]
    tm, cb, nrow, main, before, own, after, par = _conv_specs(cfg, lp)

    def fn(x, halo, w, b):
        taps = _conv_taps(x, halo, pl.program_id(0) == 0)
        pre = b
        for k in range(SSD_CONV):
            pre = pre + taps[k] * w[k:k + 1, :]
        return _silu(pre)
    return _tile_call(fn, (nrow, cfg.conv_dim // cb), [zx, zx, conv_w, conv_b],
                      [main, before, par(SSD_CONV), par(1)], [((lp, cfg.conv_dim), F32)], [own], "ssd_conv")[0]


def _conv_bwd(zx, conv_w, conv_b, dxs, dbm, dcm, dzx, cfg):
    lp = zx.shape[0]
    tm, cb, nrow, main, before, own, after, par = _conv_specs(cfg, lp)
    ncb = cfg.conv_dim // cb
    nx, nb = cfg.ssd_inner // cb, cfg.gn // cb
    off = nx

    def fn1(x, halo, w, b, d1, d2, d3):
        j = pl.program_id(0)
        da = jnp.where(j < nx, d1, jnp.where(j < nx + nb, d2, d3))
        taps = _conv_taps(x, halo, pl.program_id(1) == 0)
        pre = b
        for k in range(SSD_CONV):
            pre = pre + taps[k] * w[k:k + 1, :]
        sg = _sigmoid(pre)
        dpre = da * sg * (1.0 + pre * (1.0 - sg))
        row8 = lax.broadcasted_iota(jnp.int32, (SUBLANES, cb), 0)
        dw = jnp.zeros((SUBLANES, cb), F32)
        for k in range(SSD_CONV):
            dw = jnp.where(row8 == k, jnp.sum(dpre * taps[k], axis=0, keepdims=True), dw)
        return dpre, dw, jnp.sum(dpre, axis=0, keepdims=True)
    sw = lambda spec: pl.BlockSpec(spec.block_shape, lambda j, i, f=spec.index_map: f(i, j))
    piece = lambda lo, n: pl.BlockSpec((tm, cb), lambda j, i: (i, jnp.clip(j - lo, 0, n - 1)))
    dpre, dw, db = _tile_call(
        fn1, (ncb, nrow), [zx, zx, conv_w, conv_b, dxs, dbm, dcm],
        [sw(main), sw(before), sw(par(SSD_CONV)), sw(par(1)), piece(0, nx), piece(nx, nb), piece(nx + nb, nb)],
        [((lp, cfg.conv_dim), F32), ((SUBLANES, cfg.conv_dim), F32), ((1, cfg.conv_dim), F32)],
        [sw(own), sw(par(SUBLANES)), sw(par(1))], "ssd_conv_bwd1", acc=(1, 2), acc_axis=1)

    def fn2(dp, nxt, w):
        nxt = jnp.where(pl.program_id(0) == nrow - 1, 0.0, nxt)
        row8 = lax.broadcasted_iota(jnp.int32, nxt.shape, 0)
        dx = dp * w[SSD_CONV - 1:SSD_CONV, :]
        for s in range(1, SSD_CONV):
            r = pltpu.roll(dp, tm - s, axis=0)
            bot = jnp.where(row8 >= SUBLANES - s, pltpu.roll(nxt, SUBLANES - s, axis=0), r[tm - SUBLANES:])
            up = jnp.concatenate([r[:tm - SUBLANES], bot], axis=0)
            dx = dx + up * w[SSD_CONV - 1 - s:SSD_CONV - s, :]
        return dx
    dzx = _tile_call(fn2, (nrow, ncb), [dpre, dpre, conv_w], [own, after, par(SSD_CONV)],
                     [(dzx.shape, BF16)], [main], "ssd_conv_bwd2", fill=(dzx, 0))[0]
    return dzx, dw, db


def _ssd_common(x_ref, b_ref, c_ref, dt_ref, dtt_ref, ar_ref, ac_ref, h):
    q = SSD_BLOCK
    x, bm, cm = x_ref[...], b_ref[...], c_ref[...]
    dt, dtt = dt_ref[0], dtt_ref[0]
    row = lax.broadcasted_iota(jnp.int32, (q, q), 0)
    col = lax.broadcasted_iota(jnp.int32, (q, q), 1)
    tri = row >= col
    cs = _dot(tri.astype(F32), dt * ar_ref[0], precision=HI)
    cst = _dot(dtt * ac_ref[0], (row <= col).astype(F32), precision=HI)
    g = _bdot(cm, bm, "nt")
    ch = _bdot(cm, h)
    hpg, gw = dt.shape[1], x.shape[1]
    e = (lax.broadcasted_iota(jnp.int32, (hpg, gw), 1) // SSD_HEAD_DIM
         == lax.broadcasted_iota(jnp.int32, (hpg, gw), 0)).astype(F32)
    et = (lax.broadcasted_iota(jnp.int32, (gw, hpg), 0) // SSD_HEAD_DIM
          == lax.broadcasted_iota(jnp.int32, (gw, hpg), 1)).astype(F32)
    spread = lambda v: _dot(v, e, precision=HI)
    gather = lambda v: _dot(v, et, precision=HI)
    return x, bm, cm, dt, tri, cs, cst, g, ch, spread, gather


def _ssd_specs(cfg, rev):
    q, n, gw, hpg = SSD_BLOCK, SSD_STATE, cfg.gw, cfg.hpg
    nc = cfg.lp // q
    cc = (lambda c: nc - 1 - c) if rev else (lambda c: c)
    boff = cfg.ssd_inner // n
    return dict(
        x=pl.BlockSpec((q, gw), lambda g, c: (cc(c), g)),
        b=pl.BlockSpec((q, n), lambda g, c: (cc(c), boff + g)),
        c=pl.BlockSpec((q, n), lambda g, c: (cc(c), boff + SSD_GROUPS + g)),
        bc_out=pl.BlockSpec((q, n), lambda g, c: (cc(c), g)),
        dt=pl.BlockSpec((1, q, hpg), lambda g, c: (g, cc(c), 0)),
        dtt=pl.BlockSpec((1, hpg, q), lambda g, c: (g, 0, cc(c))),
        ar=pl.BlockSpec((1, 1, hpg), lambda g, c: (g, 0, 0)),
        ac=pl.BlockSpec((1, hpg, 1), lambda g, c: (g, 0, 0)),
        h=pl.BlockSpec((1, n, gw), lambda g, c: (cc(c), 0, g)))


def _ssd_fwd(xbc, dt_g, dtt_g, a_row, a_col, cfg):
    q, n, gw, hpg, lp = SSD_BLOCK, SSD_STATE, cfg.gw, cfg.hpg, cfg.lp
    nc = lp // q
    hd = SSD_HEAD_DIM
    sp = _ssd_specs(cfg, False)

    def body(x_ref, b_ref, c_ref, dt_ref, dtt_ref, ar_ref, ac_ref, y_ref, hp_ref, h_s):
        @pl.when(pl.program_id(1) == 0)
        def _():
            h_s[...] = jnp.zeros((n, gw), F32)
        h = h_s[...]
        hp_ref[0] = h
        x, bm, cm, dt, tri, cs, cst, g, ch, spread, _ = _ssd_common(x_ref, b_ref, c_ref, dt_ref, dtt_ref, ar_ref,
                                                                    ac_ref, h)
        last = cs[q - 1:q, :]
        xdt = x * spread(dt)
        y_off = spread(jnp.exp(cs)) * ch
        xw = xdt * spread(jnp.exp(last - cs))
        for r in range(hpg):
            sl = slice(r * hd, (r + 1) * hd)
            lm = jnp.exp(jnp.where(tri, cs[:, r:r + 1] - cst[r:r + 1, :], NEG))
            y_ref[:, sl] = _bdot(g * lm, xdt[:, sl]) + y_off[:, sl]
        h_s[...] = h * spread(jnp.exp(last)) + _bdot(bm, xw, "tn")

    return pl.pallas_call(
        body, out_shape=[jax.ShapeDtypeStruct((lp, cfg.ssd_inner), F32),
                         jax.ShapeDtypeStruct((nc, n, cfg.ssd_inner), F32)],
        grid=(SSD_GROUPS, nc),
        in_specs=[sp["x"], sp["b"], sp["c"], sp["dt"], sp["dtt"], sp["ar"], sp["ac"]],
        out_specs=[sp["x"], sp["h"]],
        scratch_shapes=[pltpu.VMEM((n, gw), F32)],
        name="ssd_scan", compiler_params=_cparams(2))(xbc, xbc, xbc, dt_g, dtt_g, a_row, a_col)


def _ssd_bwd(xbc, dt_g, dtt_g, a_row, a_col, hprev, dy, dx_gate, cfg):
    q, n, gw, hpg, lp = SSD_BLOCK, SSD_STATE, cfg.gw, cfg.hpg, cfg.lp
    nc = lp // q
    hd = SSD_HEAD_DIM
    sp = _ssd_specs(cfg, True)

    def body(x_ref, b_ref, c_ref, dt_ref, dtt_ref, ar_ref, ac_ref, hp_ref, dy_ref, dxg_ref,
             dx_ref, db_ref, dc_ref, ddt_ref, da_ref, dh_s, dxdt_s):
        @pl.when(pl.program_id(1) == 0)
        def _():
            dh_s[...] = jnp.zeros((n, gw), F32)
            da_ref[...] = jnp.zeros((1, 1, hpg), F32)
        h = hp_ref[0]
        dhn = dh_s[...]
        dy = dy_ref[...]
        x, bm, cm, dt, tri, cs, cst, g, ch, spread, gather = _ssd_common(x_ref, b_ref, c_ref, dt_ref, dtt_ref,
                                                                         ar_ref, ac_ref, h)
        last = cs[q - 1:q, :]
        e, wv, elast = jnp.exp(cs), jnp.exp(last - cs), jnp.exp(last)
        dt_x, w_x = spread(dt), spread(wv)
        xdt = x * dt_x
        dye = dy * spread(e)
        xw = xdt * w_x
        bd = _bdot(bm, dhn)
        de = gather(dy * ch)
        dw = gather(xdt * bd)
        hsum = gather(jnp.sum(dhn * h, axis=0, keepdims=True))
        head_lane = lax.broadcasted_iota(jnp.int32, (q, hpg), 1)
        head_row = lax.broadcasted_iota(jnp.int32, (hpg, q), 0)
        z_rows = jnp.zeros((q, hpg), F32)
        z_cols = jnp.zeros((hpg, q), F32)
        dg = jnp.zeros((q, q), F32)
        for r in range(hpg):
            sl = slice(r * hd, (r + 1) * hd)
            lm = jnp.exp(jnp.where(tri, cs[:, r:r + 1] - cst[r:r + 1, :], NEG))
            m = g * lm
            dyr = dy[:, sl]
            dxdt_s[:, sl] = _bdot(m, dyr, "tn")
            dm = _bdot(dyr, xdt[:, sl], "nt")
            dg = dg + dm * lm
            z = dm * m
            z_rows = jnp.where(head_lane == r, jnp.sum(z, axis=1, keepdims=True), z_rows)
            z_cols = jnp.where(head_row == r, jnp.sum(z, axis=0, keepdims=True), z_cols)
        dxdt = dxdt_s[...] + w_x * bd
        is_last = lax.broadcasted_iota(jnp.int32, (q, 1), 0) == q - 1
        extra = jnp.sum(dw * wv, axis=0, keepdims=True) + elast * hsum
        eye = (lax.broadcasted_iota(jnp.int32, (hpg, hpg), 0)
               == lax.broadcasted_iota(jnp.int32, (hpg, hpg), 1)).astype(F32)
        dcs = (z_rows + de * e - dw * wv + jnp.where(is_last, extra, 0.0)
               - _dot(z_cols, eye, "tn", precision=HI))
        row = lax.broadcasted_iota(jnp.int32, (q, q), 0)
        col = lax.broadcasted_iota(jnp.int32, (q, q), 1)
        dda = _dot((row <= col).astype(F32), dcs, precision=HI)
        ddt_ref[0] = dda * ar_ref[0] + gather(dxdt * x)
        da_ref[0] += jnp.sum(dda * dt, axis=0, keepdims=True)
        dx_ref[...] = dxdt * dt_x + dxg_ref[...]
        dc_ref[...] = _bdot(dg, bm) + _bdot(dye, h, "nt")
        db_ref[...] = _bdot(dg, cm, "tn") + _bdot(xw, dhn, "nt")
        dh_s[...] = dhn * spread(elast) + _bdot(cm, dye, "tn")

    return pl.pallas_call(
        body, out_shape=[jax.ShapeDtypeStruct((lp, cfg.ssd_inner), F32), jax.ShapeDtypeStruct((lp, cfg.gn), F32),
                         jax.ShapeDtypeStruct((lp, cfg.gn), F32), jax.ShapeDtypeStruct((SSD_GROUPS, lp, hpg), F32),
                         jax.ShapeDtypeStruct((SSD_GROUPS, 1, hpg), F32)],
        grid=(SSD_GROUPS, nc),
        in_specs=[sp["x"], sp["b"], sp["c"], sp["dt"], sp["dtt"], sp["ar"], sp["ac"], sp["h"], sp["x"], sp["x"]],
        out_specs=[sp["x"], sp["bc_out"], sp["bc_out"], sp["dt"], sp["ar"]],
        scratch_shapes=[pltpu.VMEM((n, gw), F32), pltpu.VMEM((q, gw), F32)],
        name="ssd_scan_bwd", compiler_params=_cparams(2))(xbc, xbc, xbc, dt_g, dtt_g, a_row, a_col, hprev, dy, dx_gate)


def _gate_fn(y, xs, z, dexp, ng):
    return _rms((y + dexp * xs) * _silu(z), ng)


def _gate_specs(cfg):
    tm, gw = cfg.tm, cfg.gw
    blk = pl.BlockSpec((tm, gw), lambda g, i: (i, g))
    par = pl.BlockSpec((1, gw), lambda g, i: (0, g))
    return blk, par


def _mamba_fwd(h, w, w_in_t, w_out, conv_w, cfg):
    lp, tm, nh, hpg, inner = cfg.lp, cfg.tm, cfg.ssd_heads, cfg.hpg, cfg.ssd_inner
    zx = _mm(h, w_in_t, "nt", "l1_in")
    conv_b = w["l1_conv_b"].reshape(1, -1)
    xbc = _conv_fwd(zx, conv_w, conv_b, cfg)
    dt_raw = zx[:, inner + cfg.conv_dim:inner + cfg.conv_dim + nh]
    dt_bias = w["l1_dt_bias"].reshape(1, nh)
    a_log = w["l1_a_log"].reshape(1, nh)
    dt = _rowwise(lambda r, b: _softplus(r + b), [dt_raw], [dt_bias], [(nh, F32)], [], tm, "ssd_dt")[0]
    a = _small_call(lambda al: -jnp.exp(al), [a_log], [(1, nh)], "ssd_a")[0]
    dt_g = dt.reshape(lp, SSD_GROUPS, hpg).transpose(1, 0, 2)
    dtt_g = dt_g.transpose(0, 2, 1)
    a_row, a_col = a.reshape(SSD_GROUPS, 1, hpg), a.reshape(SSD_GROUPS, hpg, 1)
    y, hprev = _ssd_fwd(xbc, dt_g, dtt_g, a_row, a_col, cfg)
    dexp = jnp.repeat(w["l1_d"], SSD_HEAD_DIM).reshape(1, inner)
    ng = w["l1_norm_g"].reshape(1, inner)
    blk, par = _gate_specs(cfg)
    yn = _tile_call(_gate_fn, (SSD_GROUPS, lp // tm), [y, xbc, zx, dexp, ng], [blk, blk, blk, par, par],
                    [((lp, inner), F32)], [blk], "ssd_gate")[0]
    mo = _mm(yn, w_out, "nn", "l1_out")
    return mo, (zx, xbc, dt_raw, dt_g, dtt_g, a, a_row, a_col, y, hprev, dexp, ng, yn)


def _mamba_bwd(h, saved, dmo, w, w_in_t, w_out, conv_w, cfg):
    zx, xbc, dt_raw, dt_g, dtt_g, a, a_row, a_col, y, hprev, dexp, ng, yn = saved
    lp, tm, nh, hpg, inner = cfg.lp, cfg.tm, cfg.ssd_heads, cfg.hpg, cfg.ssd_inner
    d_wout = _mm(yn, dmo, "tn", "l1_dwout", out_dtype=BF16)
    dyn = _mm(dmo, w_out, "nt", "l1_dyn")
    blk, par = _gate_specs(cfg)

    def gate_bwd(y, xs, z, dexp, ng, dyn):
        _, vjp = jax.vjp(_gate_fn, y, xs, z, dexp, ng)
        return vjp(dyn)
    dy, dxs_gate, dzx, ddexp, dng = _tile_call(
        gate_bwd, (SSD_GROUPS, lp // tm), [y, xbc, zx, dexp, ng, dyn], [blk, blk, blk, par, par, blk],
        [((lp, inner), F32)] * 2 + [((lp, cfg.l1_inp), BF16)] + [((1, inner), F32)] * 2, [blk, blk, blk, par, par],
        "ssd_gate_bwd", acc=(3, 4), acc_axis=1)
    dxs, dbm, dcm, ddt_g, da_g = _ssd_bwd(xbc, dt_g, dtt_g, a_row, a_col, hprev, dy, dxs_gate, cfg)
    conv_b = w["l1_conv_b"].reshape(1, -1)
    dzx, dconv_w, dconv_b = _conv_bwd(zx, conv_w, conv_b, dxs, dbm, dcm, dzx, cfg)
    assert cfg.l1_inp - inner - cfg.conv_dim == LANES
    ddt = jnp.pad(ddt_g.transpose(1, 0, 2).reshape(lp, nh), ((0, 0), (0, LANES - nh)))
    dt_bias = jnp.pad(w["l1_dt_bias"].reshape(1, nh), ((0, 0), (0, LANES - nh)))
    last = (inner + cfg.conv_dim) // LANES
    tail = pl.BlockSpec((tm, LANES), lambda i: (i, last))

    def dt_bwd(ddt, r, b):
        lane = lax.broadcasted_iota(jnp.int32, ddt.shape, 1)
        d = jnp.where(lane < nh, ddt * _sigmoid(r + b), 0.0)
        return d, jnp.sum(d, axis=0, keepdims=True)
    dzx, ddt_bias = _tile_call(dt_bwd, (lp // tm,), [ddt, zx, dt_bias], [_rows(tm, LANES), tail, _whole((1, LANES))],
                               [(dzx.shape, BF16), ((1, LANES), F32)], [tail, _whole((1, LANES))], "ssd_dt_bwd",
                               acc=(1,), fill=(dzx, 0))
    ddt_bias = ddt_bias[:, :nh]
    da_log, dd = _small_call(lambda da, a, dde: (da * a, jnp.sum(dde, axis=1, keepdims=True)),
                             [da_g.reshape(1, nh), a, ddexp.reshape(nh, SSD_HEAD_DIM)], [(1, nh), (nh, 1)],
                             "ssd_small_bwd")
    d_win_t = _mm(dzx, h, "tn", "l1_dwin", out_dtype=BF16, tm_t=1152)
    dh = _mm(dzx, w_in_t, "nn", "l1_dh")
    grads = {"l1_w_in": d_win_t, "l1_conv_w": dconv_w[:SSD_CONV], "l1_conv_b": dconv_b.reshape(-1),
             "l1_dt_bias": ddt_bias.reshape(-1), "l1_a_log": da_log.reshape(-1), "l1_d": dd.reshape(-1),
             "l1_norm_g": dng.reshape(-1), "l1_w_out": d_wout}
    return dh, grads


def _local_step(x, target, w, cfg, late_weights=None, early_grads=None):
    lp, n, d, tm, sw = cfg.lp, cfg.n, cfg.d, cfg.tr, cfg.s5_w
    row = lambda name: w[name].reshape(1, -1)
    h0 = jnp.concatenate([w["meta_tokens"], x, jnp.zeros((lp - n, d), F32)], axis=0)
    proj = _mm(h0, w["l0_w_in"], "nn", "l0_in")
    o1, o2, o3 = sw, sw + cfg.q_rank, sw + cfg.q_rank + cfg.kv_rank
    prm = _s5_prep(w, cfg)
    a_out, s5_saved = _s5_fwd(proj[:, :o1], prm, w["l0_s5_w_glu"], cfg)
    b_out, mla_saved, arrived = _mla_fwd(proj[:, o1:o2], proj[:, o2:o3], proj[:, o3:], w, w["l0_mla_w_uq_p"],
                                         w["l0_mla_w_ukv"], cfg, late_weights[0] if late_weights else None)
    if late_weights:
        w = dict(w, **late_weights[1](arrived))
    mix = jnp.concatenate([a_out, b_out], axis=1).astype(BF16)
    mo0 = _mm(mix, w["l0_w_out"], "nn", "l0_out")
    h1, h1b = _ln_fwd(h0, mo0, row("l0_ln1_g"), row("l0_ln1_b"), cfg, "l0_ln1")
    fo0, ffn0 = _ffn_fwd(h1b, w["l0_w_gu"], w["l0_ffn_w_down"], cfg, "l0_ffn")
    h2, h2b = _ln_fwd(h1, fo0, row("l0_ln2_g"), row("l0_ln2_b"), cfg, "l0_ln2")
    mo1, mam = _mamba_fwd(h2b, w, w["l1_w_in_t"], w["l1_w_out"], w["l1_conv_w"], cfg)
    h3, h3b = _ln_fwd(h2, mo1, row("l1_ln1_g"), row("l1_ln1_b"), cfg, "l1_ln1")
    fo1, ffn1 = _ffn_fwd(h3b, w["l1_w_gu"], w["l1_ffn_w_down"], cfg, "l1_ffn")
    h4, _ = _ln_fwd(h3, fo1, row("l1_ln2_g"), row("l1_ln2_b"), cfg, "l1_ln2")
    tgt = jnp.concatenate([jnp.zeros((N_META, d), F32), target, jnp.zeros((lp - n, d), F32)], axis=0)

    def loss_fn(y, t):
        r = pl.program_id(0) * tm + lax.broadcasted_iota(jnp.int32, (tm, 1), 0)
        diff = jnp.where(jnp.logical_and(r >= N_META, r < n), y - t, 0.0)
        return diff * (1.0 / d), jnp.sum(diff * diff, axis=0, keepdims=True) * (0.5 / d)
    dh4, loss_lanes = _rowwise(loss_fn, [h4, tgt], [], [(d, F32)], [(1, d)], tm, "loss")
    grads = {}
    dr4, dr4b, dg, db = _ln_bwd(h3, fo1, row("l1_ln2_g"), row("l1_ln2_b"), [dh4], [1.0], cfg, "l1_ln2_bwd")
    grads["l1_ln2_g"], grads["l1_ln2_b"] = dg.reshape(-1), db.reshape(-1)
    dh3, grads["l1_w_gu"], grads["l1_ffn_w_down"] = _ffn_bwd(h3b, ffn1, dr4b, w["l1_w_gu"], w["l1_ffn_w_down"], cfg,
                                                             "l1_ffn")
    dr3, dr3b, dg, db = _ln_bwd(h2, mo1, row("l1_ln1_g"), row("l1_ln1_b"), [dr4, dh3], [DN_ALPHA, 1.0], cfg,
                                "l1_ln1_bwd")
    grads["l1_ln1_g"], grads["l1_ln1_b"] = dg.reshape(-1), db.reshape(-1)
    dh2, mg = _mamba_bwd(h2b, mam, dr3b, w, w["l1_w_in_t"], w["l1_w_out"], w["l1_conv_w"], cfg)
    grads.update(mg)
    dr2, dr2b, dg, db = _ln_bwd(h1, fo0, row("l0_ln2_g"), row("l0_ln2_b"), [dr3, dh2], [DN_ALPHA, 1.0], cfg,
                                "l0_ln2_bwd")
    grads["l0_ln2_g"], grads["l0_ln2_b"] = dg.reshape(-1), db.reshape(-1)
    dh1, grads["l0_w_gu"], grads["l0_ffn_w_down"] = _ffn_bwd(h1b, ffn0, dr2b, w["l0_w_gu"], w["l0_ffn_w_down"], cfg,
                                                             "l0_ffn")
    dr1, dr1b, dg, db = _ln_bwd(h0, mo0, row("l0_ln1_g"), row("l0_ln1_b"), [dr2, dh1], [DN_ALPHA, 1.0], cfg,
                                "l0_ln1_bwd")
    grads["l0_ln1_g"], grads["l0_ln1_b"] = dg.reshape(-1), db.reshape(-1)
    grads["l0_w_out"] = _mm(mix, dr1b, "tn", "l0_dwout", out_dtype=BF16)
    dmix = _mm(dr1b, w["l0_w_out"], "nt", "l0_dmix")
    du, sg = _s5_bwd(dmix[:, :sw], s5_saved, prm, w, w["l0_s5_w_glu"], cfg)
    dql, dkl, dkr, ag, exchanged = _mla_bwd(dmix[:, sw:], mla_saved, w, w["l0_mla_w_uq_p"], w["l0_mla_w_ukv"], cfg,
                                            early_grads(grads) if early_grads else None)
    grads.update(sg)
    grads.update(ag)
    dproj = jnp.concatenate([du, dql, dkl, dkr], axis=1)
    grads["l0_w_in"] = _mm(h0, dproj, "tn", "l0_dwin", out_dtype=BF16)
    dh0m = _mm(dproj, w["l0_w_in"], "nt", "l0_dh")
    dh0 = _rowwise(lambda a, b: DN_ALPHA * a + b, [dr1, dh0m], [], [(d, F32)], [], tm, "l0_dh0")[0]
    grads["meta_tokens"] = dh0[:N_META]
    return loss_lanes, dh0[N_META:n], grads, exchanged


FLAT_W = 1024
N_SLOTS = 4
HBM_SPEC = pl.BlockSpec(memory_space=pltpu.HBM)


def _place():
    x, y, c = lax.axis_index("x"), lax.axis_index("y"), lax.axis_index("c")
    chips = [(1 - x, y), (x, 1 - y), (1 - x, 1 - y)]
    return x, y, c, chips


def _remote(src, dst, ssem, rsem, dev):
    return pltpu.make_async_remote_copy(src_ref=src, dst_ref=dst, send_sem=ssem, recv_sem=rsem, device_id=dev,
                                        device_id_type=MESH)


class _GatherPlan:
    def __init__(self, shards, groups):
        self.arrays = list(shards)
        self.n = n = len(shards)
        self.rows = [s.shape[0] for s in shards]
        self.place = {t: (g, row0) for g, members in enumerate(groups) for t, row0 in members}
        self.n_out = len(groups)
        self.out_shapes = []
        for members in groups:
            t0 = members[0][0]
            rows = max(row0 + N_SLOTS * shards[t].shape[0] for t, row0 in members)
            self.out_shapes.append(jax.ShapeDtypeStruct((rows, shards[t0].shape[1]), shards[t0].dtype))
        sems = pltpu.SemaphoreType.DMA((3 * n,))
        self.scratch = [sems, sems, sems, sems, pltpu.SemaphoreType.DMA((n,))]

    def _copies(self, refs):
        n = self.n
        srcs, outs = refs[:n], refs[n:n + self.n_out]
        send_sems, recv_sems, fsend, frecv, lsems = refs[n + self.n_out:]
        x, y, c, chips = _place()
        me = 2 * x + y
        sib = (x, y, 1 - c)

        def rows_of(t, slot, half):
            r = self.rows[t]
            g, row0 = self.place[t]
            return outs[g].at[pl.ds(row0 + slot * r + half * (r // 2), r // 2)]
        local, send, arrive, relay, arrive_sib = [], [], [], [], []
        for t in range(n):
            r = self.rows[t]
            g, row0 = self.place[t]
            local.append(pltpu.make_async_copy(srcs[t], outs[g].at[pl.ds(row0 + me * r, r)], lsems.at[t]))
            mine = srcs[t].at[pl.ds(c * (r // 2), r // 2)]
            for j, (cx, cy) in enumerate(chips):
                k = 3 * t + j
                send.append(_remote(mine, rows_of(t, me, c), send_sems.at[k], recv_sems.at[k], (cx, cy, c)))
                got = rows_of(t, 2 * cx + cy, c)
                arrive.append(_remote(got, got, send_sems.at[k], recv_sems.at[k], (cx, cy, c)))
                relay.append(_remote(got, got, fsend.at[k], frecv.at[k], sib))
                got_sib = rows_of(t, 2 * cx + cy, 1 - c)
                arrive_sib.append(_remote(got_sib, got_sib, fsend.at[k], frecv.at[k], sib))
        return local, send, arrive, relay, arrive_sib

    def start(self, refs):
        local, send, _, _, _ = self._copies(refs)
        for cp in local + send:
            cp.start()

    def relay(self, refs):
        _, _, arrive, relay, _ = self._copies(refs)
        for a, r in zip(arrive, relay):
            a.wait_recv()
            r.start()

    def finish(self, refs):
        local, send, _, relay, arrive_sib = self._copies(refs)
        for cp in arrive_sib:
            cp.wait_recv()
        for cp in send + relay:
            cp.wait_send()
        for cp in local:
            cp.wait()


class _ExchangePlan:
    def __init__(self, items):
        self.items = items
        self.arrays = [a for a, _, _ in items]
        self.n = self.n_out = n = len(items)
        self.out_shapes = [jax.ShapeDtypeStruct((3, rps, a.shape[1]), a.dtype) for a, _, rps in items]
        sems = pltpu.SemaphoreType.DMA((3 * n,))
        self.scratch = [sems, sems]

    def _copies(self, refs):
        n = self.n
        srcs, outs = refs[:n], refs[n:2 * n]
        send_sems, recv_sems = refs[2 * n:]
        x, y, c, chips = _place()
        cps = []
        for t, (_, row0, rps) in enumerate(self.items):
            for j, (cx, cy) in enumerate(chips):
                cps.append(_remote(srcs[t].at[pl.ds(row0 + (2 * cx + cy) * rps, rps)], outs[t].at[j],
                                   send_sems.at[3 * t + j], recv_sems.at[3 * t + j], (cx, cy, c)))
        return cps

    def start(self, refs):
        for cp in self._copies(refs):
            cp.start()

    def relay(self, refs):
        pass

    def finish(self, refs):
        for cp in self._copies(refs):
            cp.wait()


def _run_plan(plan, name):
    def body(*refs):
        plan.start(refs)
        plan.relay(refs)
        plan.finish(refs)
    return pl.pallas_call(body, out_shape=plan.out_shapes, in_specs=[HBM_SPEC] * plan.n,
                          out_specs=[HBM_SPEC] * plan.n_out, scratch_shapes=plan.scratch, name=name)(*plan.arrays)


def _exchange_sibling(arrays):
    n = len(arrays)

    def body(*refs):
        srcs, outs = refs[:n], refs[n:2 * n]
        ssems, rsems = refs[2 * n:]
        x, y, c, _ = _place()
        cps = []
        for t in range(n):
            cp = _remote(srcs[t], outs[t], ssems.at[t], rsems.at[t], (x, y, 1 - c))
            cp.start()
            cps.append(cp)
        for cp in cps:
            cp.wait()

    sems = pltpu.SemaphoreType.DMA((n,))
    return pl.pallas_call(
        body, out_shape=[jax.ShapeDtypeStruct(a.shape, a.dtype) for a in arrays], in_specs=[HBM_SPEC] * n,
        out_specs=[HBM_SPEC] * n, scratch_shapes=[sems, sems], name="exchange_sibling")(*arrays)


def _gather_all(v):
    flips = [(fx, fy, fc) for fx in (0, 1) for fy in (0, 1) for fc in (0, 1)][1:]

    def body(src, out, send_sems, recv_sems, lsem):
        x, y, c, _ = _place()
        local = pltpu.make_async_copy(src, out.at[4 * x + 2 * y + c], lsem)
        local.start()
        cps = []
        for k, (fx, fy, fc) in enumerate(flips):
            px, py, pc = (1 - x if fx else x), (1 - y if fy else y), (1 - c if fc else c)
            cp = _remote(src, out.at[4 * x + 2 * y + c], send_sems.at[k], recv_sems.at[k], (px, py, pc))
            cp.start()
            cps.append(cp)
        for cp in cps:
            cp.wait()
        local.wait()

    return pl.pallas_call(
        body, out_shape=jax.ShapeDtypeStruct((8,) + v.shape, v.dtype), in_specs=[HBM_SPEC], out_specs=HBM_SPEC,
        scratch_shapes=[pltpu.SemaphoreType.DMA((7,)), pltpu.SemaphoreType.DMA((7,)), pltpu.SemaphoreType.DMA],
        name="gather_all")(v)


def _flat_rows(n_elems, row_unit):
    return -(-n_elems // (FLAT_W * row_unit)) * row_unit


def _to_flat(pieces, rows):
    flat = jnp.concatenate([p.reshape(-1) for p in pieces])
    return jnp.pad(flat, (0, rows * FLAT_W - flat.shape[0])).reshape(rows, FLAT_W)


def _sum_parts(parts, name, tm=512):
    rows = parts[0].shape[1]
    tm = _div_tile(rows, tm, SUBLANES)

    def fn(*ps):
        acc = None
        for p in ps:
            for k in range(p.shape[0]):
                acc = p[k].astype(F32) if acc is None else acc + p[k].astype(F32)
        return acc
    return _tile_call(fn, (rows // tm,), parts,
                      [pl.BlockSpec((p.shape[0], tm, FLAT_W), lambda i: (0, i, 0)) for p in parts],
                      [((rows, FLAT_W), F32)], [_rows(tm, FLAT_W)], name)[0]


ELEMENTWISE_BLOCK = 1 << 19


def _row_tile(rows, cols, unit):
    return _div_tile(rows, max(unit, ELEMENTWISE_BLOCK // cols), unit)


def _sum_slot(g, row0, rps, others, me, name):
    c = g.shape[1]
    tm = _row_tile(rps, c, 2 * SUBLANES)
    nrb = rps // tm
    assert row0 % tm == 0

    def body(me_ref, g_ref, o_ref, out_ref):
        out_ref[...] = ((g_ref[...].astype(F32) + o_ref[0].astype(F32)) + o_ref[1].astype(F32)) + o_ref[2].astype(F32)

    grid_spec = pltpu.PrefetchScalarGridSpec(
        num_scalar_prefetch=1, grid=(nrb,),
        in_specs=[pl.BlockSpec((tm, c), lambda i, me_ref: (row0 // tm + me_ref[0] * nrb + i, 0)),
                  pl.BlockSpec((3, tm, c), lambda i, me_ref: (0, i, 0))],
        out_specs=pl.BlockSpec((tm, c), lambda i, me_ref: (i, 0)))
    return pl.pallas_call(body, out_shape=jax.ShapeDtypeStruct((rps, c), F32), grid_spec=grid_spec, name=name,
                          compiler_params=_cparams(1))(me, g, others)


def _adamw(gparts, w, m, v, name, tm=None):
    rows, cols = w.shape
    tm = _row_tile(rows, cols, SUBLANES) if tm is None else _div_tile(rows, tm, SUBLANES)
    ng = len(gparts)

    def fn(*a):
        g = a[0]
        for t in a[1:ng]:
            g = g + t
        w, m, v = a[ng:]
        m = ADAM_B1 * m + (1.0 - ADAM_B1) * g
        v = ADAM_B2 * v + (1.0 - ADAM_B2) * (g * g)
        m_hat = m / (1.0 - ADAM_B1 ** ADAM_STEP)
        v_hat = v / (1.0 - ADAM_B2 ** ADAM_STEP)
        delta = -ADAM_LR * (m_hat / (jnp.sqrt(v_hat) + ADAM_EPS) + ADAM_WD * w)
        return g, delta, m, v
    ins = list(gparts) + [w, m, v]
    return _tile_call(fn, (rows // tm,), ins, [_rows(tm, cols)] * len(ins), [((rows, cols), F32)] * 4,
                      [_rows(tm, cols)] * 4, name)


FIRST = ("l0_w_in", "l0_s5_w_glu", "l0_mla_w_uq", "l0_mla_w_ukv")
REST = ("l0_w_out", "l0_ffn_w_gate", "l0_ffn_w_up", "l0_ffn_w_down", "l1_w_in", "l1_w_out", "l1_ffn_w_gate",
        "l1_ffn_w_up", "l1_ffn_w_down")
BIG = FIRST + REST
TINY = ("meta_tokens", "l1_conv_w")
REPLICATED = ("l0_s5_log_dt", "l0_s5_a_re", "l0_s5_a_im", "l0_s5_b_re", "l0_s5_b_im", "l0_s5_c_re", "l0_s5_c_im",
              "l0_s5_d", "l0_mla_q_norm", "l0_mla_kv_norm", "l0_ln1_g", "l0_ln1_b", "l0_ln2_g", "l0_ln2_b",
              "l1_conv_b", "l1_dt_bias", "l1_a_log", "l1_d", "l1_norm_g", "l1_ln1_g", "l1_ln1_b", "l1_ln2_g",
              "l1_ln2_b")
WEIGHTS = ("meta_tokens", "l0_w_in", "l0_s5_log_dt", "l0_s5_a_re", "l0_s5_a_im", "l0_s5_b_re", "l0_s5_b_im",
           "l0_s5_c_re", "l0_s5_c_im", "l0_s5_d", "l0_s5_w_glu", "l0_mla_q_norm", "l0_mla_w_uq", "l0_mla_kv_norm",
           "l0_mla_w_ukv", "l0_w_out", "l0_ln1_g", "l0_ln1_b", "l0_ffn_w_gate", "l0_ffn_w_up", "l0_ffn_w_down",
           "l0_ln2_g", "l0_ln2_b", "l1_w_in", "l1_conv_w", "l1_conv_b", "l1_dt_bias", "l1_a_log", "l1_d",
           "l1_norm_g", "l1_w_out", "l1_ln1_g", "l1_ln1_b", "l1_ffn_w_gate", "l1_ffn_w_up", "l1_ffn_w_down",
           "l1_ln2_g", "l1_ln2_b")
def _split_flat(flat2d, shapes):
    flat = flat2d.reshape(-1)
    out, off = [], 0
    for s in shapes:
        n = math.prod(s)
        out.append(flat[off:off + n].reshape(s))
        off += n
    return out


def _cols_to_slots(full):
    r, c = full.shape
    return full.reshape(r, N_SLOTS, c // N_SLOTS).transpose(1, 0, 2).reshape(N_SLOTS * r, c // N_SLOTS)


def _slots_to_cols(slabs):
    r4, c = slabs.shape
    return slabs.reshape(N_SLOTS, r4 // N_SLOTS, c).transpose(1, 0, 2).reshape(r4 // N_SLOTS, N_SLOTS * c)


def _step(cfg, x, loss_target, ws, ms, vs):
    d, f = cfg.d, cfg.ffn
    me = 2 * lax.axis_index("x") + lax.axis_index("y")
    kinds = ("grad", "delta", "new_m", "new_v")
    def gather_plan(names):
        groups = []
        for t, name in enumerate(names):
            if name.endswith("_ffn_w_up"):
                groups[-1].append((t, N_SLOTS * d))
            else:
                groups.append([(t, 0)])
        heads = [names[members[0][0]] for members in groups]
        return _GatherPlan([ws[name].astype(BF16) for name in names], groups), heads
    plan, heads = gather_plan(FIRST)
    got = dict(zip(heads, _run_plan(plan, "gather_first")))
    w = {"l0_w_in": got["l0_w_in"], "l0_s5_w_glu": got["l0_s5_w_glu"],
         "l0_mla_w_uq_p": _pad_heads(_slots_to_cols(got["l0_mla_w_uq"]), cfg.heads, MLA_NOPE + MLA_ROPE),
         "l0_mla_w_ukv": _slots_to_cols(got["l0_mla_w_ukv"])}
    late_plan, late_heads = gather_plan(REST)

    def late_weights(arrived):
        got = dict(zip(late_heads, arrived))
        lw = {name: got[name] for name in ("l0_w_out", "l0_ffn_w_down", "l1_w_out", "l1_ffn_w_down")}
        for l in ("l0", "l1"):
            lw[l + "_w_gu"] = got[l + "_ffn_w_gate"].reshape(2 * N_SLOTS, d, f // N_SLOTS)
        w_in_t = got["l1_w_in"].reshape(N_SLOTS, d, cfg.l1_in // N_SLOTS).transpose(0, 2, 1).reshape(cfg.l1_in, d)
        lw["l1_w_in_t"] = jnp.pad(w_in_t, ((0, cfg.l1_inp - cfg.l1_in), (0, 0)))
        return lw
    tiny_shapes = [ws[name].shape for name in TINY]
    trows = _flat_rows(sum(math.prod(s) for s in tiny_shapes), SUBLANES)
    tiny = _gather_all(_to_flat([ws[name] for name in TINY], trows))[0::2]
    for k, name in enumerate(TINY):
        blocks = jnp.stack([_split_flat(tiny[s], tiny_shapes)[k] for s in range(N_SLOTS)])
        w[name] = blocks.transpose(1, 0, 2).reshape(blocks.shape[1], -1)
    for name in REPLICATED:
        w[name] = ws[name]
    rps = {name: ws[name].shape[1 if name == "l1_w_in" else 0] for name in BIG}

    def triples(grads, names):
        out = []
        for name in names:
            if name.endswith(("_ffn_w_gate", "_ffn_w_up")):
                g = grads[name[:3] + "w_gu"].reshape(2 * N_SLOTS * d, f // N_SLOTS)
                out.append((g, N_SLOTS * d if name.endswith("_up") else 0, rps[name]))
            elif name in ("l0_mla_w_uq", "l0_mla_w_ukv"):
                out.append((_cols_to_slots(grads[name]).astype(BF16), 0, rps[name]))
            else:
                out.append((grads[name], 0, rps[name]))
        return out
    held = {}

    def early_grads(grads):
        held["rest"] = triples(grads, REST)
        return _ExchangePlan(held["rest"])
    loss_lanes, grad_x, grads, others_rest = _local_step(x[0], loss_target[0], w, cfg, (late_plan, late_weights),
                                                         early_grads)
    first = triples(grads, FIRST)
    others = list(_run_plan(_ExchangePlan(first), "exchange_first")) + list(others_rest)
    me1 = me.reshape(1).astype(jnp.int32)
    parts = [_sum_slot(a, row0, r, o, me1, "sum_" + name)
             for (a, row0, r), o, name in zip(first + held["rest"], others, BIG)]
    sibs = _exchange_sibling(parts)
    res = {}
    for name, p, q in zip(BIG, parts, sibs):
        if name == "l1_w_in":
            p, q = p.T, q.T
        for kind, arr in zip(kinds, _adamw([p, q], ws[name], ms[name], vs[name], "adamw_" + name)):
            res[kind + "_" + name] = arr
    rep_shapes = [(1, d)] + [ws[name].shape for name in REPLICATED]
    all_shapes = rep_shapes + [grads[name].shape for name in TINY]
    srows = _flat_rows(sum(math.prod(s) for s in all_shapes), SUBLANES)
    small = _to_flat([loss_lanes] + [grads[name] for name in REPLICATED + TINY], srows)
    total = _sum_parts([_gather_all(small)], "sum_small", tm=srows)
    zero = jnp.zeros((1, d), F32)
    flat = lambda dct: _to_flat([zero] + [dct[name] for name in REPLICATED], srows)
    outs = _adamw([total], flat(ws), flat(ms), flat(vs), "adamw_replicated", tm=srows)
    for kind, arr in zip(kinds, outs):
        vals = _split_flat(arr, rep_shapes)
        for name, val in zip(REPLICATED, vals[1:]):
            res[kind + "_" + name] = val
    for name, g in zip(TINY, _split_flat(total, all_shapes)[len(rep_shapes):]):
        cols = ws[name].shape[1]
        mine = lax.dynamic_slice_in_dim(g, me * cols, cols, axis=1)
        for kind, arr in zip(kinds, _adamw([mine], ws[name], ms[name], vs[name], "adamw_" + name)):
            res[kind + "_" + name] = arr
    loss = _small_call(lambda t: jnp.sum(t, axis=1, keepdims=True), [_split_flat(total, rep_shapes)[0]], [(1, 1)],
                       "loss_sum")[0].reshape(())
    ordered = [res[kind + "_" + name] for kind in ("grad", "delta", "new_m", "new_v") for name in WEIGHTS]
    return (loss, grad_x[None]) + tuple(ordered)


def kernel(x, meta_tokens, l0_w_in, l0_s5_log_dt, l0_s5_a_re, l0_s5_a_im, l0_s5_b_re, l0_s5_b_im, l0_s5_c_re,
           l0_s5_c_im, l0_s5_d, l0_s5_w_glu, l0_mla_q_norm, l0_mla_w_uq, l0_mla_kv_norm, l0_mla_w_ukv, l0_w_out,
           l0_ln1_g, l0_ln1_b, l0_ffn_w_gate, l0_ffn_w_up, l0_ffn_w_down, l0_ln2_g, l0_ln2_b, l1_w_in, l1_conv_w,
           l1_conv_b, l1_dt_bias, l1_a_log, l1_d, l1_norm_g, l1_w_out, l1_ln1_g, l1_ln1_b, l1_ffn_w_gate,
           l1_ffn_w_up, l1_ffn_w_down, l1_ln2_g, l1_ln2_b, loss_target, m_meta_tokens, m_l0_w_in, m_l0_s5_log_dt,
           m_l0_s5_a_re, m_l0_s5_a_im, m_l0_s5_b_re, m_l0_s5_b_im, m_l0_s5_c_re, m_l0_s5_c_im, m_l0_s5_d,
           m_l0_s5_w_glu, m_l0_mla_q_norm, m_l0_mla_w_uq, m_l0_mla_kv_norm, m_l0_mla_w_ukv, m_l0_w_out, m_l0_ln1_g,
           m_l0_ln1_b, m_l0_ffn_w_gate, m_l0_ffn_w_up, m_l0_ffn_w_down, m_l0_ln2_g, m_l0_ln2_b, m_l1_w_in,
           m_l1_conv_w, m_l1_conv_b, m_l1_dt_bias, m_l1_a_log, m_l1_d, m_l1_norm_g, m_l1_w_out, m_l1_ln1_g,
           m_l1_ln1_b, m_l1_ffn_w_gate, m_l1_ffn_w_up, m_l1_ffn_w_down, m_l1_ln2_g, m_l1_ln2_b, v_meta_tokens,
           v_l0_w_in, v_l0_s5_log_dt, v_l0_s5_a_re, v_l0_s5_a_im, v_l0_s5_b_re, v_l0_s5_b_im, v_l0_s5_c_re,
           v_l0_s5_c_im, v_l0_s5_d, v_l0_s5_w_glu, v_l0_mla_q_norm, v_l0_mla_w_uq, v_l0_mla_kv_norm,
           v_l0_mla_w_ukv, v_l0_w_out, v_l0_ln1_g, v_l0_ln1_b, v_l0_ffn_w_gate, v_l0_ffn_w_up, v_l0_ffn_w_down,
           v_l0_ln2_g, v_l0_ln2_b, v_l1_w_in, v_l1_conv_w, v_l1_conv_b, v_l1_dt_bias, v_l1_a_log, v_l1_d,
           v_l1_norm_g, v_l1_w_out, v_l1_ln1_g, v_l1_ln1_b, v_l1_ffn_w_gate, v_l1_ffn_w_up, v_l1_ffn_w_down,
           v_l1_ln2_g, v_l1_ln2_b):
    given = dict(locals())
    ws = {name: given[name] for name in WEIGHTS}
    ms = {name: given["m_" + name] for name in WEIGHTS}
    vs = {name: given["v_" + name] for name in WEIGHTS}
    return _step(FULL, x, loss_target, ws, ms, vs)
```

```python
import functools
import math

import numpy as np
import jax
import jax.numpy as jnp
from jax import lax
from jax.experimental import pallas as pl
from jax.experimental.pallas import tpu as pltpu

F32 = jnp.float32
BF16 = jnp.bfloat16
HI = lax.Precision.HIGHEST
MESH = pl.DeviceIdType.MESH

LANES = 128
SUBLANES = 8
VMEM_LIMIT_BYTES = 56 * 1024 * 1024

N_META = 16
CHUNK = 64
DEPTH = 2
DN_ALPHA = (2 * DEPTH) ** 0.25
LN_EPS = 1e-5
RMS_EPS = 1e-6
ROPE_BASE = 10000.0
S5_GROUP = 16
S5_STATE = 64
S5_GPB = 8
MLA_NOPE = 128
MLA_ROPE = 64
MLA_V = 128
MLA_QW = 256
ATT_SHIFT = 48
SSD_HEAD_DIM = 64
SSD_GROUPS = 8
SSD_STATE = 128
SSD_CONV = 4
SSD_BLOCK = 128
ADAM_LR = 0.001
ADAM_B1 = 0.9
ADAM_B2 = 0.999
ADAM_EPS = 1e-08
ADAM_WD = 0.01
ADAM_STEP = 10
NEG = -1e30


class _Cfg:
    def __init__(self, d_model, seq, row_tile, att_tile, scan_tiles, small_row_tile):
        d = d_model
        self.tr = small_row_tile
        self.d = d
        self.seq = seq
        self.n = seq + N_META
        lp = -(-(self.n + ATT_SHIFT) // row_tile) * row_tile
        self.lp = lp
        self.tm = row_tile
        self.tq = att_tile
        self.scan_tiles = scan_tiles
        self.s5_w = d // 2
        self.s5_g = self.s5_w // S5_GROUP
        self.s5_nb = self.s5_g // S5_GPB
        self.s5_c = self.s5_g * S5_STATE
        self.heads = d // 256
        self.q_rank = d // 4
        self.kv_rank = d // 8
        self.l0_in = self.s5_w + self.q_rank + self.kv_rank + MLA_ROPE
        self.l0_mix = self.s5_w + self.heads * MLA_V
        self.ssd_inner = 2 * d
        self.ssd_heads = self.ssd_inner // SSD_HEAD_DIM
        self.hpg = self.ssd_heads // SSD_GROUPS
        self.gw = self.hpg * SSD_HEAD_DIM
        self.gn = SSD_GROUPS * SSD_STATE
        self.conv_dim = self.ssd_inner + 2 * self.gn
        self.l1_in = self.ssd_inner + self.conv_dim + self.ssd_heads
        self.l1_inp = -(-self.l1_in // LANES) * LANES
        self.ffn = -(-(8 * d) // (3 * 256)) * 256
        assert lp % att_tile == 0 and lp % SSD_BLOCK == 0 and lp % (8 * scan_tiles) == 0


FULL = _Cfg(2048, 8192, 640, 640, 4, 160)


def _cparams(n_grid):
    return pltpu.CompilerParams(dimension_semantics=("arbitrary",) * n_grid,
                                vmem_limit_bytes=VMEM_LIMIT_BYTES)


def _div_tile(n, target, unit=LANES):
    if n <= target:
        return n
    best = None
    for t in range(unit, target + 1, unit):
        if n % t == 0:
            best = t
    return n if best is None else best


ANY_SPEC = pl.BlockSpec(memory_space=pl.ANY)


def _tile_call(fn, grid, ins, in_specs, outs, out_specs, name, acc=(), acc_axis=0, fill=None):
    n_in = len(ins)
    n_out = len(outs)
    acc = tuple(acc)
    aliases = {}
    if fill is not None:
        aliases = {n_in: fill[1]}
        ins = list(ins) + [fill[0]]
        in_specs = list(in_specs) + [ANY_SPEC]

    def body(*refs):
        refs = refs[:n_in] + refs[len(ins):]
        vals = fn(*[r[...] for r in refs[:n_in]])
        if not isinstance(vals, (tuple, list)):
            vals = (vals,)
        for k in range(n_out):
            r = refs[n_in + k]
            v = vals[k].astype(r.dtype)
            if k in acc:
                first = pl.program_id(acc_axis) == 0

                @pl.when(first)
                def _(r=r, v=v):
                    r[...] = v

                @pl.when(jnp.logical_not(first))
                def _(r=r, v=v):
                    r[...] += v
            else:
                r[...] = v

    res = pl.pallas_call(
        body, out_shape=[jax.ShapeDtypeStruct(s, d) for s, d in outs], grid=grid,
        in_specs=in_specs, out_specs=out_specs, name=name, compiler_params=_cparams(len(grid)),
        input_output_aliases=aliases,
    )(*ins)
    return res


def _rows(tm, c):
    return pl.BlockSpec((tm, c), lambda i: (i, 0))


def _whole(shape):
    nd = len(shape)
    return pl.BlockSpec(shape, lambda *a: (0,) * nd)


def _rowwise(fn, rows, params, outs, accs, tm, name):
    lp = rows[0].shape[0]
    n_row_out = len(outs)
    res = _tile_call(
        fn, (lp // tm,), list(rows) + list(params),
        [_rows(tm, r.shape[1]) for r in rows] + [_whole(p.shape) for p in params],
        [((lp, c), dt) for c, dt in outs] + [(s, F32) for s in accs],
        [_rows(tm, c) for c, _ in outs] + [_whole(s) for s in accs],
        name, acc=range(n_row_out, n_row_out + len(accs)))
    return res


_DIMS = {"nn": (((1,), (0,)), ((), ())), "nt": (((1,), (1,)), ((), ())), "tn": (((0,), (0,)), ((), ()))}


def _dot(a, b, mode="nn", precision=None):
    return lax.dot_general(a, b, _DIMS[mode], preferred_element_type=F32, precision=precision)


def _bdot(a, b, mode="nn"):
    return _dot(a.astype(BF16), b.astype(BF16), mode)


def _mm(a, b, mode, name, out_dtype=F32, out_slabs=None, b_col0=0, b_cols=None, tm_t=640, tn_t=1536, tk_t=2048):
    slab_b = b.ndim == 3
    if mode == "nn":
        m, k = a.shape
        k2, n, unit_n = (b.shape[1], b.shape[0] * b.shape[2], b.shape[2]) if slab_b else (b.shape[0], b.shape[1], b.shape[1])
        unit_k = k
    elif mode == "nt":
        m, k = a.shape
        n, k2, unit_k = (b.shape[1], b.shape[0] * b.shape[2], b.shape[2]) if slab_b else (b.shape[0], b.shape[1], b.shape[1])
        unit_n = n
    else:
        (k, m), k2 = a.shape, b.shape[0]
        n = b.shape[1] if b_cols is None else b_cols
        unit_n, unit_k = n, k
        tm_t = max(tm_t, 1024)
        tk_t = 1664 if a.dtype == BF16 and b.dtype == BF16 else 1024
    if out_slabs:
        unit_n = n // out_slabs
    assert k == k2, (name, a.shape, b.shape)
    tm, tn, tk = _div_tile(m, tm_t), _div_tile(unit_n, tn_t), _div_tile(unit_k, tk_t)
    nk = k // tk
    nps, kps = unit_n // tn, unit_k // tk
    c0 = b_col0 // tn
    assert b_col0 % tn == 0
    a_spec = {"nn": pl.BlockSpec((tm, tk), lambda i, j, kk: (i, kk)),
              "nt": pl.BlockSpec((tm, tk), lambda i, j, kk: (i, kk)),
              "tn": pl.BlockSpec((tk, tm), lambda i, j, kk: (kk, i))}[mode]
    if slab_b:
        b_spec = {"nn": pl.BlockSpec((None, tk, tn), lambda i, j, kk: (j // nps, kk, j % nps)),
                  "nt": pl.BlockSpec((None, tn, tk), lambda i, j, kk: (kk // kps, j, kk % kps))}[mode]
    else:
        b_spec = {"nn": pl.BlockSpec((tk, tn), lambda i, j, kk: (kk, j)),
                  "nt": pl.BlockSpec((tn, tk), lambda i, j, kk: (j, kk)),
                  "tn": pl.BlockSpec((tk, tn), lambda i, j, kk: (kk, j + c0))}[mode]
    if out_slabs:
        out_shape = jax.ShapeDtypeStruct((out_slabs, m, unit_n), out_dtype)
        out_spec = pl.BlockSpec((None, tm, tn), lambda i, j, kk: (j // nps, i, j % nps))
    else:
        out_shape = jax.ShapeDtypeStruct((m, n), out_dtype)
        out_spec = pl.BlockSpec((tm, tn), lambda i, j, kk: (i, j))

    def body(a_ref, b_ref, o_ref, acc_ref):
        part = _bdot(a_ref[...], b_ref[...], mode)
        if nk == 1:
            o_ref[...] = part.astype(o_ref.dtype)
        else:
            kk = pl.program_id(2)

            @pl.when(kk == 0)
            def _():
                acc_ref[...] = part

            @pl.when(kk > 0)
            def _():
                acc_ref[...] += part

            @pl.when(kk == nk - 1)
            def _():
                o_ref[...] = acc_ref[...].astype(o_ref.dtype)

    return pl.pallas_call(
        body, out_shape=out_shape, grid=(m // tm, n // tn, nk),
        in_specs=[a_spec, b_spec], out_specs=out_spec,
        scratch_shapes=[pltpu.VMEM((tm, tn) if nk > 1 else (SUBLANES, LANES), F32)],
        name=name, compiler_params=_cparams(3))(a, b)


def _layer_norm(r, g, b):
    mu = jnp.mean(r, axis=-1, keepdims=True)
    xc = r - mu
    var = jnp.mean(xc * xc, axis=-1, keepdims=True)
    return xc * lax.rsqrt(var + LN_EPS) * g + b


def _rms(x, g):
    return x * lax.rsqrt(jnp.mean(x * x, axis=-1, keepdims=True) + RMS_EPS) * g


def _sigmoid(x):
    return 1.0 / (1.0 + jnp.exp(-x))


def _silu(x):
    return x * _sigmoid(x)


def _gelu(x):
    return 0.5 * x * (1.0 + jnp.tanh(0.7978845608028654 * (x + 0.044715 * x * x * x)))


def _softplus(x):
    return jnp.maximum(x, 0.0) + jnp.log(1.0 + jnp.exp(-jnp.abs(x)))


def _ln_fwd(h, mo, g, b, cfg, name):
    def fn(h, mo, g, b):
        y = _layer_norm(DN_ALPHA * h + mo, g, b)
        return y, y
    return _rowwise(fn, [h, mo], [g, b], [(cfg.d, F32), (cfg.d, BF16)], [], cfg.tr, name)


def _ln_bwd(h, mo, g, b, douts, scales, cfg, name):
    def fn(h, mo, *rest):
        ds, (g, b) = rest[:-2], rest[-2:]
        dy = ds[0] * scales[0]
        for t, s in zip(ds[1:], scales[1:]):
            dy = dy + t * s
        _, vjp = jax.vjp(_layer_norm, DN_ALPHA * h + mo, g, b)
        dr, dg, db = vjp(dy)
        return dr, dr, dg, db
    d = cfg.d
    return _rowwise(fn, [h, mo] + list(douts), [g, b], [(d, F32), (d, BF16)], [(1, d), (1, d)], cfg.tr, name)


def _ffn_act(gu, cfg, name):
    f = cfg.ffn
    cb = _div_tile(f, 1536)
    nf = f // cb
    lp = gu.shape[0]
    tm = cfg.tr

    def fn(gate, up):
        return _silu(gate.astype(F32)) * up.astype(F32)
    return _tile_call(fn, (lp // tm, nf), [gu, gu],
                      [pl.BlockSpec((tm, cb), lambda i, j: (i, j)),
                       pl.BlockSpec((tm, cb), lambda i, j: (i, j + nf))],
                      [((lp, f), BF16)], [pl.BlockSpec((tm, cb), lambda i, j: (i, j))], name)[0]


def _ffn_act_bwd(gu, dact, cfg, name):
    f = cfg.ffn
    cb = _div_tile(f, 1536)
    nf = f // cb
    lp = gu.shape[0]
    tm = cfg.tr

    def fn(gate, up, da):
        gate, up, da = gate.astype(F32), up.astype(F32), da.astype(F32)
        sg = _sigmoid(gate)
        dgate = da * up * sg * (1.0 + gate * (1.0 - sg))
        dup = da * gate * sg
        return jnp.where(pl.program_id(1) < nf, dgate, dup)
    return _tile_call(fn, (lp // tm, 2 * nf), [gu, gu, dact],
                      [pl.BlockSpec((tm, cb), lambda i, j: (i, j % nf)),
                       pl.BlockSpec((tm, cb), lambda i, j: (i, j % nf + nf)),
                       pl.BlockSpec((tm, cb), lambda i, j: (i, j % nf))],
                      [((lp, 2 * f), BF16)], [pl.BlockSpec((tm, cb), lambda i, j: (i, j))], name)[0]


def _ffn_fwd(h, w_gu, w_down, cfg, tag):
    gu = _mm(h, w_gu, "nn", tag + "_gu", out_dtype=BF16)
    act = _ffn_act(gu, cfg, tag + "_act")
    fo = _mm(act, w_down, "nn", tag + "_down")
    return fo, (gu, act)


def _ffn_bwd(h, saved, dfo, w_gu, w_down, cfg, tag):
    gu, act = saved
    dact = _mm(dfo, w_down, "nt", tag + "_dact", out_dtype=BF16)
    d_wdown = _mm(act, dfo, "tn", tag + "_dwdown", out_dtype=BF16)
    dgu = _ffn_act_bwd(gu, dact, cfg, tag + "_dgu")
    dh = _mm(dgu, w_gu, "nt", tag + "_dh")
    d_wgu = _mm(h, dgu, "tn", tag + "_dwgu", out_dtype=BF16, out_slabs=w_gu.shape[0])
    return dh, d_wgu, d_wdown


def _small_call(fn, ins, outs, name):
    return _tile_call(fn, (1,), ins, [_whole(x.shape) for x in ins], [(s, F32) for s in outs],
                      [_whole(s) for s in outs], name)


def _perm(x):
    lp, c = x.shape
    return x.reshape(SUBLANES, lp // SUBLANES, c).transpose(1, 0, 2).reshape(lp, c)


def _unperm(x):
    lp, c = x.shape
    return x.reshape(lp // SUBLANES, SUBLANES, c).transpose(1, 0, 2).reshape(lp, c)


def _s5_disc(log_dt, a_re, a_im):
    dt = jnp.exp(log_dt)
    mag = jnp.exp(dt * a_re)
    ab_re = mag * jnp.cos(dt * a_im)
    ab_im = mag * jnp.sin(dt * a_im)
    den = a_re * a_re + a_im * a_im
    nr = ab_re - 1.0
    f_re = (nr * a_re + ab_im * a_im) / den
    f_im = (ab_im * a_re - nr * a_im) / den
    return ab_re, ab_im, f_re, f_im


def _s5_bbar(f_re, f_im, b_re, b_im):
    return f_re * b_re - f_im * b_im, f_re * b_im + f_im * b_re


def _bd_from(w, cfg):
    g, p, j = w.shape
    w4 = w.reshape(cfg.s5_nb, S5_GPB, p, j)
    eye = jnp.eye(S5_GPB, dtype=w.dtype)
    return jnp.einsum("bgpj,gh->bgjhp", w4, eye).reshape(cfg.s5_nb, S5_GPB * j, S5_GPB * p)


def _bd_to(blocks, cfg, p, j):
    b5 = blocks.reshape(cfg.s5_nb, S5_GPB, j, S5_GPB, p)
    eye = jnp.eye(S5_GPB, dtype=blocks.dtype)
    return jnp.einsum("bgjhp,gh->bgpj", b5, eye).reshape(cfg.s5_g, p, j)


def _bd_split(x, w1, w2, cfg, name):
    nb, ci, co = w1.shape
    lp, tm = x.shape[0], cfg.tm

    def fn(x, w1, w2):
        xb = x.astype(BF16)
        return _dot(xb, w1[0].astype(BF16)), _dot(xb, w2[0].astype(BF16))
    wspec = pl.BlockSpec((1, ci, co), lambda i, j: (j, 0, 0))
    ospec = pl.BlockSpec((tm, co), lambda i, j: (i, j))
    return _tile_call(fn, (lp // tm, nb), [x, w1, w2],
                      [pl.BlockSpec((tm, ci), lambda i, j: (i, j)), wspec, wspec],
                      [((lp, nb * co), F32)] * 2, [ospec, ospec], name)


def _bd_join(x1, x2, w1, w2, extra, scale, cfg, name):
    nb, ci, co = w1.shape
    lp, tm = x1.shape[0], cfg.tm

    def fn(x1, x2, w1, w2, e, s):
        return _bdot(x1, w1[0]) + _bdot(x2, w2[0]) + e * s
    xspec = pl.BlockSpec((tm, ci), lambda i, j: (i, j))
    wspec = pl.BlockSpec((1, ci, co), lambda i, j: (j, 0, 0))
    ospec = pl.BlockSpec((tm, co), lambda i, j: (i, j))
    return _tile_call(fn, (lp // tm, nb), [x1, x2, w1, w2, extra, scale],
                      [xspec, xspec, wspec, wspec, ospec, pl.BlockSpec((1, co), lambda i, j: (0, j))],
                      [((lp, nb * co), F32)], [ospec], name)[0]


def _bd_tn(a, b, nb, cfg, name):
    lp, tk = a.shape[0], cfg.tm
    ca, cb = a.shape[1] // nb, b.shape[1] // nb

    def fn(a, b):
        return _bdot(a, b, "tn")[None]
    return _tile_call(fn, (nb, lp // tk), [a, b],
                      [pl.BlockSpec((tk, ca), lambda j, k: (k, j)), pl.BlockSpec((tk, cb), lambda j, k: (k, j))],
                      [((nb, ca, cb), F32)], [pl.BlockSpec((1, ca, cb), lambda j, k: (j, 0, 0))],
                      name, acc=(0,), acc_axis=1)[0]


def _s5_scan(bu_re, bu_im, ab_re, ab_im, cfg, reverse, tag):
    lp, c = bu_re.shape
    nt = cfg.scan_tiles
    rows = lp // nt
    steps = rows // SUBLANES
    cb = _div_tile(c, 512)
    tmap = (lambda j, t: (nt - 1 - t, j)) if reverse else (lambda j, t: (t, j))
    row_spec = pl.BlockSpec((rows, cb), tmap)
    par_spec = pl.BlockSpec((1, cb), lambda j, t: (0, j))
    st_spec = pl.BlockSpec((SUBLANES, cb), lambda j, t: (0, j))
    grid = (c // cb, nt)
    full = jax.ShapeDtypeStruct((lp, c), F32)
    small = jax.ShapeDtypeStruct((SUBLANES, c), F32)

    def offset(k):
        kk = steps - 1 - k if reverse else k
        return pl.multiple_of(kk * SUBLANES, SUBLANES)

    def local_body(bre, bim, ar_ref, ai_ref, sre, sim, fre, fim, pre, pim, st):
        t = pl.program_id(1)

        @pl.when(t == 0)
        def _():
            zero = jnp.zeros((SUBLANES, cb), F32)
            st[0] = zero
            st[1] = zero
            st[2] = zero + 1.0
            st[3] = zero
        ar = jnp.broadcast_to(ar_ref[...], (SUBLANES, cb))
        ai = jnp.broadcast_to(ai_ref[...], (SUBLANES, cb))

        def step(k, carry):
            s_r, s_i, p_r, p_i = carry
            off = offset(k)
            n_r = ar * s_r - ai * s_i + bre[pl.ds(off, SUBLANES), :]
            n_i = ar * s_i + ai * s_r + bim[pl.ds(off, SUBLANES), :]
            sre[pl.ds(off, SUBLANES), :] = n_r
            sim[pl.ds(off, SUBLANES), :] = n_i
            return n_r, n_i, ar * p_r - ai * p_i, ar * p_i + ai * p_r
        s_r, s_i, p_r, p_i = lax.fori_loop(0, steps, step, (st[0], st[1], st[2], st[3]))
        st[0] = s_r
        st[1] = s_i
        st[2] = p_r
        st[3] = p_i

        @pl.when(t == nt - 1)
        def _():
            fre[...] = s_r
            fim[...] = s_i
            pre[...] = p_r
            pim[...] = p_i

    loc_re, loc_im, f_re, f_im, pn_re, pn_im = pl.pallas_call(
        local_body, out_shape=[full, full, small, small, small, small], grid=grid,
        in_specs=[row_spec, row_spec, par_spec, par_spec],
        out_specs=[row_spec, row_spec, st_spec, st_spec, st_spec, st_spec],
        scratch_shapes=[pltpu.VMEM((4, SUBLANES, cb), F32)],
        name=tag + "_local", compiler_params=_cparams(2))(bu_re, bu_im, ab_re, ab_im)

    def fix_body(lre, lim, fre, fim, pre, pim, ar_ref, ai_ref, sre, sim, st):
        t = pl.program_id(1)
        ar = jnp.broadcast_to(ar_ref[...], (SUBLANES, cb))
        ai = jnp.broadcast_to(ai_ref[...], (SUBLANES, cb))

        @pl.when(t == 0)
        def _():
            f_r, f_i = fre[...], fim[...]
            n_r, n_i = pre[0:1, :], pim[0:1, :]
            row = lax.broadcasted_iota(jnp.int32, (SUBLANES, cb), 0)
            c_r = jnp.zeros((1, cb), F32)
            c_i = jnp.zeros((1, cb), F32)
            car_r = jnp.zeros((SUBLANES, cb), F32)
            car_i = jnp.zeros((SUBLANES, cb), F32)
            order = range(SUBLANES - 2, -1, -1) if reverse else range(1, SUBLANES)
            for i in order:
                src = i + 1 if reverse else i - 1
                c_r, c_i = (n_r * c_r - n_i * c_i + f_r[src:src + 1, :],
                            n_r * c_i + n_i * c_r + f_i[src:src + 1, :])
                car_r = jnp.where(row == i, c_r, car_r)
                car_i = jnp.where(row == i, c_i, car_i)
            st[0] = car_r
            st[1] = car_i
            st[2] = ar
            st[3] = ai
        car_r = st[0]
        car_i = st[1]

        def step(k, carry):
            p_r, p_i = carry
            off = offset(k)
            sre[pl.ds(off, SUBLANES), :] = lre[pl.ds(off, SUBLANES), :] + p_r * car_r - p_i * car_i
            sim[pl.ds(off, SUBLANES), :] = lim[pl.ds(off, SUBLANES), :] + p_r * car_i + p_i * car_r
            return ar * p_r - ai * p_i, ar * p_i + ai * p_r
        p_r, p_i = lax.fori_loop(0, steps, step, (st[2], st[3]))
        st[2] = p_r
        st[3] = p_i

    return pl.pallas_call(
        fix_body, out_shape=[full, full], grid=grid,
        in_specs=[row_spec, row_spec, st_spec, st_spec, st_spec, st_spec, par_spec, par_spec],
        out_specs=[row_spec, row_spec], scratch_shapes=[pltpu.VMEM((4, SUBLANES, cb), F32)],
        name=tag + "_fix", compiler_params=_cparams(2))(loc_re, loc_im, f_re, f_im, pn_re, pn_im, ab_re, ab_im)


def _swap(x):
    return jnp.swapaxes(x, -1, -2)


def _s5_prep(w, cfg):
    g, p, j = cfg.s5_g, S5_STATE, S5_GROUP
    gp = g * p
    log_dt = w["l0_s5_log_dt"].reshape(g, 1)
    ab_re, ab_im, f_re, f_im = _small_call(_s5_disc, [log_dt, w["l0_s5_a_re"], w["l0_s5_a_im"]],
                                           [(g, p)] * 4, "s5_disc")
    b_re2 = w["l0_s5_b_re"].transpose(2, 0, 1).reshape(j, gp)
    b_im2 = w["l0_s5_b_im"].transpose(2, 0, 1).reshape(j, gp)
    f_re1, f_im1 = f_re.reshape(1, gp), f_im.reshape(1, gp)
    bb_re2, bb_im2 = _small_call(_s5_bbar, [f_re1, f_im1, b_re2, b_im2], [(j, gp)] * 2, "s5_bbar")
    bb_re = _bd_from(bb_re2.reshape(j, g, p).transpose(1, 2, 0), cfg).astype(BF16)
    bb_im = _bd_from(bb_im2.reshape(j, g, p).transpose(1, 2, 0), cfg).astype(BF16)
    c_re_t = _bd_from(w["l0_s5_c_re"].transpose(0, 2, 1), cfg).astype(BF16)
    c_imn_t = _bd_from(-w["l0_s5_c_im"].transpose(0, 2, 1), cfg).astype(BF16)
    return dict(log_dt=log_dt, f_re1=f_re1, f_im1=f_im1, b_re2=b_re2, b_im2=b_im2,
                ab_re=ab_re.reshape(1, gp), ab_im=ab_im.reshape(1, gp),
                bb_re=bb_re, bb_im=bb_im, bb_re_t=_swap(bb_re), bb_im_t=_swap(bb_im),
                c_re=_swap(c_re_t), c_imn=_swap(c_imn_t), c_re_t=c_re_t, c_imn_t=c_imn_t,
                d=w["l0_s5_d"].reshape(1, cfg.s5_w))


def _s5_fwd(u, prm, w_glu, cfg):
    tm = cfg.tr
    up = _perm(u)
    bu_re, bu_im = _bd_split(up, prm["bb_re"], prm["bb_im"], cfg, "s5_bu")
    s_re, s_im = _s5_scan(bu_re, bu_im, prm["ab_re"], prm["ab_im"], cfg, False, "s5_scan")
    y = _bd_join(s_re, s_im, prm["c_re"], prm["c_imn"], up, prm["d"], cfg, "s5_y")
    g = _rowwise(_gelu, [y], [], [(cfg.s5_w, F32)], [], tm, "s5_gelu")[0]
    z = _mm(g, w_glu, "nn", "s5_glu_mm")
    a_out = _rowwise(lambda g, z: g * _sigmoid(z), [g, z], [], [(cfg.s5_w, F32)], [], tm, "s5_glu")[0]
    return _unperm(a_out), (up, s_re, s_im, y, g, z)


def _s5_bwd(d_a_out, saved, prm, w, w_glu, cfg):
    up, s_re, s_im, y, g, z = saved
    tm, sw, nb = cfg.tr, cfg.s5_w, cfg.s5_nb
    gs, p, j = cfg.s5_g, S5_STATE, S5_GROUP
    gp = gs * p
    dap = _perm(d_a_out)

    def glu_bwd(da, g, z):
        sg = _sigmoid(z)
        return da * sg, da * g * sg * (1.0 - sg)
    dg1, dz = _rowwise(glu_bwd, [dap, g, z], [], [(sw, F32)] * 2, [], tm, "s5_glu_bwd")
    d_wglu = _mm(g, dz, "tn", "s5_dwglu", out_dtype=BF16)
    dg2 = _mm(dz, w_glu, "nt", "s5_dg2")

    def gelu_bwd(dg1, dg2, y, up, d):
        _, vjp = jax.vjp(_gelu, y)
        dy = vjp(dg1 + dg2)[0]
        return dy, dy * d, jnp.sum(dy * up, axis=0, keepdims=True)
    dy, dup_direct, dd = _rowwise(gelu_bwd, [dg1, dg2, y, up], [prm["d"]], [(sw, F32)] * 2, [(1, sw)], tm,
                                  "s5_gelu_bwd")
    ds_re, ds_im = _bd_split(dy, prm["c_re_t"], prm["c_imn_t"], cfg, "s5_ds")
    dc_re_t = _bd_tn(dy, s_re, nb, cfg, "s5_dcre")
    dc_imn_t = _bd_tn(dy, s_im, nb, cfg, "s5_dcim")
    g_re, g_im = _s5_scan(ds_re, ds_im, prm["ab_re"], -prm["ab_im"], cfg, True, "s5_adj")

    cb = _div_tile(gp, 512)
    per = tm // SUBLANES
    spec = pl.BlockSpec((tm, cb), lambda jj, i: (i, jj))
    before = pl.BlockSpec((SUBLANES, cb), lambda jj, i: (jnp.maximum(i * per - 1, 0), jj))
    final = pl.BlockSpec((SUBLANES, cb), lambda jj, i: (cfg.lp // SUBLANES - 1, jj))
    aspec = pl.BlockSpec((1, cb), lambda jj, i: (0, jj))

    def dab(g_r, g_i, s_r, s_i, h_r, h_i, l_r, l_i):
        first = pl.program_id(1) == 0
        row8 = lax.broadcasted_iota(jnp.int32, (SUBLANES, cb), 0)

        def prev(s, h, l):
            wrap = jnp.where(row8 == 0, 0.0, pltpu.roll(l, 1, axis=0))
            return jnp.concatenate([jnp.where(first, wrap, h), s[:tm - SUBLANES]], axis=0)
        p_r, p_i = prev(s_r, h_r, l_r), prev(s_i, h_i, l_i)
        return (jnp.sum(g_r * p_r + g_i * p_i, axis=0, keepdims=True),
                jnp.sum(g_i * p_r - g_r * p_i, axis=0, keepdims=True))
    dab_re, dab_im = _tile_call(dab, (gp // cb, cfg.lp // tm), [g_re, g_im, s_re, s_im, s_re, s_im, s_re, s_im],
                                [spec] * 4 + [before] * 2 + [final] * 2,
                                [((1, gp), F32)] * 2, [aspec] * 2, "s5_dab", acc=(0, 1), acc_axis=1)
    no_scale = jnp.ones((1, sw), F32)
    dup = _bd_join(g_re, g_im, prm["bb_re_t"], prm["bb_im_t"], dup_direct, no_scale, cfg, "s5_dup")
    dbb_re_blk = _bd_tn(up, g_re, nb, cfg, "s5_dbbre")
    dbb_im_blk = _bd_tn(up, g_im, nb, cfg, "s5_dbbim")

    def to2(blk):
        return _bd_to(blk, cfg, p, j).transpose(2, 0, 1).reshape(j, gp)

    def bbar_bwd(f_re, f_im, b_re, b_im, dr, di):
        _, vjp = jax.vjp(_s5_bbar, f_re, f_im, b_re, b_im)
        return vjp((dr, di))
    df_re, df_im, db_re2, db_im2 = _small_call(
        bbar_bwd, [prm["f_re1"], prm["f_im1"], prm["b_re2"], prm["b_im2"], to2(dbb_re_blk), to2(dbb_im_blk)],
        [(1, gp), (1, gp), (j, gp), (j, gp)], "s5_bbar_bwd")

    def disc_bwd(log_dt, a_re, a_im, d1, d2, d3, d4):
        _, vjp = jax.vjp(_s5_disc, log_dt, a_re, a_im)
        return vjp((d1, d2, d3, d4))
    dlog_dt, da_re, da_im = _small_call(
        disc_bwd, [prm["log_dt"], w["l0_s5_a_re"], w["l0_s5_a_im"], dab_re.reshape(gs, p), dab_im.reshape(gs, p),
                   df_re.reshape(gs, p), df_im.reshape(gs, p)], [(gs, 1), (gs, p), (gs, p)], "s5_disc_bwd")
    grads = {
        "l0_s5_log_dt": dlog_dt.reshape(gs), "l0_s5_a_re": da_re, "l0_s5_a_im": da_im,
        "l0_s5_b_re": db_re2.reshape(j, gs, p).transpose(1, 2, 0),
        "l0_s5_b_im": db_im2.reshape(j, gs, p).transpose(1, 2, 0),
        "l0_s5_c_re": _bd_to(dc_re_t, cfg, p, j).transpose(0, 2, 1),
        "l0_s5_c_im": -_bd_to(dc_imn_t, cfg, p, j).transpose(0, 2, 1),
        "l0_s5_d": dd.reshape(sw), "l0_s5_w_glu": d_wglu,
    }
    return _unperm(dup), grads


def _shift(x):
    return jnp.concatenate([jnp.zeros((ATT_SHIFT, x.shape[1]), x.dtype), x[:-ATT_SHIFT]], axis=0)


def _unshift(x):
    return jnp.concatenate([x[ATT_SHIFT:], jnp.zeros((ATT_SHIFT, x.shape[1]), x.dtype)], axis=0)


def _rope_tables(cfg):
    pos = (jnp.arange(cfg.lp) - ATT_SHIFT).astype(F32)
    inv = ROPE_BASE ** (-jnp.arange(0, MLA_ROPE, 2, dtype=F32) / MLA_ROPE)
    ang = pos[:, None] * inv[None, :]
    cos, sin = jnp.cos(ang), jnp.sin(ang)
    z = jnp.zeros((cfg.lp, LANES - MLA_ROPE), F32)
    return jnp.concatenate([cos, cos, z], axis=1), jnp.concatenate([-sin, sin, z], axis=1)


def _swap_halves(x):
    half = MLA_ROPE // 2
    lane = lax.broadcasted_iota(jnp.int32, x.shape, 1)
    left = pltpu.roll(x, LANES - half, axis=1)
    right = pltpu.roll(x, half, axis=1)
    return jnp.where(lane < half, left, jnp.where(lane < 2 * half, right, 0.0))


def _rope(x, cosp, sinp):
    return x * cosp + _swap_halves(x) * sinp


def _rope_t(dy, cosp, sinp):
    return dy * cosp + _swap_halves(dy * sinp)


def _visible(i, j, t):
    row = i * t + lax.broadcasted_iota(jnp.int32, (t, t), 0)
    col = j * t + lax.broadcasted_iota(jnp.int32, (t, t), 1)
    return jnp.logical_and(col // CHUNK <= row // CHUNK, col >= ATT_SHIFT)


class _NoPlan:
    n = n_out = 0
    arrays, out_shapes, scratch = [], [], []


def _side_refs(refs, n_in, n_out, n_scratch, side):
    a = n_in + side.n
    b = a + n_out + side.n_out
    c = b + n_scratch
    own = refs[:n_in] + refs[a:a + n_out] + refs[b:c]
    return own, refs[n_in:a] + refs[a + n_out:b] + refs[c:]


def _needs_mask(i, j):
    return jnp.logical_or(i == j, j == 0)


def _flash_fwd(q, kv, kr, cfg, side=None):
    lp, t, nh = cfg.lp, cfg.tq, cfg.heads
    n = lp // t
    scale = (MLA_NOPE + MLA_ROPE) ** -0.5
    side = side or _NoPlan()

    def body(*refs):
        (q_ref, kv_ref, kr_ref, o_ref, lse_ref, m_s, l_s, acc_s), ex = _side_refs(refs, 3, 2, 3, side)
        hh, i, j = pl.program_id(0), pl.program_id(1), pl.program_id(2)
        if side.n:
            at_tile0 = jnp.logical_and(i == 0, j == 0)
            pl.when(jnp.logical_and(hh == 0, at_tile0))(lambda: side.start(ex))
            pl.when(jnp.logical_and(hh == nh // 2, at_tile0))(lambda: side.relay(ex))

        @pl.when(j == 0)
        def _():
            m_s[...] = jnp.full((t, 1), NEG, F32)
            l_s[...] = jnp.zeros((t, 1), F32)
            acc_s[...] = jnp.zeros((t, MLA_V), F32)

        def tile(masked):
            s = (_dot(q_ref[:, :MLA_NOPE], kv_ref[:, :MLA_NOPE], "nt")
                 + _dot(q_ref[:, MLA_NOPE:], kr_ref[...], "nt")) * scale
            if masked:
                s = jnp.where(_visible(i, j, t), s, NEG)
            m_old = m_s[...]
            m_new = jnp.maximum(m_old, jnp.max(s, axis=1, keepdims=True))
            alpha = jnp.exp(m_old - m_new)
            p = jnp.exp(s - m_new)
            l_s[...] = alpha * l_s[...] + jnp.sum(p, axis=1, keepdims=True)
            acc_s[...] = alpha * acc_s[...] + _dot(p.astype(BF16), kv_ref[:, MLA_NOPE:])
            m_s[...] = m_new
        pl.when(jnp.logical_and(j <= i, _needs_mask(i, j)))(lambda: tile(True))
        pl.when(jnp.logical_and(j < i, j > 0))(lambda: tile(False))

        @pl.when(j == i)
        def _():
            o_ref[...] = acc_s[...] / l_s[...]
            lse_ref[...] = jnp.broadcast_to(m_s[...] + jnp.log(l_s[...]), (t, MLA_V))

        if side.n:
            pl.when(jnp.logical_and(hh == nh - 1, jnp.logical_and(i == n - 1, j == n - 1)))(lambda: side.finish(ex))

    res = pl.pallas_call(
        body, out_shape=[jax.ShapeDtypeStruct((lp, nh * MLA_V), F32)] * 2 + side.out_shapes, grid=(nh, n, n),
        in_specs=[pl.BlockSpec((t, MLA_QW), lambda h, i, j: (i, h)),
                  pl.BlockSpec((t, MLA_QW), lambda h, i, j: (jnp.minimum(i, j), h)),
                  pl.BlockSpec((t, LANES), lambda h, i, j: (jnp.minimum(i, j), 0))] + [HBM_SPEC] * side.n,
        out_specs=[pl.BlockSpec((t, MLA_V), lambda h, i, j: (i, h))] * 2 + [HBM_SPEC] * side.n_out,
        scratch_shapes=[pltpu.VMEM((t, 1), F32), pltpu.VMEM((t, 1), F32), pltpu.VMEM((t, MLA_V), F32)] + side.scratch,
        name="mla_flash_fwd", compiler_params=_cparams(3))(q, kv, kr, *side.arrays)
    return res[0], res[1], res[2:]


def _flash_bwd(q, kv, kr, o, lse, do, cfg, side=None):
    lp, t, nh = cfg.lp, cfg.tq, cfg.heads
    n = lp // t
    scale = (MLA_NOPE + MLA_ROPE) ** -0.5
    side = side or _NoPlan()

    def body(*refs):
        own, ex = _side_refs(refs, 6, 3, 2, side)
        q_ref, kv_ref, kr_ref, o_ref, lse_ref, do_ref, dq_ref, dkv_ref, dkr_ref, dkv_s, dkr_s = own
        hh, j, i = pl.program_id(0), pl.program_id(1), pl.program_id(2)
        if side.n:
            pl.when(jnp.logical_and(hh == 0, jnp.logical_and(i == 0, j == 0)))(lambda: side.start(ex))

        @pl.when(jnp.logical_and(j == 0, i == 0))
        def _():
            dq_ref[...] = jnp.zeros((lp, MLA_QW), F32)

        @pl.when(i == j)
        def _():
            dkv_s[...] = jnp.zeros((t, MLA_QW), F32)
            dkr_s[...] = jnp.zeros((t, LANES), F32)

        def tile(masked):
            qn, qr = q_ref[:, :MLA_NOPE], q_ref[:, MLA_NOPE:]
            kn, v = kv_ref[:, :MLA_NOPE], kv_ref[:, MLA_NOPE:]
            krv = kr_ref[...]
            s = (_dot(qn, kn, "nt") + _dot(qr, krv, "nt")) * scale
            p = jnp.exp(s - lse_ref[:, :1])
            if masked:
                p = jnp.where(_visible(i, j, t), p, 0.0)
            dov = do_ref[...]
            dob = dov.astype(BF16)
            dp = _dot(dob, v, "nt")
            delta = jnp.sum(dov * o_ref[...], axis=1, keepdims=True)
            ds = (p * (dp - delta) * scale).astype(BF16)
            dkv_s[:, MLA_NOPE:] += _dot(p.astype(BF16), dob, "tn")
            dkv_s[:, :MLA_NOPE] += _dot(ds, qn, "tn")
            dkr_s[...] += _dot(ds, qr, "tn")
            off = pl.multiple_of(i * t, t)
            dq_ref[pl.ds(off, t), :MLA_NOPE] += _dot(ds, kn)
            dq_ref[pl.ds(off, t), MLA_NOPE:] += _dot(ds, krv)
        pl.when(jnp.logical_and(i >= j, _needs_mask(i, j)))(lambda: tile(True))
        pl.when(jnp.logical_and(i > j, j > 0))(lambda: tile(False))

        @pl.when(i == n - 1)
        def _():
            dkv_ref[...] = dkv_s[...]
            dkr_ref[0] = dkr_s[...]

        if side.n:
            pl.when(jnp.logical_and(hh == nh - 1, jnp.logical_and(i == n - 1, j == n - 1)))(lambda: side.finish(ex))

    qspec = pl.BlockSpec((t, MLA_QW), lambda h, j, i: (jnp.maximum(i, j), h))
    ospec = pl.BlockSpec((t, MLA_V), lambda h, j, i: (jnp.maximum(i, j), h))
    res = pl.pallas_call(
        body, out_shape=[jax.ShapeDtypeStruct((lp, nh * MLA_QW), F32), jax.ShapeDtypeStruct((lp, nh * MLA_QW), F32),
                         jax.ShapeDtypeStruct((nh, lp, LANES), F32)] + side.out_shapes, grid=(nh, n, n),
        in_specs=[qspec, pl.BlockSpec((t, MLA_QW), lambda h, j, i: (j, h)),
                  pl.BlockSpec((t, LANES), lambda h, j, i: (j, 0)), ospec, ospec, ospec] + [HBM_SPEC] * side.n,
        out_specs=[pl.BlockSpec((lp, MLA_QW), lambda h, j, i: (0, h)),
                   pl.BlockSpec((t, MLA_QW), lambda h, j, i: (j, h)),
                   pl.BlockSpec((1, t, LANES), lambda h, j, i: (h, j, 0))] + [HBM_SPEC] * side.n_out,
        scratch_shapes=[pltpu.VMEM((t, MLA_QW), F32), pltpu.VMEM((t, LANES), F32)] + side.scratch,
        name="mla_flash_bwd", compiler_params=_cparams(3))(q, kv, kr, o, lse, do, *side.arrays)
    return res[0], res[1], res[2], res[3:]


def _pad_heads(w, nh, width):
    r = w.shape[0]
    w3 = w.reshape(r, nh, width)
    return jnp.pad(w3, ((0, 0), (0, 0), (0, MLA_QW - width))).reshape(r, nh * MLA_QW)


def _mla_fwd(q_lat, kv_lat, k_rope_raw, wq, w_uq_p, w_ukv, cfg, side=None):
    tm, nh = cfg.tr, cfg.heads
    ql, kl = _shift(q_lat), _shift(kv_lat)
    kr_raw = jnp.pad(_shift(k_rope_raw), ((0, 0), (0, LANES - MLA_ROPE)))
    cosp, sinp = _rope_tables(cfg)
    qg, kg = wq["l0_mla_q_norm"].reshape(1, -1), wq["l0_mla_kv_norm"].reshape(1, -1)
    qn, kvn = _rowwise(lambda a, b, g1, g2: (_rms(a, g1), _rms(b, g2)), [ql, kl], [qg, kg],
                       [(cfg.q_rank, F32), (cfg.kv_rank, F32)], [], tm, "mla_norm")
    q0 = _mm(qn, w_uq_p, "nn", "mla_q")
    kv = _mm(kvn, w_ukv, "nn", "mla_kv", out_dtype=BF16)

    def rope_fn(q0, kr, cosp, sinp):
        parts = []
        for h in range(nh):
            parts.append(q0[:, h * MLA_QW:h * MLA_QW + MLA_NOPE])
            parts.append(_rope(q0[:, h * MLA_QW + MLA_NOPE:(h + 1) * MLA_QW], cosp, sinp))
        return jnp.concatenate(parts, axis=1), _rope(kr, cosp, sinp)
    q, kr = _rowwise(rope_fn, [q0, kr_raw, cosp, sinp], [], [(nh * MLA_QW, BF16), (LANES, BF16)], [], tm,
                     "mla_rope")
    o, lse, side_out = _flash_fwd(q, kv, kr, cfg, side)
    return _unshift(o), (ql, kl, qn, kvn, q, kv, kr, o, lse, cosp, sinp), side_out


def _mla_bwd(d_b_out, saved, wq, w_uq_p, w_ukv, cfg, side=None):
    ql, kl, qn, kvn, q, kv, kr, o, lse, cosp, sinp = saved
    tm, nh, lp = cfg.tr, cfg.heads, cfg.lp
    dq, dkv, dkr_h, side_out = _flash_bwd(q, kv, kr, o, lse, _shift(d_b_out), cfg, side)

    def rope_bwd(dq, dkr_h, cosp, sinp):
        parts = []
        for h in range(nh):
            parts.append(dq[:, h * MLA_QW:h * MLA_QW + MLA_NOPE])
            parts.append(_rope_t(dq[:, h * MLA_QW + MLA_NOPE:(h + 1) * MLA_QW], cosp, sinp))
        dkr = dkr_h[0]
        for h in range(1, nh):
            dkr = dkr + dkr_h[h]
        return jnp.concatenate(parts, axis=1), _rope_t(dkr, cosp, sinp)
    dq0, dkr_raw = _tile_call(
        rope_bwd, (lp // tm,), [dq, dkr_h, cosp, sinp],
        [_rows(tm, nh * MLA_QW), pl.BlockSpec((nh, tm, LANES), lambda i: (0, i, 0)), _rows(tm, LANES),
         _rows(tm, LANES)],
        [((lp, nh * MLA_QW), F32), ((lp, LANES), F32)], [_rows(tm, nh * MLA_QW), _rows(tm, LANES)], "mla_rope_bwd")
    d_wuq_p = _mm(qn, dq0, "tn", "mla_dwuq")
    dqn = _mm(dq0, w_uq_p, "nt", "mla_dqn")
    d_wukv = _mm(kvn, dkv, "tn", "mla_dwukv")
    dkvn = _mm(dkv, w_ukv, "nt", "mla_dkvn")
    qg, kg = wq["l0_mla_q_norm"].reshape(1, -1), wq["l0_mla_kv_norm"].reshape(1, -1)

    def norm_bwd(ql, kl, dqn, dkvn, g1, g2):
        _, vjp1 = jax.vjp(_rms, ql, g1)
        _, vjp2 = jax.vjp(_rms, kl, g2)
        dql, dg1 = vjp1(dqn)
        dkl, dg2 = vjp2(dkvn)
        return dql, dkl, dg1, dg2
    dql, dkl, dg1, dg2 = _rowwise(norm_bwd, [ql, kl, dqn, dkvn], [qg, kg],
                                  [(cfg.q_rank, F32), (cfg.kv_rank, F32)], [(1, cfg.q_rank), (1, cfg.kv_rank)], tm,
                                  "mla_norm_bwd")
    width = MLA_NOPE + MLA_ROPE
    d_wuq = d_wuq_p.reshape(cfg.q_rank, nh, MLA_QW)[:, :, :width].reshape(cfg.q_rank, nh * width)
    grads = {"l0_mla_q_norm": dg1.reshape(-1), "l0_mla_kv_norm": dg2.reshape(-1), "l0_mla_w_uq": d_wuq,
             "l0_mla_w_ukv": d_wukv}
    return _unshift(dql), _unshift(dkl), _unshift(dkr_raw[:, :MLA_ROPE]), grads, side_out


def _conv_taps(x, halo, first):
    halo = jnp.where(first, 0.0, halo)
    row8 = lax.broadcasted_iota(jnp.int32, halo.shape, 0)
    taps = []
    for s in range(SSD_CONV - 1, 0, -1):
        r = pltpu.roll(x, s, axis=0)
        top = jnp.where(row8 < s, pltpu.roll(halo, s, axis=0), r[:SUBLANES])
        taps.append(jnp.concatenate([top, r[SUBLANES:]], axis=0))
    taps.append(x)
    return taps


def _conv_specs(cfg, lp):
    tm = cfg.tr
    cb = _div_tile(math.gcd(cfg.ssd_inner, cfg.gn), 1024)
    off = cfg.ssd_inner // cb
    per = tm // SUBLANES
    nrow = lp // tm
    main = pl.BlockSpec((tm, cb), lambda i, j: (i, j + off))
    before = pl.BlockSpec((SUBLANES, cb), lambda i, j: (jnp.maximum(i * per - 1, 0), j + off))
    own = pl.BlockSpec((tm, cb), lambda i, j: (i, j))
    after = pl.BlockSpec((SUBLANES, cb), lambda i, j: (jnp.minimum((i + 1) * per, nrow * per - 1), j))
    par = lambda r: pl.BlockSpec((r, cb), lambda i, j: (0, j))
    return tm, cb, nrow, main, before, own, after, par


def _conv_fwd(zx, conv_w, conv_b, cfg):
    lp = zx.shape[0]
    tm, cb, nrow, main, before, own, after, par = _conv_specs(cfg, lp)

    def fn(x, halo, w, b):
        taps = _conv_taps(x, halo, pl.program_id(0) == 0)
        pre = b
        for k in range(SSD_CONV):
            pre = pre + taps[k] * w[k:k + 1, :]
        return _silu(pre)
    return _tile_call(fn, (nrow, cfg.conv_dim // cb), [zx, zx, conv_w, conv_b],
                      [main, before, par(SSD_CONV), par(1)], [((lp, cfg.conv_dim), F32)], [own], "ssd_conv")[0]


def _conv_bwd(zx, conv_w, conv_b, dxs, dbm, dcm, dzx, cfg):
    lp = zx.shape[0]
    tm, cb, nrow, main, before, own, after, par = _conv_specs(cfg, lp)
    ncb = cfg.conv_dim // cb
    nx, nb = cfg.ssd_inner // cb, cfg.gn // cb
    off = nx

    def fn1(x, halo, w, b, d1, d2, d3):
        j = pl.program_id(0)
        da = jnp.where(j < nx, d1, jnp.where(j < nx + nb, d2, d3))
        taps = _conv_taps(x, halo, pl.program_id(1) == 0)
        pre = b
        for k in range(SSD_CONV):
            pre = pre + taps[k] * w[k:k + 1, :]
        sg = _sigmoid(pre)
        dpre = da * sg * (1.0 + pre * (1.0 - sg))
        row8 = lax.broadcasted_iota(jnp.int32, (SUBLANES, cb), 0)
        dw = jnp.zeros((SUBLANES, cb), F32)
        for k in range(SSD_CONV):
            dw = jnp.where(row8 == k, jnp.sum(dpre * taps[k], axis=0, keepdims=True), dw)
        return dpre, dw, jnp.sum(dpre, axis=0, keepdims=True)
    sw = lambda spec: pl.BlockSpec(spec.block_shape, lambda j, i, f=spec.index_map: f(i, j))
    piece = lambda lo, n: pl.BlockSpec((tm, cb), lambda j, i: (i, jnp.clip(j - lo, 0, n - 1)))
    dpre, dw, db = _tile_call(
        fn1, (ncb, nrow), [zx, zx, conv_w, conv_b, dxs, dbm, dcm],
        [sw(main), sw(before), sw(par(SSD_CONV)), sw(par(1)), piece(0, nx), piece(nx, nb), piece(nx + nb, nb)],
        [((lp, cfg.conv_dim), F32), ((SUBLANES, cfg.conv_dim), F32), ((1, cfg.conv_dim), F32)],
        [sw(own), sw(par(SUBLANES)), sw(par(1))], "ssd_conv_bwd1", acc=(1, 2), acc_axis=1)

    def fn2(dp, nxt, w):
        nxt = jnp.where(pl.program_id(0) == nrow - 1, 0.0, nxt)
        row8 = lax.broadcasted_iota(jnp.int32, nxt.shape, 0)
        dx = dp * w[SSD_CONV - 1:SSD_CONV, :]
        for s in range(1, SSD_CONV):
            r = pltpu.roll(dp, tm - s, axis=0)
            bot = jnp.where(row8 >= SUBLANES - s, pltpu.roll(nxt, SUBLANES - s, axis=0), r[tm - SUBLANES:])
            up = jnp.concatenate([r[:tm - SUBLANES], bot], axis=0)
            dx = dx + up * w[SSD_CONV - 1 - s:SSD_CONV - s, :]
        return dx
    dzx = _tile_call(fn2, (nrow, ncb), [dpre, dpre, conv_w], [own, after, par(SSD_CONV)],
                     [(dzx.shape, BF16)], [main], "ssd_conv_bwd2", fill=(dzx, 0))[0]
    return dzx, dw, db


def _ssd_common(x_ref, b_ref, c_ref, dt_ref, dtt_ref, ar_ref, ac_ref, h):
    q = SSD_BLOCK
    x, bm, cm = x_ref[...], b_ref[...], c_ref[...]
    dt, dtt = dt_ref[0], dtt_ref[0]
    row = lax.broadcasted_iota(jnp.int32, (q, q), 0)
    col = lax.broadcasted_iota(jnp.int32, (q, q), 1)
    tri = row >= col
    cs = _dot(tri.astype(F32), dt * ar_ref[0], precision=HI)
    cst = _dot(dtt * ac_ref[0], (row <= col).astype(F32), precision=HI)
    g = _bdot(cm, bm, "nt")
    ch = _bdot(cm, h)
    hpg, gw = dt.shape[1], x.shape[1]
    e = (lax.broadcasted_iota(jnp.int32, (hpg, gw), 1) // SSD_HEAD_DIM
         == lax.broadcasted_iota(jnp.int32, (hpg, gw), 0)).astype(F32)
    et = (lax.broadcasted_iota(jnp.int32, (gw, hpg), 0) // SSD_HEAD_DIM
          == lax.broadcasted_iota(jnp.int32, (gw, hpg), 1)).astype(F32)
    spread = lambda v: _dot(v, e, precision=HI)
    gather = lambda v: _dot(v, et, precision=HI)
    return x, bm, cm, dt, tri, cs, cst, g, ch, spread, gather


def _ssd_specs(cfg, rev):
    q, n, gw, hpg = SSD_BLOCK, SSD_STATE, cfg.gw, cfg.hpg
    nc = cfg.lp // q
    cc = (lambda c: nc - 1 - c) if rev else (lambda c: c)
    boff = cfg.ssd_inner // n
    return dict(
        x=pl.BlockSpec((q, gw), lambda g, c: (cc(c), g)),
        b=pl.BlockSpec((q, n), lambda g, c: (cc(c), boff + g)),
        c=pl.BlockSpec((q, n), lambda g, c: (cc(c), boff + SSD_GROUPS + g)),
        bc_out=pl.BlockSpec((q, n), lambda g, c: (cc(c), g)),
        dt=pl.BlockSpec((1, q, hpg), lambda g, c: (g, cc(c), 0)),
        dtt=pl.BlockSpec((1, hpg, q), lambda g, c: (g, 0, cc(c))),
        ar=pl.BlockSpec((1, 1, hpg), lambda g, c: (g, 0, 0)),
        ac=pl.BlockSpec((1, hpg, 1), lambda g, c: (g, 0, 0)),
        h=pl.BlockSpec((1, n, gw), lambda g, c: (cc(c), 0, g)))


def _ssd_fwd(xbc, dt_g, dtt_g, a_row, a_col, cfg):
    q, n, gw, hpg, lp = SSD_BLOCK, SSD_STATE, cfg.gw, cfg.hpg, cfg.lp
    nc = lp // q
    hd = SSD_HEAD_DIM
    sp = _ssd_specs(cfg, False)

    def body(x_ref, b_ref, c_ref, dt_ref, dtt_ref, ar_ref, ac_ref, y_ref, hp_ref, h_s):
        @pl.when(pl.program_id(1) == 0)
        def _():
            h_s[...] = jnp.zeros((n, gw), F32)
        h = h_s[...]
        hp_ref[0] = h
        x, bm, cm, dt, tri, cs, cst, g, ch, spread, _ = _ssd_common(x_ref, b_ref, c_ref, dt_ref, dtt_ref, ar_ref,
                                                                    ac_ref, h)
        last = cs[q - 1:q, :]
        xdt = x * spread(dt)
        y_off = spread(jnp.exp(cs)) * ch
        xw = xdt * spread(jnp.exp(last - cs))
        for r in range(hpg):
            sl = slice(r * hd, (r + 1) * hd)
            lm = jnp.exp(jnp.where(tri, cs[:, r:r + 1] - cst[r:r + 1, :], NEG))
            y_ref[:, sl] = _bdot(g * lm, xdt[:, sl]) + y_off[:, sl]
        h_s[...] = h * spread(jnp.exp(last)) + _bdot(bm, xw, "tn")

    return pl.pallas_call(
        body, out_shape=[jax.ShapeDtypeStruct((lp, cfg.ssd_inner), F32),
                         jax.ShapeDtypeStruct((nc, n, cfg.ssd_inner), F32)],
        grid=(SSD_GROUPS, nc),
        in_specs=[sp["x"], sp["b"], sp["c"], sp["dt"], sp["dtt"], sp["ar"], sp["ac"]],
        out_specs=[sp["x"], sp["h"]],
        scratch_shapes=[pltpu.VMEM((n, gw), F32)],
        name="ssd_scan", compiler_params=_cparams(2))(xbc, xbc, xbc, dt_g, dtt_g, a_row, a_col)


def _ssd_bwd(xbc, dt_g, dtt_g, a_row, a_col, hprev, dy, dx_gate, cfg):
    q, n, gw, hpg, lp = SSD_BLOCK, SSD_STATE, cfg.gw, cfg.hpg, cfg.lp
    nc = lp // q
    hd = SSD_HEAD_DIM
    sp = _ssd_specs(cfg, True)

    def body(x_ref, b_ref, c_ref, dt_ref, dtt_ref, ar_ref, ac_ref, hp_ref, dy_ref, dxg_ref,
             dx_ref, db_ref, dc_ref, ddt_ref, da_ref, dh_s, dxdt_s):
        @pl.when(pl.program_id(1) == 0)
        def _():
            dh_s[...] = jnp.zeros((n, gw), F32)
            da_ref[...] = jnp.zeros((1, 1, hpg), F32)
        h = hp_ref[0]
        dhn = dh_s[...]
        dy = dy_ref[...]
        x, bm, cm, dt, tri, cs, cst, g, ch, spread, gather = _ssd_common(x_ref, b_ref, c_ref, dt_ref, dtt_ref,
                                                                         ar_ref, ac_ref, h)
        last = cs[q - 1:q, :]
        e, wv, elast = jnp.exp(cs), jnp.exp(last - cs), jnp.exp(last)
        dt_x, w_x = spread(dt), spread(wv)
        xdt = x * dt_x
        dye = dy * spread(e)
        xw = xdt * w_x
        bd = _bdot(bm, dhn)
        de = gather(dy * ch)
        dw = gather(xdt * bd)
        hsum = gather(jnp.sum(dhn * h, axis=0, keepdims=True))
        head_lane = lax.broadcasted_iota(jnp.int32, (q, hpg), 1)
        head_row = lax.broadcasted_iota(jnp.int32, (hpg, q), 0)
        z_rows = jnp.zeros((q, hpg), F32)
        z_cols = jnp.zeros((hpg, q), F32)
        dg = jnp.zeros((q, q), F32)
        for r in range(hpg):
            sl = slice(r * hd, (r + 1) * hd)
            lm = jnp.exp(jnp.where(tri, cs[:, r:r + 1] - cst[r:r + 1, :], NEG))
            m = g * lm
            dyr = dy[:, sl]
            dxdt_s[:, sl] = _bdot(m, dyr, "tn")
            dm = _bdot(dyr, xdt[:, sl], "nt")
            dg = dg + dm * lm
            z = dm * m
            z_rows = jnp.where(head_lane == r, jnp.sum(z, axis=1, keepdims=True), z_rows)
            z_cols = jnp.where(head_row == r, jnp.sum(z, axis=0, keepdims=True), z_cols)
        dxdt = dxdt_s[...] + w_x * bd
        is_last = lax.broadcasted_iota(jnp.int32, (q, 1), 0) == q - 1
        extra = jnp.sum(dw * wv, axis=0, keepdims=True) + elast * hsum
        eye = (lax.broadcasted_iota(jnp.int32, (hpg, hpg), 0)
               == lax.broadcasted_iota(jnp.int32, (hpg, hpg), 1)).astype(F32)
        dcs = (z_rows + de * e - dw * wv + jnp.where(is_last, extra, 0.0)
               - _dot(z_cols, eye, "tn", precision=HI))
        row = lax.broadcasted_iota(jnp.int32, (q, q), 0)
        col = lax.broadcasted_iota(jnp.int32, (q, q), 1)
        dda = _dot((row <= col).astype(F32), dcs, precision=HI)
        ddt_ref[0] = dda * ar_ref[0] + gather(dxdt * x)
        da_ref[0] += jnp.sum(dda * dt, axis=0, keepdims=True)
        dx_ref[...] = dxdt * dt_x + dxg_ref[...]
        dc_ref[...] = _bdot(dg, bm) + _bdot(dye, h, "nt")
        db_ref[...] = _bdot(dg, cm, "tn") + _bdot(xw, dhn, "nt")
        dh_s[...] = dhn * spread(elast) + _bdot(cm, dye, "tn")

    return pl.pallas_call(
        body, out_shape=[jax.ShapeDtypeStruct((lp, cfg.ssd_inner), F32), jax.ShapeDtypeStruct((lp, cfg.gn), F32),
                         jax.ShapeDtypeStruct((lp, cfg.gn), F32), jax.ShapeDtypeStruct((SSD_GROUPS, lp, hpg), F32),
                         jax.ShapeDtypeStruct((SSD_GROUPS, 1, hpg), F32)],
        grid=(SSD_GROUPS, nc),
        in_specs=[sp["x"], sp["b"], sp["c"], sp["dt"], sp["dtt"], sp["ar"], sp["ac"], sp["h"], sp["x"], sp["x"]],
        out_specs=[sp["x"], sp["bc_out"], sp["bc_out"], sp["dt"], sp["ar"]],
        scratch_shapes=[pltpu.VMEM((n, gw), F32), pltpu.VMEM((q, gw), F32)],
        name="ssd_scan_bwd", compiler_params=_cparams(2))(xbc, xbc, xbc, dt_g, dtt_g, a_row, a_col, hprev, dy, dx_gate)


def _gate_fn(y, xs, z, dexp, ng):
    return _rms((y + dexp * xs) * _silu(z), ng)


def _gate_specs(cfg):
    tm, gw = cfg.tm, cfg.gw
    blk = pl.BlockSpec((tm, gw), lambda g, i: (i, g))
    par = pl.BlockSpec((1, gw), lambda g, i: (0, g))
    return blk, par


def _mamba_fwd(h, w, w_in_t, w_out, conv_w, cfg):
    lp, tm, nh, hpg, inner = cfg.lp, cfg.tm, cfg.ssd_heads, cfg.hpg, cfg.ssd_inner
    zx = _mm(h, w_in_t, "nt", "l1_in")
    conv_b = w["l1_conv_b"].reshape(1, -1)
    xbc = _conv_fwd(zx, conv_w, conv_b, cfg)
    dt_raw = zx[:, inner + cfg.conv_dim:inner + cfg.conv_dim + nh]
    dt_bias = w["l1_dt_bias"].reshape(1, nh)
    a_log = w["l1_a_log"].reshape(1, nh)
    dt = _rowwise(lambda r, b: _softplus(r + b), [dt_raw], [dt_bias], [(nh, F32)], [], tm, "ssd_dt")[0]
    a = _small_call(lambda al: -jnp.exp(al), [a_log], [(1, nh)], "ssd_a")[0]
    dt_g = dt.reshape(lp, SSD_GROUPS, hpg).transpose(1, 0, 2)
    dtt_g = dt_g.transpose(0, 2, 1)
    a_row, a_col = a.reshape(SSD_GROUPS, 1, hpg), a.reshape(SSD_GROUPS, hpg, 1)
    y, hprev = _ssd_fwd(xbc, dt_g, dtt_g, a_row, a_col, cfg)
    dexp = jnp.repeat(w["l1_d"], SSD_HEAD_DIM).reshape(1, inner)
    ng = w["l1_norm_g"].reshape(1, inner)
    blk, par = _gate_specs(cfg)
    yn = _tile_call(_gate_fn, (SSD_GROUPS, lp // tm), [y, xbc, zx, dexp, ng], [blk, blk, blk, par, par],
                    [((lp, inner), F32)], [blk], "ssd_gate")[0]
    mo = _mm(yn, w_out, "nn", "l1_out")
    return mo, (zx, xbc, dt_raw, dt_g, dtt_g, a, a_row, a_col, y, hprev, dexp, ng, yn)


def _mamba_bwd(h, saved, dmo, w, w_in_t, w_out, conv_w, cfg):
    zx, xbc, dt_raw, dt_g, dtt_g, a, a_row, a_col, y, hprev, dexp, ng, yn = saved
    lp, tm, nh, hpg, inner = cfg.lp, cfg.tm, cfg.ssd_heads, cfg.hpg, cfg.ssd_inner
    d_wout = _mm(yn, dmo, "tn", "l1_dwout", out_dtype=BF16)
    dyn = _mm(dmo, w_out, "nt", "l1_dyn")
    blk, par = _gate_specs(cfg)

    def gate_bwd(y, xs, z, dexp, ng, dyn):
        _, vjp = jax.vjp(_gate_fn, y, xs, z, dexp, ng)
        return vjp(dyn)
    dy, dxs_gate, dzx, ddexp, dng = _tile_call(
        gate_bwd, (SSD_GROUPS, lp // tm), [y, xbc, zx, dexp, ng, dyn], [blk, blk, blk, par, par, blk],
        [((lp, inner), F32)] * 2 + [((lp, cfg.l1_inp), BF16)] + [((1, inner), F32)] * 2, [blk, blk, blk, par, par],
        "ssd_gate_bwd", acc=(3, 4), acc_axis=1)
    dxs, dbm, dcm, ddt_g, da_g = _ssd_bwd(xbc, dt_g, dtt_g, a_row, a_col, hprev, dy, dxs_gate, cfg)
    conv_b = w["l1_conv_b"].reshape(1, -1)
    dzx, dconv_w, dconv_b = _conv_bwd(zx, conv_w, conv_b, dxs, dbm, dcm, dzx, cfg)
    assert cfg.l1_inp - inner - cfg.conv_dim == LANES
    ddt = jnp.pad(ddt_g.transpose(1, 0, 2).reshape(lp, nh), ((0, 0), (0, LANES - nh)))
    dt_bias = jnp.pad(w["l1_dt_bias"].reshape(1, nh), ((0, 0), (0, LANES - nh)))
    last = (inner + cfg.conv_dim) // LANES
    tail = pl.BlockSpec((tm, LANES), lambda i: (i, last))

    def dt_bwd(ddt, r, b):
        lane = lax.broadcasted_iota(jnp.int32, ddt.shape, 1)
        d = jnp.where(lane < nh, ddt * _sigmoid(r + b), 0.0)
        return d, jnp.sum(d, axis=0, keepdims=True)
    dzx, ddt_bias = _tile_call(dt_bwd, (lp // tm,), [ddt, zx, dt_bias], [_rows(tm, LANES), tail, _whole((1, LANES))],
                               [(dzx.shape, BF16), ((1, LANES), F32)], [tail, _whole((1, LANES))], "ssd_dt_bwd",
                               acc=(1,), fill=(dzx, 0))
    ddt_bias = ddt_bias[:, :nh]
    da_log, dd = _small_call(lambda da, a, dde: (da * a, jnp.sum(dde, axis=1, keepdims=True)),
                             [da_g.reshape(1, nh), a, ddexp.reshape(nh, SSD_HEAD_DIM)], [(1, nh), (nh, 1)],
                             "ssd_small_bwd")
    d_win_t = _mm(dzx, h, "tn", "l1_dwin", out_dtype=BF16, tm_t=1152)
    dh = _mm(dzx, w_in_t, "nn", "l1_dh")
    grads = {"l1_w_in": d_win_t, "l1_conv_w": dconv_w[:SSD_CONV], "l1_conv_b": dconv_b.reshape(-1),
             "l1_dt_bias": ddt_bias.reshape(-1), "l1_a_log": da_log.reshape(-1), "l1_d": dd.reshape(-1),
             "l1_norm_g": dng.reshape(-1), "l1_w_out": d_wout}
    return dh, grads


def _local_step(x, target, w, cfg, late_weights=None, early_grads=None):
    lp, n, d, tm, sw = cfg.lp, cfg.n, cfg.d, cfg.tr, cfg.s5_w
    row = lambda name: w[name].reshape(1, -1)
    h0 = jnp.concatenate([w["meta_tokens"], x, jnp.zeros((lp - n, d), F32)], axis=0)
    proj = _mm(h0, w["l0_w_in"], "nn", "l0_in")
    o1, o2, o3 = sw, sw + cfg.q_rank, sw + cfg.q_rank + cfg.kv_rank
    prm = _s5_prep(w, cfg)
    a_out, s5_saved = _s5_fwd(proj[:, :o1], prm, w["l0_s5_w_glu"], cfg)
    b_out, mla_saved, arrived = _mla_fwd(proj[:, o1:o2], proj[:, o2:o3], proj[:, o3:], w, w["l0_mla_w_uq_p"],
                                         w["l0_mla_w_ukv"], cfg, late_weights[0] if late_weights else None)
    if late_weights:
        w = dict(w, **late_weights[1](arrived))
    mix = jnp.concatenate([a_out, b_out], axis=1).astype(BF16)
    mo0 = _mm(mix, w["l0_w_out"], "nn", "l0_out")
    h1, h1b = _ln_fwd(h0, mo0, row("l0_ln1_g"), row("l0_ln1_b"), cfg, "l0_ln1")
    fo0, ffn0 = _ffn_fwd(h1b, w["l0_w_gu"], w["l0_ffn_w_down"], cfg, "l0_ffn")
    h2, h2b = _ln_fwd(h1, fo0, row("l0_ln2_g"), row("l0_ln2_b"), cfg, "l0_ln2")
    mo1, mam = _mamba_fwd(h2b, w, w["l1_w_in_t"], w["l1_w_out"], w["l1_conv_w"], cfg)
    h3, h3b = _ln_fwd(h2, mo1, row("l1_ln1_g"), row("l1_ln1_b"), cfg, "l1_ln1")
    fo1, ffn1 = _ffn_fwd(h3b, w["l1_w_gu"], w["l1_ffn_w_down"], cfg, "l1_ffn")
    h4, _ = _ln_fwd(h3, fo1, row("l1_ln2_g"), row("l1_ln2_b"), cfg, "l1_ln2")
    tgt = jnp.concatenate([jnp.zeros((N_META, d), F32), target, jnp.zeros((lp - n, d), F32)], axis=0)

    def loss_fn(y, t):
        r = pl.program_id(0) * tm + lax.broadcasted_iota(jnp.int32, (tm, 1), 0)
        diff = jnp.where(jnp.logical_and(r >= N_META, r < n), y - t, 0.0)
        return diff * (1.0 / d), jnp.sum(diff * diff, axis=0, keepdims=True) * (0.5 / d)
    dh4, loss_lanes = _rowwise(loss_fn, [h4, tgt], [], [(d, F32)], [(1, d)], tm, "loss")
    grads = {}
    dr4, dr4b, dg, db = _ln_bwd(h3, fo1, row("l1_ln2_g"), row("l1_ln2_b"), [dh4], [1.0], cfg, "l1_ln2_bwd")
    grads["l1_ln2_g"], grads["l1_ln2_b"] = dg.reshape(-1), db.reshape(-1)
    dh3, grads["l1_w_gu"], grads["l1_ffn_w_down"] = _ffn_bwd(h3b, ffn1, dr4b, w["l1_w_gu"], w["l1_ffn_w_down"], cfg,
                                                             "l1_ffn")
    dr3, dr3b, dg, db = _ln_bwd(h2, mo1, row("l1_ln1_g"), row("l1_ln1_b"), [dr4, dh3], [DN_ALPHA, 1.0], cfg,
                                "l1_ln1_bwd")
    grads["l1_ln1_g"], grads["l1_ln1_b"] = dg.reshape(-1), db.reshape(-1)
    dh2, mg = _mamba_bwd(h2b, mam, dr3b, w, w["l1_w_in_t"], w["l1_w_out"], w["l1_conv_w"], cfg)
    grads.update(mg)
    dr2, dr2b, dg, db = _ln_bwd(h1, fo0, row("l0_ln2_g"), row("l0_ln2_b"), [dr3, dh2], [DN_ALPHA, 1.0], cfg,
                                "l0_ln2_bwd")
    grads["l0_ln2_g"], grads["l0_ln2_b"] = dg.reshape(-1), db.reshape(-1)
    dh1, grads["l0_w_gu"], grads["l0_ffn_w_down"] = _ffn_bwd(h1b, ffn0, dr2b, w["l0_w_gu"], w["l0_ffn_w_down"], cfg,
                                                             "l0_ffn")
    dr1, dr1b, dg, db = _ln_bwd(h0, mo0, row("l0_ln1_g"), row("l0_ln1_b"), [dr2, dh1], [DN_ALPHA, 1.0], cfg,
                                "l0_ln1_bwd")
    grads["l0_ln1_g"], grads["l0_ln1_b"] = dg.reshape(-1), db.reshape(-1)
    grads["l0_w_out"] = _mm(mix, dr1b, "tn", "l0_dwout", out_dtype=BF16)
    dmix = _mm(dr1b, w["l0_w_out"], "nt", "l0_dmix")
    du, sg = _s5_bwd(dmix[:, :sw], s5_saved, prm, w, w["l0_s5_w_glu"], cfg)
    dql, dkl, dkr, ag, exchanged = _mla_bwd(dmix[:, sw:], mla_saved, w, w["l0_mla_w_uq_p"], w["l0_mla_w_ukv"], cfg,
                                            early_grads(grads) if early_grads else None)
    grads.update(sg)
    grads.update(ag)
    dproj = jnp.concatenate([du, dql, dkl, dkr], axis=1)
    grads["l0_w_in"] = _mm(h0, dproj, "tn", "l0_dwin", out_dtype=BF16)
    dh0m = _mm(dproj, w["l0_w_in"], "nt", "l0_dh")
    dh0 = _rowwise(lambda a, b: DN_ALPHA * a + b, [dr1, dh0m], [], [(d, F32)], [], tm, "l0_dh0")[0]
    grads["meta_tokens"] = dh0[:N_META]
    return loss_lanes, dh0[N_META:n], grads, exchanged


FLAT_W = 1024
N_SLOTS = 4
HBM_SPEC = pl.BlockSpec(memory_space=pltpu.HBM)


def _place():
    x, y, c = lax.axis_index("x"), lax.axis_index("y"), lax.axis_index("c")
    chips = [(1 - x, y), (x, 1 - y), (1 - x, 1 - y)]
    return x, y, c, chips


def _remote(src, dst, ssem, rsem, dev):
    return pltpu.make_async_remote_copy(src_ref=src, dst_ref=dst, send_sem=ssem, recv_sem=rsem, device_id=dev,
                                        device_id_type=MESH)


class _GatherPlan:
    def __init__(self, shards, groups):
        self.arrays = list(shards)
        self.n = n = len(shards)
        self.rows = [s.shape[0] for s in shards]
        self.place = {t: (g, row0) for g, members in enumerate(groups) for t, row0 in members}
        self.n_out = len(groups)
        self.out_shapes = []
        for members in groups:
            t0 = members[0][0]
            rows = max(row0 + N_SLOTS * shards[t].shape[0] for t, row0 in members)
            self.out_shapes.append(jax.ShapeDtypeStruct((rows, shards[t0].shape[1]), shards[t0].dtype))
        sems = pltpu.SemaphoreType.DMA((3 * n,))
        self.scratch = [sems, sems, sems, sems, pltpu.SemaphoreType.DMA((n,))]

    def _copies(self, refs):
        n = self.n
        srcs, outs = refs[:n], refs[n:n + self.n_out]
        send_sems, recv_sems, fsend, frecv, lsems = refs[n + self.n_out:]
        x, y, c, chips = _place()
        me = 2 * x + y
        sib = (x, y, 1 - c)

        def rows_of(t, slot, half):
            r = self.rows[t]
            g, row0 = self.place[t]
            return outs[g].at[pl.ds(row0 + slot * r + half * (r // 2), r // 2)]
        local, send, arrive, relay, arrive_sib = [], [], [], [], []
        for t in range(n):
            r = self.rows[t]
            g, row0 = self.place[t]
            local.append(pltpu.make_async_copy(srcs[t], outs[g].at[pl.ds(row0 + me * r, r)], lsems.at[t]))
            mine = srcs[t].at[pl.ds(c * (r // 2), r // 2)]
            for j, (cx, cy) in enumerate(chips):
                k = 3 * t + j
                send.append(_remote(mine, rows_of(t, me, c), send_sems.at[k], recv_sems.at[k], (cx, cy, c)))
                got = rows_of(t, 2 * cx + cy, c)
                arrive.append(_remote(got, got, send_sems.at[k], recv_sems.at[k], (cx, cy, c)))
                relay.append(_remote(got, got, fsend.at[k], frecv.at[k], sib))
                got_sib = rows_of(t, 2 * cx + cy, 1 - c)
                arrive_sib.append(_remote(got_sib, got_sib, fsend.at[k], frecv.at[k], sib))
        return local, send, arrive, relay, arrive_sib

    def start(self, refs):
        local, send, _, _, _ = self._copies(refs)
        for cp in local + send:
            cp.start()

    def relay(self, refs):
        _, _, arrive, relay, _ = self._copies(refs)
        for a, r in zip(arrive, relay):
            a.wait_recv()
            r.start()

    def finish(self, refs):
        local, send, _, relay, arrive_sib = self._copies(refs)
        for cp in arrive_sib:
            cp.wait_recv()
        for cp in send + relay:
            cp.wait_send()
        for cp in local:
            cp.wait()


class _ExchangePlan:
    def __init__(self, items):
        self.items = items
        self.arrays = [a for a, _, _ in items]
        self.n = self.n_out = n = len(items)
        self.out_shapes = [jax.ShapeDtypeStruct((3, rps, a.shape[1]), a.dtype) for a, _, rps in items]
        sems = pltpu.SemaphoreType.DMA((3 * n,))
        self.scratch = [sems, sems]

    def _copies(self, refs):
        n = self.n
        srcs, outs = refs[:n], refs[n:2 * n]
        send_sems, recv_sems = refs[2 * n:]
        x, y, c, chips = _place()
        cps = []
        for t, (_, row0, rps) in enumerate(self.items):
            for j, (cx, cy) in enumerate(chips):
                cps.append(_remote(srcs[t].at[pl.ds(row0 + (2 * cx + cy) * rps, rps)], outs[t].at[j],
                                   send_sems.at[3 * t + j], recv_sems.at[3 * t + j], (cx, cy, c)))
        return cps

    def start(self, refs):
        for cp in self._copies(refs):
            cp.start()

    def relay(self, refs):
        pass

    def finish(self, refs):
        for cp in self._copies(refs):
            cp.wait()


def _run_plan(plan, name):
    def body(*refs):
        plan.start(refs)
        plan.relay(refs)
        plan.finish(refs)
    return pl.pallas_call(body, out_shape=plan.out_shapes, in_specs=[HBM_SPEC] * plan.n,
                          out_specs=[HBM_SPEC] * plan.n_out, scratch_shapes=plan.scratch, name=name)(*plan.arrays)


def _exchange_sibling(arrays):
    n = len(arrays)

    def body(*refs):
        srcs, outs = refs[:n], refs[n:2 * n]
        ssems, rsems = refs[2 * n:]
        x, y, c, _ = _place()
        cps = []
        for t in range(n):
            cp = _remote(srcs[t], outs[t], ssems.at[t], rsems.at[t], (x, y, 1 - c))
            cp.start()
            cps.append(cp)
        for cp in cps:
            cp.wait()

    sems = pltpu.SemaphoreType.DMA((n,))
    return pl.pallas_call(
        body, out_shape=[jax.ShapeDtypeStruct(a.shape, a.dtype) for a in arrays], in_specs=[HBM_SPEC] * n,
        out_specs=[HBM_SPEC] * n, scratch_shapes=[sems, sems], name="exchange_sibling")(*arrays)


def _gather_all(v):
    flips = [(fx, fy, fc) for fx in (0, 1) for fy in (0, 1) for fc in (0, 1)][1:]

    def body(src, out, send_sems, recv_sems, lsem):
        x, y, c, _ = _place()
        local = pltpu.make_async_copy(src, out.at[4 * x + 2 * y + c], lsem)
        local.start()
        cps = []
        for k, (fx, fy, fc) in enumerate(flips):
            px, py, pc = (1 - x if fx else x), (1 - y if fy else y), (1 - c if fc else c)
            cp = _remote(src, out.at[4 * x + 2 * y + c], send_sems.at[k], recv_sems.at[k], (px, py, pc))
            cp.start()
            cps.append(cp)
        for cp in cps:
            cp.wait()
        local.wait()

    return pl.pallas_call(
        body, out_shape=jax.ShapeDtypeStruct((8,) + v.shape, v.dtype), in_specs=[HBM_SPEC], out_specs=HBM_SPEC,
        scratch_shapes=[pltpu.SemaphoreType.DMA((7,)), pltpu.SemaphoreType.DMA((7,)), pltpu.SemaphoreType.DMA],
        name="gather_all")(v)


def _flat_rows(n_elems, row_unit):
    return -(-n_elems // (FLAT_W * row_unit)) * row_unit


def _to_flat(pieces, rows):
    flat = jnp.concatenate([p.reshape(-1) for p in pieces])
    return jnp.pad(flat, (0, rows * FLAT_W - flat.shape[0])).reshape(rows, FLAT_W)


def _sum_parts(parts, name, tm=512):
    rows = parts[0].shape[1]
    tm = _div_tile(rows, tm, SUBLANES)

    def fn(*ps):
        acc = None
        for p in ps:
            for k in range(p.shape[0]):
                acc = p[k].astype(F32) if acc is None else acc + p[k].astype(F32)
        return acc
    return _tile_call(fn, (rows // tm,), parts,
                      [pl.BlockSpec((p.shape[0], tm, FLAT_W), lambda i: (0, i, 0)) for p in parts],
                      [((rows, FLAT_W), F32)], [_rows(tm, FLAT_W)], name)[0]


ELEMENTWISE_BLOCK = 1 << 19


def _row_tile(rows, cols, unit):
    return _div_tile(rows, max(unit, ELEMENTWISE_BLOCK // cols), unit)


def _sum_slot(g, row0, rps, others, me, name):
    c = g.shape[1]
    tm = _row_tile(rps, c, 2 * SUBLANES)
    nrb = rps // tm
    assert row0 % tm == 0

    def body(me_ref, g_ref, o_ref, out_ref):
        out_ref[...] = ((g_ref[...].astype(F32) + o_ref[0].astype(F32)) + o_ref[1].astype(F32)) + o_ref[2].astype(F32)

    grid_spec = pltpu.PrefetchScalarGridSpec(
        num_scalar_prefetch=1, grid=(nrb,),
        in_specs=[pl.BlockSpec((tm, c), lambda i, me_ref: (row0 // tm + me_ref[0] * nrb + i, 0)),
                  pl.BlockSpec((3, tm, c), lambda i, me_ref: (0, i, 0))],
        out_specs=pl.BlockSpec((tm, c), lambda i, me_ref: (i, 0)))
    return pl.pallas_call(body, out_shape=jax.ShapeDtypeStruct((rps, c), F32), grid_spec=grid_spec, name=name,
                          compiler_params=_cparams(1))(me, g, others)


def _adamw(gparts, w, m, v, name, tm=None):
    rows, cols = w.shape
    tm = _row_tile(rows, cols, SUBLANES) if tm is None else _div_tile(rows, tm, SUBLANES)
    ng = len(gparts)

    def fn(*a):
        g = a[0]
        for t in a[1:ng]:
            g = g + t
        w, m, v = a[ng:]
        m = ADAM_B1 * m + (1.0 - ADAM_B1) * g
        v = ADAM_B2 * v + (1.0 - ADAM_B2) * (g * g)
        m_hat = m / (1.0 - ADAM_B1 ** ADAM_STEP)
        v_hat = v / (1.0 - ADAM_B2 ** ADAM_STEP)
        delta = -ADAM_LR * (m_hat / (jnp.sqrt(v_hat) + ADAM_EPS) + ADAM_WD * w)
        return g, delta, m, v
    ins = list(gparts) + [w, m, v]
    return _tile_call(fn, (rows // tm,), ins, [_rows(tm, cols)] * len(ins), [((rows, cols), F32)] * 4,
                      [_rows(tm, cols)] * 4, name)


FIRST = ("l0_w_in", "l0_s5_w_glu", "l0_mla_w_uq", "l0_mla_w_ukv")
REST = ("l0_w_out", "l0_ffn_w_gate", "l0_ffn_w_up", "l0_ffn_w_down", "l1_w_in", "l1_w_out", "l1_ffn_w_gate",
        "l1_ffn_w_up", "l1_ffn_w_down")
BIG = FIRST + REST
TINY = ("meta_tokens", "l1_conv_w")
REPLICATED = ("l0_s5_log_dt", "l0_s5_a_re", "l0_s5_a_im", "l0_s5_b_re", "l0_s5_b_im", "l0_s5_c_re", "l0_s5_c_im",
              "l0_s5_d", "l0_mla_q_norm", "l0_mla_kv_norm", "l0_ln1_g", "l0_ln1_b", "l0_ln2_g", "l0_ln2_b",
              "l1_conv_b", "l1_dt_bias", "l1_a_log", "l1_d", "l1_norm_g", "l1_ln1_g", "l1_ln1_b", "l1_ln2_g",
              "l1_ln2_b")
WEIGHTS = ("meta_tokens", "l0_w_in", "l0_s5_log_dt", "l0_s5_a_re", "l0_s5_a_im", "l0_s5_b_re", "l0_s5_b_im",
           "l0_s5_c_re", "l0_s5_c_im", "l0_s5_d", "l0_s5_w_glu", "l0_mla_q_norm", "l0_mla_w_uq", "l0_mla_kv_norm",
           "l0_mla_w_ukv", "l0_w_out", "l0_ln1_g", "l0_ln1_b", "l0_ffn_w_gate", "l0_ffn_w_up", "l0_ffn_w_down",
           "l0_ln2_g", "l0_ln2_b", "l1_w_in", "l1_conv_w", "l1_conv_b", "l1_dt_bias", "l1_a_log", "l1_d",
           "l1_norm_g", "l1_w_out", "l1_ln1_g", "l1_ln1_b", "l1_ffn_w_gate", "l1_ffn_w_up", "l1_ffn_w_down",
           "l1_ln2_g", "l1_ln2_b")
def _split_flat(flat2d, shapes):
    flat = flat2d.reshape(-1)
    out, off = [], 0
    for s in shapes:
        n = math.prod(s)
        out.append(flat[off:off + n].reshape(s))
        off += n
    return out


def _cols_to_slots(full):
    r, c = full.shape
    return full.reshape(r, N_SLOTS, c // N_SLOTS).transpose(1, 0, 2).reshape(N_SLOTS * r, c // N_SLOTS)


def _slots_to_cols(slabs):
    r4, c = slabs.shape
    return slabs.reshape(N_SLOTS, r4 // N_SLOTS, c).transpose(1, 0, 2).reshape(r4 // N_SLOTS, N_SLOTS * c)


def _step(cfg, x, loss_target, ws, ms, vs):
    d, f = cfg.d, cfg.ffn
    me = 2 * lax.axis_index("x") + lax.axis_index("y")
    kinds = ("grad", "delta", "new_m", "new_v")
    def gather_plan(names):
        groups = []
        for t, name in enumerate(names):
            if name.endswith("_ffn_w_up"):
                groups[-1].append((t, N_SLOTS * d))
            else:
                groups.append([(t, 0)])
        heads = [names[members[0][0]] for members in groups]
        return _GatherPlan([ws[name].astype(BF16) for name in names], groups), heads
    plan, heads = gather_plan(FIRST)
    got = dict(zip(heads, _run_plan(plan, "gather_first")))
    w = {"l0_w_in": got["l0_w_in"], "l0_s5_w_glu": got["l0_s5_w_glu"],
         "l0_mla_w_uq_p": _pad_heads(_slots_to_cols(got["l0_mla_w_uq"]), cfg.heads, MLA_NOPE + MLA_ROPE),
         "l0_mla_w_ukv": _slots_to_cols(got["l0_mla_w_ukv"])}
    late_plan, late_heads = gather_plan(REST)

    def late_weights(arrived):
        got = dict(zip(late_heads, arrived))
        lw = {name: got[name] for name in ("l0_w_out", "l0_ffn_w_down", "l1_w_out", "l1_ffn_w_down")}
        for l in ("l0", "l1"):
            lw[l + "_w_gu"] = got[l + "_ffn_w_gate"].reshape(2 * N_SLOTS, d, f // N_SLOTS)
        w_in_t = got["l1_w_in"].reshape(N_SLOTS, d, cfg.l1_in // N_SLOTS).transpose(0, 2, 1).reshape(cfg.l1_in, d)
        lw["l1_w_in_t"] = jnp.pad(w_in_t, ((0, cfg.l1_inp - cfg.l1_in), (0, 0)))
        return lw
    tiny_shapes = [ws[name].shape for name in TINY]
    trows = _flat_rows(sum(math.prod(s) for s in tiny_shapes), SUBLANES)
    tiny = _gather_all(_to_flat([ws[name] for name in TINY], trows))[0::2]
    for k, name in enumerate(TINY):
        blocks = jnp.stack([_split_flat(tiny[s], tiny_shapes)[k] for s in range(N_SLOTS)])
        w[name] = blocks.transpose(1, 0, 2).reshape(blocks.shape[1], -1)
    for name in REPLICATED:
        w[name] = ws[name]
    rps = {name: ws[name].shape[1 if name == "l1_w_in" else 0] for name in BIG}

    def triples(grads, names):
        out = []
        for name in names:
            if name.endswith(("_ffn_w_gate", "_ffn_w_up")):
                g = grads[name[:3] + "w_gu"].reshape(2 * N_SLOTS * d, f // N_SLOTS)
                out.append((g, N_SLOTS * d if name.endswith("_up") else 0, rps[name]))
            elif name in ("l0_mla_w_uq", "l0_mla_w_ukv"):
                out.append((_cols_to_slots(grads[name]).astype(BF16), 0, rps[name]))
            else:
                out.append((grads[name], 0, rps[name]))
        return out
    held = {}

    def early_grads(grads):
        held["rest"] = triples(grads, REST)
        return _ExchangePlan(held["rest"])
    loss_lanes, grad_x, grads, others_rest = _local_step(x[0], loss_target[0], w, cfg, (late_plan, late_weights),
                                                         early_grads)
    first = triples(grads, FIRST)
    others = list(_run_plan(_ExchangePlan(first), "exchange_first")) + list(others_rest)
    me1 = me.reshape(1).astype(jnp.int32)
    parts = [_sum_slot(a, row0, r, o, me1, "sum_" + name)
             for (a, row0, r), o, name in zip(first + held["rest"], others, BIG)]
    sibs = _exchange_sibling(parts)
    res = {}
    for name, p, q in zip(BIG, parts, sibs):
        if name == "l1_w_in":
            p, q = p.T, q.T
        for kind, arr in zip(kinds, _adamw([p, q], ws[name], ms[name], vs[name], "adamw_" + name)):
            res[kind + "_" + name] = arr
    rep_shapes = [(1, d)] + [ws[name].shape for name in REPLICATED]
    all_shapes = rep_shapes + [grads[name].shape for name in TINY]
    srows = _flat_rows(sum(math.prod(s) for s in all_shapes), SUBLANES)
    small = _to_flat([loss_lanes] + [grads[name] for name in REPLICATED + TINY], srows)
    total = _sum_parts([_gather_all(small)], "sum_small", tm=srows)
    zero = jnp.zeros((1, d), F32)
    flat = lambda dct: _to_flat([zero] + [dct[name] for name in REPLICATED], srows)
    outs = _adamw([total], flat(ws), flat(ms), flat(vs), "adamw_replicated", tm=srows)
    for kind, arr in zip(kinds, outs):
        vals = _split_flat(arr, rep_shapes)
        for name, val in zip(REPLICATED, vals[1:]):
            res[kind + "_" + name] = val
    for name, g in zip(TINY, _split_flat(total, all_shapes)[len(rep_shapes):]):
        cols = ws[name].shape[1]
        mine = lax.dynamic_slice_in_dim(g, me * cols, cols, axis=1)
        for kind, arr in zip(kinds, _adamw([mine], ws[name], ms[name], vs[name], "adamw_" + name)):
            res[kind + "_" + name] = arr
    loss = _small_call(lambda t: jnp.sum(t, axis=1, keepdims=True), [_split_flat(total, rep_shapes)[0]], [(1, 1)],
                       "loss_sum")[0].reshape(())
    ordered = [res[kind + "_" + name] for kind in ("grad", "delta", "new_m", "new_v") for name in WEIGHTS]
    return (loss, grad_x[None]) + tuple(ordered)


def kernel(x, meta_tokens, l0_w_in, l0_s5_log_dt, l0_s5_a_re, l0_s5_a_im, l0_s5_b_re, l0_s5_b_im, l0_s5_c_re,
           l0_s5_c_im, l0_s5_d, l0_s5_w_glu, l0_mla_q_norm, l0_mla_w_uq, l0_mla_kv_norm, l0_mla_w_ukv, l0_w_out,
           l0_ln1_g, l0_ln1_b, l0_ffn_w_gate, l0_ffn_w_up, l0_ffn_w_down, l0_ln2_g, l0_ln2_b, l1_w_in, l1_conv_w,
           l1_conv_b, l1_dt_bias, l1_a_log, l1_d, l1_norm_g, l1_w_out, l1_ln1_g, l1_ln1_b, l1_ffn_w_gate,
           l1_ffn_w_up, l1_ffn_w_down, l1_ln2_g, l1_ln2_b, loss_target, m_meta_tokens, m_l0_w_in, m_l0_s5_log_dt,
           m_l0_s5_a_re, m_l0_s5_a_im, m_l0_s5_b_re, m_l0_s5_b_im, m_l0_s5_c_re, m_l0_s5_c_im, m_l0_s5_d,
           m_l0_s5_w_glu, m_l0_mla_q_norm, m_l0_mla_w_uq, m_l0_mla_kv_norm, m_l0_mla_w_ukv, m_l0_w_out, m_l0_ln1_g,
           m_l0_ln1_b, m_l0_ffn_w_gate, m_l0_ffn_w_up, m_l0_ffn_w_down, m_l0_ln2_g, m_l0_ln2_b, m_l1_w_in,
           m_l1_conv_w, m_l1_conv_b, m_l1_dt_bias, m_l1_a_log, m_l1_d, m_l1_norm_g, m_l1_w_out, m_l1_ln1_g,
           m_l1_ln1_b, m_l1_ffn_w_gate, m_l1_ffn_w_up, m_l1_ffn_w_down, m_l1_ln2_g, m_l1_ln2_b, v_meta_tokens,
           v_l0_w_in, v_l0_s5_log_dt, v_l0_s5_a_re, v_l0_s5_a_im, v_l0_s5_b_re, v_l0_s5_b_im, v_l0_s5_c_re,
           v_l0_s5_c_im, v_l0_s5_d, v_l0_s5_w_glu, v_l0_mla_q_norm, v_l0_mla_w_uq, v_l0_mla_kv_norm,
           v_l0_mla_w_ukv, v_l0_w_out, v_l0_ln1_g, v_l0_ln1_b, v_l0_ffn_w_gate, v_l0_ffn_w_up, v_l0_ffn_w_down,
           v_l0_ln2_g, v_l0_ln2_b, v_l1_w_in, v_l1_conv_w, v_l1_conv_b, v_l1_dt_bias, v_l1_a_log, v_l1_d,
           v_l1_norm_g, v_l1_w_out, v_l1_ln1_g, v_l1_ln1_b, v_l1_ffn_w_gate, v_l1_ffn_w_up, v_l1_ffn_w_down,
           v_l1_ln2_g, v_l1_ln2_b):
    given = dict(locals())
    ws = {name: given[name] for name in WEIGHTS}
    ms = {name: given["m_" + name] for name in WEIGHTS}
    vs = {name: given["v_" + name] for name in WEIGHTS}
    return _step(FULL, x, loss_target, ws, ms, vs)
```

```python
import functools
import math

import numpy as np
import jax
import jax.numpy as jnp
from jax import lax
from jax.experimental import pallas as pl
from jax.experimental.pallas import tpu as pltpu

F32 = jnp.float32
BF16 = jnp.bfloat16
HI = lax.Precision.HIGHEST
MESH = pl.DeviceIdType.MESH

LANES = 128
SUBLANES = 8
VMEM_LIMIT_BYTES = 56 * 1024 * 1024

N_META = 16
CHUNK = 64
DEPTH = 2
DN_ALPHA = (2 * DEPTH) ** 0.25
LN_EPS = 1e-5
RMS_EPS = 1e-6
ROPE_BASE = 10000.0
S5_GROUP = 16
S5_STATE = 64
S5_GPB = 8
MLA_NOPE = 128
MLA_ROPE = 64
MLA_V = 128
MLA_QW = 256
ATT_SHIFT = 48
SSD_HEAD_DIM = 64
SSD_GROUPS = 8
SSD_STATE = 128
SSD_CONV = 4
SSD_BLOCK = 128
ADAM_LR = 0.001
ADAM_B1 = 0.9
ADAM_B2 = 0.999
ADAM_EPS = 1e-08
ADAM_WD = 0.01
ADAM_STEP = 10
NEG = -1e30


class _Cfg:
    def __init__(self, d_model, seq, row_tile, att_tile, scan_tiles, small_row_tile):
        d = d_model
        self.tr = small_row_tile
        self.d = d
        self.seq = seq
        self.n = seq + N_META
        lp = -(-(self.n + ATT_SHIFT) // row_tile) * row_tile
        self.lp = lp
        self.tm = row_tile
        self.tq = att_tile
        self.scan_tiles = scan_tiles
        self.s5_w = d // 2
        self.s5_g = self.s5_w // S5_GROUP
        self.s5_nb = self.s5_g // S5_GPB
        self.s5_c = self.s5_g * S5_STATE
        self.heads = d // 256
        self.q_rank = d // 4
        self.kv_rank = d // 8
        self.l0_in = self.s5_w + self.q_rank + self.kv_rank + MLA_ROPE
        self.l0_mix = self.s5_w + self.heads * MLA_V
        self.ssd_inner = 2 * d
        self.ssd_heads = self.ssd_inner // SSD_HEAD_DIM
        self.hpg = self.ssd_heads // SSD_GROUPS
        self.gw = self.hpg * SSD_HEAD_DIM
        self.gn = SSD_GROUPS * SSD_STATE
        self.conv_dim = self.ssd_inner + 2 * self.gn
        self.l1_in = self.ssd_inner + self.conv_dim + self.ssd_heads
        self.l1_inp = -(-self.l1_in // LANES) * LANES
        self.ffn = -(-(8 * d) // (3 * 256)) * 256
        assert lp % att_tile == 0 and lp % SSD_BLOCK == 0 and lp % (8 * scan_tiles) == 0


FULL = _Cfg(2048, 8192, 640, 640, 4, 160)


def _cparams(n_grid):
    return pltpu.CompilerParams(dimension_semantics=("arbitrary",) * n_grid,
                                vmem_limit_bytes=VMEM_LIMIT_BYTES)


def _div_tile(n, target, unit=LANES):
    if n <= target:
        return n
    best = None
    for t in range(unit, target + 1, unit):
        if n % t == 0:
            best = t
    return n if best is None else best


ANY_SPEC = pl.BlockSpec(memory_space=pl.ANY)


def _tile_call(fn, grid, ins, in_specs, outs, out_specs, name, acc=(), acc_axis=0, fill=None):
    n_in = len(ins)
    n_out = len(outs)
    acc = tuple(acc)
    aliases = {}
    if fill is not None:
        aliases = {n_in: fill[1]}
        ins = list(ins) + [fill[0]]
        in_specs = list(in_specs) + [ANY_SPEC]

    def body(*refs):
        refs = refs[:n_in] + refs[len(ins):]
        vals = fn(*[r[...] for r in refs[:n_in]])
        if not isinstance(vals, (tuple, list)):
            vals = (vals,)
        for k in range(n_out):
            r = refs[n_in + k]
            v = vals[k].astype(r.dtype)
            if k in acc:
                first = pl.program_id(acc_axis) == 0

                @pl.when(first)
                def _(r=r, v=v):
                    r[...] = v

                @pl.when(jnp.logical_not(first))
                def _(r=r, v=v):
                    r[...] += v
            else:
                r[...] = v

    res = pl.pallas_call(
        body, out_shape=[jax.ShapeDtypeStruct(s, d) for s, d in outs], grid=grid,
        in_specs=in_specs, out_specs=out_specs, name=name, compiler_params=_cparams(len(grid)),
        input_output_aliases=aliases,
    )(*ins)
    return res


def _rows(tm, c):
    return pl.BlockSpec((tm, c), lambda i: (i, 0))


def _whole(shape):
    nd = len(shape)
    return pl.BlockSpec(shape, lambda *a: (0,) * nd)


def _rowwise(fn, rows, params, outs, accs, tm, name):
    lp = rows[0].shape[0]
    n_row_out = len(outs)
    res = _tile_call(
        fn, (lp // tm,), list(rows) + list(params),
        [_rows(tm, r.shape[1]) for r in rows] + [_whole(p.shape) for p in params],
        [((lp, c), dt) for c, dt in outs] + [(s, F32) for s in accs],
        [_rows(tm, c) for c, _ in outs] + [_whole(s) for s in accs],
        name, acc=range(n_row_out, n_row_out + len(accs)))
    return res


_DIMS = {"nn": (((1,), (0,)), ((), ())), "nt": (((1,), (1,)), ((), ())), "tn": (((0,), (0,)), ((), ()))}


def _dot(a, b, mode="nn", precision=None):
    return lax.dot_general(a, b, _DIMS[mode], preferred_element_type=F32, precision=precision)


def _bdot(a, b, mode="nn"):
    return _dot(a.astype(BF16), b.astype(BF16), mode)


def _mm(a, b, mode, name, out_dtype=F32, out_slabs=None, b_col0=0, b_cols=None, tm_t=640, tn_t=1536, tk_t=2048):
    slab_b = b.ndim == 3
    if mode == "nn":
        m, k = a.shape
        k2, n, unit_n = (b.shape[1], b.shape[0] * b.shape[2], b.shape[2]) if slab_b else (b.shape[0], b.shape[1], b.shape[1])
        unit_k = k
    elif mode == "nt":
        m, k = a.shape
        n, k2, unit_k = (b.shape[1], b.shape[0] * b.shape[2], b.shape[2]) if slab_b else (b.shape[0], b.shape[1], b.shape[1])
        unit_n = n
    else:
        (k, m), k2 = a.shape, b.shape[0]
        n = b.shape[1] if b_cols is None else b_cols
        unit_n, unit_k = n, k
        tm_t = max(tm_t, 1024)
        tk_t = 1664 if a.dtype == BF16 and b.dtype == BF16 else 1024
    if out_slabs:
        unit_n = n // out_slabs
    assert k == k2, (name, a.shape, b.shape)
    tm, tn, tk = _div_tile(m, tm_t), _div_tile(unit_n, tn_t), _div_tile(unit_k, tk_t)
    nk = k // tk
    nps, kps = unit_n // tn, unit_k // tk
    c0 = b_col0 // tn
    assert b_col0 % tn == 0
    a_spec = {"nn": pl.BlockSpec((tm, tk), lambda i, j, kk: (i, kk)),
              "nt": pl.BlockSpec((tm, tk), lambda i, j, kk: (i, kk)),
              "tn": pl.BlockSpec((tk, tm), lambda i, j, kk: (kk, i))}[mode]
    if slab_b:
        b_spec = {"nn": pl.BlockSpec((None, tk, tn), lambda i, j, kk: (j // nps, kk, j % nps)),
                  "nt": pl.BlockSpec((None, tn, tk), lambda i, j, kk: (kk // kps, j, kk % kps))}[mode]
    else:
        b_spec = {"nn": pl.BlockSpec((tk, tn), lambda i, j, kk: (kk, j)),
                  "nt": pl.BlockSpec((tn, tk), lambda i, j, kk: (j, kk)),
                  "tn": pl.BlockSpec((tk, tn), lambda i, j, kk: (kk, j + c0))}[mode]
    if out_slabs:
        out_shape = jax.ShapeDtypeStruct((out_slabs, m, unit_n), out_dtype)
        out_spec = pl.BlockSpec((None, tm, tn), lambda i, j, kk: (j // nps, i, j % nps))
    else:
        out_shape = jax.ShapeDtypeStruct((m, n), out_dtype)
        out_spec = pl.BlockSpec((tm, tn), lambda i, j, kk: (i, j))

    def body(a_ref, b_ref, o_ref, acc_ref):
        part = _bdot(a_ref[...], b_ref[...], mode)
        if nk == 1:
            o_ref[...] = part.astype(o_ref.dtype)
        else:
            kk = pl.program_id(2)

            @pl.when(kk == 0)
            def _():
                acc_ref[...] = part

            @pl.when(kk > 0)
            def _():
                acc_ref[...] += part

            @pl.when(kk == nk - 1)
            def _():
                o_ref[...] = acc_ref[...].astype(o_ref.dtype)

    return pl.pallas_call(
        body, out_shape=out_shape, grid=(m // tm, n // tn, nk),
        in_specs=[a_spec, b_spec], out_specs=out_spec,
        scratch_shapes=[pltpu.VMEM((tm, tn) if nk > 1 else (SUBLANES, LANES), F32)],
        name=name, compiler_params=_cparams(3))(a, b)


def _layer_norm(r, g, b):
    mu = jnp.mean(r, axis=-1, keepdims=True)
    xc = r - mu
    var = jnp.mean(xc * xc, axis=-1, keepdims=True)
    return xc * lax.rsqrt(var + LN_EPS) * g + b


def _rms(x, g):
    return x * lax.rsqrt(jnp.mean(x * x, axis=-1, keepdims=True) + RMS_EPS) * g


def _sigmoid(x):
    return 1.0 / (1.0 + jnp.exp(-x))


def _silu(x):
    return x * _sigmoid(x)


def _gelu(x):
    return 0.5 * x * (1.0 + jnp.tanh(0.7978845608028654 * (x + 0.044715 * x * x * x)))


def _softplus(x):
    return jnp.maximum(x, 0.0) + jnp.log(1.0 + jnp.exp(-jnp.abs(x)))


def _ln_fwd(h, mo, g, b, cfg, name):
    def fn(h, mo, g, b):
        y = _layer_norm(DN_ALPHA * h + mo, g, b)
        return y, y
    return _rowwise(fn, [h, mo], [g, b], [(cfg.d, F32), (cfg.d, BF16)], [], cfg.tr, name)


def _ln_bwd(h, mo, g, b, douts, scales, cfg, name):
    def fn(h, mo, *rest):
        ds, (g, b) = rest[:-2], rest[-2:]
        dy = ds[0] * scales[0]
        for t, s in zip(ds[1:], scales[1:]):
            dy = dy + t * s
        _, vjp = jax.vjp(_layer_norm, DN_ALPHA * h + mo, g, b)
        dr, dg, db = vjp(dy)
        return dr, dr, dg, db
    d = cfg.d
    return _rowwise(fn, [h, mo] + list(douts), [g, b], [(d, F32), (d, BF16)], [(1, d), (1, d)], cfg.tr, name)


def _ffn_act(gu, cfg, name):
    f = cfg.ffn
    cb = _div_tile(f, 1536)
    nf = f // cb
    lp = gu.shape[0]
    tm = cfg.tr

    def fn(gate, up):
        return _silu(gate.astype(F32)) * up.astype(F32)
    return _tile_call(fn, (lp // tm, nf), [gu, gu],
                      [pl.BlockSpec((tm, cb), lambda i, j: (i, j)),
                       pl.BlockSpec((tm, cb), lambda i, j: (i, j + nf))],
                      [((lp, f), BF16)], [pl.BlockSpec((tm, cb), lambda i, j: (i, j))], name)[0]


def _ffn_act_bwd(gu, dact, cfg, name):
    f = cfg.ffn
    cb = _div_tile(f, 1536)
    nf = f // cb
    lp = gu.shape[0]
    tm = cfg.tr

    def body(gate_ref, up_ref, da_ref, out_ref):
        gate, da = gate_ref[...].astype(F32), da_ref[...].astype(F32)
        sg = _sigmoid(gate)
        j = pl.program_id(1)

        @pl.when(j < nf)
        def _():
            out_ref[...] = (da * up_ref[...].astype(F32) * sg * (1.0 + gate * (1.0 - sg))).astype(BF16)

        @pl.when(j >= nf)
        def _():
            out_ref[...] = (da * gate * sg).astype(BF16)

    return pl.pallas_call(
        body, out_shape=jax.ShapeDtypeStruct((lp, 2 * f), BF16), grid=(lp // tm, 2 * nf),
        in_specs=[pl.BlockSpec((tm, cb), lambda i, j: (i, j % nf)),
                  pl.BlockSpec((tm, cb), lambda i, j: (i, j % nf + nf)),
                  pl.BlockSpec((tm, cb), lambda i, j: (i, j % nf))],
        out_specs=pl.BlockSpec((tm, cb), lambda i, j: (i, j)), name=name, compiler_params=_cparams(2))(gu, gu, dact)


def _ffn_fwd(h, w_gu, w_down, cfg, tag):
    gu = _mm(h, w_gu, "nn", tag + "_gu", out_dtype=BF16)
    act = _ffn_act(gu, cfg, tag + "_act")
    fo = _mm(act, w_down, "nn", tag + "_down")
    return fo, (gu, act)


def _ffn_bwd(h, saved, dfo, w_gu, w_down, cfg, tag):
    gu, act = saved
    dact = _mm(dfo, w_down, "nt", tag + "_dact", out_dtype=BF16)
    d_wdown = _mm(act, dfo, "tn", tag + "_dwdown", out_dtype=BF16)
    dgu = _ffn_act_bwd(gu, dact, cfg, tag + "_dgu")
    dh = _mm(dgu, w_gu, "nt", tag + "_dh", tn_t=2048)
    d_wgu = _mm(h, dgu, "tn", tag + "_dwgu", out_dtype=BF16, out_slabs=w_gu.shape[0])
    return dh, d_wgu, d_wdown


def _small_call(fn, ins, outs, name):
    return _tile_call(fn, (1,), ins, [_whole(x.shape) for x in ins], [(s, F32) for s in outs],
                      [_whole(s) for s in outs], name)


def _perm(x):
    lp, c = x.shape
    return x.reshape(SUBLANES, lp // SUBLANES, c).transpose(1, 0, 2).reshape(lp, c)


def _unperm(x):
    lp, c = x.shape
    return x.reshape(lp // SUBLANES, SUBLANES, c).transpose(1, 0, 2).reshape(lp, c)


def _s5_disc(log_dt, a_re, a_im):
    dt = jnp.exp(log_dt)
    mag = jnp.exp(dt * a_re)
    ab_re = mag * jnp.cos(dt * a_im)
    ab_im = mag * jnp.sin(dt * a_im)
    den = a_re * a_re + a_im * a_im
    nr = ab_re - 1.0
    f_re = (nr * a_re + ab_im * a_im) / den
    f_im = (ab_im * a_re - nr * a_im) / den
    return ab_re, ab_im, f_re, f_im


def _s5_bbar(f_re, f_im, b_re, b_im):
    return f_re * b_re - f_im * b_im, f_re * b_im + f_im * b_re


def _bd_from(w, cfg):
    g, p, j = w.shape
    w4 = w.reshape(cfg.s5_nb, S5_GPB, p, j)
    eye = jnp.eye(S5_GPB, dtype=w.dtype)
    return jnp.einsum("bgpj,gh->bgjhp", w4, eye).reshape(cfg.s5_nb, S5_GPB * j, S5_GPB * p)


def _bd_to(blocks, cfg, p, j):
    b5 = blocks.reshape(cfg.s5_nb, S5_GPB, j, S5_GPB, p)
    eye = jnp.eye(S5_GPB, dtype=blocks.dtype)
    return jnp.einsum("bgjhp,gh->bgpj", b5, eye).reshape(cfg.s5_g, p, j)


def _bd_split(x, w1, w2, cfg, name):
    nb, ci, co = w1.shape
    lp, tm = x.shape[0], cfg.tm

    def fn(x, w1, w2):
        xb = x.astype(BF16)
        return _dot(xb, w1[0].astype(BF16)), _dot(xb, w2[0].astype(BF16))
    wspec = pl.BlockSpec((1, ci, co), lambda i, j: (j, 0, 0))
    ospec = pl.BlockSpec((tm, co), lambda i, j: (i, j))
    return _tile_call(fn, (lp // tm, nb), [x, w1, w2],
                      [pl.BlockSpec((tm, ci), lambda i, j: (i, j)), wspec, wspec],
                      [((lp, nb * co), F32)] * 2, [ospec, ospec], name)


def _bd_join(x1, x2, w1, w2, extra, scale, cfg, name):
    nb, ci, co = w1.shape
    lp, tm = x1.shape[0], cfg.tm

    def fn(x1, x2, w1, w2, e, s):
        return _bdot(x1, w1[0]) + _bdot(x2, w2[0]) + e * s
    xspec = pl.BlockSpec((tm, ci), lambda i, j: (i, j))
    wspec = pl.BlockSpec((1, ci, co), lambda i, j: (j, 0, 0))
    ospec = pl.BlockSpec((tm, co), lambda i, j: (i, j))
    return _tile_call(fn, (lp // tm, nb), [x1, x2, w1, w2, extra, scale],
                      [xspec, xspec, wspec, wspec, ospec, pl.BlockSpec((1, co), lambda i, j: (0, j))],
                      [((lp, nb * co), F32)], [ospec], name)[0]


def _bd_tn(a, b, nb, cfg, name):
    lp, tk = a.shape[0], cfg.tm
    ca, cb = a.shape[1] // nb, b.shape[1] // nb

    def fn(a, b):
        return _bdot(a, b, "tn")[None]
    return _tile_call(fn, (nb, lp // tk), [a, b],
                      [pl.BlockSpec((tk, ca), lambda j, k: (k, j)), pl.BlockSpec((tk, cb), lambda j, k: (k, j))],
                      [((nb, ca, cb), F32)], [pl.BlockSpec((1, ca, cb), lambda j, k: (j, 0, 0))],
                      name, acc=(0,), acc_axis=1)[0]


def _s5_scan(bu_re, bu_im, ab_re, ab_im, cfg, reverse, tag):
    lp, c = bu_re.shape
    nt = cfg.scan_tiles
    rows = lp // nt
    steps = rows // SUBLANES
    cb = _div_tile(c, 512)
    tmap = (lambda j, t: (nt - 1 - t, j)) if reverse else (lambda j, t: (t, j))
    row_spec = pl.BlockSpec((rows, cb), tmap)
    par_spec = pl.BlockSpec((1, cb), lambda j, t: (0, j))
    st_spec = pl.BlockSpec((SUBLANES, cb), lambda j, t: (0, j))
    grid = (c // cb, nt)
    full = jax.ShapeDtypeStruct((lp, c), F32)
    small = jax.ShapeDtypeStruct((SUBLANES, c), F32)

    def offset(k):
        kk = steps - 1 - k if reverse else k
        return pl.multiple_of(kk * SUBLANES, SUBLANES)

    def local_body(bre, bim, ar_ref, ai_ref, sre, sim, fre, fim, pre, pim, st):
        t = pl.program_id(1)

        @pl.when(t == 0)
        def _():
            zero = jnp.zeros((SUBLANES, cb), F32)
            st[0] = zero
            st[1] = zero
            st[2] = zero + 1.0
            st[3] = zero
        ar = jnp.broadcast_to(ar_ref[...], (SUBLANES, cb))
        ai = jnp.broadcast_to(ai_ref[...], (SUBLANES, cb))

        def step(k, carry):
            s_r, s_i, p_r, p_i = carry
            off = offset(k)
            n_r = ar * s_r - ai * s_i + bre[pl.ds(off, SUBLANES), :]
            n_i = ar * s_i + ai * s_r + bim[pl.ds(off, SUBLANES), :]
            sre[pl.ds(off, SUBLANES), :] = n_r
            sim[pl.ds(off, SUBLANES), :] = n_i
            return n_r, n_i, ar * p_r - ai * p_i, ar * p_i + ai * p_r
        s_r, s_i, p_r, p_i = lax.fori_loop(0, steps, step, (st[0], st[1], st[2], st[3]))
        st[0] = s_r
        st[1] = s_i
        st[2] = p_r
        st[3] = p_i

        @pl.when(t == nt - 1)
        def _():
            fre[...] = s_r
            fim[...] = s_i
            pre[...] = p_r
            pim[...] = p_i

    loc_re, loc_im, f_re, f_im, pn_re, pn_im = pl.pallas_call(
        local_body, out_shape=[full, full, small, small, small, small], grid=grid,
        in_specs=[row_spec, row_spec, par_spec, par_spec],
        out_specs=[row_spec, row_spec, st_spec, st_spec, st_spec, st_spec],
        scratch_shapes=[pltpu.VMEM((4, SUBLANES, cb), F32)],
        name=tag + "_local", compiler_params=_cparams(2))(bu_re, bu_im, ab_re, ab_im)

    def fix_body(lre, lim, fre, fim, pre, pim, ar_ref, ai_ref, sre, sim, st):
        t = pl.program_id(1)
        ar = jnp.broadcast_to(ar_ref[...], (SUBLANES, cb))
        ai = jnp.broadcast_to(ai_ref[...], (SUBLANES, cb))

        @pl.when(t == 0)
        def _():
            f_r, f_i = fre[...], fim[...]
            n_r, n_i = pre[0:1, :], pim[0:1, :]
            row = lax.broadcasted_iota(jnp.int32, (SUBLANES, cb), 0)
            c_r = jnp.zeros((1, cb), F32)
            c_i = jnp.zeros((1, cb), F32)
            car_r = jnp.zeros((SUBLANES, cb), F32)
            car_i = jnp.zeros((SUBLANES, cb), F32)
            order = range(SUBLANES - 2, -1, -1) if reverse else range(1, SUBLANES)
            for i in order:
                src = i + 1 if reverse else i - 1
                c_r, c_i = (n_r * c_r - n_i * c_i + f_r[src:src + 1, :],
                            n_r * c_i + n_i * c_r + f_i[src:src + 1, :])
                car_r = jnp.where(row == i, c_r, car_r)
                car_i = jnp.where(row == i, c_i, car_i)
            st[0] = car_r
            st[1] = car_i
            st[2] = ar
            st[3] = ai
        car_r = st[0]
        car_i = st[1]

        def step(k, carry):
            p_r, p_i = carry
            off = offset(k)
            sre[pl.ds(off, SUBLANES), :] = lre[pl.ds(off, SUBLANES), :] + p_r * car_r - p_i * car_i
            sim[pl.ds(off, SUBLANES), :] = lim[pl.ds(off, SUBLANES), :] + p_r * car_i + p_i * car_r
            return ar * p_r - ai * p_i, ar * p_i + ai * p_r
        p_r, p_i = lax.fori_loop(0, steps, step, (st[2], st[3]))
        st[2] = p_r
        st[3] = p_i

    return pl.pallas_call(
        fix_body, out_shape=[full, full], grid=grid,
        in_specs=[row_spec, row_spec, st_spec, st_spec, st_spec, st_spec, par_spec, par_spec],
        out_specs=[row_spec, row_spec], scratch_shapes=[pltpu.VMEM((4, SUBLANES, cb), F32)],
        name=tag + "_fix", compiler_params=_cparams(2))(loc_re, loc_im, f_re, f_im, pn_re, pn_im, ab_re, ab_im)


def _swap(x):
    return jnp.swapaxes(x, -1, -2)


def _s5_prep(w, cfg):
    g, p, j = cfg.s5_g, S5_STATE, S5_GROUP
    gp = g * p
    log_dt = w["l0_s5_log_dt"].reshape(g, 1)
    ab_re, ab_im, f_re, f_im = _small_call(_s5_disc, [log_dt, w["l0_s5_a_re"], w["l0_s5_a_im"]],
                                           [(g, p)] * 4, "s5_disc")
    b_re2 = w["l0_s5_b_re"].transpose(2, 0, 1).reshape(j, gp)
    b_im2 = w["l0_s5_b_im"].transpose(2, 0, 1).reshape(j, gp)
    f_re1, f_im1 = f_re.reshape(1, gp), f_im.reshape(1, gp)
    bb_re2, bb_im2 = _small_call(_s5_bbar, [f_re1, f_im1, b_re2, b_im2], [(j, gp)] * 2, "s5_bbar")
    bb_re = _bd_from(bb_re2.reshape(j, g, p).transpose(1, 2, 0), cfg).astype(BF16)
    bb_im = _bd_from(bb_im2.reshape(j, g, p).transpose(1, 2, 0), cfg).astype(BF16)
    c_re_t = _bd_from(w["l0_s5_c_re"].transpose(0, 2, 1), cfg).astype(BF16)
    c_imn_t = _bd_from(-w["l0_s5_c_im"].transpose(0, 2, 1), cfg).astype(BF16)
    return dict(log_dt=log_dt, f_re1=f_re1, f_im1=f_im1, b_re2=b_re2, b_im2=b_im2,
                ab_re=ab_re.reshape(1, gp), ab_im=ab_im.reshape(1, gp),
                bb_re=bb_re, bb_im=bb_im, bb_re_t=_swap(bb_re), bb_im_t=_swap(bb_im),
                c_re=_swap(c_re_t), c_imn=_swap(c_imn_t), c_re_t=c_re_t, c_imn_t=c_imn_t,
                d=w["l0_s5_d"].reshape(1, cfg.s5_w))


def _s5_fwd(u, prm, w_glu, cfg):
    tm = cfg.tr
    up = _perm(u)
    bu_re, bu_im = _bd_split(up, prm["bb_re"], prm["bb_im"], cfg, "s5_bu")
    s_re, s_im = _s5_scan(bu_re, bu_im, prm["ab_re"], prm["ab_im"], cfg, False, "s5_scan")
    y = _bd_join(s_re, s_im, prm["c_re"], prm["c_imn"], up, prm["d"], cfg, "s5_y")
    g = _rowwise(_gelu, [y], [], [(cfg.s5_w, F32)], [], tm, "s5_gelu")[0]
    z = _mm(g, w_glu, "nn", "s5_glu_mm")
    a_out = _rowwise(lambda g, z: g * _sigmoid(z), [g, z], [], [(cfg.s5_w, F32)], [], tm, "s5_glu")[0]
    return _unperm(a_out), (up, s_re, s_im, y, g, z)


def _s5_bwd(d_a_out, saved, prm, w, w_glu, cfg):
    up, s_re, s_im, y, g, z = saved
    tm, sw, nb = cfg.tr, cfg.s5_w, cfg.s5_nb
    gs, p, j = cfg.s5_g, S5_STATE, S5_GROUP
    gp = gs * p
    dap = _perm(d_a_out)

    def glu_bwd(da, g, z):
        sg = _sigmoid(z)
        return da * sg, da * g * sg * (1.0 - sg)
    dg1, dz = _rowwise(glu_bwd, [dap, g, z], [], [(sw, F32)] * 2, [], tm, "s5_glu_bwd")
    d_wglu = _mm(g, dz, "tn", "s5_dwglu", out_dtype=BF16)
    dg2 = _mm(dz, w_glu, "nt", "s5_dg2")

    def gelu_bwd(dg1, dg2, y, up, d):
        _, vjp = jax.vjp(_gelu, y)
        dy = vjp(dg1 + dg2)[0]
        return dy, dy * d, jnp.sum(dy * up, axis=0, keepdims=True)
    dy, dup_direct, dd = _rowwise(gelu_bwd, [dg1, dg2, y, up], [prm["d"]], [(sw, F32)] * 2, [(1, sw)], tm,
                                  "s5_gelu_bwd")
    ds_re, ds_im = _bd_split(dy, prm["c_re_t"], prm["c_imn_t"], cfg, "s5_ds")
    dc_re_t = _bd_tn(dy, s_re, nb, cfg, "s5_dcre")
    dc_imn_t = _bd_tn(dy, s_im, nb, cfg, "s5_dcim")
    g_re, g_im = _s5_scan(ds_re, ds_im, prm["ab_re"], -prm["ab_im"], cfg, True, "s5_adj")

    cb = _div_tile(gp, 512)
    per = tm // SUBLANES
    spec = pl.BlockSpec((tm, cb), lambda jj, i: (i, jj))
    before = pl.BlockSpec((SUBLANES, cb), lambda jj, i: (jnp.maximum(i * per - 1, 0), jj))
    final = pl.BlockSpec((SUBLANES, cb), lambda jj, i: (cfg.lp // SUBLANES - 1, jj))
    aspec = pl.BlockSpec((1, cb), lambda jj, i: (0, jj))

    def dab(g_r, g_i, s_r, s_i, h_r, h_i, l_r, l_i):
        first = pl.program_id(1) == 0
        row8 = lax.broadcasted_iota(jnp.int32, (SUBLANES, cb), 0)

        def prev(s, h, l):
            wrap = jnp.where(row8 == 0, 0.0, pltpu.roll(l, 1, axis=0))
            return jnp.concatenate([jnp.where(first, wrap, h), s[:tm - SUBLANES]], axis=0)
        p_r, p_i = prev(s_r, h_r, l_r), prev(s_i, h_i, l_i)
        return (jnp.sum(g_r * p_r + g_i * p_i, axis=0, keepdims=True),
                jnp.sum(g_i * p_r - g_r * p_i, axis=0, keepdims=True))
    dab_re, dab_im = _tile_call(dab, (gp // cb, cfg.lp // tm), [g_re, g_im, s_re, s_im, s_re, s_im, s_re, s_im],
                                [spec] * 4 + [before] * 2 + [final] * 2,
                                [((1, gp), F32)] * 2, [aspec] * 2, "s5_dab", acc=(0, 1), acc_axis=1)
    no_scale = jnp.ones((1, sw), F32)
    dup = _bd_join(g_re, g_im, prm["bb_re_t"], prm["bb_im_t"], dup_direct, no_scale, cfg, "s5_dup")
    dbb_re_blk = _bd_tn(up, g_re, nb, cfg, "s5_dbbre")
    dbb_im_blk = _bd_tn(up, g_im, nb, cfg, "s5_dbbim")

    def to2(blk):
        return _bd_to(blk, cfg, p, j).transpose(2, 0, 1).reshape(j, gp)

    def bbar_bwd(f_re, f_im, b_re, b_im, dr, di):
        _, vjp = jax.vjp(_s5_bbar, f_re, f_im, b_re, b_im)
        return vjp((dr, di))
    df_re, df_im, db_re2, db_im2 = _small_call(
        bbar_bwd, [prm["f_re1"], prm["f_im1"], prm["b_re2"], prm["b_im2"], to2(dbb_re_blk), to2(dbb_im_blk)],
        [(1, gp), (1, gp), (j, gp), (j, gp)], "s5_bbar_bwd")

    def disc_bwd(log_dt, a_re, a_im, d1, d2, d3, d4):
        _, vjp = jax.vjp(_s5_disc, log_dt, a_re, a_im)
        return vjp((d1, d2, d3, d4))
    dlog_dt, da_re, da_im = _small_call(
        disc_bwd, [prm["log_dt"], w["l0_s5_a_re"], w["l0_s5_a_im"], dab_re.reshape(gs, p), dab_im.reshape(gs, p),
                   df_re.reshape(gs, p), df_im.reshape(gs, p)], [(gs, 1), (gs, p), (gs, p)], "s5_disc_bwd")
    grads = {
        "l0_s5_log_dt": dlog_dt.reshape(gs), "l0_s5_a_re": da_re, "l0_s5_a_im": da_im,
        "l0_s5_b_re": db_re2.reshape(j, gs, p).transpose(1, 2, 0),
        "l0_s5_b_im": db_im2.reshape(j, gs, p).transpose(1, 2, 0),
        "l0_s5_c_re": _bd_to(dc_re_t, cfg, p, j).transpose(0, 2, 1),
        "l0_s5_c_im": -_bd_to(dc_imn_t, cfg, p, j).transpose(0, 2, 1),
        "l0_s5_d": dd.reshape(sw), "l0_s5_w_glu": d_wglu,
    }
    return _unperm(dup), grads


def _shift(x):
    return jnp.concatenate([jnp.zeros((ATT_SHIFT, x.shape[1]), x.dtype), x[:-ATT_SHIFT]], axis=0)


def _unshift(x):
    return jnp.concatenate([x[ATT_SHIFT:], jnp.zeros((ATT_SHIFT, x.shape[1]), x.dtype)], axis=0)


def _rope_tables(cfg):
    pos = (jnp.arange(cfg.lp) - ATT_SHIFT).astype(F32)
    inv = ROPE_BASE ** (-jnp.arange(0, MLA_ROPE, 2, dtype=F32) / MLA_ROPE)
    ang = pos[:, None] * inv[None, :]
    cos, sin = jnp.cos(ang), jnp.sin(ang)
    z = jnp.zeros((cfg.lp, LANES - MLA_ROPE), F32)
    return jnp.concatenate([cos, cos, z], axis=1), jnp.concatenate([-sin, sin, z], axis=1)


def _swap_halves(x):
    half = MLA_ROPE // 2
    lane = lax.broadcasted_iota(jnp.int32, x.shape, 1)
    left = pltpu.roll(x, LANES - half, axis=1)
    right = pltpu.roll(x, half, axis=1)
    return jnp.where(lane < half, left, jnp.where(lane < 2 * half, right, 0.0))


def _rope(x, cosp, sinp):
    return x * cosp + _swap_halves(x) * sinp


def _rope_t(dy, cosp, sinp):
    return dy * cosp + _swap_halves(dy * sinp)


def _visible(i, j, t):
    row = i * t + lax.broadcasted_iota(jnp.int32, (t, t), 0)
    col = j * t + lax.broadcasted_iota(jnp.int32, (t, t), 1)
    return jnp.logical_and(col // CHUNK <= row // CHUNK, col >= ATT_SHIFT)


class _NoPlan:
    n = n_out = 0
    arrays, out_shapes, scratch = [], [], []


def _side_refs(refs, n_in, n_out, n_scratch, side):
    a = n_in + side.n
    b = a + n_out + side.n_out
    c = b + n_scratch
    own = refs[:n_in] + refs[a:a + n_out] + refs[b:c]
    return own, refs[n_in:a] + refs[a + n_out:b] + refs[c:]


def _needs_mask(i, j):
    return jnp.logical_or(i == j, j == 0)


def _flash_fwd(q, kv, kr, cfg, side=None):
    lp, t, nh = cfg.lp, cfg.tq, cfg.heads
    n = lp // t
    scale = (MLA_NOPE + MLA_ROPE) ** -0.5
    side = side or _NoPlan()

    def body(*refs):
        (q_ref, kv_ref, kr_ref, o_ref, lse_ref, m_s, l_s, acc_s), ex = _side_refs(refs, 3, 2, 3, side)
        hh, i, j = pl.program_id(0), pl.program_id(1), pl.program_id(2)
        if side.n:
            at_tile0 = jnp.logical_and(i == 0, j == 0)
            pl.when(jnp.logical_and(hh == 0, at_tile0))(lambda: side.start(ex))
            pl.when(jnp.logical_and(hh == nh // 2, at_tile0))(lambda: side.relay(ex))

        @pl.when(j == 0)
        def _():
            m_s[...] = jnp.full((t, 1), NEG, F32)
            l_s[...] = jnp.zeros((t, 1), F32)
            acc_s[...] = jnp.zeros((t, MLA_V), F32)

        def tile(masked):
            s = (_dot(q_ref[:, :MLA_NOPE], kv_ref[:, :MLA_NOPE], "nt")
                 + _dot(q_ref[:, MLA_NOPE:], kr_ref[...], "nt")) * scale
            if masked:
                s = jnp.where(_visible(i, j, t), s, NEG)
            m_old = m_s[...]
            m_new = jnp.maximum(m_old, jnp.max(s, axis=1, keepdims=True))
            alpha = jnp.exp(m_old - m_new)
            p = jnp.exp(s - m_new)
            l_s[...] = alpha * l_s[...] + jnp.sum(p, axis=1, keepdims=True)
            acc_s[...] = alpha * acc_s[...] + _dot(p.astype(BF16), kv_ref[:, MLA_NOPE:])
            m_s[...] = m_new
        pl.when(jnp.logical_and(j <= i, _needs_mask(i, j)))(lambda: tile(True))
        pl.when(jnp.logical_and(j < i, j > 0))(lambda: tile(False))

        @pl.when(j == i)
        def _():
            o_ref[...] = acc_s[...] / l_s[...]
            lse_ref[...] = jnp.broadcast_to(m_s[...] + jnp.log(l_s[...]), (t, MLA_V))

        if side.n:
            pl.when(jnp.logical_and(hh == nh - 1, jnp.logical_and(i == n - 1, j == n - 1)))(lambda: side.finish(ex))

    res = pl.pallas_call(
        body, out_shape=[jax.ShapeDtypeStruct((lp, nh * MLA_V), F32)] * 2 + side.out_shapes, grid=(nh, n, n),
        in_specs=[pl.BlockSpec((t, MLA_QW), lambda h, i, j: (i, h)),
                  pl.BlockSpec((t, MLA_QW), lambda h, i, j: (jnp.minimum(i, j), h)),
                  pl.BlockSpec((t, LANES), lambda h, i, j: (jnp.minimum(i, j), 0))] + [HBM_SPEC] * side.n,
        out_specs=[pl.BlockSpec((t, MLA_V), lambda h, i, j: (i, h))] * 2 + [HBM_SPEC] * side.n_out,
        scratch_shapes=[pltpu.VMEM((t, 1), F32), pltpu.VMEM((t, 1), F32), pltpu.VMEM((t, MLA_V), F32)] + side.scratch,
        name="mla_flash_fwd", compiler_params=_cparams(3))(q, kv, kr, *side.arrays)
    return res[0], res[1], res[2:]


def _flash_bwd(q, kv, kr, o, lse, do, cfg, side=None):
    lp, t, nh = cfg.lp, cfg.tq, cfg.heads
    n = lp // t
    scale = (MLA_NOPE + MLA_ROPE) ** -0.5
    side = side or _NoPlan()

    def body(*refs):
        own, ex = _side_refs(refs, 6, 3, 2, side)
        q_ref, kv_ref, kr_ref, o_ref, lse_ref, do_ref, dq_ref, dkv_ref, dkr_ref, dkv_s, dkr_s = own
        hh, j, i = pl.program_id(0), pl.program_id(1), pl.program_id(2)
        if side.n:
            pl.when(jnp.logical_and(hh == 0, jnp.logical_and(i == 0, j == 0)))(lambda: side.start(ex))

        @pl.when(jnp.logical_and(j == 0, i == 0))
        def _():
            dq_ref[...] = jnp.zeros((lp, MLA_QW), F32)

        @pl.when(i == j)
        def _():
            dkv_s[...] = jnp.zeros((t, MLA_QW), F32)
            dkr_s[...] = jnp.zeros((t, LANES), F32)

        def tile(masked):
            qn, qr = q_ref[:, :MLA_NOPE], q_ref[:, MLA_NOPE:]
            kn, v = kv_ref[:, :MLA_NOPE], kv_ref[:, MLA_NOPE:]
            krv = kr_ref[...]
            s = (_dot(qn, kn, "nt") + _dot(qr, krv, "nt")) * scale
            p = jnp.exp(s - lse_ref[:, :1])
            if masked:
                p = jnp.where(_visible(i, j, t), p, 0.0)
            dov = do_ref[...]
            dob = dov.astype(BF16)
            dp = _dot(dob, v, "nt")
            delta = jnp.sum(dov * o_ref[...], axis=1, keepdims=True)
            ds = (p * (dp - delta) * scale).astype(BF16)
            dkv_s[:, MLA_NOPE:] += _dot(p.astype(BF16), dob, "tn")
            dkv_s[:, :MLA_NOPE] += _dot(ds, qn, "tn")
            dkr_s[...] += _dot(ds, qr, "tn")
            off = pl.multiple_of(i * t, t)
            dq_ref[pl.ds(off, t), :MLA_NOPE] += _dot(ds, kn)
            dq_ref[pl.ds(off, t), MLA_NOPE:] += _dot(ds, krv)
        pl.when(jnp.logical_and(i >= j, _needs_mask(i, j)))(lambda: tile(True))
        pl.when(jnp.logical_and(i > j, j > 0))(lambda: tile(False))

        @pl.when(i == n - 1)
        def _():
            dkv_ref[...] = dkv_s[...]
            dkr_ref[0] = dkr_s[...]

        if side.n:
            pl.when(jnp.logical_and(hh == nh - 1, jnp.logical_and(i == n - 1, j == n - 1)))(lambda: side.finish(ex))

    qspec = pl.BlockSpec((t, MLA_QW), lambda h, j, i: (jnp.maximum(i, j), h))
    ospec = pl.BlockSpec((t, MLA_V), lambda h, j, i: (jnp.maximum(i, j), h))
    res = pl.pallas_call(
        body, out_shape=[jax.ShapeDtypeStruct((lp, nh * MLA_QW), F32), jax.ShapeDtypeStruct((lp, nh * MLA_QW), F32),
                         jax.ShapeDtypeStruct((nh, lp, LANES), F32)] + side.out_shapes, grid=(nh, n, n),
        in_specs=[qspec, pl.BlockSpec((t, MLA_QW), lambda h, j, i: (j, h)),
                  pl.BlockSpec((t, LANES), lambda h, j, i: (j, 0)), ospec, ospec, ospec] + [HBM_SPEC] * side.n,
        out_specs=[pl.BlockSpec((lp, MLA_QW), lambda h, j, i: (0, h)),
                   pl.BlockSpec((t, MLA_QW), lambda h, j, i: (j, h)),
                   pl.BlockSpec((1, t, LANES), lambda h, j, i: (h, j, 0))] + [HBM_SPEC] * side.n_out,
        scratch_shapes=[pltpu.VMEM((t, MLA_QW), F32), pltpu.VMEM((t, LANES), F32)] + side.scratch,
        name="mla_flash_bwd", compiler_params=_cparams(3))(q, kv, kr, o, lse, do, *side.arrays)
    return res[0], res[1], res[2], res[3:]


def _pad_heads(w, nh, width):
    r = w.shape[0]
    w3 = w.reshape(r, nh, width)
    return jnp.pad(w3, ((0, 0), (0, 0), (0, MLA_QW - width))).reshape(r, nh * MLA_QW)


def _mla_fwd(q_lat, kv_lat, k_rope_raw, wq, w_uq_p, w_ukv, cfg, side=None):
    tm, nh = cfg.tr, cfg.heads
    ql, kl = _shift(q_lat), _shift(kv_lat)
    kr_raw = jnp.pad(_shift(k_rope_raw), ((0, 0), (0, LANES - MLA_ROPE)))
    cosp, sinp = _rope_tables(cfg)
    qg, kg = wq["l0_mla_q_norm"].reshape(1, -1), wq["l0_mla_kv_norm"].reshape(1, -1)
    qn, kvn = _rowwise(lambda a, b, g1, g2: (_rms(a, g1), _rms(b, g2)), [ql, kl], [qg, kg],
                       [(cfg.q_rank, F32), (cfg.kv_rank, F32)], [], tm, "mla_norm")
    q0 = _mm(qn, w_uq_p, "nn", "mla_q")
    kv = _mm(kvn, w_ukv, "nn", "mla_kv", out_dtype=BF16)

    def rope_fn(q0, kr, cosp, sinp):
        parts = []
        for h in range(nh):
            parts.append(q0[:, h * MLA_QW:h * MLA_QW + MLA_NOPE])
            parts.append(_rope(q0[:, h * MLA_QW + MLA_NOPE:(h + 1) * MLA_QW], cosp, sinp))
        return jnp.concatenate(parts, axis=1), _rope(kr, cosp, sinp)
    q, kr = _rowwise(rope_fn, [q0, kr_raw, cosp, sinp], [], [(nh * MLA_QW, BF16), (LANES, BF16)], [], tm,
                     "mla_rope")
    o, lse, side_out = _flash_fwd(q, kv, kr, cfg, side)
    return _unshift(o), (ql, kl, qn, kvn, q, kv, kr, o, lse, cosp, sinp), side_out


def _mla_bwd(d_b_out, saved, wq, w_uq_p, w_ukv, cfg, side=None):
    ql, kl, qn, kvn, q, kv, kr, o, lse, cosp, sinp = saved
    tm, nh, lp = cfg.tr, cfg.heads, cfg.lp
    dq, dkv, dkr_h, side_out = _flash_bwd(q, kv, kr, o, lse, _shift(d_b_out), cfg, side)

    def rope_bwd(dq, dkr_h, cosp, sinp):
        parts = []
        for h in range(nh):
            parts.append(dq[:, h * MLA_QW:h * MLA_QW + MLA_NOPE])
            parts.append(_rope_t(dq[:, h * MLA_QW + MLA_NOPE:(h + 1) * MLA_QW], cosp, sinp))
        dkr = dkr_h[0]
        for h in range(1, nh):
            dkr = dkr + dkr_h[h]
        return jnp.concatenate(parts, axis=1), _rope_t(dkr, cosp, sinp)
    dq0, dkr_raw = _tile_call(
        rope_bwd, (lp // tm,), [dq, dkr_h, cosp, sinp],
        [_rows(tm, nh * MLA_QW), pl.BlockSpec((nh, tm, LANES), lambda i: (0, i, 0)), _rows(tm, LANES),
         _rows(tm, LANES)],
        [((lp, nh * MLA_QW), F32), ((lp, LANES), F32)], [_rows(tm, nh * MLA_QW), _rows(tm, LANES)], "mla_rope_bwd")
    d_wuq_p = _mm(qn, dq0, "tn", "mla_dwuq")
    dqn = _mm(dq0, w_uq_p, "nt", "mla_dqn")
    d_wukv = _mm(kvn, dkv, "tn", "mla_dwukv")
    dkvn = _mm(dkv, w_ukv, "nt", "mla_dkvn")
    qg, kg = wq["l0_mla_q_norm"].reshape(1, -1), wq["l0_mla_kv_norm"].reshape(1, -1)

    def norm_bwd(ql, kl, dqn, dkvn, g1, g2):
        _, vjp1 = jax.vjp(_rms, ql, g1)
        _, vjp2 = jax.vjp(_rms, kl, g2)
        dql, dg1 = vjp1(dqn)
        dkl, dg2 = vjp2(dkvn)
        return dql, dkl, dg1, dg2
    dql, dkl, dg1, dg2 = _rowwise(norm_bwd, [ql, kl, dqn, dkvn], [qg, kg],
                                  [(cfg.q_rank, F32), (cfg.kv_rank, F32)], [(1, cfg.q_rank), (1, cfg.kv_rank)], tm,
                                  "mla_norm_bwd")
    width = MLA_NOPE + MLA_ROPE
    d_wuq = d_wuq_p.reshape(cfg.q_rank, nh, MLA_QW)[:, :, :width].reshape(cfg.q_rank, nh * width)
    grads = {"l0_mla_q_norm": dg1.reshape(-1), "l0_mla_kv_norm": dg2.reshape(-1), "l0_mla_w_uq": d_wuq,
             "l0_mla_w_ukv": d_wukv}
    return _unshift(dql), _unshift(dkl), _unshift(dkr_raw[:, :MLA_ROPE]), grads, side_out


def _conv_taps(x, halo, first):
    halo = jnp.where(first, 0.0, halo)
    row8 = lax.broadcasted_iota(jnp.int32, halo.shape, 0)
    taps = []
    for s in range(SSD_CONV - 1, 0, -1):
        r = pltpu.roll(x, s, axis=0)
        top = jnp.where(row8 < s, pltpu.roll(halo, s, axis=0), r[:SUBLANES])
        taps.append(jnp.concatenate([top, r[SUBLANES:]], axis=0))
    taps.append(x)
    return taps


def _conv_specs(cfg, lp):
    tm = cfg.tr
    cb = _div_tile(math.gcd(cfg.ssd_inner, cfg.gn), 1024)
    off = cfg.ssd_inner // cb
    per = tm // SUBLANES
    nrow = lp // tm
    main = pl.BlockSpec((tm, cb), lambda i, j: (i, j + off))
    before = pl.BlockSpec((SUBLANES, cb), lambda i, j: (jnp.maximum(i * per - 1, 0), j + off))
    own = pl.BlockSpec((tm, cb), lambda i, j: (i, j))
    after = pl.BlockSpec((SUBLANES, cb), lambda i, j: (jnp.minimum((i + 1) * per, nrow * per - 1), j))
    par = lambda r: pl.BlockSpec((r, cb), lambda i, j: (0, j))
    return tm, cb, nrow, main, before, own, after, par


def _conv_fwd(zx, conv_w, conv_b, cfg):
    lp = zx.shape[0]
    tm, cb, nrow, main, before, own, after, par = _conv_specs(cfg, lp)

    def fn(x, halo, w, b):
        taps = _conv_taps(x, halo, pl.program_id(0) == 0)
        pre = b
        for k in range(SSD_CONV):
            pre = pre + taps[k] * w[k:k + 1, :]
        return _silu(pre)
    return _tile_call(fn, (nrow, cfg.conv_dim // cb), [zx, zx, conv_w, conv_b],
                      [main, before, par(SSD_CONV), par(1)], [((lp, cfg.conv_dim), F32)], [own], "ssd_conv")[0]


def _conv_bwd(zx, conv_w, conv_b, dxs, dbm, dcm, dzx, cfg):
    lp = zx.shape[0]
    tm, cb, nrow, main, before, own, after, par = _conv_specs(cfg, lp)
    ncb = cfg.conv_dim // cb
    nx, nb = cfg.ssd_inner // cb, cfg.gn // cb
    off = nx

    def fn1(x, halo, w, b, d1, d2, d3):
        j = pl.program_id(0)
        da = jnp.where(j < nx, d1, jnp.where(j < nx + nb, d2, d3))
        taps = _conv_taps(x, halo, pl.program_id(1) == 0)
        pre = b
        for k in range(SSD_CONV):
            pre = pre + taps[k] * w[k:k + 1, :]
        sg = _sigmoid(pre)
        dpre = da * sg * (1.0 + pre * (1.0 - sg))
        row8 = lax.broadcasted_iota(jnp.int32, (SUBLANES, cb), 0)
        dw = jnp.zeros((SUBLANES, cb), F32)
        for k in range(SSD_CONV):
            dw = jnp.where(row8 == k, jnp.sum(dpre * taps[k], axis=0, keepdims=True), dw)
        return dpre, dw, jnp.sum(dpre, axis=0, keepdims=True)
    sw = lambda spec: pl.BlockSpec(spec.block_shape, lambda j, i, f=spec.index_map: f(i, j))
    piece = lambda lo, n: pl.BlockSpec((tm, cb), lambda j, i: (i, jnp.clip(j - lo, 0, n - 1)))
    dpre, dw, db = _tile_call(
        fn1, (ncb, nrow), [zx, zx, conv_w, conv_b, dxs, dbm, dcm],
        [sw(main), sw(before), sw(par(SSD_CONV)), sw(par(1)), piece(0, nx), piece(nx, nb), piece(nx + nb, nb)],
        [((lp, cfg.conv_dim), F32), ((SUBLANES, cfg.conv_dim), F32), ((1, cfg.conv_dim), F32)],
        [sw(own), sw(par(SUBLANES)), sw(par(1))], "ssd_conv_bwd1", acc=(1, 2), acc_axis=1)

    def fn2(dp, nxt, w):
        nxt = jnp.where(pl.program_id(0) == nrow - 1, 0.0, nxt)
        row8 = lax.broadcasted_iota(jnp.int32, nxt.shape, 0)
        dx = dp * w[SSD_CONV - 1:SSD_CONV, :]
        for s in range(1, SSD_CONV):
            r = pltpu.roll(dp, tm - s, axis=0)
            bot = jnp.where(row8 >= SUBLANES - s, pltpu.roll(nxt, SUBLANES - s, axis=0), r[tm - SUBLANES:])
            up = jnp.concatenate([r[:tm - SUBLANES], bot], axis=0)
            dx = dx + up * w[SSD_CONV - 1 - s:SSD_CONV - s, :]
        return dx
    dzx = _tile_call(fn2, (nrow, ncb), [dpre, dpre, conv_w], [own, after, par(SSD_CONV)],
                     [(dzx.shape, BF16)], [main], "ssd_conv_bwd2", fill=(dzx, 0))[0]
    return dzx, dw, db


def _ssd_common(x_ref, b_ref, c_ref, dt_ref, dtt_ref, ar_ref, ac_ref, h):
    q = SSD_BLOCK
    x, bm, cm = x_ref[...], b_ref[...], c_ref[...]
    dt, dtt = dt_ref[0], dtt_ref[0]
    row = lax.broadcasted_iota(jnp.int32, (q, q), 0)
    col = lax.broadcasted_iota(jnp.int32, (q, q), 1)
    tri = row >= col
    cs = _dot(tri.astype(F32), dt * ar_ref[0], precision=HI)
    cst = _dot(dtt * ac_ref[0], (row <= col).astype(F32), precision=HI)
    g = _bdot(cm, bm, "nt")
    ch = _bdot(cm, h)
    hpg, gw = dt.shape[1], x.shape[1]
    e = (lax.broadcasted_iota(jnp.int32, (hpg, gw), 1) // SSD_HEAD_DIM
         == lax.broadcasted_iota(jnp.int32, (hpg, gw), 0)).astype(F32)
    et = (lax.broadcasted_iota(jnp.int32, (gw, hpg), 0) // SSD_HEAD_DIM
          == lax.broadcasted_iota(jnp.int32, (gw, hpg), 1)).astype(F32)
    spread = lambda v: _dot(v, e, precision=HI)
    gather = lambda v: _dot(v, et, precision=HI)
    return x, bm, cm, dt, tri, cs, cst, g, ch, spread, gather


def _ssd_specs(cfg, rev):
    q, n, gw, hpg = SSD_BLOCK, SSD_STATE, cfg.gw, cfg.hpg
    nc = cfg.lp // q
    cc = (lambda c: nc - 1 - c) if rev else (lambda c: c)
    boff = cfg.ssd_inner // n
    return dict(
        x=pl.BlockSpec((q, gw), lambda g, c: (cc(c), g)),
        b=pl.BlockSpec((q, n), lambda g, c: (cc(c), boff + g)),
        c=pl.BlockSpec((q, n), lambda g, c: (cc(c), boff + SSD_GROUPS + g)),
        bc_out=pl.BlockSpec((q, n), lambda g, c: (cc(c), g)),
        dt=pl.BlockSpec((1, q, hpg), lambda g, c: (g, cc(c), 0)),
        dtt=pl.BlockSpec((1, hpg, q), lambda g, c: (g, 0, cc(c))),
        ar=pl.BlockSpec((1, 1, hpg), lambda g, c: (g, 0, 0)),
        ac=pl.BlockSpec((1, hpg, 1), lambda g, c: (g, 0, 0)),
        h=pl.BlockSpec((1, n, gw), lambda g, c: (cc(c), 0, g)))


def _ssd_fwd(xbc, dt_g, dtt_g, a_row, a_col, cfg):
    q, n, gw, hpg, lp = SSD_BLOCK, SSD_STATE, cfg.gw, cfg.hpg, cfg.lp
    nc = lp // q
    hd = SSD_HEAD_DIM
    sp = _ssd_specs(cfg, False)

    def body(x_ref, b_ref, c_ref, dt_ref, dtt_ref, ar_ref, ac_ref, y_ref, hp_ref, h_s):
        @pl.when(pl.program_id(1) == 0)
        def _():
            h_s[...] = jnp.zeros((n, gw), F32)
        h = h_s[...]
        hp_ref[0] = h
        x, bm, cm, dt, tri, cs, cst, g, ch, spread, _ = _ssd_common(x_ref, b_ref, c_ref, dt_ref, dtt_ref, ar_ref,
                                                                    ac_ref, h)
        last = cs[q - 1:q, :]
        xdt = x * spread(dt)
        y_off = spread(jnp.exp(cs)) * ch
        xw = xdt * spread(jnp.exp(last - cs))
        for r in range(hpg):
            sl = slice(r * hd, (r + 1) * hd)
            lm = jnp.exp(jnp.where(tri, cs[:, r:r + 1] - cst[r:r + 1, :], NEG))
            y_ref[:, sl] = _bdot(g * lm, xdt[:, sl]) + y_off[:, sl]
        h_s[...] = h * spread(jnp.exp(last)) + _bdot(bm, xw, "tn")

    return pl.pallas_call(
        body, out_shape=[jax.ShapeDtypeStruct((lp, cfg.ssd_inner), F32),
                         jax.ShapeDtypeStruct((nc, n, cfg.ssd_inner), F32)],
        grid=(SSD_GROUPS, nc),
        in_specs=[sp["x"], sp["b"], sp["c"], sp["dt"], sp["dtt"], sp["ar"], sp["ac"]],
        out_specs=[sp["x"], sp["h"]],
        scratch_shapes=[pltpu.VMEM((n, gw), F32)],
        name="ssd_scan", compiler_params=_cparams(2))(xbc, xbc, xbc, dt_g, dtt_g, a_row, a_col)


def _ssd_bwd(xbc, dt_g, dtt_g, a_row, a_col, hprev, dy, dx_gate, cfg):
    q, n, gw, hpg, lp = SSD_BLOCK, SSD_STATE, cfg.gw, cfg.hpg, cfg.lp
    nc = lp // q
    hd = SSD_HEAD_DIM
    sp = _ssd_specs(cfg, True)

    def body(x_ref, b_ref, c_ref, dt_ref, dtt_ref, ar_ref, ac_ref, hp_ref, dy_ref, dxg_ref,
             dx_ref, db_ref, dc_ref, ddt_ref, da_ref, dh_s, dxdt_s):
        @pl.when(pl.program_id(1) == 0)
        def _():
            dh_s[...] = jnp.zeros((n, gw), F32)
            da_ref[...] = jnp.zeros((1, 1, hpg), F32)
        h = hp_ref[0]
        dhn = dh_s[...]
        dy = dy_ref[...]
        x, bm, cm, dt, tri, cs, cst, g, ch, spread, gather = _ssd_common(x_ref, b_ref, c_ref, dt_ref, dtt_ref,
                                                                         ar_ref, ac_ref, h)
        last = cs[q - 1:q, :]
        e, wv, elast = jnp.exp(cs), jnp.exp(last - cs), jnp.exp(last)
        dt_x, w_x = spread(dt), spread(wv)
        xdt = x * dt_x
        dye = dy * spread(e)
        xw = xdt * w_x
        bd = _bdot(bm, dhn)
        de = gather(dy * ch)
        dw = gather(xdt * bd)
        hsum = gather(jnp.sum(dhn * h, axis=0, keepdims=True))
        head_lane = lax.broadcasted_iota(jnp.int32, (q, hpg), 1)
        head_row = lax.broadcasted_iota(jnp.int32, (hpg, q), 0)
        z_rows = jnp.zeros((q, hpg), F32)
        z_cols = jnp.zeros((hpg, q), F32)
        dg = jnp.zeros((q, q), F32)
        for r in range(hpg):
            sl = slice(r * hd, (r + 1) * hd)
            lm = jnp.exp(jnp.where(tri, cs[:, r:r + 1] - cst[r:r + 1, :], NEG))
            m = g * lm
            dyr = dy[:, sl]
            dxdt_s[:, sl] = _bdot(m, dyr, "tn")
            dm = _bdot(dyr, xdt[:, sl], "nt")
            dg = dg + dm * lm
            z = dm * m
            z_rows = jnp.where(head_lane == r, jnp.sum(z, axis=1, keepdims=True), z_rows)
            z_cols = jnp.where(head_row == r, jnp.sum(z, axis=0, keepdims=True), z_cols)
        dxdt = dxdt_s[...] + w_x * bd
        is_last = lax.broadcasted_iota(jnp.int32, (q, 1), 0) == q - 1
        extra = jnp.sum(dw * wv, axis=0, keepdims=True) + elast * hsum
        eye = (lax.broadcasted_iota(jnp.int32, (hpg, hpg), 0)
               == lax.broadcasted_iota(jnp.int32, (hpg, hpg), 1)).astype(F32)
        dcs = (z_rows + de * e - dw * wv + jnp.where(is_last, extra, 0.0)
               - _dot(z_cols, eye, "tn", precision=HI))
        row = lax.broadcasted_iota(jnp.int32, (q, q), 0)
        col = lax.broadcasted_iota(jnp.int32, (q, q), 1)
        dda = _dot((row <= col).astype(F32), dcs, precision=HI)
        ddt_ref[0] = dda * ar_ref[0] + gather(dxdt * x)
        da_ref[0] += jnp.sum(dda * dt, axis=0, keepdims=True)
        dx_ref[...] = dxdt * dt_x + dxg_ref[...]
        dc_ref[...] = _bdot(dg, bm) + _bdot(dye, h, "nt")
        db_ref[...] = _bdot(dg, cm, "tn") + _bdot(xw, dhn, "nt")
        dh_s[...] = dhn * spread(elast) + _bdot(cm, dye, "tn")

    return pl.pallas_call(
        body, out_shape=[jax.ShapeDtypeStruct((lp, cfg.ssd_inner), F32), jax.ShapeDtypeStruct((lp, cfg.gn), F32),
                         jax.ShapeDtypeStruct((lp, cfg.gn), F32), jax.ShapeDtypeStruct((SSD_GROUPS, lp, hpg), F32),
                         jax.ShapeDtypeStruct((SSD_GROUPS, 1, hpg), F32)],
        grid=(SSD_GROUPS, nc),
        in_specs=[sp["x"], sp["b"], sp["c"], sp["dt"], sp["dtt"], sp["ar"], sp["ac"], sp["h"], sp["x"], sp["x"]],
        out_specs=[sp["x"], sp["bc_out"], sp["bc_out"], sp["dt"], sp["ar"]],
        scratch_shapes=[pltpu.VMEM((n, gw), F32), pltpu.VMEM((q, gw), F32)],
        name="ssd_scan_bwd", compiler_params=_cparams(2))(xbc, xbc, xbc, dt_g, dtt_g, a_row, a_col, hprev, dy, dx_gate)


def _gate_fn(y, xs, z, dexp, ng):
    return _rms((y + dexp * xs) * _silu(z), ng)


def _gate_specs(cfg):
    tm, gw = cfg.tm, cfg.gw
    blk = pl.BlockSpec((tm, gw), lambda g, i: (i, g))
    par = pl.BlockSpec((1, gw), lambda g, i: (0, g))
    return blk, par


def _mamba_fwd(h, w, w_in_t, w_out, conv_w, cfg):
    lp, tm, nh, hpg, inner = cfg.lp, cfg.tm, cfg.ssd_heads, cfg.hpg, cfg.ssd_inner
    zx = _mm(h, w_in_t, "nt", "l1_in")
    conv_b = w["l1_conv_b"].reshape(1, -1)
    xbc = _conv_fwd(zx, conv_w, conv_b, cfg)
    dt_raw = zx[:, inner + cfg.conv_dim:inner + cfg.conv_dim + nh]
    dt_bias = w["l1_dt_bias"].reshape(1, nh)
    a_log = w["l1_a_log"].reshape(1, nh)
    dt = _rowwise(lambda r, b: _softplus(r + b), [dt_raw], [dt_bias], [(nh, F32)], [], tm, "ssd_dt")[0]
    a = _small_call(lambda al: -jnp.exp(al), [a_log], [(1, nh)], "ssd_a")[0]
    dt_g = dt.reshape(lp, SSD_GROUPS, hpg).transpose(1, 0, 2)
    dtt_g = dt_g.transpose(0, 2, 1)
    a_row, a_col = a.reshape(SSD_GROUPS, 1, hpg), a.reshape(SSD_GROUPS, hpg, 1)
    y, hprev = _ssd_fwd(xbc, dt_g, dtt_g, a_row, a_col, cfg)
    dexp = jnp.repeat(w["l1_d"], SSD_HEAD_DIM).reshape(1, inner)
    ng = w["l1_norm_g"].reshape(1, inner)
    blk, par = _gate_specs(cfg)
    yn = _tile_call(_gate_fn, (SSD_GROUPS, lp // tm), [y, xbc, zx, dexp, ng], [blk, blk, blk, par, par],
                    [((lp, inner), F32)], [blk], "ssd_gate")[0]
    mo = _mm(yn, w_out, "nn", "l1_out")
    return mo, (zx, xbc, dt_raw, dt_g, dtt_g, a, a_row, a_col, y, hprev, dexp, ng, yn)


def _mamba_bwd(h, saved, dmo, w, w_in_t, w_out, conv_w, cfg):
    zx, xbc, dt_raw, dt_g, dtt_g, a, a_row, a_col, y, hprev, dexp, ng, yn = saved
    lp, tm, nh, hpg, inner = cfg.lp, cfg.tm, cfg.ssd_heads, cfg.hpg, cfg.ssd_inner
    d_wout = _mm(yn, dmo, "tn", "l1_dwout", out_dtype=BF16)
    dyn = _mm(dmo, w_out, "nt", "l1_dyn")
    blk, par = _gate_specs(cfg)

    def gate_bwd(y, xs, z, dexp, ng, dyn):
        _, vjp = jax.vjp(_gate_fn, y, xs, z, dexp, ng)
        return vjp(dyn)
    dy, dxs_gate, dzx, ddexp, dng = _tile_call(
        gate_bwd, (SSD_GROUPS, lp // tm), [y, xbc, zx, dexp, ng, dyn], [blk, blk, blk, par, par, blk],
        [((lp, inner), F32)] * 2 + [((lp, cfg.l1_inp), BF16)] + [((1, inner), F32)] * 2, [blk, blk, blk, par, par],
        "ssd_gate_bwd", acc=(3, 4), acc_axis=1)
    dxs, dbm, dcm, ddt_g, da_g = _ssd_bwd(xbc, dt_g, dtt_g, a_row, a_col, hprev, dy, dxs_gate, cfg)
    conv_b = w["l1_conv_b"].reshape(1, -1)
    dzx, dconv_w, dconv_b = _conv_bwd(zx, conv_w, conv_b, dxs, dbm, dcm, dzx, cfg)
    assert cfg.l1_inp - inner - cfg.conv_dim == LANES
    ddt = jnp.pad(ddt_g.transpose(1, 0, 2).reshape(lp, nh), ((0, 0), (0, LANES - nh)))
    dt_bias = jnp.pad(w["l1_dt_bias"].reshape(1, nh), ((0, 0), (0, LANES - nh)))
    last = (inner + cfg.conv_dim) // LANES
    tail = pl.BlockSpec((tm, LANES), lambda i: (i, last))

    def dt_bwd(ddt, r, b):
        lane = lax.broadcasted_iota(jnp.int32, ddt.shape, 1)
        d = jnp.where(lane < nh, ddt * _sigmoid(r + b), 0.0)
        return d, jnp.sum(d, axis=0, keepdims=True)
    dzx, ddt_bias = _tile_call(dt_bwd, (lp // tm,), [ddt, zx, dt_bias], [_rows(tm, LANES), tail, _whole((1, LANES))],
                               [(dzx.shape, BF16), ((1, LANES), F32)], [tail, _whole((1, LANES))], "ssd_dt_bwd",
                               acc=(1,), fill=(dzx, 0))
    ddt_bias = ddt_bias[:, :nh]
    da_log, dd = _small_call(lambda da, a, dde: (da * a, jnp.sum(dde, axis=1, keepdims=True)),
                             [da_g.reshape(1, nh), a, ddexp.reshape(nh, SSD_HEAD_DIM)], [(1, nh), (nh, 1)],
                             "ssd_small_bwd")
    d_win_t = _mm(dzx, h, "tn", "l1_dwin", out_dtype=BF16, tm_t=1152)
    dh = _mm(dzx, w_in_t, "nn", "l1_dh", tn_t=2048)
    grads = {"l1_w_in": d_win_t, "l1_conv_w": dconv_w[:SSD_CONV], "l1_conv_b": dconv_b.reshape(-1),
             "l1_dt_bias": ddt_bias.reshape(-1), "l1_a_log": da_log.reshape(-1), "l1_d": dd.reshape(-1),
             "l1_norm_g": dng.reshape(-1), "l1_w_out": d_wout}
    return dh, grads


def _local_step(x, target, w, cfg, late_weights=None, early_grads=None):
    lp, n, d, tm, sw = cfg.lp, cfg.n, cfg.d, cfg.tr, cfg.s5_w
    row = lambda name: w[name].reshape(1, -1)
    h0 = jnp.concatenate([w["meta_tokens"], x, jnp.zeros((lp - n, d), F32)], axis=0)
    proj = _mm(h0, w["l0_w_in"], "nn", "l0_in")
    o1, o2, o3 = sw, sw + cfg.q_rank, sw + cfg.q_rank + cfg.kv_rank
    prm = _s5_prep(w, cfg)
    a_out, s5_saved = _s5_fwd(proj[:, :o1], prm, w["l0_s5_w_glu"], cfg)
    b_out, mla_saved, arrived = _mla_fwd(proj[:, o1:o2], proj[:, o2:o3], proj[:, o3:], w, w["l0_mla_w_uq_p"],
                                         w["l0_mla_w_ukv"], cfg, late_weights[0] if late_weights else None)
    if late_weights:
        w = dict(w, **late_weights[1](arrived))
    mix = jnp.concatenate([a_out, b_out], axis=1).astype(BF16)
    mo0 = _mm(mix, w["l0_w_out"], "nn", "l0_out")
    h1, h1b = _ln_fwd(h0, mo0, row("l0_ln1_g"), row("l0_ln1_b"), cfg, "l0_ln1")
    fo0, ffn0 = _ffn_fwd(h1b, w["l0_w_gu"], w["l0_ffn_w_down"], cfg, "l0_ffn")
    h2, h2b = _ln_fwd(h1, fo0, row("l0_ln2_g"), row("l0_ln2_b"), cfg, "l0_ln2")
    mo1, mam = _mamba_fwd(h2b, w, w["l1_w_in_t"], w["l1_w_out"], w["l1_conv_w"], cfg)
    h3, h3b = _ln_fwd(h2, mo1, row("l1_ln1_g"), row("l1_ln1_b"), cfg, "l1_ln1")
    fo1, ffn1 = _ffn_fwd(h3b, w["l1_w_gu"], w["l1_ffn_w_down"], cfg, "l1_ffn")
    h4, _ = _ln_fwd(h3, fo1, row("l1_ln2_g"), row("l1_ln2_b"), cfg, "l1_ln2")
    tgt = jnp.concatenate([jnp.zeros((N_META, d), F32), target, jnp.zeros((lp - n, d), F32)], axis=0)

    def loss_fn(y, t):
        r = pl.program_id(0) * tm + lax.broadcasted_iota(jnp.int32, (tm, 1), 0)
        diff = jnp.where(jnp.logical_and(r >= N_META, r < n), y - t, 0.0)
        return diff * (1.0 / d), jnp.sum(diff * diff, axis=0, keepdims=True) * (0.5 / d)
    dh4, loss_lanes = _rowwise(loss_fn, [h4, tgt], [], [(d, F32)], [(1, d)], tm, "loss")
    grads = {}
    dr4, dr4b, dg, db = _ln_bwd(h3, fo1, row("l1_ln2_g"), row("l1_ln2_b"), [dh4], [1.0], cfg, "l1_ln2_bwd")
    grads["l1_ln2_g"], grads["l1_ln2_b"] = dg.reshape(-1), db.reshape(-1)
    dh3, grads["l1_w_gu"], grads["l1_ffn_w_down"] = _ffn_bwd(h3b, ffn1, dr4b, w["l1_w_gu"], w["l1_ffn_w_down"], cfg,
                                                             "l1_ffn")
    dr3, dr3b, dg, db = _ln_bwd(h2, mo1, row("l1_ln1_g"), row("l1_ln1_b"), [dr4, dh3], [DN_ALPHA, 1.0], cfg,
                                "l1_ln1_bwd")
    grads["l1_ln1_g"], grads["l1_ln1_b"] = dg.reshape(-1), db.reshape(-1)
    dh2, mg = _mamba_bwd(h2b, mam, dr3b, w, w["l1_w_in_t"], w["l1_w_out"], w["l1_conv_w"], cfg)
    grads.update(mg)
    dr2, dr2b, dg, db = _ln_bwd(h1, fo0, row("l0_ln2_g"), row("l0_ln2_b"), [dr3, dh2], [DN_ALPHA, 1.0], cfg,
                                "l0_ln2_bwd")
    grads["l0_ln2_g"], grads["l0_ln2_b"] = dg.reshape(-1), db.reshape(-1)
    dh1, grads["l0_w_gu"], grads["l0_ffn_w_down"] = _ffn_bwd(h1b, ffn0, dr2b, w["l0_w_gu"], w["l0_ffn_w_down"], cfg,
                                                             "l0_ffn")
    dr1, dr1b, dg, db = _ln_bwd(h0, mo0, row("l0_ln1_g"), row("l0_ln1_b"), [dr2, dh1], [DN_ALPHA, 1.0], cfg,
                                "l0_ln1_bwd")
    grads["l0_ln1_g"], grads["l0_ln1_b"] = dg.reshape(-1), db.reshape(-1)
    grads["l0_w_out"] = _mm(mix, dr1b, "tn", "l0_dwout", out_dtype=BF16)
    dmix = _mm(dr1b, w["l0_w_out"], "nt", "l0_dmix")
    du, sg = _s5_bwd(dmix[:, :sw], s5_saved, prm, w, w["l0_s5_w_glu"], cfg)
    dql, dkl, dkr, ag, exchanged = _mla_bwd(dmix[:, sw:], mla_saved, w, w["l0_mla_w_uq_p"], w["l0_mla_w_ukv"], cfg,
                                            early_grads(grads) if early_grads else None)
    grads.update(sg)
    grads.update(ag)
    dproj = jnp.concatenate([du, dql, dkl, dkr], axis=1)
    grads["l0_w_in"] = _mm(h0, dproj, "tn", "l0_dwin", out_dtype=BF16)
    dh0m = _mm(dproj, w["l0_w_in"], "nt", "l0_dh")
    dh0 = _rowwise(lambda a, b: DN_ALPHA * a + b, [dr1, dh0m], [], [(d, F32)], [], tm, "l0_dh0")[0]
    grads["meta_tokens"] = dh0[:N_META]
    return loss_lanes, dh0[N_META:n], grads, exchanged


FLAT_W = 1024
N_SLOTS = 4
HBM_SPEC = pl.BlockSpec(memory_space=pltpu.HBM)


def _place():
    x, y, c = lax.axis_index("x"), lax.axis_index("y"), lax.axis_index("c")
    chips = [(1 - x, y), (x, 1 - y), (1 - x, 1 - y)]
    return x, y, c, chips


def _remote(src, dst, ssem, rsem, dev):
    return pltpu.make_async_remote_copy(src_ref=src, dst_ref=dst, send_sem=ssem, recv_sem=rsem, device_id=dev,
                                        device_id_type=MESH)


class _GatherPlan:
    def __init__(self, shards, groups):
        self.arrays = list(shards)
        self.n = n = len(shards)
        self.rows = [s.shape[0] for s in shards]
        self.place = {t: (g, row0) for g, members in enumerate(groups) for t, row0 in members}
        self.n_out = len(groups)
        self.out_shapes = []
        for members in groups:
            t0 = members[0][0]
            rows = max(row0 + N_SLOTS * shards[t].shape[0] for t, row0 in members)
            self.out_shapes.append(jax.ShapeDtypeStruct((rows, shards[t0].shape[1]), shards[t0].dtype))
        sems = pltpu.SemaphoreType.DMA((3 * n,))
        self.scratch = [sems, sems, sems, sems, pltpu.SemaphoreType.DMA((n,))]

    def _copies(self, refs):
        n = self.n
        srcs, outs = refs[:n], refs[n:n + self.n_out]
        send_sems, recv_sems, fsend, frecv, lsems = refs[n + self.n_out:]
        x, y, c, chips = _place()
        me = 2 * x + y
        sib = (x, y, 1 - c)

        def rows_of(t, slot, half):
            r = self.rows[t]
            g, row0 = self.place[t]
            return outs[g].at[pl.ds(row0 + slot * r + half * (r // 2), r // 2)]
        local, send, arrive, relay, arrive_sib = [], [], [], [], []
        for t in range(n):
            r = self.rows[t]
            g, row0 = self.place[t]
            local.append(pltpu.make_async_copy(srcs[t], outs[g].at[pl.ds(row0 + me * r, r)], lsems.at[t]))
            mine = srcs[t].at[pl.ds(c * (r // 2), r // 2)]
            for j, (cx, cy) in enumerate(chips):
                k = 3 * t + j
                send.append(_remote(mine, rows_of(t, me, c), send_sems.at[k], recv_sems.at[k], (cx, cy, c)))
                got = rows_of(t, 2 * cx + cy, c)
                arrive.append(_remote(got, got, send_sems.at[k], recv_sems.at[k], (cx, cy, c)))
                relay.append(_remote(got, got, fsend.at[k], frecv.at[k], sib))
                got_sib = rows_of(t, 2 * cx + cy, 1 - c)
                arrive_sib.append(_remote(got_sib, got_sib, fsend.at[k], frecv.at[k], sib))
        return local, send, arrive, relay, arrive_sib

    def start(self, refs):
        local, send, _, _, _ = self._copies(refs)
        for cp in local + send:
            cp.start()

    def relay(self, refs):
        _, _, arrive, relay, _ = self._copies(refs)
        for a, r in zip(arrive, relay):
            a.wait_recv()
            r.start()

    def finish(self, refs):
        local, send, _, relay, arrive_sib = self._copies(refs)
        for cp in arrive_sib:
            cp.wait_recv()
        for cp in send + relay:
            cp.wait_send()
        for cp in local:
            cp.wait()


class _ExchangePlan:
    def __init__(self, items):
        self.items = items
        self.arrays = [a for a, _, _ in items]
        self.n = self.n_out = n = len(items)
        self.out_shapes = [jax.ShapeDtypeStruct((3, rps, a.shape[1]), a.dtype) for a, _, rps in items]
        sems = pltpu.SemaphoreType.DMA((3 * n,))
        self.scratch = [sems, sems]

    def _copies(self, refs):
        n = self.n
        srcs, outs = refs[:n], refs[n:2 * n]
        send_sems, recv_sems = refs[2 * n:]
        x, y, c, chips = _place()
        cps = []
        for t, (_, row0, rps) in enumerate(self.items):
            for j, (cx, cy) in enumerate(chips):
                cps.append(_remote(srcs[t].at[pl.ds(row0 + (2 * cx + cy) * rps, rps)], outs[t].at[j],
                                   send_sems.at[3 * t + j], recv_sems.at[3 * t + j], (cx, cy, c)))
        return cps

    def start(self, refs):
        for cp in self._copies(refs):
            cp.start()

    def relay(self, refs):
        pass

    def finish(self, refs):
        for cp in self._copies(refs):
            cp.wait()


def _run_plan(plan, name):
    def body(*refs):
        plan.start(refs)
        plan.relay(refs)
        plan.finish(refs)
    return pl.pallas_call(body, out_shape=plan.out_shapes, in_specs=[HBM_SPEC] * plan.n,
                          out_specs=[HBM_SPEC] * plan.n_out, scratch_shapes=plan.scratch, name=name)(*plan.arrays)


def _exchange_sibling(arrays):
    n = len(arrays)

    def body(*refs):
        srcs, outs = refs[:n], refs[n:2 * n]
        ssems, rsems = refs[2 * n:]
        x, y, c, _ = _place()
        cps = []
        for t in range(n):
            cp = _remote(srcs[t], outs[t], ssems.at[t], rsems.at[t], (x, y, 1 - c))
            cp.start()
            cps.append(cp)
        for cp in cps:
            cp.wait()

    sems = pltpu.SemaphoreType.DMA((n,))
    return pl.pallas_call(
        body, out_shape=[jax.ShapeDtypeStruct(a.shape, a.dtype) for a in arrays], in_specs=[HBM_SPEC] * n,
        out_specs=[HBM_SPEC] * n, scratch_shapes=[sems, sems], name="exchange_sibling")(*arrays)


def _gather_all(v):
    flips = [(fx, fy, fc) for fx in (0, 1) for fy in (0, 1) for fc in (0, 1)][1:]

    def body(src, out, send_sems, recv_sems, lsem):
        x, y, c, _ = _place()
        local = pltpu.make_async_copy(src, out.at[4 * x + 2 * y + c], lsem)
        local.start()
        cps = []
        for k, (fx, fy, fc) in enumerate(flips):
            px, py, pc = (1 - x if fx else x), (1 - y if fy else y), (1 - c if fc else c)
            cp = _remote(src, out.at[4 * x + 2 * y + c], send_sems.at[k], recv_sems.at[k], (px, py, pc))
            cp.start()
            cps.append(cp)
        for cp in cps:
            cp.wait()
        local.wait()

    return pl.pallas_call(
        body, out_shape=jax.ShapeDtypeStruct((8,) + v.shape, v.dtype), in_specs=[HBM_SPEC], out_specs=HBM_SPEC,
        scratch_shapes=[pltpu.SemaphoreType.DMA((7,)), pltpu.SemaphoreType.DMA((7,)), pltpu.SemaphoreType.DMA],
        name="gather_all")(v)


def _flat_rows(n_elems, row_unit):
    return -(-n_elems // (FLAT_W * row_unit)) * row_unit


def _to_flat(pieces, rows):
    flat = jnp.concatenate([p.reshape(-1) for p in pieces])
    return jnp.pad(flat, (0, rows * FLAT_W - flat.shape[0])).reshape(rows, FLAT_W)


def _sum_parts(parts, name, tm=512):
    rows = parts[0].shape[1]
    tm = _div_tile(rows, tm, SUBLANES)

    def fn(*ps):
        acc = None
        for p in ps:
            for k in range(p.shape[0]):
                acc = p[k].astype(F32) if acc is None else acc + p[k].astype(F32)
        return acc
    return _tile_call(fn, (rows // tm,), parts,
                      [pl.BlockSpec((p.shape[0], tm, FLAT_W), lambda i: (0, i, 0)) for p in parts],
                      [((rows, FLAT_W), F32)], [_rows(tm, FLAT_W)], name)[0]


ELEMENTWISE_BLOCK = 1 << 19


def _row_tile(rows, cols, unit):
    return _div_tile(rows, max(unit, ELEMENTWISE_BLOCK // cols), unit)


def _sum_slot(g, row0, rps, others, me, name):
    c = g.shape[1]
    tm = _row_tile(rps, c, 2 * SUBLANES)
    nrb = rps // tm
    assert row0 % tm == 0

    def body(me_ref, g_ref, o_ref, out_ref):
        out_ref[...] = ((g_ref[...].astype(F32) + o_ref[0].astype(F32)) + o_ref[1].astype(F32)) + o_ref[2].astype(F32)

    grid_spec = pltpu.PrefetchScalarGridSpec(
        num_scalar_prefetch=1, grid=(nrb,),
        in_specs=[pl.BlockSpec((tm, c), lambda i, me_ref: (row0 // tm + me_ref[0] * nrb + i, 0)),
                  pl.BlockSpec((3, tm, c), lambda i, me_ref: (0, i, 0))],
        out_specs=pl.BlockSpec((tm, c), lambda i, me_ref: (i, 0)))
    return pl.pallas_call(body, out_shape=jax.ShapeDtypeStruct((rps, c), F32), grid_spec=grid_spec, name=name,
                          compiler_params=_cparams(1))(me, g, others)


def _adamw(gparts, w, m, v, name, tm=None):
    rows, cols = w.shape
    tm = _row_tile(rows, cols, SUBLANES) if tm is None else _div_tile(rows, tm, SUBLANES)
    ng = len(gparts)

    def fn(*a):
        g = a[0]
        for t in a[1:ng]:
            g = g + t
        w, m, v = a[ng:]
        m = ADAM_B1 * m + (1.0 - ADAM_B1) * g
        v = ADAM_B2 * v + (1.0 - ADAM_B2) * (g * g)
        m_hat = m / (1.0 - ADAM_B1 ** ADAM_STEP)
        v_hat = v / (1.0 - ADAM_B2 ** ADAM_STEP)
        delta = -ADAM_LR * (m_hat / (jnp.sqrt(v_hat) + ADAM_EPS) + ADAM_WD * w)
        return g, delta, m, v
    ins = list(gparts) + [w, m, v]
    return _tile_call(fn, (rows // tm,), ins, [_rows(tm, cols)] * len(ins), [((rows, cols), F32)] * 4,
                      [_rows(tm, cols)] * 4, name)


FIRST = ("l0_w_in", "l0_s5_w_glu", "l0_mla_w_uq", "l0_mla_w_ukv")
REST = ("l0_w_out", "l0_ffn_w_gate", "l0_ffn_w_up", "l0_ffn_w_down", "l1_w_in", "l1_w_out", "l1_ffn_w_gate",
        "l1_ffn_w_up", "l1_ffn_w_down")
BIG = FIRST + REST
TINY = ("meta_tokens", "l1_conv_w")
REPLICATED = ("l0_s5_log_dt", "l0_s5_a_re", "l0_s5_a_im", "l0_s5_b_re", "l0_s5_b_im", "l0_s5_c_re", "l0_s5_c_im",
              "l0_s5_d", "l0_mla_q_norm", "l0_mla_kv_norm", "l0_ln1_g", "l0_ln1_b", "l0_ln2_g", "l0_ln2_b",
              "l1_conv_b", "l1_dt_bias", "l1_a_log", "l1_d", "l1_norm_g", "l1_ln1_g", "l1_ln1_b", "l1_ln2_g",
              "l1_ln2_b")
WEIGHTS = ("meta_tokens", "l0_w_in", "l0_s5_log_dt", "l0_s5_a_re", "l0_s5_a_im", "l0_s5_b_re", "l0_s5_b_im",
           "l0_s5_c_re", "l0_s5_c_im", "l0_s5_d", "l0_s5_w_glu", "l0_mla_q_norm", "l0_mla_w_uq", "l0_mla_kv_norm",
           "l0_mla_w_ukv", "l0_w_out", "l0_ln1_g", "l0_ln1_b", "l0_ffn_w_gate", "l0_ffn_w_up", "l0_ffn_w_down",
           "l0_ln2_g", "l0_ln2_b", "l1_w_in", "l1_conv_w", "l1_conv_b", "l1_dt_bias", "l1_a_log", "l1_d",
           "l1_norm_g", "l1_w_out", "l1_ln1_g", "l1_ln1_b", "l1_ffn_w_gate", "l1_ffn_w_up", "l1_ffn_w_down",
           "l1_ln2_g", "l1_ln2_b")
def _split_flat(flat2d, shapes):
    flat = flat2d.reshape(-1)
    out, off = [], 0
    for s in shapes:
        n = math.prod(s)
        out.append(flat[off:off + n].reshape(s))
        off += n
    return out


def _cols_to_slots(full):
    r, c = full.shape
    return full.reshape(r, N_SLOTS, c // N_SLOTS).transpose(1, 0, 2).reshape(N_SLOTS * r, c // N_SLOTS)


def _slots_to_cols(slabs):
    r4, c = slabs.shape
    return slabs.reshape(N_SLOTS, r4 // N_SLOTS, c).transpose(1, 0, 2).reshape(r4 // N_SLOTS, N_SLOTS * c)


def _step(cfg, x, loss_target, ws, ms, vs):
    d, f = cfg.d, cfg.ffn
    me = 2 * lax.axis_index("x") + lax.axis_index("y")
    kinds = ("grad", "delta", "new_m", "new_v")
    def gather_plan(names):
        groups = []
        for t, name in enumerate(names):
            if name.endswith("_ffn_w_up"):
                groups[-1].append((t, N_SLOTS * d))
            else:
                groups.append([(t, 0)])
        heads = [names[members[0][0]] for members in groups]
        return _GatherPlan([ws[name].astype(BF16) for name in names], groups), heads
    plan, heads = gather_plan(FIRST)
    got = dict(zip(heads, _run_plan(plan, "gather_first")))
    w = {"l0_w_in": got["l0_w_in"], "l0_s5_w_glu": got["l0_s5_w_glu"],
         "l0_mla_w_uq_p": _pad_heads(_slots_to_cols(got["l0_mla_w_uq"]), cfg.heads, MLA_NOPE + MLA_ROPE),
         "l0_mla_w_ukv": _slots_to_cols(got["l0_mla_w_ukv"])}
    late_plan, late_heads = gather_plan(REST)

    def late_weights(arrived):
        got = dict(zip(late_heads, arrived))
        lw = {name: got[name] for name in ("l0_w_out", "l0_ffn_w_down", "l1_w_out", "l1_ffn_w_down")}
        for l in ("l0", "l1"):
            lw[l + "_w_gu"] = got[l + "_ffn_w_gate"].reshape(2 * N_SLOTS, d, f // N_SLOTS)
        w_in_t = got["l1_w_in"].reshape(N_SLOTS, d, cfg.l1_in // N_SLOTS).transpose(0, 2, 1).reshape(cfg.l1_in, d)
        lw["l1_w_in_t"] = jnp.pad(w_in_t, ((0, cfg.l1_inp - cfg.l1_in), (0, 0)))
        return lw
    tiny_shapes = [ws[name].shape for name in TINY]
    trows = _flat_rows(sum(math.prod(s) for s in tiny_shapes), SUBLANES)
    tiny = _gather_all(_to_flat([ws[name] for name in TINY], trows))[0::2]
    for k, name in enumerate(TINY):
        blocks = jnp.stack([_split_flat(tiny[s], tiny_shapes)[k] for s in range(N_SLOTS)])
        w[name] = blocks.transpose(1, 0, 2).reshape(blocks.shape[1], -1)
    for name in REPLICATED:
        w[name] = ws[name]
    rps = {name: ws[name].shape[1 if name == "l1_w_in" else 0] for name in BIG}

    def triples(grads, names):
        out = []
        for name in names:
            if name.endswith(("_ffn_w_gate", "_ffn_w_up")):
                g = grads[name[:3] + "w_gu"].reshape(2 * N_SLOTS * d, f // N_SLOTS)
                out.append((g, N_SLOTS * d if name.endswith("_up") else 0, rps[name]))
            elif name in ("l0_mla_w_uq", "l0_mla_w_ukv"):
                out.append((_cols_to_slots(grads[name]).astype(BF16), 0, rps[name]))
            else:
                out.append((grads[name], 0, rps[name]))
        return out
    held = {}

    def early_grads(grads):
        held["rest"] = triples(grads, REST)
        return _ExchangePlan(held["rest"])
    loss_lanes, grad_x, grads, others_rest = _local_step(x[0], loss_target[0], w, cfg, (late_plan, late_weights),
                                                         early_grads)
    first = triples(grads, FIRST)
    others = list(_run_plan(_ExchangePlan(first), "exchange_first")) + list(others_rest)
    me1 = me.reshape(1).astype(jnp.int32)
    parts = [_sum_slot(a, row0, r, o, me1, "sum_" + name)
             for (a, row0, r), o, name in zip(first + held["rest"], others, BIG)]
    sibs = _exchange_sibling(parts)
    res = {}
    for name, p, q in zip(BIG, parts, sibs):
        if name == "l1_w_in":
            p, q = p.T, q.T
        for kind, arr in zip(kinds, _adamw([p, q], ws[name], ms[name], vs[name], "adamw_" + name)):
            res[kind + "_" + name] = arr
    rep_shapes = [(1, d)] + [ws[name].shape for name in REPLICATED]
    all_shapes = rep_shapes + [grads[name].shape for name in TINY]
    srows = _flat_rows(sum(math.prod(s) for s in all_shapes), SUBLANES)
    small = _to_flat([loss_lanes] + [grads[name] for name in REPLICATED + TINY], srows)
    total = _sum_parts([_gather_all(small)], "sum_small", tm=srows)
    zero = jnp.zeros((1, d), F32)
    flat = lambda dct: _to_flat([zero] + [dct[name] for name in REPLICATED], srows)
    outs = _adamw([total], flat(ws), flat(ms), flat(vs), "adamw_replicated", tm=srows)
    for kind, arr in zip(kinds, outs):
        vals = _split_flat(arr, rep_shapes)
        for name, val in zip(REPLICATED, vals[1:]):
            res[kind + "_" + name] = val
    for name, g in zip(TINY, _split_flat(total, all_shapes)[len(rep_shapes):]):
        cols = ws[name].shape[1]
        mine = lax.dynamic_slice_in_dim(g, me * cols, cols, axis=1)
        for kind, arr in zip(kinds, _adamw([mine], ws[name], ms[name], vs[name], "adamw_" + name)):
            res[kind + "_" + name] = arr
    loss = _small_call(lambda t: jnp.sum(t, axis=1, keepdims=True), [_split_flat(total, rep_shapes)[0]], [(1, 1)],
                       "loss_sum")[0].reshape(())
    ordered = [res[kind + "_" + name] for kind in ("grad", "delta", "new_m", "new_v") for name in WEIGHTS]
    return (loss, grad_x[None]) + tuple(ordered)


def kernel(x, meta_tokens, l0_w_in, l0_s5_log_dt, l0_s5_a_re, l0_s5_a_im, l0_s5_b_re, l0_s5_b_im, l0_s5_c_re,
           l0_s5_c_im, l0_s5_d, l0_s5_w_glu, l0_mla_q_norm, l0_mla_w_uq, l0_mla_kv_norm, l0_mla_w_ukv, l0_w_out,
           l0_ln1_g, l0_ln1_b, l0_ffn_w_gate, l0_ffn_w_up, l0_ffn_w_down, l0_ln2_g, l0_ln2_b, l1_w_in, l1_conv_w,
           l1_conv_b, l1_dt_bias, l1_a_log, l1_d, l1_norm_g, l1_w_out, l1_ln1_g, l1_ln1_b, l1_ffn_w_gate,
           l1_ffn_w_up, l1_ffn_w_down, l1_ln2_g, l1_ln2_b, loss_target, m_meta_tokens, m_l0_w_in, m_l0_s5_log_dt,
           m_l0_s5_a_re, m_l0_s5_a_im, m_l0_s5_b_re, m_l0_s5_b_im, m_l0_s5_c_re, m_l0_s5_c_im, m_l0_s5_d,
           m_l0_s5_w_glu, m_l0_mla_q_norm, m_l0_mla_w_uq, m_l0_mla_kv_norm, m_l0_mla_w_ukv, m_l0_w_out, m_l0_ln1_g,
           m_l0_ln1_b, m_l0_ffn_w_gate, m_l0_ffn_w_up, m_l0_ffn_w_down, m_l0_ln2_g, m_l0_ln2_b, m_l1_w_in,
           m_l1_conv_w, m_l1_conv_b, m_l1_dt_bias, m_l1_a_log, m_l1_d, m_l1_norm_g, m_l1_w_out, m_l1_ln1_g,
           m_l1_ln1_b, m_l1_ffn_w_gate, m_l1_ffn_w_up, m_l1_ffn_w_down, m_l1_ln2_g, m_l1_ln2_b, v_meta_tokens,
           v_l0_w_in, v_l0_s5_log_dt, v_l0_s5_a_re, v_l0_s5_a_im, v_l0_s5_b_re, v_l0_s5_b_im, v_l0_s5_c_re,
           v_l0_s5_c_im, v_l0_s5_d, v_l0_s5_w_glu, v_l0_mla_q_norm, v_l0_mla_w_uq, v_l0_mla_kv_norm,
           v_l0_mla_w_ukv, v_l0_w_out, v_l0_ln1_g, v_l0_ln1_b, v_l0_ffn_w_gate, v_l0_ffn_w_up, v_l0_ffn_w_down,
           v_l0_ln2_g, v_l0_ln2_b, v_l1_w_in, v_l1_conv_w, v_l1_conv_b, v_l1_dt_bias, v_l1_a_log, v_l1_d,
           v_l1_norm_g, v_l1_w_out, v_l1_ln1_g, v_l1_ln1_b, v_l1_ffn_w_gate, v_l1_ffn_w_up, v_l1_ffn_w_down,
           v_l1_ln2_g, v_l1_ln2_b):
    given = dict(locals())
    ws = {name: given[name] for name in WEIGHTS}
    ms = {name: given["m_" + name] for name in WEIGHTS}
    vs = {name: given["v_" + name] for name in WEIGHTS}
    return _step(FULL, x, loss_target, ws, ms, vs)
```

```python
import functools
import math

import numpy as np
import jax
import jax.numpy as jnp
from jax import lax
from jax.experimental import pallas as pl
from jax.experimental.pallas import tpu as pltpu

F32 = jnp.float32
BF16 = jnp.bfloat16
HI = lax.Precision.HIGHEST
MESH = pl.DeviceIdType.MESH

LANES = 128
SUBLANES = 8
VMEM_LIMIT_BYTES = 56 * 1024 * 1024

N_META = 16
CHUNK = 64
DEPTH = 2
DN_ALPHA = (2 * DEPTH) ** 0.25
LN_EPS = 1e-5
RMS_EPS = 1e-6
ROPE_BASE = 10000.0
S5_GROUP = 16
S5_STATE = 64
S5_GPB = 8
MLA_NOPE = 128
MLA_ROPE = 64
MLA_V = 128
MLA_QW = 256
ATT_SHIFT = 48
SSD_HEAD_DIM = 64
SSD_GROUPS = 8
SSD_STATE = 128
SSD_CONV = 4
SSD_BLOCK = 128
ADAM_LR = 0.001
ADAM_B1 = 0.9
ADAM_B2 = 0.999
ADAM_EPS = 1e-08
ADAM_WD = 0.01
ADAM_STEP = 10
NEG = -1e30


class _Cfg:
    def __init__(self, d_model, seq, row_tile, att_tile, scan_tiles, small_row_tile):
        d = d_model
        self.tr = small_row_tile
        self.d = d
        self.seq = seq
        self.n = seq + N_META
        lp = -(-(self.n + ATT_SHIFT) // row_tile) * row_tile
        self.lp = lp
        self.tm = row_tile
        self.tq = att_tile
        self.scan_tiles = scan_tiles
        self.s5_w = d // 2
        self.s5_g = self.s5_w // S5_GROUP
        self.s5_nb = self.s5_g // S5_GPB
        self.s5_c = self.s5_g * S5_STATE
        self.heads = d // 256
        self.q_rank = d // 4
        self.kv_rank = d // 8
        self.l0_in = self.s5_w + self.q_rank + self.kv_rank + MLA_ROPE
        self.l0_mix = self.s5_w + self.heads * MLA_V
        self.ssd_inner = 2 * d
        self.ssd_heads = self.ssd_inner // SSD_HEAD_DIM
        self.hpg = self.ssd_heads // SSD_GROUPS
        self.gw = self.hpg * SSD_HEAD_DIM
        self.gn = SSD_GROUPS * SSD_STATE
        self.conv_dim = self.ssd_inner + 2 * self.gn
        self.l1_in = self.ssd_inner + self.conv_dim + self.ssd_heads
        self.l1_inp = -(-self.l1_in // LANES) * LANES
        self.ffn = -(-(8 * d) // (3 * 256)) * 256
        assert lp % att_tile == 0 and lp % SSD_BLOCK == 0 and lp % (8 * scan_tiles) == 0


FULL = _Cfg(2048, 8192, 640, 640, 4, 160)


def _cparams(n_grid):
    return pltpu.CompilerParams(dimension_semantics=("arbitrary",) * n_grid,
                                vmem_limit_bytes=VMEM_LIMIT_BYTES)


def _div_tile(n, target, unit=LANES):
    if n <= target:
        return n
    best = None
    for t in range(unit, target + 1, unit):
        if n % t == 0:
            best = t
    return n if best is None else best


ANY_SPEC = pl.BlockSpec(memory_space=pl.ANY)


def _tile_call(fn, grid, ins, in_specs, outs, out_specs, name, acc=(), acc_axis=0, fill=None):
    n_in = len(ins)
    n_out = len(outs)
    acc = tuple(acc)
    aliases = {}
    if fill is not None:
        aliases = {n_in: fill[1]}
        ins = list(ins) + [fill[0]]
        in_specs = list(in_specs) + [ANY_SPEC]

    def body(*refs):
        refs = refs[:n_in] + refs[len(ins):]
        vals = fn(*[r[...] for r in refs[:n_in]])
        if not isinstance(vals, (tuple, list)):
            vals = (vals,)
        for k in range(n_out):
            r = refs[n_in + k]
            v = vals[k].astype(r.dtype)
            if k in acc:
                first = pl.program_id(acc_axis) == 0

                @pl.when(first)
                def _(r=r, v=v):
                    r[...] = v

                @pl.when(jnp.logical_not(first))
                def _(r=r, v=v):
                    r[...] += v
            else:
                r[...] = v

    res = pl.pallas_call(
        body, out_shape=[jax.ShapeDtypeStruct(s, d) for s, d in outs], grid=grid,
        in_specs=in_specs, out_specs=out_specs, name=name, compiler_params=_cparams(len(grid)),
        input_output_aliases=aliases,
    )(*ins)
    return res


def _rows(tm, c):
    return pl.BlockSpec((tm, c), lambda i: (i, 0))


def _whole(shape):
    nd = len(shape)
    return pl.BlockSpec(shape, lambda *a: (0,) * nd)


def _rowwise(fn, rows, params, outs, accs, tm, name):
    lp = rows[0].shape[0]
    n_row_out = len(outs)
    res = _tile_call(
        fn, (lp // tm,), list(rows) + list(params),
        [_rows(tm, r.shape[1]) for r in rows] + [_whole(p.shape) for p in params],
        [((lp, c), dt) for c, dt in outs] + [(s, F32) for s in accs],
        [_rows(tm, c) for c, _ in outs] + [_whole(s) for s in accs],
        name, acc=range(n_row_out, n_row_out + len(accs)))
    return res


_DIMS = {"nn": (((1,), (0,)), ((), ())), "nt": (((1,), (1,)), ((), ())), "tn": (((0,), (0,)), ((), ()))}


def _dot(a, b, mode="nn", precision=None):
    return lax.dot_general(a, b, _DIMS[mode], preferred_element_type=F32, precision=precision)


def _bdot(a, b, mode="nn"):
    return _dot(a.astype(BF16), b.astype(BF16), mode)


def _mm(a, b, mode, name, out_dtype=F32, out_slabs=None, b_col0=0, b_cols=None, tm_t=640, tn_t=1536, tk_t=2048):
    slab_b = b.ndim == 3
    if mode == "nn":
        m, k = a.shape
        k2, n, unit_n = (b.shape[1], b.shape[0] * b.shape[2], b.shape[2]) if slab_b else (b.shape[0], b.shape[1], b.shape[1])
        unit_k = k
    elif mode == "nt":
        m, k = a.shape
        n, k2, unit_k = (b.shape[1], b.shape[0] * b.shape[2], b.shape[2]) if slab_b else (b.shape[0], b.shape[1], b.shape[1])
        unit_n = n
    else:
        (k, m), k2 = a.shape, b.shape[0]
        n = b.shape[1] if b_cols is None else b_cols
        unit_n, unit_k = n, k
        tm_t = max(tm_t, 1024)
        tk_t = 1664 if a.dtype == BF16 and b.dtype == BF16 else 1024
    if out_slabs:
        unit_n = n // out_slabs
    assert k == k2, (name, a.shape, b.shape)
    tm, tn, tk = _div_tile(m, tm_t), _div_tile(unit_n, tn_t), _div_tile(unit_k, tk_t)
    nk = k // tk
    nps, kps = unit_n // tn, unit_k // tk
    c0 = b_col0 // tn
    assert b_col0 % tn == 0
    a_spec = {"nn": pl.BlockSpec((tm, tk), lambda i, j, kk: (i, kk)),
              "nt": pl.BlockSpec((tm, tk), lambda i, j, kk: (i, kk)),
              "tn": pl.BlockSpec((tk, tm), lambda i, j, kk: (kk, i))}[mode]
    if slab_b:
        b_spec = {"nn": pl.BlockSpec((None, tk, tn), lambda i, j, kk: (j // nps, kk, j % nps)),
                  "nt": pl.BlockSpec((None, tn, tk), lambda i, j, kk: (kk // kps, j, kk % kps))}[mode]
    else:
        b_spec = {"nn": pl.BlockSpec((tk, tn), lambda i, j, kk: (kk, j)),
                  "nt": pl.BlockSpec((tn, tk), lambda i, j, kk: (j, kk)),
                  "tn": pl.BlockSpec((tk, tn), lambda i, j, kk: (kk, j + c0))}[mode]
    if out_slabs:
        out_shape = jax.ShapeDtypeStruct((out_slabs, m, unit_n), out_dtype)
        out_spec = pl.BlockSpec((None, tm, tn), lambda i, j, kk: (j // nps, i, j % nps))
    else:
        out_shape = jax.ShapeDtypeStruct((m, n), out_dtype)
        out_spec = pl.BlockSpec((tm, tn), lambda i, j, kk: (i, j))

    def body(a_ref, b_ref, o_ref, acc_ref):
        part = _bdot(a_ref[...], b_ref[...], mode)
        if nk == 1:
            o_ref[...] = part.astype(o_ref.dtype)
        else:
            kk = pl.program_id(2)

            @pl.when(kk == 0)
            def _():
                acc_ref[...] = part

            @pl.when(kk > 0)
            def _():
                acc_ref[...] += part

            @pl.when(kk == nk - 1)
            def _():
                o_ref[...] = acc_ref[...].astype(o_ref.dtype)

    return pl.pallas_call(
        body, out_shape=out_shape, grid=(m // tm, n // tn, nk),
        in_specs=[a_spec, b_spec], out_specs=out_spec,
        scratch_shapes=[pltpu.VMEM((tm, tn) if nk > 1 else (SUBLANES, LANES), F32)],
        name=name, compiler_params=_cparams(3))(a, b)


def _layer_norm(r, g, b):
    mu = jnp.mean(r, axis=-1, keepdims=True)
    xc = r - mu
    var = jnp.mean(xc * xc, axis=-1, keepdims=True)
    return xc * lax.rsqrt(var + LN_EPS) * g + b


def _rms(x, g):
    return x * lax.rsqrt(jnp.mean(x * x, axis=-1, keepdims=True) + RMS_EPS) * g


def _sigmoid(x):
    return 1.0 / (1.0 + jnp.exp(-x))


def _silu(x):
    return x * _sigmoid(x)


def _gelu(x):
    return 0.5 * x * (1.0 + jnp.tanh(0.7978845608028654 * (x + 0.044715 * x * x * x)))


def _softplus(x):
    return jnp.maximum(x, 0.0) + jnp.log(1.0 + jnp.exp(-jnp.abs(x)))


def _ln_fwd(h, mo, g, b, cfg, name):
    def fn(h, mo, g, b):
        y = _layer_norm(DN_ALPHA * h + mo, g, b)
        return y, y
    return _rowwise(fn, [h, mo], [g, b], [(cfg.d, F32), (cfg.d, BF16)], [], cfg.tr, name)


def _ln_bwd(h, mo, g, b, douts, scales, cfg, name):
    def fn(h, mo, *rest):
        ds, (g, b) = rest[:-2], rest[-2:]
        dy = ds[0] * scales[0]
        for t, s in zip(ds[1:], scales[1:]):
            dy = dy + t * s
        _, vjp = jax.vjp(_layer_norm, DN_ALPHA * h + mo, g, b)
        dr, dg, db = vjp(dy)
        return dr, dr, dg, db
    d = cfg.d
    return _rowwise(fn, [h, mo] + list(douts), [g, b], [(d, F32), (d, BF16)], [(1, d), (1, d)], cfg.tr, name)


def _ffn_act(gu, cfg, name):
    f = cfg.ffn
    cb = _div_tile(f, 1536)
    nf = f // cb
    lp = gu.shape[0]
    tm = cfg.tm

    def fn(gate, up):
        return _silu(gate.astype(F32)) * up.astype(F32)
    return _tile_call(fn, (lp // tm, nf), [gu, gu],
                      [pl.BlockSpec((tm, cb), lambda i, j: (i, j)),
                       pl.BlockSpec((tm, cb), lambda i, j: (i, j + nf))],
                      [((lp, f), BF16)], [pl.BlockSpec((tm, cb), lambda i, j: (i, j))], name)[0]


def _ffn_act_bwd(gu, dact, cfg, name):
    f = cfg.ffn
    cb = _div_tile(f, 1536)
    nf = f // cb
    lp = gu.shape[0]
    tm = cfg.tm

    def body(gate_ref, up_ref, da_ref, out_ref):
        gate, da = gate_ref[...].astype(F32), da_ref[...].astype(F32)
        sg = _sigmoid(gate)
        j = pl.program_id(1)

        @pl.when(j < nf)
        def _():
            out_ref[...] = (da * up_ref[...].astype(F32) * sg * (1.0 + gate * (1.0 - sg))).astype(BF16)

        @pl.when(j >= nf)
        def _():
            out_ref[...] = (da * gate * sg).astype(BF16)

    return pl.pallas_call(
        body, out_shape=jax.ShapeDtypeStruct((lp, 2 * f), BF16), grid=(lp // tm, 2 * nf),
        in_specs=[pl.BlockSpec((tm, cb), lambda i, j: (i, j % nf)),
                  pl.BlockSpec((tm, cb), lambda i, j: (i, j % nf + nf)),
                  pl.BlockSpec((tm, cb), lambda i, j: (i, j % nf))],
        out_specs=pl.BlockSpec((tm, cb), lambda i, j: (i, j)), name=name, compiler_params=_cparams(2))(gu, gu, dact)


def _ffn_fwd(h, w_gu, w_down, cfg, tag):
    gu = _mm(h, w_gu, "nn", tag + "_gu", out_dtype=BF16)
    act = _ffn_act(gu, cfg, tag + "_act")
    fo = _mm(act, w_down, "nn", tag + "_down")
    return fo, (gu, act)


def _ffn_bwd(h, saved, dfo, w_gu, w_down, cfg, tag):
    gu, act = saved
    dact = _mm(dfo, w_down, "nt", tag + "_dact", out_dtype=BF16)
    d_wdown = _mm(act, dfo, "tn", tag + "_dwdown", out_dtype=BF16)
    dgu = _ffn_act_bwd(gu, dact, cfg, tag + "_dgu")
    dh = _mm(dgu, w_gu, "nt", tag + "_dh", tn_t=2048)
    d_wgu = _mm(h, dgu, "tn", tag + "_dwgu", out_dtype=BF16, out_slabs=w_gu.shape[0])
    return dh, d_wgu, d_wdown


def _small_call(fn, ins, outs, name):
    return _tile_call(fn, (1,), ins, [_whole(x.shape) for x in ins], [(s, F32) for s in outs],
                      [_whole(s) for s in outs], name)


def _perm(x):
    lp, c = x.shape
    return x.reshape(SUBLANES, lp // SUBLANES, c).transpose(1, 0, 2).reshape(lp, c)


def _unperm(x):
    lp, c = x.shape
    return x.reshape(lp // SUBLANES, SUBLANES, c).transpose(1, 0, 2).reshape(lp, c)


def _s5_disc(log_dt, a_re, a_im):
    dt = jnp.exp(log_dt)
    mag = jnp.exp(dt * a_re)
    ab_re = mag * jnp.cos(dt * a_im)
    ab_im = mag * jnp.sin(dt * a_im)
    den = a_re * a_re + a_im * a_im
    nr = ab_re - 1.0
    f_re = (nr * a_re + ab_im * a_im) / den
    f_im = (ab_im * a_re - nr * a_im) / den
    return ab_re, ab_im, f_re, f_im


def _s5_bbar(f_re, f_im, b_re, b_im):
    return f_re * b_re - f_im * b_im, f_re * b_im + f_im * b_re


def _bd_from(w, cfg):
    g, p, j = w.shape
    w4 = w.reshape(cfg.s5_nb, S5_GPB, p, j)
    eye = jnp.eye(S5_GPB, dtype=w.dtype)
    return jnp.einsum("bgpj,gh->bgjhp", w4, eye).reshape(cfg.s5_nb, S5_GPB * j, S5_GPB * p)


def _bd_to(blocks, cfg, p, j):
    b5 = blocks.reshape(cfg.s5_nb, S5_GPB, j, S5_GPB, p)
    eye = jnp.eye(S5_GPB, dtype=blocks.dtype)
    return jnp.einsum("bgjhp,gh->bgpj", b5, eye).reshape(cfg.s5_g, p, j)


def _bd_split(x, w1, w2, cfg, name):
    nb, ci, co = w1.shape
    lp, tm = x.shape[0], cfg.tm

    def fn(x, w1, w2):
        xb = x.astype(BF16)
        return _dot(xb, w1[0].astype(BF16)), _dot(xb, w2[0].astype(BF16))
    wspec = pl.BlockSpec((1, ci, co), lambda i, j: (j, 0, 0))
    ospec = pl.BlockSpec((tm, co), lambda i, j: (i, j))
    return _tile_call(fn, (lp // tm, nb), [x, w1, w2],
                      [pl.BlockSpec((tm, ci), lambda i, j: (i, j)), wspec, wspec],
                      [((lp, nb * co), F32)] * 2, [ospec, ospec], name)


def _bd_join(x1, x2, w1, w2, extra, scale, cfg, name):
    nb, ci, co = w1.shape
    lp, tm = x1.shape[0], cfg.tm

    def fn(x1, x2, w1, w2, e, s):
        return _bdot(x1, w1[0]) + _bdot(x2, w2[0]) + e * s
    xspec = pl.BlockSpec((tm, ci), lambda i, j: (i, j))
    wspec = pl.BlockSpec((1, ci, co), lambda i, j: (j, 0, 0))
    ospec = pl.BlockSpec((tm, co), lambda i, j: (i, j))
    return _tile_call(fn, (lp // tm, nb), [x1, x2, w1, w2, extra, scale],
                      [xspec, xspec, wspec, wspec, ospec, pl.BlockSpec((1, co), lambda i, j: (0, j))],
                      [((lp, nb * co), F32)], [ospec], name)[0]


def _bd_tn(a, b, nb, cfg, name):
    lp, tk = a.shape[0], cfg.tm
    ca, cb = a.shape[1] // nb, b.shape[1] // nb

    def fn(a, b):
        return _bdot(a, b, "tn")[None]
    return _tile_call(fn, (nb, lp // tk), [a, b],
                      [pl.BlockSpec((tk, ca), lambda j, k: (k, j)), pl.BlockSpec((tk, cb), lambda j, k: (k, j))],
                      [((nb, ca, cb), F32)], [pl.BlockSpec((1, ca, cb), lambda j, k: (j, 0, 0))],
                      name, acc=(0,), acc_axis=1)[0]


def _s5_scan(bu_re, bu_im, ab_re, ab_im, cfg, reverse, tag):
    lp, c = bu_re.shape
    nt = cfg.scan_tiles
    rows = lp // nt
    steps = rows // SUBLANES
    cb = _div_tile(c, 512)
    tmap = (lambda j, t: (nt - 1 - t, j)) if reverse else (lambda j, t: (t, j))
    row_spec = pl.BlockSpec((rows, cb), tmap)
    par_spec = pl.BlockSpec((1, cb), lambda j, t: (0, j))
    st_spec = pl.BlockSpec((SUBLANES, cb), lambda j, t: (0, j))
    grid = (c // cb, nt)
    full = jax.ShapeDtypeStruct((lp, c), F32)
    small = jax.ShapeDtypeStruct((SUBLANES, c), F32)

    def offset(k):
        kk = steps - 1 - k if reverse else k
        return pl.multiple_of(kk * SUBLANES, SUBLANES)

    def local_body(bre, bim, ar_ref, ai_ref, sre, sim, fre, fim, pre, pim, st):
        t = pl.program_id(1)

        @pl.when(t == 0)
        def _():
            zero = jnp.zeros((SUBLANES, cb), F32)
            st[0] = zero
            st[1] = zero
            st[2] = zero + 1.0
            st[3] = zero
        ar = jnp.broadcast_to(ar_ref[...], (SUBLANES, cb))
        ai = jnp.broadcast_to(ai_ref[...], (SUBLANES, cb))

        def step(k, carry):
            s_r, s_i, p_r, p_i = carry
            off = offset(k)
            n_r = ar * s_r - ai * s_i + bre[pl.ds(off, SUBLANES), :]
            n_i = ar * s_i + ai * s_r + bim[pl.ds(off, SUBLANES), :]
            sre[pl.ds(off, SUBLANES), :] = n_r
            sim[pl.ds(off, SUBLANES), :] = n_i
            return n_r, n_i, ar * p_r - ai * p_i, ar * p_i + ai * p_r
        s_r, s_i, p_r, p_i = lax.fori_loop(0, steps, step, (st[0], st[1], st[2], st[3]))
        st[0] = s_r
        st[1] = s_i
        st[2] = p_r
        st[3] = p_i

        @pl.when(t == nt - 1)
        def _():
            fre[...] = s_r
            fim[...] = s_i
            pre[...] = p_r
            pim[...] = p_i

    loc_re, loc_im, f_re, f_im, pn_re, pn_im = pl.pallas_call(
        local_body, out_shape=[full, full, small, small, small, small], grid=grid,
        in_specs=[row_spec, row_spec, par_spec, par_spec],
        out_specs=[row_spec, row_spec, st_spec, st_spec, st_spec, st_spec],
        scratch_shapes=[pltpu.VMEM((4, SUBLANES, cb), F32)],
        name=tag + "_local", compiler_params=_cparams(2))(bu_re, bu_im, ab_re, ab_im)

    def fix_body(lre, lim, fre, fim, pre, pim, ar_ref, ai_ref, sre, sim, st):
        t = pl.program_id(1)
        ar = jnp.broadcast_to(ar_ref[...], (SUBLANES, cb))
        ai = jnp.broadcast_to(ai_ref[...], (SUBLANES, cb))

        @pl.when(t == 0)
        def _():
            f_r, f_i = fre[...], fim[...]
            n_r, n_i = pre[0:1, :], pim[0:1, :]
            row = lax.broadcasted_iota(jnp.int32, (SUBLANES, cb), 0)
            c_r = jnp.zeros((1, cb), F32)
            c_i = jnp.zeros((1, cb), F32)
            car_r = jnp.zeros((SUBLANES, cb), F32)
            car_i = jnp.zeros((SUBLANES, cb), F32)
            order = range(SUBLANES - 2, -1, -1) if reverse else range(1, SUBLANES)
            for i in order:
                src = i + 1 if reverse else i - 1
                c_r, c_i = (n_r * c_r - n_i * c_i + f_r[src:src + 1, :],
                            n_r * c_i + n_i * c_r + f_i[src:src + 1, :])
                car_r = jnp.where(row == i, c_r, car_r)
                car_i = jnp.where(row == i, c_i, car_i)
            st[0] = car_r
            st[1] = car_i
            st[2] = ar
            st[3] = ai
        car_r = st[0]
        car_i = st[1]

        def step(k, carry):
            p_r, p_i = carry
            off = offset(k)
            sre[pl.ds(off, SUBLANES), :] = lre[pl.ds(off, SUBLANES), :] + p_r * car_r - p_i * car_i
            sim[pl.ds(off, SUBLANES), :] = lim[pl.ds(off, SUBLANES), :] + p_r * car_i + p_i * car_r
            return ar * p_r - ai * p_i, ar * p_i + ai * p_r
        p_r, p_i = lax.fori_loop(0, steps, step, (st[2], st[3]))
        st[2] = p_r
        st[3] = p_i

    return pl.pallas_call(
        fix_body, out_shape=[full, full], grid=grid,
        in_specs=[row_spec, row_spec, st_spec, st_spec, st_spec, st_spec, par_spec, par_spec],
        out_specs=[row_spec, row_spec], scratch_shapes=[pltpu.VMEM((4, SUBLANES, cb), F32)],
        name=tag + "_fix", compiler_params=_cparams(2))(loc_re, loc_im, f_re, f_im, pn_re, pn_im, ab_re, ab_im)


def _swap(x):
    return jnp.swapaxes(x, -1, -2)


def _s5_prep(w, cfg):
    g, p, j = cfg.s5_g, S5_STATE, S5_GROUP
    gp = g * p
    log_dt = w["l0_s5_log_dt"].reshape(g, 1)
    ab_re, ab_im, f_re, f_im = _small_call(_s5_disc, [log_dt, w["l0_s5_a_re"], w["l0_s5_a_im"]],
                                           [(g, p)] * 4, "s5_disc")
    b_re2 = w["l0_s5_b_re"].transpose(2, 0, 1).reshape(j, gp)
    b_im2 = w["l0_s5_b_im"].transpose(2, 0, 1).reshape(j, gp)
    f_re1, f_im1 = f_re.reshape(1, gp), f_im.reshape(1, gp)
    bb_re2, bb_im2 = _small_call(_s5_bbar, [f_re1, f_im1, b_re2, b_im2], [(j, gp)] * 2, "s5_bbar")
    bb_re = _bd_from(bb_re2.reshape(j, g, p).transpose(1, 2, 0), cfg).astype(BF16)
    bb_im = _bd_from(bb_im2.reshape(j, g, p).transpose(1, 2, 0), cfg).astype(BF16)
    c_re_t = _bd_from(w["l0_s5_c_re"].transpose(0, 2, 1), cfg).astype(BF16)
    c_imn_t = _bd_from(-w["l0_s5_c_im"].transpose(0, 2, 1), cfg).astype(BF16)
    return dict(log_dt=log_dt, f_re1=f_re1, f_im1=f_im1, b_re2=b_re2, b_im2=b_im2,
                ab_re=ab_re.reshape(1, gp), ab_im=ab_im.reshape(1, gp),
                bb_re=bb_re, bb_im=bb_im, bb_re_t=_swap(bb_re), bb_im_t=_swap(bb_im),
                c_re=_swap(c_re_t), c_imn=_swap(c_imn_t), c_re_t=c_re_t, c_imn_t=c_imn_t,
                d=w["l0_s5_d"].reshape(1, cfg.s5_w))


def _s5_fwd(u, prm, w_glu, cfg):
    tm = cfg.tr
    up = _perm(u)
    bu_re, bu_im = _bd_split(up, prm["bb_re"], prm["bb_im"], cfg, "s5_bu")
    s_re, s_im = _s5_scan(bu_re, bu_im, prm["ab_re"], prm["ab_im"], cfg, False, "s5_scan")
    y = _bd_join(s_re, s_im, prm["c_re"], prm["c_imn"], up, prm["d"], cfg, "s5_y")
    g = _rowwise(_gelu, [y], [], [(cfg.s5_w, F32)], [], tm, "s5_gelu")[0]
    z = _mm(g, w_glu, "nn", "s5_glu_mm")
    a_out = _rowwise(lambda g, z: g * _sigmoid(z), [g, z], [], [(cfg.s5_w, F32)], [], tm, "s5_glu")[0]
    return _unperm(a_out), (up, s_re, s_im, y, g, z)


def _s5_bwd(d_a_out, saved, prm, w, w_glu, cfg):
    up, s_re, s_im, y, g, z = saved
    tm, sw, nb = cfg.tr, cfg.s5_w, cfg.s5_nb
    gs, p, j = cfg.s5_g, S5_STATE, S5_GROUP
    gp = gs * p
    dap = _perm(d_a_out)

    def glu_bwd(da, g, z):
        sg = _sigmoid(z)
        return da * sg, da * g * sg * (1.0 - sg)
    dg1, dz = _rowwise(glu_bwd, [dap, g, z], [], [(sw, F32)] * 2, [], tm, "s5_glu_bwd")
    d_wglu = _mm(g, dz, "tn", "s5_dwglu", out_dtype=BF16)
    dg2 = _mm(dz, w_glu, "nt", "s5_dg2")

    def gelu_bwd(dg1, dg2, y, up, d):
        _, vjp = jax.vjp(_gelu, y)
        dy = vjp(dg1 + dg2)[0]
        return dy, dy * d, jnp.sum(dy * up, axis=0, keepdims=True)
    dy, dup_direct, dd = _rowwise(gelu_bwd, [dg1, dg2, y, up], [prm["d"]], [(sw, F32)] * 2, [(1, sw)], tm,
                                  "s5_gelu_bwd")
    ds_re, ds_im = _bd_split(dy, prm["c_re_t"], prm["c_imn_t"], cfg, "s5_ds")
    dc_re_t = _bd_tn(dy, s_re, nb, cfg, "s5_dcre")
    dc_imn_t = _bd_tn(dy, s_im, nb, cfg, "s5_dcim")
    g_re, g_im = _s5_scan(ds_re, ds_im, prm["ab_re"], -prm["ab_im"], cfg, True, "s5_adj")

    cb = _div_tile(gp, 512)
    per = tm // SUBLANES
    spec = pl.BlockSpec((tm, cb), lambda jj, i: (i, jj))
    before = pl.BlockSpec((SUBLANES, cb), lambda jj, i: (jnp.maximum(i * per - 1, 0), jj))
    final = pl.BlockSpec((SUBLANES, cb), lambda jj, i: (cfg.lp // SUBLANES - 1, jj))
    aspec = pl.BlockSpec((1, cb), lambda jj, i: (0, jj))

    def dab(g_r, g_i, s_r, s_i, h_r, h_i, l_r, l_i):
        first = pl.program_id(1) == 0
        row8 = lax.broadcasted_iota(jnp.int32, (SUBLANES, cb), 0)

        def prev(s, h, l):
            wrap = jnp.where(row8 == 0, 0.0, pltpu.roll(l, 1, axis=0))
            return jnp.concatenate([jnp.where(first, wrap, h), s[:tm - SUBLANES]], axis=0)
        p_r, p_i = prev(s_r, h_r, l_r), prev(s_i, h_i, l_i)
        return (jnp.sum(g_r * p_r + g_i * p_i, axis=0, keepdims=True),
                jnp.sum(g_i * p_r - g_r * p_i, axis=0, keepdims=True))
    dab_re, dab_im = _tile_call(dab, (gp // cb, cfg.lp // tm), [g_re, g_im, s_re, s_im, s_re, s_im, s_re, s_im],
                                [spec] * 4 + [before] * 2 + [final] * 2,
                                [((1, gp), F32)] * 2, [aspec] * 2, "s5_dab", acc=(0, 1), acc_axis=1)
    no_scale = jnp.ones((1, sw), F32)
    dup = _bd_join(g_re, g_im, prm["bb_re_t"], prm["bb_im_t"], dup_direct, no_scale, cfg, "s5_dup")
    dbb_re_blk = _bd_tn(up, g_re, nb, cfg, "s5_dbbre")
    dbb_im_blk = _bd_tn(up, g_im, nb, cfg, "s5_dbbim")

    def to2(blk):
        return _bd_to(blk, cfg, p, j).transpose(2, 0, 1).reshape(j, gp)

    def bbar_bwd(f_re, f_im, b_re, b_im, dr, di):
        _, vjp = jax.vjp(_s5_bbar, f_re, f_im, b_re, b_im)
        return vjp((dr, di))
    df_re, df_im, db_re2, db_im2 = _small_call(
        bbar_bwd, [prm["f_re1"], prm["f_im1"], prm["b_re2"], prm["b_im2"], to2(dbb_re_blk), to2(dbb_im_blk)],
        [(1, gp), (1, gp), (j, gp), (j, gp)], "s5_bbar_bwd")

    def disc_bwd(log_dt, a_re, a_im, d1, d2, d3, d4):
        _, vjp = jax.vjp(_s5_disc, log_dt, a_re, a_im)
        return vjp((d1, d2, d3, d4))
    dlog_dt, da_re, da_im = _small_call(
        disc_bwd, [prm["log_dt"], w["l0_s5_a_re"], w["l0_s5_a_im"], dab_re.reshape(gs, p), dab_im.reshape(gs, p),
                   df_re.reshape(gs, p), df_im.reshape(gs, p)], [(gs, 1), (gs, p), (gs, p)], "s5_disc_bwd")
    grads = {
        "l0_s5_log_dt": dlog_dt.reshape(gs), "l0_s5_a_re": da_re, "l0_s5_a_im": da_im,
        "l0_s5_b_re": db_re2.reshape(j, gs, p).transpose(1, 2, 0),
        "l0_s5_b_im": db_im2.reshape(j, gs, p).transpose(1, 2, 0),
        "l0_s5_c_re": _bd_to(dc_re_t, cfg, p, j).transpose(0, 2, 1),
        "l0_s5_c_im": -_bd_to(dc_imn_t, cfg, p, j).transpose(0, 2, 1),
        "l0_s5_d": dd.reshape(sw), "l0_s5_w_glu": d_wglu,
    }
    return _unperm(dup), grads


def _shift(x):
    return jnp.concatenate([jnp.zeros((ATT_SHIFT, x.shape[1]), x.dtype), x[:-ATT_SHIFT]], axis=0)


def _unshift(x):
    return jnp.concatenate([x[ATT_SHIFT:], jnp.zeros((ATT_SHIFT, x.shape[1]), x.dtype)], axis=0)


def _rope_tables(cfg):
    pos = (jnp.arange(cfg.lp) - ATT_SHIFT).astype(F32)
    inv = ROPE_BASE ** (-jnp.arange(0, MLA_ROPE, 2, dtype=F32) / MLA_ROPE)
    ang = pos[:, None] * inv[None, :]
    cos, sin = jnp.cos(ang), jnp.sin(ang)
    z = jnp.zeros((cfg.lp, LANES - MLA_ROPE), F32)
    return jnp.concatenate([cos, cos, z], axis=1), jnp.concatenate([-sin, sin, z], axis=1)


def _swap_halves(x):
    half = MLA_ROPE // 2
    lane = lax.broadcasted_iota(jnp.int32, x.shape, 1)
    left = pltpu.roll(x, LANES - half, axis=1)
    right = pltpu.roll(x, half, axis=1)
    return jnp.where(lane < half, left, jnp.where(lane < 2 * half, right, 0.0))


def _rope(x, cosp, sinp):
    return x * cosp + _swap_halves(x) * sinp


def _rope_t(dy, cosp, sinp):
    return dy * cosp + _swap_halves(dy * sinp)


def _visible(i, j, t):
    row = i * t + lax.broadcasted_iota(jnp.int32, (t, t), 0)
    col = j * t + lax.broadcasted_iota(jnp.int32, (t, t), 1)
    return jnp.logical_and(col // CHUNK <= row // CHUNK, col >= ATT_SHIFT)


class _NoPlan:
    n = n_out = 0
    arrays, out_shapes, scratch = [], [], []


def _side_refs(refs, n_in, n_out, n_scratch, side):
    a = n_in + side.n
    b = a + n_out + side.n_out
    c = b + n_scratch
    own = refs[:n_in] + refs[a:a + n_out] + refs[b:c]
    return own, refs[n_in:a] + refs[a + n_out:b] + refs[c:]


def _needs_mask(i, j):
    return jnp.logical_or(i == j, j == 0)


def _flash_fwd(q, kv, kr, cfg, side=None):
    lp, t, nh = cfg.lp, cfg.tq, cfg.heads
    n = lp // t
    scale = (MLA_NOPE + MLA_ROPE) ** -0.5
    side = side or _NoPlan()

    def body(*refs):
        (q_ref, kv_ref, kr_ref, o_ref, lse_ref, m_s, l_s, acc_s), ex = _side_refs(refs, 3, 2, 3, side)
        hh, i, j = pl.program_id(0), pl.program_id(1), pl.program_id(2)
        if side.n:
            at_tile0 = jnp.logical_and(i == 0, j == 0)
            pl.when(jnp.logical_and(hh == 0, at_tile0))(lambda: side.start(ex))
            pl.when(jnp.logical_and(hh == nh // 2, at_tile0))(lambda: side.relay(ex))

        @pl.when(j == 0)
        def _():
            m_s[...] = jnp.full((t, 1), NEG, F32)
            l_s[...] = jnp.zeros((t, 1), F32)
            acc_s[...] = jnp.zeros((t, MLA_V), F32)

        def tile(masked):
            s = (_dot(q_ref[:, :MLA_NOPE], kv_ref[:, :MLA_NOPE], "nt")
                 + _dot(q_ref[:, MLA_NOPE:], kr_ref[...], "nt")) * scale
            if masked:
                s = jnp.where(_visible(i, j, t), s, NEG)
            m_old = m_s[...]
            m_new = jnp.maximum(m_old, jnp.max(s, axis=1, keepdims=True))
            alpha = jnp.exp(m_old - m_new)
            p = jnp.exp(s - m_new)
            l_s[...] = alpha * l_s[...] + jnp.sum(p, axis=1, keepdims=True)
            acc_s[...] = alpha * acc_s[...] + _dot(p.astype(BF16), kv_ref[:, MLA_NOPE:])
            m_s[...] = m_new
        pl.when(jnp.logical_and(j <= i, _needs_mask(i, j)))(lambda: tile(True))
        pl.when(jnp.logical_and(j < i, j > 0))(lambda: tile(False))

        @pl.when(j == i)
        def _():
            o_ref[...] = acc_s[...] / l_s[...]
            lse_ref[...] = jnp.broadcast_to(m_s[...] + jnp.log(l_s[...]), (t, MLA_V))

        if side.n:
            pl.when(jnp.logical_and(hh == nh - 1, jnp.logical_and(i == n - 1, j == n - 1)))(lambda: side.finish(ex))

    res = pl.pallas_call(
        body, out_shape=[jax.ShapeDtypeStruct((lp, nh * MLA_V), F32)] * 2 + side.out_shapes, grid=(nh, n, n),
        in_specs=[pl.BlockSpec((t, MLA_QW), lambda h, i, j: (i, h)),
                  pl.BlockSpec((t, MLA_QW), lambda h, i, j: (jnp.minimum(i, j), h)),
                  pl.BlockSpec((t, LANES), lambda h, i, j: (jnp.minimum(i, j), 0))] + [HBM_SPEC] * side.n,
        out_specs=[pl.BlockSpec((t, MLA_V), lambda h, i, j: (i, h))] * 2 + [HBM_SPEC] * side.n_out,
        scratch_shapes=[pltpu.VMEM((t, 1), F32), pltpu.VMEM((t, 1), F32), pltpu.VMEM((t, MLA_V), F32)] + side.scratch,
        name="mla_flash_fwd", compiler_params=_cparams(3))(q, kv, kr, *side.arrays)
    return res[0], res[1], res[2:]


def _flash_bwd(q, kv, kr, o, lse, do, cfg, side=None):
    lp, t, nh = cfg.lp, cfg.tq, cfg.heads
    n = lp // t
    scale = (MLA_NOPE + MLA_ROPE) ** -0.5
    side = side or _NoPlan()

    def body(*refs):
        own, ex = _side_refs(refs, 6, 3, 2, side)
        q_ref, kv_ref, kr_ref, o_ref, lse_ref, do_ref, dq_ref, dkv_ref, dkr_ref, dkv_s, dkr_s = own
        hh, j, i = pl.program_id(0), pl.program_id(1), pl.program_id(2)
        if side.n:
            pl.when(jnp.logical_and(hh == 0, jnp.logical_and(i == 0, j == 0)))(lambda: side.start(ex))

        @pl.when(jnp.logical_and(j == 0, i == 0))
        def _():
            dq_ref[...] = jnp.zeros((lp, MLA_QW), F32)

        @pl.when(i == j)
        def _():
            dkv_s[...] = jnp.zeros((t, MLA_QW), F32)
            dkr_s[...] = jnp.zeros((t, LANES), F32)

        def tile(masked):
            qn, qr = q_ref[:, :MLA_NOPE], q_ref[:, MLA_NOPE:]
            kn, v = kv_ref[:, :MLA_NOPE], kv_ref[:, MLA_NOPE:]
            krv = kr_ref[...]
            s = (_dot(qn, kn, "nt") + _dot(qr, krv, "nt")) * scale
            p = jnp.exp(s - lse_ref[:, :1])
            if masked:
                p = jnp.where(_visible(i, j, t), p, 0.0)
            dov = do_ref[...]
            dob = dov.astype(BF16)
            dp = _dot(dob, v, "nt")
            delta = jnp.sum(dov * o_ref[...], axis=1, keepdims=True)
            ds = (p * (dp - delta) * scale).astype(BF16)
            dkv_s[:, MLA_NOPE:] += _dot(p.astype(BF16), dob, "tn")
            dkv_s[:, :MLA_NOPE] += _dot(ds, qn, "tn")
            dkr_s[...] += _dot(ds, qr, "tn")
            off = pl.multiple_of(i * t, t)
            dq_ref[pl.ds(off, t), :MLA_NOPE] += _dot(ds, kn)
            dq_ref[pl.ds(off, t), MLA_NOPE:] += _dot(ds, krv)
        pl.when(jnp.logical_and(i >= j, _needs_mask(i, j)))(lambda: tile(True))
        pl.when(jnp.logical_and(i > j, j > 0))(lambda: tile(False))

        @pl.when(i == n - 1)
        def _():
            dkv_ref[...] = dkv_s[...]
            dkr_ref[0] = dkr_s[...]

        if side.n:
            pl.when(jnp.logical_and(hh == nh - 1, jnp.logical_and(i == n - 1, j == n - 1)))(lambda: side.finish(ex))

    qspec = pl.BlockSpec((t, MLA_QW), lambda h, j, i: (jnp.maximum(i, j), h))
    ospec = pl.BlockSpec((t, MLA_V), lambda h, j, i: (jnp.maximum(i, j), h))
    res = pl.pallas_call(
        body, out_shape=[jax.ShapeDtypeStruct((lp, nh * MLA_QW), F32), jax.ShapeDtypeStruct((lp, nh * MLA_QW), F32),
                         jax.ShapeDtypeStruct((nh, lp, LANES), F32)] + side.out_shapes, grid=(nh, n, n),
        in_specs=[qspec, pl.BlockSpec((t, MLA_QW), lambda h, j, i: (j, h)),
                  pl.BlockSpec((t, LANES), lambda h, j, i: (j, 0)), ospec, ospec, ospec] + [HBM_SPEC] * side.n,
        out_specs=[pl.BlockSpec((lp, MLA_QW), lambda h, j, i: (0, h)),
                   pl.BlockSpec((t, MLA_QW), lambda h, j, i: (j, h)),
                   pl.BlockSpec((1, t, LANES), lambda h, j, i: (h, j, 0))] + [HBM_SPEC] * side.n_out,
        scratch_shapes=[pltpu.VMEM((t, MLA_QW), F32), pltpu.VMEM((t, LANES), F32)] + side.scratch,
        name="mla_flash_bwd", compiler_params=_cparams(3))(q, kv, kr, o, lse, do, *side.arrays)
    return res[0], res[1], res[2], res[3:]


def _pad_heads(w, nh, width):
    r = w.shape[0]
    w3 = w.reshape(r, nh, width)
    return jnp.pad(w3, ((0, 0), (0, 0), (0, MLA_QW - width))).reshape(r, nh * MLA_QW)


def _mla_fwd(q_lat, kv_lat, k_rope_raw, wq, w_uq_p, w_ukv, cfg, side=None):
    tm, nh = cfg.tr, cfg.heads
    ql, kl = _shift(q_lat), _shift(kv_lat)
    kr_raw = jnp.pad(_shift(k_rope_raw), ((0, 0), (0, LANES - MLA_ROPE)))
    cosp, sinp = _rope_tables(cfg)
    qg, kg = wq["l0_mla_q_norm"].reshape(1, -1), wq["l0_mla_kv_norm"].reshape(1, -1)
    qn, kvn = _rowwise(lambda a, b, g1, g2: (_rms(a, g1), _rms(b, g2)), [ql, kl], [qg, kg],
                       [(cfg.q_rank, F32), (cfg.kv_rank, F32)], [], tm, "mla_norm")
    q0 = _mm(qn, w_uq_p, "nn", "mla_q")
    kv = _mm(kvn, w_ukv, "nn", "mla_kv", out_dtype=BF16)

    def rope_fn(q0, kr, cosp, sinp):
        parts = []
        for h in range(nh):
            parts.append(q0[:, h * MLA_QW:h * MLA_QW + MLA_NOPE])
            parts.append(_rope(q0[:, h * MLA_QW + MLA_NOPE:(h + 1) * MLA_QW], cosp, sinp))
        return jnp.concatenate(parts, axis=1), _rope(kr, cosp, sinp)
    q, kr = _rowwise(rope_fn, [q0, kr_raw, cosp, sinp], [], [(nh * MLA_QW, BF16), (LANES, BF16)], [], tm,
                     "mla_rope")
    o, lse, side_out = _flash_fwd(q, kv, kr, cfg, side)
    return _unshift(o), (ql, kl, qn, kvn, q, kv, kr, o, lse, cosp, sinp), side_out


def _mla_bwd(d_b_out, saved, wq, w_uq_p, w_ukv, cfg, side=None):
    ql, kl, qn, kvn, q, kv, kr, o, lse, cosp, sinp = saved
    tm, nh, lp = cfg.tr, cfg.heads, cfg.lp
    dq, dkv, dkr_h, side_out = _flash_bwd(q, kv, kr, o, lse, _shift(d_b_out), cfg, side)

    def rope_bwd(dq, dkr_h, cosp, sinp):
        parts = []
        for h in range(nh):
            parts.append(dq[:, h * MLA_QW:h * MLA_QW + MLA_NOPE])
            parts.append(_rope_t(dq[:, h * MLA_QW + MLA_NOPE:(h + 1) * MLA_QW], cosp, sinp))
        dkr = dkr_h[0]
        for h in range(1, nh):
            dkr = dkr + dkr_h[h]
        return jnp.concatenate(parts, axis=1), _rope_t(dkr, cosp, sinp)
    dq0, dkr_raw = _tile_call(
        rope_bwd, (lp // tm,), [dq, dkr_h, cosp, sinp],
        [_rows(tm, nh * MLA_QW), pl.BlockSpec((nh, tm, LANES), lambda i: (0, i, 0)), _rows(tm, LANES),
         _rows(tm, LANES)],
        [((lp, nh * MLA_QW), F32), ((lp, LANES), F32)], [_rows(tm, nh * MLA_QW), _rows(tm, LANES)], "mla_rope_bwd")
    d_wuq_p = _mm(qn, dq0, "tn", "mla_dwuq")
    dqn = _mm(dq0, w_uq_p, "nt", "mla_dqn")
    d_wukv = _mm(kvn, dkv, "tn", "mla_dwukv")
    dkvn = _mm(dkv, w_ukv, "nt", "mla_dkvn")
    qg, kg = wq["l0_mla_q_norm"].reshape(1, -1), wq["l0_mla_kv_norm"].reshape(1, -1)

    def norm_bwd(ql, kl, dqn, dkvn, g1, g2):
        _, vjp1 = jax.vjp(_rms, ql, g1)
        _, vjp2 = jax.vjp(_rms, kl, g2)
        dql, dg1 = vjp1(dqn)
        dkl, dg2 = vjp2(dkvn)
        return dql, dkl, dg1, dg2
    dql, dkl, dg1, dg2 = _rowwise(norm_bwd, [ql, kl, dqn, dkvn], [qg, kg],
                                  [(cfg.q_rank, F32), (cfg.kv_rank, F32)], [(1, cfg.q_rank), (1, cfg.kv_rank)], tm,
                                  "mla_norm_bwd")
    width = MLA_NOPE + MLA_ROPE
    d_wuq = d_wuq_p.reshape(cfg.q_rank, nh, MLA_QW)[:, :, :width].reshape(cfg.q_rank, nh * width)
    grads = {"l0_mla_q_norm": dg1.reshape(-1), "l0_mla_kv_norm": dg2.reshape(-1), "l0_mla_w_uq": d_wuq,
             "l0_mla_w_ukv": d_wukv}
    return _unshift(dql), _unshift(dkl), _unshift(dkr_raw[:, :MLA_ROPE]), grads, side_out


def _conv_taps(x, halo, first):
    halo = jnp.where(first, 0.0, halo)
    row8 = lax.broadcasted_iota(jnp.int32, halo.shape, 0)
    taps = []
    for s in range(SSD_CONV - 1, 0, -1):
        r = pltpu.roll(x, s, axis=0)
        top = jnp.where(row8 < s, pltpu.roll(halo, s, axis=0), r[:SUBLANES])
        taps.append(jnp.concatenate([top, r[SUBLANES:]], axis=0))
    taps.append(x)
    return taps


def _conv_specs(cfg, lp):
    tm = 2 * cfg.tr
    cb = _div_tile(math.gcd(cfg.ssd_inner, cfg.gn), 1024)
    off = cfg.ssd_inner // cb
    per = tm // SUBLANES
    nrow = lp // tm
    main = pl.BlockSpec((tm, cb), lambda i, j: (i, j + off))
    before = pl.BlockSpec((SUBLANES, cb), lambda i, j: (jnp.maximum(i * per - 1, 0), j + off))
    own = pl.BlockSpec((tm, cb), lambda i, j: (i, j))
    after = pl.BlockSpec((SUBLANES, cb), lambda i, j: (jnp.minimum((i + 1) * per, nrow * per - 1), j))
    par = lambda r: pl.BlockSpec((r, cb), lambda i, j: (0, j))
    return tm, cb, nrow, main, before, own, after, par


def _conv_fwd(zx, conv_w, conv_b, cfg):
    lp = zx.shape[0]
    tm, cb, nrow, main, before, own, after, par = _conv_specs(cfg, lp)

    def fn(x, halo, w, b):
        taps = _conv_taps(x, halo, pl.program_id(0) == 0)
        pre = b
        for k in range(SSD_CONV):
            pre = pre + taps[k] * w[k:k + 1, :]
        return _silu(pre)
    return _tile_call(fn, (nrow, cfg.conv_dim // cb), [zx, zx, conv_w, conv_b],
                      [main, before, par(SSD_CONV), par(1)], [((lp, cfg.conv_dim), F32)], [own], "ssd_conv")[0]


def _conv_bwd(zx, conv_w, conv_b, dxs, dbm, dcm, dzx, cfg):
    lp = zx.shape[0]
    tm, cb, nrow, main, before, own, after, par = _conv_specs(cfg, lp)
    ncb = cfg.conv_dim // cb
    nx, nb = cfg.ssd_inner // cb, cfg.gn // cb
    off = nx

    def fn1(x, halo, w, b, d1, d2, d3):
        j = pl.program_id(0)
        da = jnp.where(j < nx, d1, jnp.where(j < nx + nb, d2, d3))
        taps = _conv_taps(x, halo, pl.program_id(1) == 0)
        pre = b
        for k in range(SSD_CONV):
            pre = pre + taps[k] * w[k:k + 1, :]
        sg = _sigmoid(pre)
        dpre = da * sg * (1.0 + pre * (1.0 - sg))
        row8 = lax.broadcasted_iota(jnp.int32, (SUBLANES, cb), 0)
        dw = jnp.zeros((SUBLANES, cb), F32)
        for k in range(SSD_CONV):
            dw = jnp.where(row8 == k, jnp.sum(dpre * taps[k], axis=0, keepdims=True), dw)
        return dpre, dw, jnp.sum(dpre, axis=0, keepdims=True)
    sw = lambda spec: pl.BlockSpec(spec.block_shape, lambda j, i, f=spec.index_map: f(i, j))
    piece = lambda lo, n: pl.BlockSpec((tm, cb), lambda j, i: (i, jnp.clip(j - lo, 0, n - 1)))
    dpre, dw, db = _tile_call(
        fn1, (ncb, nrow), [zx, zx, conv_w, conv_b, dxs, dbm, dcm],
        [sw(main), sw(before), sw(par(SSD_CONV)), sw(par(1)), piece(0, nx), piece(nx, nb), piece(nx + nb, nb)],
        [((lp, cfg.conv_dim), F32), ((SUBLANES, cfg.conv_dim), F32), ((1, cfg.conv_dim), F32)],
        [sw(own), sw(par(SUBLANES)), sw(par(1))], "ssd_conv_bwd1", acc=(1, 2), acc_axis=1)

    def fn2(dp, nxt, w):
        nxt = jnp.where(pl.program_id(0) == nrow - 1, 0.0, nxt)
        row8 = lax.broadcasted_iota(jnp.int32, nxt.shape, 0)
        dx = dp * w[SSD_CONV - 1:SSD_CONV, :]
        for s in range(1, SSD_CONV):
            r = pltpu.roll(dp, tm - s, axis=0)
            bot = jnp.where(row8 >= SUBLANES - s, pltpu.roll(nxt, SUBLANES - s, axis=0), r[tm - SUBLANES:])
            up = jnp.concatenate([r[:tm - SUBLANES], bot], axis=0)
            dx = dx + up * w[SSD_CONV - 1 - s:SSD_CONV - s, :]
        return dx
    dzx = _tile_call(fn2, (nrow, ncb), [dpre, dpre, conv_w], [own, after, par(SSD_CONV)],
                     [(dzx.shape, BF16)], [main], "ssd_conv_bwd2", fill=(dzx, 0))[0]
    return dzx, dw, db


def _ssd_common(x_ref, b_ref, c_ref, dt_ref, dtt_ref, ar_ref, ac_ref, h):
    q = SSD_BLOCK
    x, bm, cm = x_ref[...], b_ref[...], c_ref[...]
    dt, dtt = dt_ref[0], dtt_ref[0]
    row = lax.broadcasted_iota(jnp.int32, (q, q), 0)
    col = lax.broadcasted_iota(jnp.int32, (q, q), 1)
    tri = row >= col
    cs = _dot(tri.astype(F32), dt * ar_ref[0], precision=HI)
    cst = _dot(dtt * ac_ref[0], (row <= col).astype(F32), precision=HI)
    g = _bdot(cm, bm, "nt")
    ch = _bdot(cm, h)
    hpg, gw = dt.shape[1], x.shape[1]
    e = (lax.broadcasted_iota(jnp.int32, (hpg, gw), 1) // SSD_HEAD_DIM
         == lax.broadcasted_iota(jnp.int32, (hpg, gw), 0)).astype(F32)
    et = (lax.broadcasted_iota(jnp.int32, (gw, hpg), 0) // SSD_HEAD_DIM
          == lax.broadcasted_iota(jnp.int32, (gw, hpg), 1)).astype(F32)
    spread = lambda v: _dot(v, e, precision=HI)
    gather = lambda v: _dot(v, et, precision=HI)
    return x, bm, cm, dt, tri, cs, cst, g, ch, spread, gather


def _ssd_specs(cfg, rev):
    q, n, gw, hpg = SSD_BLOCK, SSD_STATE, cfg.gw, cfg.hpg
    nc = cfg.lp // q
    cc = (lambda c: nc - 1 - c) if rev else (lambda c: c)
    boff = cfg.ssd_inner // n
    return dict(
        x=pl.BlockSpec((q, gw), lambda g, c: (cc(c), g)),
        b=pl.BlockSpec((q, n), lambda g, c: (cc(c), boff + g)),
        c=pl.BlockSpec((q, n), lambda g, c: (cc(c), boff + SSD_GROUPS + g)),
        bc_out=pl.BlockSpec((q, n), lambda g, c: (cc(c), g)),
        dt=pl.BlockSpec((1, q, hpg), lambda g, c: (g, cc(c), 0)),
        dtt=pl.BlockSpec((1, hpg, q), lambda g, c: (g, 0, cc(c))),
        ar=pl.BlockSpec((1, 1, hpg), lambda g, c: (g, 0, 0)),
        ac=pl.BlockSpec((1, hpg, 1), lambda g, c: (g, 0, 0)),
        h=pl.BlockSpec((1, n, gw), lambda g, c: (cc(c), 0, g)))


def _ssd_fwd(xbc, dt_g, dtt_g, a_row, a_col, cfg):
    q, n, gw, hpg, lp = SSD_BLOCK, SSD_STATE, cfg.gw, cfg.hpg, cfg.lp
    nc = lp // q
    hd = SSD_HEAD_DIM
    sp = _ssd_specs(cfg, False)

    def body(x_ref, b_ref, c_ref, dt_ref, dtt_ref, ar_ref, ac_ref, y_ref, hp_ref, h_s):
        @pl.when(pl.program_id(1) == 0)
        def _():
            h_s[...] = jnp.zeros((n, gw), F32)
        h = h_s[...]
        hp_ref[0] = h
        x, bm, cm, dt, tri, cs, cst, g, ch, spread, _ = _ssd_common(x_ref, b_ref, c_ref, dt_ref, dtt_ref, ar_ref,
                                                                    ac_ref, h)
        last = cs[q - 1:q, :]
        xdt = x * spread(dt)
        y_off = spread(jnp.exp(cs)) * ch
        xw = xdt * spread(jnp.exp(last - cs))
        for r in range(hpg):
            sl = slice(r * hd, (r + 1) * hd)
            lm = jnp.exp(jnp.where(tri, cs[:, r:r + 1] - cst[r:r + 1, :], NEG))
            y_ref[:, sl] = _bdot(g * lm, xdt[:, sl]) + y_off[:, sl]
        h_s[...] = h * spread(jnp.exp(last)) + _bdot(bm, xw, "tn")

    return pl.pallas_call(
        body, out_shape=[jax.ShapeDtypeStruct((lp, cfg.ssd_inner), F32),
                         jax.ShapeDtypeStruct((nc, n, cfg.ssd_inner), F32)],
        grid=(SSD_GROUPS, nc),
        in_specs=[sp["x"], sp["b"], sp["c"], sp["dt"], sp["dtt"], sp["ar"], sp["ac"]],
        out_specs=[sp["x"], sp["h"]],
        scratch_shapes=[pltpu.VMEM((n, gw), F32)],
        name="ssd_scan", compiler_params=_cparams(2))(xbc, xbc, xbc, dt_g, dtt_g, a_row, a_col)


def _ssd_bwd(xbc, dt_g, dtt_g, a_row, a_col, hprev, dy, dx_gate, cfg):
    q, n, gw, hpg, lp = SSD_BLOCK, SSD_STATE, cfg.gw, cfg.hpg, cfg.lp
    nc = lp // q
    hd = SSD_HEAD_DIM
    sp = _ssd_specs(cfg, True)

    def body(x_ref, b_ref, c_ref, dt_ref, dtt_ref, ar_ref, ac_ref, hp_ref, dy_ref, dxg_ref,
             dx_ref, db_ref, dc_ref, ddt_ref, da_ref, dh_s, dxdt_s):
        @pl.when(pl.program_id(1) == 0)
        def _():
            dh_s[...] = jnp.zeros((n, gw), F32)
            da_ref[...] = jnp.zeros((1, 1, hpg), F32)
        h = hp_ref[0]
        dhn = dh_s[...]
        dy = dy_ref[...]
        x, bm, cm, dt, tri, cs, cst, g, ch, spread, gather = _ssd_common(x_ref, b_ref, c_ref, dt_ref, dtt_ref,
                                                                         ar_ref, ac_ref, h)
        last = cs[q - 1:q, :]
        e, wv, elast = jnp.exp(cs), jnp.exp(last - cs), jnp.exp(last)
        dt_x, w_x = spread(dt), spread(wv)
        xdt = x * dt_x
        dye = dy * spread(e)
        xw = xdt * w_x
        bd = _bdot(bm, dhn)
        de = gather(dy * ch)
        dw = gather(xdt * bd)
        hsum = gather(jnp.sum(dhn * h, axis=0, keepdims=True))
        head_lane = lax.broadcasted_iota(jnp.int32, (q, hpg), 1)
        head_row = lax.broadcasted_iota(jnp.int32, (hpg, q), 0)
        z_rows = jnp.zeros((q, hpg), F32)
        z_cols = jnp.zeros((hpg, q), F32)
        dg = jnp.zeros((q, q), F32)
        for r in range(hpg):
            sl = slice(r * hd, (r + 1) * hd)
            lm = jnp.exp(jnp.where(tri, cs[:, r:r + 1] - cst[r:r + 1, :], NEG))
            m = g * lm
            dyr = dy[:, sl]
            dxdt_s[:, sl] = _bdot(m, dyr, "tn")
            dm = _bdot(dyr, xdt[:, sl], "nt")
            dg = dg + dm * lm
            z = dm * m
            z_rows = jnp.where(head_lane == r, jnp.sum(z, axis=1, keepdims=True), z_rows)
            z_cols = jnp.where(head_row == r, jnp.sum(z, axis=0, keepdims=True), z_cols)
        dxdt = dxdt_s[...] + w_x * bd
        is_last = lax.broadcasted_iota(jnp.int32, (q, 1), 0) == q - 1
        extra = jnp.sum(dw * wv, axis=0, keepdims=True) + elast * hsum
        eye = (lax.broadcasted_iota(jnp.int32, (hpg, hpg), 0)
               == lax.broadcasted_iota(jnp.int32, (hpg, hpg), 1)).astype(F32)
        dcs = (z_rows + de * e - dw * wv + jnp.where(is_last, extra, 0.0)
               - _dot(z_cols, eye, "tn", precision=HI))
        row = lax.broadcasted_iota(jnp.int32, (q, q), 0)
        col = lax.broadcasted_iota(jnp.int32, (q, q), 1)
        dda = _dot((row <= col).astype(F32), dcs, precision=HI)
        ddt_ref[0] = dda * ar_ref[0] + gather(dxdt * x)
        da_ref[0] += jnp.sum(dda * dt, axis=0, keepdims=True)
        dx_ref[...] = dxdt * dt_x + dxg_ref[...]
        dc_ref[...] = _bdot(dg, bm) + _bdot(dye, h, "nt")
        db_ref[...] = _bdot(dg, cm, "tn") + _bdot(xw, dhn, "nt")
        dh_s[...] = dhn * spread(elast) + _bdot(cm, dye, "tn")

    return pl.pallas_call(
        body, out_shape=[jax.ShapeDtypeStruct((lp, cfg.ssd_inner), F32), jax.ShapeDtypeStruct((lp, cfg.gn), F32),
                         jax.ShapeDtypeStruct((lp, cfg.gn), F32), jax.ShapeDtypeStruct((SSD_GROUPS, lp, hpg), F32),
                         jax.ShapeDtypeStruct((SSD_GROUPS, 1, hpg), F32)],
        grid=(SSD_GROUPS, nc),
        in_specs=[sp["x"], sp["b"], sp["c"], sp["dt"], sp["dtt"], sp["ar"], sp["ac"], sp["h"], sp["x"], sp["x"]],
        out_specs=[sp["x"], sp["bc_out"], sp["bc_out"], sp["dt"], sp["ar"]],
        scratch_shapes=[pltpu.VMEM((n, gw), F32), pltpu.VMEM((q, gw), F32)],
        name="ssd_scan_bwd", compiler_params=_cparams(2))(xbc, xbc, xbc, dt_g, dtt_g, a_row, a_col, hprev, dy, dx_gate)


def _gate_fn(y, xs, z, dexp, ng):
    return _rms((y + dexp * xs) * _silu(z), ng)


def _gate_specs(cfg):
    tm, gw = cfg.tm, cfg.gw
    blk = pl.BlockSpec((tm, gw), lambda g, i: (i, g))
    par = pl.BlockSpec((1, gw), lambda g, i: (0, g))
    return blk, par


def _mamba_fwd(h, w, w_in_t, w_out, conv_w, cfg):
    lp, tm, nh, hpg, inner = cfg.lp, cfg.tm, cfg.ssd_heads, cfg.hpg, cfg.ssd_inner
    zx = _mm(h, w_in_t, "nt", "l1_in")
    conv_b = w["l1_conv_b"].reshape(1, -1)
    xbc = _conv_fwd(zx, conv_w, conv_b, cfg)
    dt_raw = zx[:, inner + cfg.conv_dim:inner + cfg.conv_dim + nh]
    dt_bias = w["l1_dt_bias"].reshape(1, nh)
    a_log = w["l1_a_log"].reshape(1, nh)
    dt = _rowwise(lambda r, b: _softplus(r + b), [dt_raw], [dt_bias], [(nh, F32)], [], tm, "ssd_dt")[0]
    a = _small_call(lambda al: -jnp.exp(al), [a_log], [(1, nh)], "ssd_a")[0]
    dt_g = dt.reshape(lp, SSD_GROUPS, hpg).transpose(1, 0, 2)
    dtt_g = dt_g.transpose(0, 2, 1)
    a_row, a_col = a.reshape(SSD_GROUPS, 1, hpg), a.reshape(SSD_GROUPS, hpg, 1)
    y, hprev = _ssd_fwd(xbc, dt_g, dtt_g, a_row, a_col, cfg)
    dexp = jnp.repeat(w["l1_d"], SSD_HEAD_DIM).reshape(1, inner)
    ng = w["l1_norm_g"].reshape(1, inner)
    blk, par = _gate_specs(cfg)
    yn = _tile_call(_gate_fn, (SSD_GROUPS, lp // tm), [y, xbc, zx, dexp, ng], [blk, blk, blk, par, par],
                    [((lp, inner), F32)], [blk], "ssd_gate")[0]
    mo = _mm(yn, w_out, "nn", "l1_out")
    return mo, (zx, xbc, dt_raw, dt_g, dtt_g, a, a_row, a_col, y, hprev, dexp, ng, yn)


def _mamba_bwd(h, saved, dmo, w, w_in_t, w_out, conv_w, cfg):
    zx, xbc, dt_raw, dt_g, dtt_g, a, a_row, a_col, y, hprev, dexp, ng, yn = saved
    lp, tm, nh, hpg, inner = cfg.lp, cfg.tm, cfg.ssd_heads, cfg.hpg, cfg.ssd_inner
    d_wout = _mm(yn, dmo, "tn", "l1_dwout", out_dtype=BF16)
    dyn = _mm(dmo, w_out, "nt", "l1_dyn")
    blk, par = _gate_specs(cfg)

    def gate_bwd(y, xs, z, dexp, ng, dyn):
        _, vjp = jax.vjp(_gate_fn, y, xs, z, dexp, ng)
        return vjp(dyn)
    dy, dxs_gate, dzx, ddexp, dng = _tile_call(
        gate_bwd, (SSD_GROUPS, lp // tm), [y, xbc, zx, dexp, ng, dyn], [blk, blk, blk, par, par, blk],
        [((lp, inner), F32)] * 2 + [((lp, cfg.l1_inp), BF16)] + [((1, inner), F32)] * 2, [blk, blk, blk, par, par],
        "ssd_gate_bwd", acc=(3, 4), acc_axis=1)
    dxs, dbm, dcm, ddt_g, da_g = _ssd_bwd(xbc, dt_g, dtt_g, a_row, a_col, hprev, dy, dxs_gate, cfg)
    conv_b = w["l1_conv_b"].reshape(1, -1)
    dzx, dconv_w, dconv_b = _conv_bwd(zx, conv_w, conv_b, dxs, dbm, dcm, dzx, cfg)
    assert cfg.l1_inp - inner - cfg.conv_dim == LANES
    ddt = jnp.pad(ddt_g.transpose(1, 0, 2).reshape(lp, nh), ((0, 0), (0, LANES - nh)))
    dt_bias = jnp.pad(w["l1_dt_bias"].reshape(1, nh), ((0, 0), (0, LANES - nh)))
    last = (inner + cfg.conv_dim) // LANES
    tail = pl.BlockSpec((tm, LANES), lambda i: (i, last))

    def dt_bwd(ddt, r, b):
        lane = lax.broadcasted_iota(jnp.int32, ddt.shape, 1)
        d = jnp.where(lane < nh, ddt * _sigmoid(r + b), 0.0)
        return d, jnp.sum(d, axis=0, keepdims=True)
    dzx, ddt_bias = _tile_call(dt_bwd, (lp // tm,), [ddt, zx, dt_bias], [_rows(tm, LANES), tail, _whole((1, LANES))],
                               [(dzx.shape, BF16), ((1, LANES), F32)], [tail, _whole((1, LANES))], "ssd_dt_bwd",
                               acc=(1,), fill=(dzx, 0))
    ddt_bias = ddt_bias[:, :nh]
    da_log, dd = _small_call(lambda da, a, dde: (da * a, jnp.sum(dde, axis=1, keepdims=True)),
                             [da_g.reshape(1, nh), a, ddexp.reshape(nh, SSD_HEAD_DIM)], [(1, nh), (nh, 1)],
                             "ssd_small_bwd")
    d_win_t = _mm(dzx, h, "tn", "l1_dwin", out_dtype=BF16, tm_t=1152)
    dh = _mm(dzx, w_in_t, "nn", "l1_dh", tn_t=2048)
    grads = {"l1_w_in": d_win_t, "l1_conv_w": dconv_w[:SSD_CONV], "l1_conv_b": dconv_b.reshape(-1),
             "l1_dt_bias": ddt_bias.reshape(-1), "l1_a_log": da_log.reshape(-1), "l1_d": dd.reshape(-1),
             "l1_norm_g": dng.reshape(-1), "l1_w_out": d_wout}
    return dh, grads


def _local_step(x, target, w, cfg, late_weights=None, early_grads=None):
    lp, n, d, tm, sw = cfg.lp, cfg.n, cfg.d, cfg.tr, cfg.s5_w
    row = lambda name: w[name].reshape(1, -1)
    h0 = jnp.concatenate([w["meta_tokens"], x, jnp.zeros((lp - n, d), F32)], axis=0)
    proj = _mm(h0, w["l0_w_in"], "nn", "l0_in")
    o1, o2, o3 = sw, sw + cfg.q_rank, sw + cfg.q_rank + cfg.kv_rank
    prm = _s5_prep(w, cfg)
    a_out, s5_saved = _s5_fwd(proj[:, :o1], prm, w["l0_s5_w_glu"], cfg)
    b_out, mla_saved, arrived = _mla_fwd(proj[:, o1:o2], proj[:, o2:o3], proj[:, o3:], w, w["l0_mla_w_uq_p"],
                                         w["l0_mla_w_ukv"], cfg, late_weights[0] if late_weights else None)
    if late_weights:
        w = dict(w, **late_weights[1](arrived))
    mix = jnp.concatenate([a_out, b_out], axis=1).astype(BF16)
    mo0 = _mm(mix, w["l0_w_out"], "nn", "l0_out")
    h1, h1b = _ln_fwd(h0, mo0, row("l0_ln1_g"), row("l0_ln1_b"), cfg, "l0_ln1")
    fo0, ffn0 = _ffn_fwd(h1b, w["l0_w_gu"], w["l0_ffn_w_down"], cfg, "l0_ffn")
    h2, h2b = _ln_fwd(h1, fo0, row("l0_ln2_g"), row("l0_ln2_b"), cfg, "l0_ln2")
    mo1, mam = _mamba_fwd(h2b, w, w["l1_w_in_t"], w["l1_w_out"], w["l1_conv_w"], cfg)
    h3, h3b = _ln_fwd(h2, mo1, row("l1_ln1_g"), row("l1_ln1_b"), cfg, "l1_ln1")
    fo1, ffn1 = _ffn_fwd(h3b, w["l1_w_gu"], w["l1_ffn_w_down"], cfg, "l1_ffn")
    h4, _ = _ln_fwd(h3, fo1, row("l1_ln2_g"), row("l1_ln2_b"), cfg, "l1_ln2")
    tgt = jnp.concatenate([jnp.zeros((N_META, d), F32), target, jnp.zeros((lp - n, d), F32)], axis=0)

    def loss_fn(y, t):
        r = pl.program_id(0) * tm + lax.broadcasted_iota(jnp.int32, (tm, 1), 0)
        diff = jnp.where(jnp.logical_and(r >= N_META, r < n), y - t, 0.0)
        return diff * (1.0 / d), jnp.sum(diff * diff, axis=0, keepdims=True) * (0.5 / d)
    dh4, loss_lanes = _rowwise(loss_fn, [h4, tgt], [], [(d, F32)], [(1, d)], tm, "loss")
    grads = {}
    dr4, dr4b, dg, db = _ln_bwd(h3, fo1, row("l1_ln2_g"), row("l1_ln2_b"), [dh4], [1.0], cfg, "l1_ln2_bwd")
    grads["l1_ln2_g"], grads["l1_ln2_b"] = dg.reshape(-1), db.reshape(-1)
    dh3, grads["l1_w_gu"], grads["l1_ffn_w_down"] = _ffn_bwd(h3b, ffn1, dr4b, w["l1_w_gu"], w["l1_ffn_w_down"], cfg,
                                                             "l1_ffn")
    dr3, dr3b, dg, db = _ln_bwd(h2, mo1, row("l1_ln1_g"), row("l1_ln1_b"), [dr4, dh3], [DN_ALPHA, 1.0], cfg,
                                "l1_ln1_bwd")
    grads["l1_ln1_g"], grads["l1_ln1_b"] = dg.reshape(-1), db.reshape(-1)
    dh2, mg = _mamba_bwd(h2b, mam, dr3b, w, w["l1_w_in_t"], w["l1_w_out"], w["l1_conv_w"], cfg)
    grads.update(mg)
    dr2, dr2b, dg, db = _ln_bwd(h1, fo0, row("l0_ln2_g"), row("l0_ln2_b"), [dr3, dh2], [DN_ALPHA, 1.0], cfg,
                                "l0_ln2_bwd")
    grads["l0_ln2_g"], grads["l0_ln2_b"] = dg.reshape(-1), db.reshape(-1)
    dh1, grads["l0_w_gu"], grads["l0_ffn_w_down"] = _ffn_bwd(h1b, ffn0, dr2b, w["l0_w_gu"], w["l0_ffn_w_down"], cfg,
                                                             "l0_ffn")
    dr1, dr1b, dg, db = _ln_bwd(h0, mo0, row("l0_ln1_g"), row("l0_ln1_b"), [dr2, dh1], [DN_ALPHA, 1.0], cfg,
                                "l0_ln1_bwd")
    grads["l0_ln1_g"], grads["l0_ln1_b"] = dg.reshape(-1), db.reshape(-1)
    grads["l0_w_out"] = _mm(mix, dr1b, "tn", "l0_dwout", out_dtype=BF16)
    dmix = _mm(dr1b, w["l0_w_out"], "nt", "l0_dmix")
    du, sg = _s5_bwd(dmix[:, :sw], s5_saved, prm, w, w["l0_s5_w_glu"], cfg)
    dql, dkl, dkr, ag, exchanged = _mla_bwd(dmix[:, sw:], mla_saved, w, w["l0_mla_w_uq_p"], w["l0_mla_w_ukv"], cfg,
                                            early_grads(grads) if early_grads else None)
    grads.update(sg)
    grads.update(ag)
    dproj = jnp.concatenate([du, dql, dkl, dkr], axis=1)
    grads["l0_w_in"] = _mm(h0, dproj, "tn", "l0_dwin", out_dtype=BF16)
    dh0m = _mm(dproj, w["l0_w_in"], "nt", "l0_dh")
    dh0 = _rowwise(lambda a, b: DN_ALPHA * a + b, [dr1, dh0m], [], [(d, F32)], [], tm, "l0_dh0")[0]
    grads["meta_tokens"] = dh0[:N_META]
    return loss_lanes, dh0[N_META:n], grads, exchanged


FLAT_W = 1024
N_SLOTS = 4
HBM_SPEC = pl.BlockSpec(memory_space=pltpu.HBM)


def _place():
    x, y, c = lax.axis_index("x"), lax.axis_index("y"), lax.axis_index("c")
    chips = [(1 - x, y), (x, 1 - y), (1 - x, 1 - y)]
    return x, y, c, chips


def _remote(src, dst, ssem, rsem, dev):
    return pltpu.make_async_remote_copy(src_ref=src, dst_ref=dst, send_sem=ssem, recv_sem=rsem, device_id=dev,
                                        device_id_type=MESH)


class _GatherPlan:
    def __init__(self, shards, groups):
        self.arrays = list(shards)
        self.n = n = len(shards)
        self.rows = [s.shape[0] for s in shards]
        self.place = {t: (g, row0) for g, members in enumerate(groups) for t, row0 in members}
        self.n_out = len(groups)
        self.out_shapes = []
        for members in groups:
            t0 = members[0][0]
            rows = max(row0 + N_SLOTS * shards[t].shape[0] for t, row0 in members)
            self.out_shapes.append(jax.ShapeDtypeStruct((rows, shards[t0].shape[1]), shards[t0].dtype))
        sems = pltpu.SemaphoreType.DMA((3 * n,))
        self.scratch = [sems, sems, sems, sems, pltpu.SemaphoreType.DMA((n,))]

    def _copies(self, refs):
        n = self.n
        srcs, outs = refs[:n], refs[n:n + self.n_out]
        send_sems, recv_sems, fsend, frecv, lsems = refs[n + self.n_out:]
        x, y, c, chips = _place()
        me = 2 * x + y
        sib = (x, y, 1 - c)

        def rows_of(t, slot, half):
            r = self.rows[t]
            g, row0 = self.place[t]
            return outs[g].at[pl.ds(row0 + slot * r + half * (r // 2), r // 2)]
        local, send, arrive, relay, arrive_sib = [], [], [], [], []
        for t in range(n):
            r = self.rows[t]
            g, row0 = self.place[t]
            local.append(pltpu.make_async_copy(srcs[t], outs[g].at[pl.ds(row0 + me * r, r)], lsems.at[t]))
            mine = srcs[t].at[pl.ds(c * (r // 2), r // 2)]
            for j, (cx, cy) in enumerate(chips):
                k = 3 * t + j
                send.append(_remote(mine, rows_of(t, me, c), send_sems.at[k], recv_sems.at[k], (cx, cy, c)))
                got = rows_of(t, 2 * cx + cy, c)
                arrive.append(_remote(got, got, send_sems.at[k], recv_sems.at[k], (cx, cy, c)))
                relay.append(_remote(got, got, fsend.at[k], frecv.at[k], sib))
                got_sib = rows_of(t, 2 * cx + cy, 1 - c)
                arrive_sib.append(_remote(got_sib, got_sib, fsend.at[k], frecv.at[k], sib))
        return local, send, arrive, relay, arrive_sib

    def start(self, refs):
        local, send, _, _, _ = self._copies(refs)
        for cp in local + send:
            cp.start()

    def relay(self, refs):
        _, _, arrive, relay, _ = self._copies(refs)
        for a, r in zip(arrive, relay):
            a.wait_recv()
            r.start()

    def finish(self, refs):
        local, send, _, relay, arrive_sib = self._copies(refs)
        for cp in arrive_sib:
            cp.wait_recv()
        for cp in send + relay:
            cp.wait_send()
        for cp in local:
            cp.wait()


class _ExchangePlan:
    def __init__(self, items):
        self.items = items
        self.arrays = [a for a, _, _ in items]
        self.n = self.n_out = n = len(items)
        self.out_shapes = [jax.ShapeDtypeStruct((3, rps, a.shape[1]), a.dtype) for a, _, rps in items]
        sems = pltpu.SemaphoreType.DMA((3 * n,))
        self.scratch = [sems, sems]

    def _copies(self, refs):
        n = self.n
        srcs, outs = refs[:n], refs[n:2 * n]
        send_sems, recv_sems = refs[2 * n:]
        x, y, c, chips = _place()
        cps = []
        for t, (_, row0, rps) in enumerate(self.items):
            for j, (cx, cy) in enumerate(chips):
                cps.append(_remote(srcs[t].at[pl.ds(row0 + (2 * cx + cy) * rps, rps)], outs[t].at[j],
                                   send_sems.at[3 * t + j], recv_sems.at[3 * t + j], (cx, cy, c)))
        return cps

    def start(self, refs):
        for cp in self._copies(refs):
            cp.start()

    def relay(self, refs):
        pass

    def finish(self, refs):
        for cp in self._copies(refs):
            cp.wait()


def _run_plan(plan, name):
    def body(*refs):
        plan.start(refs)
        plan.relay(refs)
        plan.finish(refs)
    return pl.pallas_call(body, out_shape=plan.out_shapes, in_specs=[HBM_SPEC] * plan.n,
                          out_specs=[HBM_SPEC] * plan.n_out, scratch_shapes=plan.scratch, name=name)(*plan.arrays)


def _exchange_sibling(arrays):
    n = len(arrays)

    def body(*refs):
        srcs, outs = refs[:n], refs[n:2 * n]
        ssems, rsems = refs[2 * n:]
        x, y, c, _ = _place()
        cps = []
        for t in range(n):
            cp = _remote(srcs[t], outs[t], ssems.at[t], rsems.at[t], (x, y, 1 - c))
            cp.start()
            cps.append(cp)
        for cp in cps:
            cp.wait()

    sems = pltpu.SemaphoreType.DMA((n,))
    return pl.pallas_call(
        body, out_shape=[jax.ShapeDtypeStruct(a.shape, a.dtype) for a in arrays], in_specs=[HBM_SPEC] * n,
        out_specs=[HBM_SPEC] * n, scratch_shapes=[sems, sems], name="exchange_sibling")(*arrays)


def _gather_all(v):
    flips = [(fx, fy, fc) for fx in (0, 1) for fy in (0, 1) for fc in (0, 1)][1:]

    def body(src, out, send_sems, recv_sems, lsem):
        x, y, c, _ = _place()
        local = pltpu.make_async_copy(src, out.at[4 * x + 2 * y + c], lsem)
        local.start()
        cps = []
        for k, (fx, fy, fc) in enumerate(flips):
            px, py, pc = (1 - x if fx else x), (1 - y if fy else y), (1 - c if fc else c)
            cp = _remote(src, out.at[4 * x + 2 * y + c], send_sems.at[k], recv_sems.at[k], (px, py, pc))
            cp.start()
            cps.append(cp)
        for cp in cps:
            cp.wait()
        local.wait()

    return pl.pallas_call(
        body, out_shape=jax.ShapeDtypeStruct((8,) + v.shape, v.dtype), in_specs=[HBM_SPEC], out_specs=HBM_SPEC,
        scratch_shapes=[pltpu.SemaphoreType.DMA((7,)), pltpu.SemaphoreType.DMA((7,)), pltpu.SemaphoreType.DMA],
        name="gather_all")(v)


def _flat_rows(n_elems, row_unit):
    return -(-n_elems // (FLAT_W * row_unit)) * row_unit


def _to_flat(pieces, rows):
    flat = jnp.concatenate([p.reshape(-1) for p in pieces])
    return jnp.pad(flat, (0, rows * FLAT_W - flat.shape[0])).reshape(rows, FLAT_W)


def _sum_parts(parts, name, tm=512):
    rows = parts[0].shape[1]
    tm = _div_tile(rows, tm, SUBLANES)

    def fn(*ps):
        acc = None
        for p in ps:
            for k in range(p.shape[0]):
                acc = p[k].astype(F32) if acc is None else acc + p[k].astype(F32)
        return acc
    return _tile_call(fn, (rows // tm,), parts,
                      [pl.BlockSpec((p.shape[0], tm, FLAT_W), lambda i: (0, i, 0)) for p in parts],
                      [((rows, FLAT_W), F32)], [_rows(tm, FLAT_W)], name)[0]


ELEMENTWISE_BLOCK = 1 << 19


def _row_tile(rows, cols, unit):
    return _div_tile(rows, max(unit, ELEMENTWISE_BLOCK // cols), unit)


def _sum_slot(g, row0, rps, others, me, name):
    c = g.shape[1]
    tm = _row_tile(rps, c, 2 * SUBLANES)
    nrb = rps // tm
    assert row0 % tm == 0

    def body(me_ref, g_ref, o_ref, out_ref):
        out_ref[...] = ((g_ref[...].astype(F32) + o_ref[0].astype(F32)) + o_ref[1].astype(F32)) + o_ref[2].astype(F32)

    grid_spec = pltpu.PrefetchScalarGridSpec(
        num_scalar_prefetch=1, grid=(nrb,),
        in_specs=[pl.BlockSpec((tm, c), lambda i, me_ref: (row0 // tm + me_ref[0] * nrb + i, 0)),
                  pl.BlockSpec((3, tm, c), lambda i, me_ref: (0, i, 0))],
        out_specs=pl.BlockSpec((tm, c), lambda i, me_ref: (i, 0)))
    return pl.pallas_call(body, out_shape=jax.ShapeDtypeStruct((rps, c), F32), grid_spec=grid_spec, name=name,
                          compiler_params=_cparams(1))(me, g, others)


def _adamw(gparts, w, m, v, name, tm=None):
    rows, cols = w.shape
    tm = _row_tile(rows, cols, SUBLANES) if tm is None else _div_tile(rows, tm, SUBLANES)
    ng = len(gparts)

    def fn(*a):
        g = a[0]
        for t in a[1:ng]:
            g = g + t
        w, m, v = a[ng:]
        m = ADAM_B1 * m + (1.0 - ADAM_B1) * g
        v = ADAM_B2 * v + (1.0 - ADAM_B2) * (g * g)
        m_hat = m / (1.0 - ADAM_B1 ** ADAM_STEP)
        v_hat = v / (1.0 - ADAM_B2 ** ADAM_STEP)
        delta = -ADAM_LR * (m_hat / (jnp.sqrt(v_hat) + ADAM_EPS) + ADAM_WD * w)
        return g, delta, m, v
    ins = list(gparts) + [w, m, v]
    return _tile_call(fn, (rows // tm,), ins, [_rows(tm, cols)] * len(ins), [((rows, cols), F32)] * 4,
                      [_rows(tm, cols)] * 4, name)


FIRST = ("l0_w_in", "l0_s5_w_glu", "l0_mla_w_uq", "l0_mla_w_ukv")
REST = ("l0_w_out", "l0_ffn_w_gate", "l0_ffn_w_up", "l0_ffn_w_down", "l1_w_in", "l1_w_out", "l1_ffn_w_gate",
        "l1_ffn_w_up", "l1_ffn_w_down")
BIG = FIRST + REST
TINY = ("meta_tokens", "l1_conv_w")
REPLICATED = ("l0_s5_log_dt", "l0_s5_a_re", "l0_s5_a_im", "l0_s5_b_re", "l0_s5_b_im", "l0_s5_c_re", "l0_s5_c_im",
              "l0_s5_d", "l0_mla_q_norm", "l0_mla_kv_norm", "l0_ln1_g", "l0_ln1_b", "l0_ln2_g", "l0_ln2_b",
              "l1_conv_b", "l1_dt_bias", "l1_a_log", "l1_d", "l1_norm_g", "l1_ln1_g", "l1_ln1_b", "l1_ln2_g",
              "l1_ln2_b")
WEIGHTS = ("meta_tokens", "l0_w_in", "l0_s5_log_dt", "l0_s5_a_re", "l0_s5_a_im", "l0_s5_b_re", "l0_s5_b_im",
           "l0_s5_c_re", "l0_s5_c_im", "l0_s5_d", "l0_s5_w_glu", "l0_mla_q_norm", "l0_mla_w_uq", "l0_mla_kv_norm",
           "l0_mla_w_ukv", "l0_w_out", "l0_ln1_g", "l0_ln1_b", "l0_ffn_w_gate", "l0_ffn_w_up", "l0_ffn_w_down",
           "l0_ln2_g", "l0_ln2_b", "l1_w_in", "l1_conv_w", "l1_conv_b", "l1_dt_bias", "l1_a_log", "l1_d",
           "l1_norm_g", "l1_w_out", "l1_ln1_g", "l1_ln1_b", "l1_ffn_w_gate", "l1_ffn_w_up", "l1_ffn_w_down",
           "l1_ln2_g", "l1_ln2_b")
def _split_flat(flat2d, shapes):
    flat = flat2d.reshape(-1)
    out, off = [], 0
    for s in shapes:
        n = math.prod(s)
        out.append(flat[off:off + n].reshape(s))
        off += n
    return out


def _cols_to_slots(full):
    r, c = full.shape
    return full.reshape(r, N_SLOTS, c // N_SLOTS).transpose(1, 0, 2).reshape(N_SLOTS * r, c // N_SLOTS)


def _slots_to_cols(slabs):
    r4, c = slabs.shape
    return slabs.reshape(N_SLOTS, r4 // N_SLOTS, c).transpose(1, 0, 2).reshape(r4 // N_SLOTS, N_SLOTS * c)


def _step(cfg, x, loss_target, ws, ms, vs):
    d, f = cfg.d, cfg.ffn
    me = 2 * lax.axis_index("x") + lax.axis_index("y")
    kinds = ("grad", "delta", "new_m", "new_v")
    def gather_plan(names):
        groups = []
        for t, name in enumerate(names):
            if name.endswith("_ffn_w_up"):
                groups[-1].append((t, N_SLOTS * d))
            else:
                groups.append([(t, 0)])
        heads = [names[members[0][0]] for members in groups]
        return _GatherPlan([ws[name].astype(BF16) for name in names], groups), heads
    plan, heads = gather_plan(FIRST)
    got = dict(zip(heads, _run_plan(plan, "gather_first")))
    w = {"l0_w_in": got["l0_w_in"], "l0_s5_w_glu": got["l0_s5_w_glu"],
         "l0_mla_w_uq_p": _pad_heads(_slots_to_cols(got["l0_mla_w_uq"]), cfg.heads, MLA_NOPE + MLA_ROPE),
         "l0_mla_w_ukv": _slots_to_cols(got["l0_mla_w_ukv"])}
    late_plan, late_heads = gather_plan(REST)

    def late_weights(arrived):
        got = dict(zip(late_heads, arrived))
        lw = {name: got[name] for name in ("l0_w_out", "l0_ffn_w_down", "l1_w_out", "l1_ffn_w_down")}
        for l in ("l0", "l1"):
            lw[l + "_w_gu"] = got[l + "_ffn_w_gate"].reshape(2 * N_SLOTS, d, f // N_SLOTS)
        w_in_t = got["l1_w_in"].reshape(N_SLOTS, d, cfg.l1_in // N_SLOTS).transpose(0, 2, 1).reshape(cfg.l1_in, d)
        lw["l1_w_in_t"] = jnp.pad(w_in_t, ((0, cfg.l1_inp - cfg.l1_in), (0, 0)))
        return lw
    tiny_shapes = [ws[name].shape for name in TINY]
    trows = _flat_rows(sum(math.prod(s) for s in tiny_shapes), SUBLANES)
    tiny = _gather_all(_to_flat([ws[name] for name in TINY], trows))[0::2]
    for k, name in enumerate(TINY):
        blocks = jnp.stack([_split_flat(tiny[s], tiny_shapes)[k] for s in range(N_SLOTS)])
        w[name] = blocks.transpose(1, 0, 2).reshape(blocks.shape[1], -1)
    for name in REPLICATED:
        w[name] = ws[name]
    rps = {name: ws[name].shape[1 if name == "l1_w_in" else 0] for name in BIG}

    def triples(grads, names):
        out = []
        for name in names:
            if name.endswith(("_ffn_w_gate", "_ffn_w_up")):
                g = grads[name[:3] + "w_gu"].reshape(2 * N_SLOTS * d, f // N_SLOTS)
                out.append((g, N_SLOTS * d if name.endswith("_up") else 0, rps[name]))
            elif name in ("l0_mla_w_uq", "l0_mla_w_ukv"):
                out.append((_cols_to_slots(grads[name]).astype(BF16), 0, rps[name]))
            else:
                out.append((grads[name], 0, rps[name]))
        return out
    held = {}

    def early_grads(grads):
        held["rest"] = triples(grads, REST)
        return _ExchangePlan(held["rest"])
    loss_lanes, grad_x, grads, others_rest = _local_step(x[0], loss_target[0], w, cfg, (late_plan, late_weights),
                                                         early_grads)
    first = triples(grads, FIRST)
    others = list(_run_plan(_ExchangePlan(first), "exchange_first")) + list(others_rest)
    me1 = me.reshape(1).astype(jnp.int32)
    parts = [_sum_slot(a, row0, r, o, me1, "sum_" + name)
             for (a, row0, r), o, name in zip(first + held["rest"], others, BIG)]
    sibs = _exchange_sibling(parts)
    res = {}
    for name, p, q in zip(BIG, parts, sibs):
        if name == "l1_w_in":
            p, q = p.T, q.T
        for kind, arr in zip(kinds, _adamw([p, q], ws[name], ms[name], vs[name], "adamw_" + name)):
            res[kind + "_" + name] = arr
    rep_shapes = [(1, d)] + [ws[name].shape for name in REPLICATED]
    all_shapes = rep_shapes + [grads[name].shape for name in TINY]
    srows = _flat_rows(sum(math.prod(s) for s in all_shapes), SUBLANES)
    small = _to_flat([loss_lanes] + [grads[name] for name in REPLICATED + TINY], srows)
    total = _sum_parts([_gather_all(small)], "sum_small", tm=srows)
    zero = jnp.zeros((1, d), F32)
    flat = lambda dct: _to_flat([zero] + [dct[name] for name in REPLICATED], srows)
    outs = _adamw([total], flat(ws), flat(ms), flat(vs), "adamw_replicated", tm=srows)
    for kind, arr in zip(kinds, outs):
        vals = _split_flat(arr, rep_shapes)
        for name, val in zip(REPLICATED, vals[1:]):
            res[kind + "_" + name] = val
    for name, g in zip(TINY, _split_flat(total, all_shapes)[len(rep_shapes):]):
        cols = ws[name].shape[1]
        mine = lax.dynamic_slice_in_dim(g, me * cols, cols, axis=1)
        for kind, arr in zip(kinds, _adamw([mine], ws[name], ms[name], vs[name], "adamw_" + name)):
            res[kind + "_" + name] = arr
    loss = _small_call(lambda t: jnp.sum(t, axis=1, keepdims=True), [_split_flat(total, rep_shapes)[0]], [(1, 1)],
                       "loss_sum")[0].reshape(())
    ordered = [res[kind + "_" + name] for kind in ("grad", "delta", "new_m", "new_v") for name in WEIGHTS]
    return (loss, grad_x[None]) + tuple(ordered)


def kernel(x, meta_tokens, l0_w_in, l0_s5_log_dt, l0_s5_a_re, l0_s5_a_im, l0_s5_b_re, l0_s5_b_im, l0_s5_c_re,
           l0_s5_c_im, l0_s5_d, l0_s5_w_glu, l0_mla_q_norm, l0_mla_w_uq, l0_mla_kv_norm, l0_mla_w_ukv, l0_w_out,
           l0_ln1_g, l0_ln1_b, l0_ffn_w_gate, l0_ffn_w_up, l0_ffn_w_down, l0_ln2_g, l0_ln2_b, l1_w_in, l1_conv_w,
           l1_conv_b, l1_dt_bias, l1_a_log, l1_d, l1_norm_g, l1_w_out, l1_ln1_g, l1_ln1_b, l1_ffn_w_gate,
           l1_ffn_w_up, l1_ffn_w_down, l1_ln2_g, l1_ln2_b, loss_target, m_meta_tokens, m_l0_w_in, m_l0_s5_log_dt,
           m_l0_s5_a_re, m_l0_s5_a_im, m_l0_s5_b_re, m_l0_s5_b_im, m_l0_s5_c_re, m_l0_s5_c_im, m_l0_s5_d,
           m_l0_s5_w_glu, m_l0_mla_q_norm, m_l0_mla_w_uq, m_l0_mla_kv_norm, m_l0_mla_w_ukv, m_l0_w_out, m_l0_ln1_g,
           m_l0_ln1_b, m_l0_ffn_w_gate, m_l0_ffn_w_up, m_l0_ffn_w_down, m_l0_ln2_g, m_l0_ln2_b, m_l1_w_in,
           m_l1_conv_w, m_l1_conv_b, m_l1_dt_bias, m_l1_a_log, m_l1_d, m_l1_norm_g, m_l1_w_out, m_l1_ln1_g,
           m_l1_ln1_b, m_l1_ffn_w_gate, m_l1_ffn_w_up, m_l1_ffn_w_down, m_l1_ln2_g, m_l1_ln2_b, v_meta_tokens,
           v_l0_w_in, v_l0_s5_log_dt, v_l0_s5_a_re, v_l0_s5_a_im, v_l0_s5_b_re, v_l0_s5_b_im, v_l0_s5_c_re,
           v_l0_s5_c_im, v_l0_s5_d, v_l0_s5_w_glu, v_l0_mla_q_norm, v_l0_mla_w_uq, v_l0_mla_kv_norm,
           v_l0_mla_w_ukv, v_l0_w_out, v_l0_ln1_g, v_l0_ln1_b, v_l0_ffn_w_gate, v_l0_ffn_w_up, v_l0_ffn_w_down,
           v_l0_ln2_g, v_l0_ln2_b, v_l1_w_in, v_l1_conv_w, v_l1_conv_b, v_l1_dt_bias, v_l1_a_log, v_l1_d,
           v_l1_norm_g, v_l1_w_out, v_l1_ln1_g, v_l1_ln1_b, v_l1_ffn_w_gate, v_l1_ffn_w_up, v_l1_ffn_w_down,
           v_l1_ln2_g, v_l1_ln2_b):
    given = dict(locals())
    ws = {name: given[name] for name in WEIGHTS}
    ms = {name: given["m_" + name] for name in WEIGHTS}
    vs = {name: given["v_" + name] for name in WEIGHTS}
    return _step(FULL, x, loss_target, ws, ms, vs)
```

```python
import functools
import math

import numpy as np
import jax
import jax.numpy as jnp
from jax import lax
from jax.experimental import pallas as pl
from jax.experimental.pallas import tpu as pltpu

F32 = jnp.float32
BF16 = jnp.bfloat16
HI = lax.Precision.HIGHEST
MESH = pl.DeviceIdType.MESH

LANES = 128
SUBLANES = 8
VMEM_LIMIT_BYTES = 56 * 1024 * 1024

N_META = 16
CHUNK = 64
DEPTH = 2
DN_ALPHA = (2 * DEPTH) ** 0.25
LN_EPS = 1e-5
RMS_EPS = 1e-6
ROPE_BASE = 10000.0
S5_GROUP = 16
S5_STATE = 64
S5_GPB = 8
MLA_NOPE = 128
MLA_ROPE = 64
MLA_V = 128
MLA_QW = 256
ATT_SHIFT = 48
SSD_HEAD_DIM = 64
SSD_GROUPS = 8
SSD_STATE = 128
SSD_CONV = 4
SSD_BLOCK = 128
ADAM_LR = 0.001
ADAM_B1 = 0.9
ADAM_B2 = 0.999
ADAM_EPS = 1e-08
ADAM_WD = 0.01
ADAM_STEP = 10
NEG = -1e30


class _Cfg:
    def __init__(self, d_model, seq, row_tile, att_tile, scan_tiles, small_row_tile):
        d = d_model
        self.tr = small_row_tile
        self.tc = max(small_row_tile, min(320, 2 * small_row_tile))
        self.d = d
        self.seq = seq
        self.n = seq + N_META
        lp = -(-(self.n + ATT_SHIFT) // row_tile) * row_tile
        self.lp = lp
        self.tm = row_tile
        self.tq = att_tile
        self.scan_tiles = scan_tiles
        self.s5_w = d // 2
        self.s5_g = self.s5_w // S5_GROUP
        self.s5_nb = self.s5_g // S5_GPB
        self.s5_c = self.s5_g * S5_STATE
        self.heads = d // 256
        self.q_rank = d // 4
        self.kv_rank = d // 8
        self.l0_in = self.s5_w + self.q_rank + self.kv_rank + MLA_ROPE
        self.l0_mix = self.s5_w + self.heads * MLA_V
        self.ssd_inner = 2 * d
        self.ssd_heads = self.ssd_inner // SSD_HEAD_DIM
        self.hpg = self.ssd_heads // SSD_GROUPS
        self.gw = self.hpg * SSD_HEAD_DIM
        self.gn = SSD_GROUPS * SSD_STATE
        self.conv_dim = self.ssd_inner + 2 * self.gn
        self.l1_in = self.ssd_inner + self.conv_dim + self.ssd_heads
        self.l1_inp = -(-self.l1_in // LANES) * LANES
        self.ffn = -(-(8 * d) // (3 * 256)) * 256
        assert lp % att_tile == 0 and lp % SSD_BLOCK == 0 and lp % (8 * scan_tiles) == 0


FULL = _Cfg(2048, 8192, 640, 640, 4, 320)


def _cparams(n_grid):
    return pltpu.CompilerParams(dimension_semantics=("arbitrary",) * n_grid,
                                vmem_limit_bytes=VMEM_LIMIT_BYTES)


def _div_tile(n, target, unit=LANES):
    if n <= target:
        return n
    best = None
    for t in range(unit, target + 1, unit):
        if n % t == 0:
            best = t
    return n if best is None else best


ANY_SPEC = pl.BlockSpec(memory_space=pl.ANY)


def _tile_call(fn, grid, ins, in_specs, outs, out_specs, name, acc=(), acc_axis=0, fill=None):
    n_in = len(ins)
    n_out = len(outs)
    acc = tuple(acc)
    aliases = {}
    if fill is not None:
        aliases = {n_in: fill[1]}
        ins = list(ins) + [fill[0]]
        in_specs = list(in_specs) + [ANY_SPEC]

    def body(*refs):
        refs = refs[:n_in] + refs[len(ins):]
        vals = fn(*[r[...] for r in refs[:n_in]])
        if not isinstance(vals, (tuple, list)):
            vals = (vals,)
        for k in range(n_out):
            r = refs[n_in + k]
            v = vals[k].astype(r.dtype)
            if k in acc:
                first = pl.program_id(acc_axis) == 0

                @pl.when(first)
                def _(r=r, v=v):
                    r[...] = v

                @pl.when(jnp.logical_not(first))
                def _(r=r, v=v):
                    r[...] += v
            else:
                r[...] = v

    res = pl.pallas_call(
        body, out_shape=[jax.ShapeDtypeStruct(s, d) for s, d in outs], grid=grid,
        in_specs=in_specs, out_specs=out_specs, name=name, compiler_params=_cparams(len(grid)),
        input_output_aliases=aliases,
    )(*ins)
    return res


def _rows(tm, c):
    return pl.BlockSpec((tm, c), lambda i: (i, 0))


def _whole(shape):
    nd = len(shape)
    return pl.BlockSpec(shape, lambda *a: (0,) * nd)


def _rowwise(fn, rows, params, outs, accs, tm, name):
    lp = rows[0].shape[0]
    n_row_out = len(outs)
    res = _tile_call(
        fn, (lp // tm,), list(rows) + list(params),
        [_rows(tm, r.shape[1]) for r in rows] + [_whole(p.shape) for p in params],
        [((lp, c), dt) for c, dt in outs] + [(s, F32) for s in accs],
        [_rows(tm, c) for c, _ in outs] + [_whole(s) for s in accs],
        name, acc=range(n_row_out, n_row_out + len(accs)))
    return res


_DIMS = {"nn": (((1,), (0,)), ((), ())), "nt": (((1,), (1,)), ((), ())), "tn": (((0,), (0,)), ((), ()))}


def _dot(a, b, mode="nn", precision=None):
    return lax.dot_general(a, b, _DIMS[mode], preferred_element_type=F32, precision=precision)


def _bdot(a, b, mode="nn"):
    return _dot(a.astype(BF16), b.astype(BF16), mode)


def _mm(a, b, mode, name, out_dtype=F32, out_slabs=None, b_col0=0, b_cols=None, tm_t=640, tn_t=1536, tk_t=2048):
    slab_b = b.ndim == 3
    if mode == "nn":
        m, k = a.shape
        k2, n, unit_n = (b.shape[1], b.shape[0] * b.shape[2], b.shape[2]) if slab_b else (b.shape[0], b.shape[1], b.shape[1])
        unit_k = k
    elif mode == "nt":
        m, k = a.shape
        n, k2, unit_k = (b.shape[1], b.shape[0] * b.shape[2], b.shape[2]) if slab_b else (b.shape[0], b.shape[1], b.shape[1])
        unit_n = n
    else:
        (k, m), k2 = a.shape, b.shape[0]
        n = b.shape[1] if b_cols is None else b_cols
        unit_n, unit_k = n, k
        tm_t = max(tm_t, 1024)
        tk_t = 1664 if a.dtype == BF16 and b.dtype == BF16 else 1024
    if out_slabs:
        unit_n = n // out_slabs
    assert k == k2, (name, a.shape, b.shape)
    tm, tn, tk = _div_tile(m, tm_t), _div_tile(unit_n, tn_t), _div_tile(unit_k, tk_t)
    nk = k // tk
    nps, kps = unit_n // tn, unit_k // tk
    c0 = b_col0 // tn
    assert b_col0 % tn == 0
    a_spec = {"nn": pl.BlockSpec((tm, tk), lambda i, j, kk: (i, kk)),
              "nt": pl.BlockSpec((tm, tk), lambda i, j, kk: (i, kk)),
              "tn": pl.BlockSpec((tk, tm), lambda i, j, kk: (kk, i))}[mode]
    if slab_b:
        b_spec = {"nn": pl.BlockSpec((None, tk, tn), lambda i, j, kk: (j // nps, kk, j % nps)),
                  "nt": pl.BlockSpec((None, tn, tk), lambda i, j, kk: (kk // kps, j, kk % kps))}[mode]
    else:
        b_spec = {"nn": pl.BlockSpec((tk, tn), lambda i, j, kk: (kk, j)),
                  "nt": pl.BlockSpec((tn, tk), lambda i, j, kk: (j, kk)),
                  "tn": pl.BlockSpec((tk, tn), lambda i, j, kk: (kk, j + c0))}[mode]
    if out_slabs:
        out_shape = jax.ShapeDtypeStruct((out_slabs, m, unit_n), out_dtype)
        out_spec = pl.BlockSpec((None, tm, tn), lambda i, j, kk: (j // nps, i, j % nps))
    else:
        out_shape = jax.ShapeDtypeStruct((m, n), out_dtype)
        out_spec = pl.BlockSpec((tm, tn), lambda i, j, kk: (i, j))

    def body(a_ref, b_ref, o_ref, acc_ref):
        part = _bdot(a_ref[...], b_ref[...], mode)
        if nk == 1:
            o_ref[...] = part.astype(o_ref.dtype)
        else:
            kk = pl.program_id(2)

            @pl.when(kk == 0)
            def _():
                acc_ref[...] = part

            @pl.when(kk > 0)
            def _():
                acc_ref[...] += part

            @pl.when(kk == nk - 1)
            def _():
                o_ref[...] = acc_ref[...].astype(o_ref.dtype)

    return pl.pallas_call(
        body, out_shape=out_shape, grid=(m // tm, n // tn, nk),
        in_specs=[a_spec, b_spec], out_specs=out_spec,
        scratch_shapes=[pltpu.VMEM((tm, tn) if nk > 1 else (SUBLANES, LANES), F32)],
        name=name, compiler_params=_cparams(3))(a, b)


def _layer_norm(r, g, b):
    mu = jnp.mean(r, axis=-1, keepdims=True)
    xc = r - mu
    var = jnp.mean(xc * xc, axis=-1, keepdims=True)
    return xc * lax.rsqrt(var + LN_EPS) * g + b


def _rms(x, g):
    return x * lax.rsqrt(jnp.mean(x * x, axis=-1, keepdims=True) + RMS_EPS) * g


def _sigmoid(x):
    return 1.0 / (1.0 + jnp.exp(-x))


def _silu(x):
    return x * _sigmoid(x)


def _gelu(x):
    return 0.5 * x * (1.0 + jnp.tanh(0.7978845608028654 * (x + 0.044715 * x * x * x)))


def _softplus(x):
    return jnp.maximum(x, 0.0) + jnp.log(1.0 + jnp.exp(-jnp.abs(x)))


def _ln_fwd(h, mo, g, b, cfg, name):
    def fn(h, mo, g, b):
        y = _layer_norm(DN_ALPHA * h + mo, g, b)
        return y, y
    return _rowwise(fn, [h, mo], [g, b], [(cfg.d, F32), (cfg.d, BF16)], [], cfg.tr, name)


def _ln_bwd(h, mo, g, b, douts, scales, cfg, name):
    def fn(h, mo, *rest):
        ds, (g, b) = rest[:-2], rest[-2:]
        dy = ds[0] * scales[0]
        for t, s in zip(ds[1:], scales[1:]):
            dy = dy + t * s
        _, vjp = jax.vjp(_layer_norm, DN_ALPHA * h + mo, g, b)
        dr, dg, db = vjp(dy)
        return dr, dr, dg, db
    d = cfg.d
    return _rowwise(fn, [h, mo] + list(douts), [g, b], [(d, F32), (d, BF16)], [(1, d), (1, d)], cfg.tr, name)


def _ffn_act(gu, cfg, name):
    f = cfg.ffn
    cb = _div_tile(f, 1536)
    nf = f // cb
    lp = gu.shape[0]
    tm = cfg.tm

    def fn(gate, up):
        return _silu(gate.astype(F32)) * up.astype(F32)
    return _tile_call(fn, (lp // tm, nf), [gu, gu],
                      [pl.BlockSpec((tm, cb), lambda i, j: (i, j)),
                       pl.BlockSpec((tm, cb), lambda i, j: (i, j + nf))],
                      [((lp, f), BF16)], [pl.BlockSpec((tm, cb), lambda i, j: (i, j))], name)[0]


def _ffn_act_bwd(gu, dact, cfg, name):
    f = cfg.ffn
    cb = _div_tile(f, 1536)
    nf = f // cb
    lp = gu.shape[0]
    tm = cfg.tm

    def body(gate_ref, up_ref, da_ref, out_ref):
        gate, da = gate_ref[...].astype(F32), da_ref[...].astype(F32)
        sg = _sigmoid(gate)
        j = pl.program_id(1)

        @pl.when(j < nf)
        def _():
            out_ref[...] = (da * up_ref[...].astype(F32) * sg * (1.0 + gate * (1.0 - sg))).astype(BF16)

        @pl.when(j >= nf)
        def _():
            out_ref[...] = (da * gate * sg).astype(BF16)

    return pl.pallas_call(
        body, out_shape=jax.ShapeDtypeStruct((lp, 2 * f), BF16), grid=(lp // tm, 2 * nf),
        in_specs=[pl.BlockSpec((tm, cb), lambda i, j: (i, j % nf)),
                  pl.BlockSpec((tm, cb), lambda i, j: (i, j % nf + nf)),
                  pl.BlockSpec((tm, cb), lambda i, j: (i, j % nf))],
        out_specs=pl.BlockSpec((tm, cb), lambda i, j: (i, j)), name=name, compiler_params=_cparams(2))(gu, gu, dact)


def _ffn_fwd(h, w_gu, w_down, cfg, tag):
    gu = _mm(h, w_gu, "nn", tag + "_gu", out_dtype=BF16)
    act = _ffn_act(gu, cfg, tag + "_act")
    fo = _mm(act, w_down, "nn", tag + "_down")
    return fo, (gu, act)


def _ffn_bwd(h, saved, dfo, w_gu, w_down, cfg, tag):
    gu, act = saved
    dact = _mm(dfo, w_down, "nt", tag + "_dact", out_dtype=BF16)
    d_wdown = _mm(act, dfo, "tn", tag + "_dwdown", out_dtype=BF16)
    dgu = _ffn_act_bwd(gu, dact, cfg, tag + "_dgu")
    dh = _mm(dgu, w_gu, "nt", tag + "_dh", tn_t=2048)
    d_wgu = _mm(h, dgu, "tn", tag + "_dwgu", out_dtype=BF16, out_slabs=w_gu.shape[0])
    return dh, d_wgu, d_wdown


def _small_call(fn, ins, outs, name):
    return _tile_call(fn, (1,), ins, [_whole(x.shape) for x in ins], [(s, F32) for s in outs],
                      [_whole(s) for s in outs], name)


def _perm(x):
    lp, c = x.shape
    return x.reshape(SUBLANES, lp // SUBLANES, c).transpose(1, 0, 2).reshape(lp, c)


def _unperm(x):
    lp, c = x.shape
    return x.reshape(lp // SUBLANES, SUBLANES, c).transpose(1, 0, 2).reshape(lp, c)


def _s5_disc(log_dt, a_re, a_im):
    dt = jnp.exp(log_dt)
    mag = jnp.exp(dt * a_re)
    ab_re = mag * jnp.cos(dt * a_im)
    ab_im = mag * jnp.sin(dt * a_im)
    den = a_re * a_re + a_im * a_im
    nr = ab_re - 1.0
    f_re = (nr * a_re + ab_im * a_im) / den
    f_im = (ab_im * a_re - nr * a_im) / den
    return ab_re, ab_im, f_re, f_im


def _s5_bbar(f_re, f_im, b_re, b_im):
    return f_re * b_re - f_im * b_im, f_re * b_im + f_im * b_re


def _bd_from(w, cfg):
    g, p, j = w.shape
    w4 = w.reshape(cfg.s5_nb, S5_GPB, p, j)
    eye = jnp.eye(S5_GPB, dtype=w.dtype)
    return jnp.einsum("bgpj,gh->bgjhp", w4, eye).reshape(cfg.s5_nb, S5_GPB * j, S5_GPB * p)


def _bd_to(blocks, cfg, p, j):
    b5 = blocks.reshape(cfg.s5_nb, S5_GPB, j, S5_GPB, p)
    eye = jnp.eye(S5_GPB, dtype=blocks.dtype)
    return jnp.einsum("bgjhp,gh->bgpj", b5, eye).reshape(cfg.s5_g, p, j)


def _bd_split(x, w1, w2, cfg, name):
    nb, ci, co = w1.shape
    lp, tm = x.shape[0], cfg.tm

    def fn(x, w1, w2):
        xb = x.astype(BF16)
        return _dot(xb, w1[0].astype(BF16)), _dot(xb, w2[0].astype(BF16))
    wspec = pl.BlockSpec((1, ci, co), lambda i, j: (j, 0, 0))
    ospec = pl.BlockSpec((tm, co), lambda i, j: (i, j))
    return _tile_call(fn, (lp // tm, nb), [x, w1, w2],
                      [pl.BlockSpec((tm, ci), lambda i, j: (i, j)), wspec, wspec],
                      [((lp, nb * co), F32)] * 2, [ospec, ospec], name)


def _bd_join(x1, x2, w1, w2, extra, scale, cfg, name):
    nb, ci, co = w1.shape
    lp, tm = x1.shape[0], cfg.tm

    def fn(x1, x2, w1, w2, e, s):
        return _bdot(x1, w1[0]) + _bdot(x2, w2[0]) + e * s
    xspec = pl.BlockSpec((tm, ci), lambda i, j: (i, j))
    wspec = pl.BlockSpec((1, ci, co), lambda i, j: (j, 0, 0))
    ospec = pl.BlockSpec((tm, co), lambda i, j: (i, j))
    return _tile_call(fn, (lp // tm, nb), [x1, x2, w1, w2, extra, scale],
                      [xspec, xspec, wspec, wspec, ospec, pl.BlockSpec((1, co), lambda i, j: (0, j))],
                      [((lp, nb * co), F32)], [ospec], name)[0]


def _bd_tn(a, b, nb, cfg, name):
    lp, tk = a.shape[0], cfg.tm
    ca, cb = a.shape[1] // nb, b.shape[1] // nb

    def fn(a, b):
        return _bdot(a, b, "tn")[None]
    return _tile_call(fn, (nb, lp // tk), [a, b],
                      [pl.BlockSpec((tk, ca), lambda j, k: (k, j)), pl.BlockSpec((tk, cb), lambda j, k: (k, j))],
                      [((nb, ca, cb), F32)], [pl.BlockSpec((1, ca, cb), lambda j, k: (j, 0, 0))],
                      name, acc=(0,), acc_axis=1)[0]


def _s5_scan(bu_re, bu_im, ab_re, ab_im, cfg, reverse, tag):
    lp, c = bu_re.shape
    nt = cfg.scan_tiles
    rows = lp // nt
    steps = rows // SUBLANES
    cb = _div_tile(c, 512)
    tmap = (lambda j, t: (nt - 1 - t, j)) if reverse else (lambda j, t: (t, j))
    row_spec = pl.BlockSpec((rows, cb), tmap)
    par_spec = pl.BlockSpec((1, cb), lambda j, t: (0, j))
    st_spec = pl.BlockSpec((SUBLANES, cb), lambda j, t: (0, j))
    grid = (c // cb, nt)
    full = jax.ShapeDtypeStruct((lp, c), F32)
    small = jax.ShapeDtypeStruct((SUBLANES, c), F32)

    def offset(k):
        kk = steps - 1 - k if reverse else k
        return pl.multiple_of(kk * SUBLANES, SUBLANES)

    def local_body(bre, bim, ar_ref, ai_ref, sre, sim, fre, fim, pre, pim, st):
        t = pl.program_id(1)

        @pl.when(t == 0)
        def _():
            zero = jnp.zeros((SUBLANES, cb), F32)
            st[0] = zero
            st[1] = zero
            st[2] = zero + 1.0
            st[3] = zero
        ar = jnp.broadcast_to(ar_ref[...], (SUBLANES, cb))
        ai = jnp.broadcast_to(ai_ref[...], (SUBLANES, cb))

        def step(k, carry):
            s_r, s_i, p_r, p_i = carry
            off = offset(k)
            n_r = ar * s_r - ai * s_i + bre[pl.ds(off, SUBLANES), :]
            n_i = ar * s_i + ai * s_r + bim[pl.ds(off, SUBLANES), :]
            sre[pl.ds(off, SUBLANES), :] = n_r
            sim[pl.ds(off, SUBLANES), :] = n_i
            return n_r, n_i, ar * p_r - ai * p_i, ar * p_i + ai * p_r
        s_r, s_i, p_r, p_i = lax.fori_loop(0, steps, step, (st[0], st[1], st[2], st[3]))
        st[0] = s_r
        st[1] = s_i
        st[2] = p_r
        st[3] = p_i

        @pl.when(t == nt - 1)
        def _():
            fre[...] = s_r
            fim[...] = s_i
            pre[...] = p_r
            pim[...] = p_i

    loc_re, loc_im, f_re, f_im, pn_re, pn_im = pl.pallas_call(
        local_body, out_shape=[full, full, small, small, small, small], grid=grid,
        in_specs=[row_spec, row_spec, par_spec, par_spec],
        out_specs=[row_spec, row_spec, st_spec, st_spec, st_spec, st_spec],
        scratch_shapes=[pltpu.VMEM((4, SUBLANES, cb), F32)],
        name=tag + "_local", compiler_params=_cparams(2))(bu_re, bu_im, ab_re, ab_im)

    def fix_body(lre, lim, fre, fim, pre, pim, ar_ref, ai_ref, sre, sim, st):
        t = pl.program_id(1)
        ar = jnp.broadcast_to(ar_ref[...], (SUBLANES, cb))
        ai = jnp.broadcast_to(ai_ref[...], (SUBLANES, cb))

        @pl.when(t == 0)
        def _():
            f_r, f_i = fre[...], fim[...]
            n_r, n_i = pre[0:1, :], pim[0:1, :]
            row = lax.broadcasted_iota(jnp.int32, (SUBLANES, cb), 0)
            c_r = jnp.zeros((1, cb), F32)
            c_i = jnp.zeros((1, cb), F32)
            car_r = jnp.zeros((SUBLANES, cb), F32)
            car_i = jnp.zeros((SUBLANES, cb), F32)
            order = range(SUBLANES - 2, -1, -1) if reverse else range(1, SUBLANES)
            for i in order:
                src = i + 1 if reverse else i - 1
                c_r, c_i = (n_r * c_r - n_i * c_i + f_r[src:src + 1, :],
                            n_r * c_i + n_i * c_r + f_i[src:src + 1, :])
                car_r = jnp.where(row == i, c_r, car_r)
                car_i = jnp.where(row == i, c_i, car_i)
            st[0] = car_r
            st[1] = car_i
            st[2] = ar
            st[3] = ai
        car_r = st[0]
        car_i = st[1]

        def step(k, carry):
            p_r, p_i = carry
            off = offset(k)
            sre[pl.ds(off, SUBLANES), :] = lre[pl.ds(off, SUBLANES), :] + p_r * car_r - p_i * car_i
            sim[pl.ds(off, SUBLANES), :] = lim[pl.ds(off, SUBLANES), :] + p_r * car_i + p_i * car_r
            return ar * p_r - ai * p_i, ar * p_i + ai * p_r
        p_r, p_i = lax.fori_loop(0, steps, step, (st[2], st[3]))
        st[2] = p_r
        st[3] = p_i

    return pl.pallas_call(
        fix_body, out_shape=[full, full], grid=grid,
        in_specs=[row_spec, row_spec, st_spec, st_spec, st_spec, st_spec, par_spec, par_spec],
        out_specs=[row_spec, row_spec], scratch_shapes=[pltpu.VMEM((4, SUBLANES, cb), F32)],
        name=tag + "_fix", compiler_params=_cparams(2))(loc_re, loc_im, f_re, f_im, pn_re, pn_im, ab_re, ab_im)


def _swap(x):
    return jnp.swapaxes(x, -1, -2)


def _s5_prep(w, cfg):
    g, p, j = cfg.s5_g, S5_STATE, S5_GROUP
    gp = g * p
    log_dt = w["l0_s5_log_dt"].reshape(g, 1)
    ab_re, ab_im, f_re, f_im = _small_call(_s5_disc, [log_dt, w["l0_s5_a_re"], w["l0_s5_a_im"]],
                                           [(g, p)] * 4, "s5_disc")
    b_re2 = w["l0_s5_b_re"].transpose(2, 0, 1).reshape(j, gp)
    b_im2 = w["l0_s5_b_im"].transpose(2, 0, 1).reshape(j, gp)
    f_re1, f_im1 = f_re.reshape(1, gp), f_im.reshape(1, gp)
    bb_re2, bb_im2 = _small_call(_s5_bbar, [f_re1, f_im1, b_re2, b_im2], [(j, gp)] * 2, "s5_bbar")
    bb_re = _bd_from(bb_re2.reshape(j, g, p).transpose(1, 2, 0), cfg).astype(BF16)
    bb_im = _bd_from(bb_im2.reshape(j, g, p).transpose(1, 2, 0), cfg).astype(BF16)
    c_re_t = _bd_from(w["l0_s5_c_re"].transpose(0, 2, 1), cfg).astype(BF16)
    c_imn_t = _bd_from(-w["l0_s5_c_im"].transpose(0, 2, 1), cfg).astype(BF16)
    return dict(log_dt=log_dt, f_re1=f_re1, f_im1=f_im1, b_re2=b_re2, b_im2=b_im2,
                ab_re=ab_re.reshape(1, gp), ab_im=ab_im.reshape(1, gp),
                bb_re=bb_re, bb_im=bb_im, bb_re_t=_swap(bb_re), bb_im_t=_swap(bb_im),
                c_re=_swap(c_re_t), c_imn=_swap(c_imn_t), c_re_t=c_re_t, c_imn_t=c_imn_t,
                d=w["l0_s5_d"].reshape(1, cfg.s5_w))


def _s5_fwd(u, prm, w_glu, cfg):
    tm = cfg.tr
    up = _perm(u)
    bu_re, bu_im = _bd_split(up, prm["bb_re"], prm["bb_im"], cfg, "s5_bu")
    s_re, s_im = _s5_scan(bu_re, bu_im, prm["ab_re"], prm["ab_im"], cfg, False, "s5_scan")
    y = _bd_join(s_re, s_im, prm["c_re"], prm["c_imn"], up, prm["d"], cfg, "s5_y")
    g = _rowwise(_gelu, [y], [], [(cfg.s5_w, F32)], [], tm, "s5_gelu")[0]
    z = _mm(g, w_glu, "nn", "s5_glu_mm")
    a_out = _rowwise(lambda g, z: g * _sigmoid(z), [g, z], [], [(cfg.s5_w, F32)], [], tm, "s5_glu")[0]
    return _unperm(a_out), (up, s_re, s_im, y, g, z)


def _s5_bwd(d_a_out, saved, prm, w, w_glu, cfg):
    up, s_re, s_im, y, g, z = saved
    tm, sw, nb = cfg.tr, cfg.s5_w, cfg.s5_nb
    gs, p, j = cfg.s5_g, S5_STATE, S5_GROUP
    gp = gs * p
    dap = _perm(d_a_out)

    def glu_bwd(da, g, z):
        sg = _sigmoid(z)
        return da * sg, da * g * sg * (1.0 - sg)
    dg1, dz = _rowwise(glu_bwd, [dap, g, z], [], [(sw, F32)] * 2, [], tm, "s5_glu_bwd")
    d_wglu = _mm(g, dz, "tn", "s5_dwglu", out_dtype=BF16)
    dg2 = _mm(dz, w_glu, "nt", "s5_dg2")

    def gelu_bwd(dg1, dg2, y, up, d):
        _, vjp = jax.vjp(_gelu, y)
        dy = vjp(dg1 + dg2)[0]
        return dy, dy * d, jnp.sum(dy * up, axis=0, keepdims=True)
    dy, dup_direct, dd = _rowwise(gelu_bwd, [dg1, dg2, y, up], [prm["d"]], [(sw, F32)] * 2, [(1, sw)], tm,
                                  "s5_gelu_bwd")
    ds_re, ds_im = _bd_split(dy, prm["c_re_t"], prm["c_imn_t"], cfg, "s5_ds")
    dc_re_t = _bd_tn(dy, s_re, nb, cfg, "s5_dcre")
    dc_imn_t = _bd_tn(dy, s_im, nb, cfg, "s5_dcim")
    g_re, g_im = _s5_scan(ds_re, ds_im, prm["ab_re"], -prm["ab_im"], cfg, True, "s5_adj")

    cb = _div_tile(gp, 512)
    per = tm // SUBLANES
    spec = pl.BlockSpec((tm, cb), lambda jj, i: (i, jj))
    before = pl.BlockSpec((SUBLANES, cb), lambda jj, i: (jnp.maximum(i * per - 1, 0), jj))
    final = pl.BlockSpec((SUBLANES, cb), lambda jj, i: (cfg.lp // SUBLANES - 1, jj))
    aspec = pl.BlockSpec((1, cb), lambda jj, i: (0, jj))

    def dab(g_r, g_i, s_r, s_i, h_r, h_i, l_r, l_i):
        first = pl.program_id(1) == 0
        row8 = lax.broadcasted_iota(jnp.int32, (SUBLANES, cb), 0)

        def prev(s, h, l):
            wrap = jnp.where(row8 == 0, 0.0, pltpu.roll(l, 1, axis=0))
            return jnp.concatenate([jnp.where(first, wrap, h), s[:tm - SUBLANES]], axis=0)
        p_r, p_i = prev(s_r, h_r, l_r), prev(s_i, h_i, l_i)
        return (jnp.sum(g_r * p_r + g_i * p_i, axis=0, keepdims=True),
                jnp.sum(g_i * p_r - g_r * p_i, axis=0, keepdims=True))
    dab_re, dab_im = _tile_call(dab, (gp // cb, cfg.lp // tm), [g_re, g_im, s_re, s_im, s_re, s_im, s_re, s_im],
                                [spec] * 4 + [before] * 2 + [final] * 2,
                                [((1, gp), F32)] * 2, [aspec] * 2, "s5_dab", acc=(0, 1), acc_axis=1)
    no_scale = jnp.ones((1, sw), F32)
    dup = _bd_join(g_re, g_im, prm["bb_re_t"], prm["bb_im_t"], dup_direct, no_scale, cfg, "s5_dup")
    dbb_re_blk = _bd_tn(up, g_re, nb, cfg, "s5_dbbre")
    dbb_im_blk = _bd_tn(up, g_im, nb, cfg, "s5_dbbim")

    def to2(blk):
        return _bd_to(blk, cfg, p, j).transpose(2, 0, 1).reshape(j, gp)

    def bbar_bwd(f_re, f_im, b_re, b_im, dr, di):
        _, vjp = jax.vjp(_s5_bbar, f_re, f_im, b_re, b_im)
        return vjp((dr, di))
    df_re, df_im, db_re2, db_im2 = _small_call(
        bbar_bwd, [prm["f_re1"], prm["f_im1"], prm["b_re2"], prm["b_im2"], to2(dbb_re_blk), to2(dbb_im_blk)],
        [(1, gp), (1, gp), (j, gp), (j, gp)], "s5_bbar_bwd")

    def disc_bwd(log_dt, a_re, a_im, d1, d2, d3, d4):
        _, vjp = jax.vjp(_s5_disc, log_dt, a_re, a_im)
        return vjp((d1, d2, d3, d4))
    dlog_dt, da_re, da_im = _small_call(
        disc_bwd, [prm["log_dt"], w["l0_s5_a_re"], w["l0_s5_a_im"], dab_re.reshape(gs, p), dab_im.reshape(gs, p),
                   df_re.reshape(gs, p), df_im.reshape(gs, p)], [(gs, 1), (gs, p), (gs, p)], "s5_disc_bwd")
    grads = {
        "l0_s5_log_dt": dlog_dt.reshape(gs), "l0_s5_a_re": da_re, "l0_s5_a_im": da_im,
        "l0_s5_b_re": db_re2.reshape(j, gs, p).transpose(1, 2, 0),
        "l0_s5_b_im": db_im2.reshape(j, gs, p).transpose(1, 2, 0),
        "l0_s5_c_re": _bd_to(dc_re_t, cfg, p, j).transpose(0, 2, 1),
        "l0_s5_c_im": -_bd_to(dc_imn_t, cfg, p, j).transpose(0, 2, 1),
        "l0_s5_d": dd.reshape(sw), "l0_s5_w_glu": d_wglu,
    }
    return _unperm(dup), grads


def _shift(x):
    return jnp.concatenate([jnp.zeros((ATT_SHIFT, x.shape[1]), x.dtype), x[:-ATT_SHIFT]], axis=0)


def _unshift(x):
    return jnp.concatenate([x[ATT_SHIFT:], jnp.zeros((ATT_SHIFT, x.shape[1]), x.dtype)], axis=0)


def _rope_tables(cfg):
    pos = (jnp.arange(cfg.lp) - ATT_SHIFT).astype(F32)
    inv = ROPE_BASE ** (-jnp.arange(0, MLA_ROPE, 2, dtype=F32) / MLA_ROPE)
    ang = pos[:, None] * inv[None, :]
    cos, sin = jnp.cos(ang), jnp.sin(ang)
    z = jnp.zeros((cfg.lp, LANES - MLA_ROPE), F32)
    return jnp.concatenate([cos, cos, z], axis=1), jnp.concatenate([-sin, sin, z], axis=1)


def _swap_halves(x):
    half = MLA_ROPE // 2
    lane = lax.broadcasted_iota(jnp.int32, x.shape, 1)
    left = pltpu.roll(x, LANES - half, axis=1)
    right = pltpu.roll(x, half, axis=1)
    return jnp.where(lane < half, left, jnp.where(lane < 2 * half, right, 0.0))


def _rope(x, cosp, sinp):
    return x * cosp + _swap_halves(x) * sinp


def _rope_t(dy, cosp, sinp):
    return dy * cosp + _swap_halves(dy * sinp)


def _visible(i, j, t):
    row = i * t + lax.broadcasted_iota(jnp.int32, (t, t), 0)
    col = j * t + lax.broadcasted_iota(jnp.int32, (t, t), 1)
    return jnp.logical_and(col // CHUNK <= row // CHUNK, col >= ATT_SHIFT)


class _NoPlan:
    n = n_out = 0
    arrays, out_shapes, scratch = [], [], []


def _side_refs(refs, n_in, n_out, n_scratch, side):
    a = n_in + side.n
    b = a + n_out + side.n_out
    c = b + n_scratch
    own = refs[:n_in] + refs[a:a + n_out] + refs[b:c]
    return own, refs[n_in:a] + refs[a + n_out:b] + refs[c:]


def _needs_mask(i, j):
    return jnp.logical_or(i == j, j == 0)


def _flash_fwd(q, kv, kr, cfg, side=None):
    lp, t, nh = cfg.lp, cfg.tq, cfg.heads
    n = lp // t
    scale = (MLA_NOPE + MLA_ROPE) ** -0.5
    side = side or _NoPlan()

    def body(*refs):
        (q_ref, kv_ref, kr_ref, o_ref, lse_ref, m_s, l_s, acc_s), ex = _side_refs(refs, 3, 2, 3, side)
        hh, i, j = pl.program_id(0), pl.program_id(1), pl.program_id(2)
        if side.n:
            at_tile0 = jnp.logical_and(i == 0, j == 0)
            pl.when(jnp.logical_and(hh == 0, at_tile0))(lambda: side.start(ex))
            pl.when(jnp.logical_and(hh == nh // 2, at_tile0))(lambda: side.relay(ex))

        @pl.when(j == 0)
        def _():
            m_s[...] = jnp.full((t, 1), NEG, F32)
            l_s[...] = jnp.zeros((t, 1), F32)
            acc_s[...] = jnp.zeros((t, MLA_V), F32)

        def tile(masked):
            s = (_dot(q_ref[:, :MLA_NOPE], kv_ref[:, :MLA_NOPE], "nt")
                 + _dot(q_ref[:, MLA_NOPE:], kr_ref[...], "nt")) * scale
            if masked:
                s = jnp.where(_visible(i, j, t), s, NEG)
            m_old = m_s[...]
            m_new = jnp.maximum(m_old, jnp.max(s, axis=1, keepdims=True))
            alpha = jnp.exp(m_old - m_new)
            p = jnp.exp(s - m_new)
            l_s[...] = alpha * l_s[...] + jnp.sum(p, axis=1, keepdims=True)
            acc_s[...] = alpha * acc_s[...] + _dot(p.astype(BF16), kv_ref[:, MLA_NOPE:])
            m_s[...] = m_new
        pl.when(jnp.logical_and(j <= i, _needs_mask(i, j)))(lambda: tile(True))
        pl.when(jnp.logical_and(j < i, j > 0))(lambda: tile(False))

        @pl.when(j == i)
        def _():
            o_ref[...] = acc_s[...] / l_s[...]
            lse_ref[...] = jnp.broadcast_to(m_s[...] + jnp.log(l_s[...]), (t, MLA_V))

        if side.n:
            pl.when(jnp.logical_and(hh == nh - 1, jnp.logical_and(i == n - 1, j == n - 1)))(lambda: side.finish(ex))

    res = pl.pallas_call(
        body, out_shape=[jax.ShapeDtypeStruct((lp, nh * MLA_V), F32)] * 2 + side.out_shapes, grid=(nh, n, n),
        in_specs=[pl.BlockSpec((t, MLA_QW), lambda h, i, j: (i, h)),
                  pl.BlockSpec((t, MLA_QW), lambda h, i, j: (jnp.minimum(i, j), h)),
                  pl.BlockSpec((t, LANES), lambda h, i, j: (jnp.minimum(i, j), 0))] + [HBM_SPEC] * side.n,
        out_specs=[pl.BlockSpec((t, MLA_V), lambda h, i, j: (i, h))] * 2 + [HBM_SPEC] * side.n_out,
        scratch_shapes=[pltpu.VMEM((t, 1), F32), pltpu.VMEM((t, 1), F32), pltpu.VMEM((t, MLA_V), F32)] + side.scratch,
        name="mla_flash_fwd", compiler_params=_cparams(3))(q, kv, kr, *side.arrays)
    return res[0], res[1], res[2:]


def _flash_bwd(q, kv, kr, o, lse, do, cfg, side=None):
    lp, t, nh = cfg.lp, cfg.tq, cfg.heads
    n = lp // t
    scale = (MLA_NOPE + MLA_ROPE) ** -0.5
    side = side or _NoPlan()

    def body(*refs):
        own, ex = _side_refs(refs, 6, 3, 2, side)
        q_ref, kv_ref, kr_ref, o_ref, lse_ref, do_ref, dq_ref, dkv_ref, dkr_ref, dkv_s, dkr_s = own
        hh, j, i = pl.program_id(0), pl.program_id(1), pl.program_id(2)
        if side.n:
            pl.when(jnp.logical_and(hh == 0, jnp.logical_and(i == 0, j == 0)))(lambda: side.start(ex))

        @pl.when(jnp.logical_and(j == 0, i == 0))
        def _():
            dq_ref[...] = jnp.zeros((lp, MLA_QW), F32)

        @pl.when(i == j)
        def _():
            dkv_s[...] = jnp.zeros((t, MLA_QW), F32)
            dkr_s[...] = jnp.zeros((t, LANES), F32)

        def tile(masked):
            qn, qr = q_ref[:, :MLA_NOPE], q_ref[:, MLA_NOPE:]
            kn, v = kv_ref[:, :MLA_NOPE], kv_ref[:, MLA_NOPE:]
            krv = kr_ref[...]
            s = (_dot(qn, kn, "nt") + _dot(qr, krv, "nt")) * scale
            p = jnp.exp(s - lse_ref[:, :1])
            if masked:
                p = jnp.where(_visible(i, j, t), p, 0.0)
            dov = do_ref[...]
            dob = dov.astype(BF16)
            dp = _dot(dob, v, "nt")
            delta = jnp.sum(dov * o_ref[...], axis=1, keepdims=True)
            ds = (p * (dp - delta) * scale).astype(BF16)
            dkv_s[:, MLA_NOPE:] += _dot(p.astype(BF16), dob, "tn")
            dkv_s[:, :MLA_NOPE] += _dot(ds, qn, "tn")
            dkr_s[...] += _dot(ds, qr, "tn")
            off = pl.multiple_of(i * t, t)
            dq_ref[pl.ds(off, t), :MLA_NOPE] += _dot(ds, kn)
            dq_ref[pl.ds(off, t), MLA_NOPE:] += _dot(ds, krv)
        pl.when(jnp.logical_and(i >= j, _needs_mask(i, j)))(lambda: tile(True))
        pl.when(jnp.logical_and(i > j, j > 0))(lambda: tile(False))

        @pl.when(i == n - 1)
        def _():
            dkv_ref[...] = dkv_s[...]
            dkr_ref[0] = dkr_s[...]

        if side.n:
            pl.when(jnp.logical_and(hh == nh - 1, jnp.logical_and(i == n - 1, j == n - 1)))(lambda: side.finish(ex))

    qspec = pl.BlockSpec((t, MLA_QW), lambda h, j, i: (jnp.maximum(i, j), h))
    ospec = pl.BlockSpec((t, MLA_V), lambda h, j, i: (jnp.maximum(i, j), h))
    res = pl.pallas_call(
        body, out_shape=[jax.ShapeDtypeStruct((lp, nh * MLA_QW), F32), jax.ShapeDtypeStruct((lp, nh * MLA_QW), F32),
                         jax.ShapeDtypeStruct((nh, lp, LANES), F32)] + side.out_shapes, grid=(nh, n, n),
        in_specs=[qspec, pl.BlockSpec((t, MLA_QW), lambda h, j, i: (j, h)),
                  pl.BlockSpec((t, LANES), lambda h, j, i: (j, 0)), ospec, ospec, ospec] + [HBM_SPEC] * side.n,
        out_specs=[pl.BlockSpec((lp, MLA_QW), lambda h, j, i: (0, h)),
                   pl.BlockSpec((t, MLA_QW), lambda h, j, i: (j, h)),
                   pl.BlockSpec((1, t, LANES), lambda h, j, i: (h, j, 0))] + [HBM_SPEC] * side.n_out,
        scratch_shapes=[pltpu.VMEM((t, MLA_QW), F32), pltpu.VMEM((t, LANES), F32)] + side.scratch,
        name="mla_flash_bwd", compiler_params=_cparams(3))(q, kv, kr, o, lse, do, *side.arrays)
    return res[0], res[1], res[2], res[3:]


def _pad_heads(w, nh, width):
    r = w.shape[0]
    w3 = w.reshape(r, nh, width)
    return jnp.pad(w3, ((0, 0), (0, 0), (0, MLA_QW - width))).reshape(r, nh * MLA_QW)


def _mla_fwd(q_lat, kv_lat, k_rope_raw, wq, w_uq_p, w_ukv, cfg, side=None):
    tm, nh = cfg.tr, cfg.heads
    ql, kl = _shift(q_lat), _shift(kv_lat)
    kr_raw = jnp.pad(_shift(k_rope_raw), ((0, 0), (0, LANES - MLA_ROPE)))
    cosp, sinp = _rope_tables(cfg)
    qg, kg = wq["l0_mla_q_norm"].reshape(1, -1), wq["l0_mla_kv_norm"].reshape(1, -1)
    qn, kvn = _rowwise(lambda a, b, g1, g2: (_rms(a, g1), _rms(b, g2)), [ql, kl], [qg, kg],
                       [(cfg.q_rank, F32), (cfg.kv_rank, F32)], [], tm, "mla_norm")
    q0 = _mm(qn, w_uq_p, "nn", "mla_q")
    kv = _mm(kvn, w_ukv, "nn", "mla_kv", out_dtype=BF16)

    def rope_fn(q0, kr, cosp, sinp):
        parts = []
        for h in range(nh):
            parts.append(q0[:, h * MLA_QW:h * MLA_QW + MLA_NOPE])
            parts.append(_rope(q0[:, h * MLA_QW + MLA_NOPE:(h + 1) * MLA_QW], cosp, sinp))
        return jnp.concatenate(parts, axis=1), _rope(kr, cosp, sinp)
    q, kr = _rowwise(rope_fn, [q0, kr_raw, cosp, sinp], [], [(nh * MLA_QW, BF16), (LANES, BF16)], [], tm,
                     "mla_rope")
    o, lse, side_out = _flash_fwd(q, kv, kr, cfg, side)
    return _unshift(o), (ql, kl, qn, kvn, q, kv, kr, o, lse, cosp, sinp), side_out


def _mla_bwd(d_b_out, saved, wq, w_uq_p, w_ukv, cfg, side=None):
    ql, kl, qn, kvn, q, kv, kr, o, lse, cosp, sinp = saved
    tm, nh, lp = cfg.tr, cfg.heads, cfg.lp
    dq, dkv, dkr_h, side_out = _flash_bwd(q, kv, kr, o, lse, _shift(d_b_out), cfg, side)

    def rope_bwd(dq, dkr_h, cosp, sinp):
        parts = []
        for h in range(nh):
            parts.append(dq[:, h * MLA_QW:h * MLA_QW + MLA_NOPE])
            parts.append(_rope_t(dq[:, h * MLA_QW + MLA_NOPE:(h + 1) * MLA_QW], cosp, sinp))
        dkr = dkr_h[0]
        for h in range(1, nh):
            dkr = dkr + dkr_h[h]
        return jnp.concatenate(parts, axis=1), _rope_t(dkr, cosp, sinp)
    dq0, dkr_raw = _tile_call(
        rope_bwd, (lp // tm,), [dq, dkr_h, cosp, sinp],
        [_rows(tm, nh * MLA_QW), pl.BlockSpec((nh, tm, LANES), lambda i: (0, i, 0)), _rows(tm, LANES),
         _rows(tm, LANES)],
        [((lp, nh * MLA_QW), F32), ((lp, LANES), F32)], [_rows(tm, nh * MLA_QW), _rows(tm, LANES)], "mla_rope_bwd")
    d_wuq_p = _mm(qn, dq0, "tn", "mla_dwuq")
    dqn = _mm(dq0, w_uq_p, "nt", "mla_dqn")
    d_wukv = _mm(kvn, dkv, "tn", "mla_dwukv")
    dkvn = _mm(dkv, w_ukv, "nt", "mla_dkvn")
    qg, kg = wq["l0_mla_q_norm"].reshape(1, -1), wq["l0_mla_kv_norm"].reshape(1, -1)

    def norm_bwd(ql, kl, dqn, dkvn, g1, g2):
        _, vjp1 = jax.vjp(_rms, ql, g1)
        _, vjp2 = jax.vjp(_rms, kl, g2)
        dql, dg1 = vjp1(dqn)
        dkl, dg2 = vjp2(dkvn)
        return dql, dkl, dg1, dg2
    dql, dkl, dg1, dg2 = _rowwise(norm_bwd, [ql, kl, dqn, dkvn], [qg, kg],
                                  [(cfg.q_rank, F32), (cfg.kv_rank, F32)], [(1, cfg.q_rank), (1, cfg.kv_rank)], tm,
                                  "mla_norm_bwd")
    width = MLA_NOPE + MLA_ROPE
    d_wuq = d_wuq_p.reshape(cfg.q_rank, nh, MLA_QW)[:, :, :width].reshape(cfg.q_rank, nh * width)
    grads = {"l0_mla_q_norm": dg1.reshape(-1), "l0_mla_kv_norm": dg2.reshape(-1), "l0_mla_w_uq": d_wuq,
             "l0_mla_w_ukv": d_wukv}
    return _unshift(dql), _unshift(dkl), _unshift(dkr_raw[:, :MLA_ROPE]), grads, side_out


def _conv_taps(x, halo, first):
    halo = jnp.where(first, 0.0, halo)
    row8 = lax.broadcasted_iota(jnp.int32, halo.shape, 0)
    taps = []
    for s in range(SSD_CONV - 1, 0, -1):
        r = pltpu.roll(x, s, axis=0)
        top = jnp.where(row8 < s, pltpu.roll(halo, s, axis=0), r[:SUBLANES])
        taps.append(jnp.concatenate([top, r[SUBLANES:]], axis=0))
    taps.append(x)
    return taps


def _conv_specs(cfg, lp):
    tm = cfg.tc
    cb = _div_tile(math.gcd(cfg.ssd_inner, cfg.gn), 1024)
    off = cfg.ssd_inner // cb
    per = tm // SUBLANES
    nrow = lp // tm
    main = pl.BlockSpec((tm, cb), lambda i, j: (i, j + off))
    before = pl.BlockSpec((SUBLANES, cb), lambda i, j: (jnp.maximum(i * per - 1, 0), j + off))
    own = pl.BlockSpec((tm, cb), lambda i, j: (i, j))
    after = pl.BlockSpec((SUBLANES, cb), lambda i, j: (jnp.minimum((i + 1) * per, nrow * per - 1), j))
    par = lambda r: pl.BlockSpec((r, cb), lambda i, j: (0, j))
    return tm, cb, nrow, main, before, own, after, par


def _conv_fwd(zx, conv_w, conv_b, cfg):
    lp = zx.shape[0]
    tm, cb, nrow, main, before, own, after, par = _conv_specs(cfg, lp)

    def fn(x, halo, w, b):
        taps = _conv_taps(x, halo, pl.program_id(0) == 0)
        pre = b
        for k in range(SSD_CONV):
            pre = pre + taps[k] * w[k:k + 1, :]
        return _silu(pre)
    return _tile_call(fn, (nrow, cfg.conv_dim // cb), [zx, zx, conv_w, conv_b],
                      [main, before, par(SSD_CONV), par(1)], [((lp, cfg.conv_dim), F32)], [own], "ssd_conv")[0]


def _conv_bwd(zx, conv_w, conv_b, dxs, dbm, dcm, dzx, cfg):
    lp = zx.shape[0]
    tm, cb, nrow, main, before, own, after, par = _conv_specs(cfg, lp)
    ncb = cfg.conv_dim // cb
    nx, nb = cfg.ssd_inner // cb, cfg.gn // cb
    off = nx

    def fn1(x, halo, w, b, d1, d2, d3):
        j = pl.program_id(0)
        da = jnp.where(j < nx, d1, jnp.where(j < nx + nb, d2, d3))
        taps = _conv_taps(x, halo, pl.program_id(1) == 0)
        pre = b
        for k in range(SSD_CONV):
            pre = pre + taps[k] * w[k:k + 1, :]
        sg = _sigmoid(pre)
        dpre = da * sg * (1.0 + pre * (1.0 - sg))
        row8 = lax.broadcasted_iota(jnp.int32, (SUBLANES, cb), 0)
        dw = jnp.zeros((SUBLANES, cb), F32)
        for k in range(SSD_CONV):
            dw = jnp.where(row8 == k, jnp.sum(dpre * taps[k], axis=0, keepdims=True), dw)
        return dpre, dw, jnp.sum(dpre, axis=0, keepdims=True)
    sw = lambda spec: pl.BlockSpec(spec.block_shape, lambda j, i, f=spec.index_map: f(i, j))
    piece = lambda lo, n: pl.BlockSpec((tm, cb), lambda j, i: (i, jnp.clip(j - lo, 0, n - 1)))
    dpre, dw, db = _tile_call(
        fn1, (ncb, nrow), [zx, zx, conv_w, conv_b, dxs, dbm, dcm],
        [sw(main), sw(before), sw(par(SSD_CONV)), sw(par(1)), piece(0, nx), piece(nx, nb), piece(nx + nb, nb)],
        [((lp, cfg.conv_dim), F32), ((SUBLANES, cfg.conv_dim), F32), ((1, cfg.conv_dim), F32)],
        [sw(own), sw(par(SUBLANES)), sw(par(1))], "ssd_conv_bwd1", acc=(1, 2), acc_axis=1)

    def fn2(dp, nxt, w):
        nxt = jnp.where(pl.program_id(0) == nrow - 1, 0.0, nxt)
        row8 = lax.broadcasted_iota(jnp.int32, nxt.shape, 0)
        dx = dp * w[SSD_CONV - 1:SSD_CONV, :]
        for s in range(1, SSD_CONV):
            r = pltpu.roll(dp, tm - s, axis=0)
            bot = jnp.where(row8 >= SUBLANES - s, pltpu.roll(nxt, SUBLANES - s, axis=0), r[tm - SUBLANES:])
            up = jnp.concatenate([r[:tm - SUBLANES], bot], axis=0)
            dx = dx + up * w[SSD_CONV - 1 - s:SSD_CONV - s, :]
        return dx
    dzx = _tile_call(fn2, (nrow, ncb), [dpre, dpre, conv_w], [own, after, par(SSD_CONV)],
                     [(dzx.shape, BF16)], [main], "ssd_conv_bwd2", fill=(dzx, 0))[0]
    return dzx, dw, db


def _ssd_common(x_ref, b_ref, c_ref, dt_ref, dtt_ref, ar_ref, ac_ref, h):
    q = SSD_BLOCK
    x, bm, cm = x_ref[...], b_ref[...], c_ref[...]
    dt, dtt = dt_ref[0], dtt_ref[0]
    row = lax.broadcasted_iota(jnp.int32, (q, q), 0)
    col = lax.broadcasted_iota(jnp.int32, (q, q), 1)
    tri = row >= col
    cs = _dot(tri.astype(F32), dt * ar_ref[0], precision=HI)
    cst = _dot(dtt * ac_ref[0], (row <= col).astype(F32), precision=HI)
    g = _bdot(cm, bm, "nt")
    ch = _bdot(cm, h)
    hpg, gw = dt.shape[1], x.shape[1]
    e = (lax.broadcasted_iota(jnp.int32, (hpg, gw), 1) // SSD_HEAD_DIM
         == lax.broadcasted_iota(jnp.int32, (hpg, gw), 0)).astype(F32)
    et = (lax.broadcasted_iota(jnp.int32, (gw, hpg), 0) // SSD_HEAD_DIM
          == lax.broadcasted_iota(jnp.int32, (gw, hpg), 1)).astype(F32)
    spread = lambda v: _dot(v, e, precision=HI)
    gather = lambda v: _dot(v, et, precision=HI)
    return x, bm, cm, dt, tri, cs, cst, g, ch, spread, gather


def _ssd_specs(cfg, rev):
    q, n, gw, hpg = SSD_BLOCK, SSD_STATE, cfg.gw, cfg.hpg
    nc = cfg.lp // q
    cc = (lambda c: nc - 1 - c) if rev else (lambda c: c)
    boff = cfg.ssd_inner // n
    return dict(
        x=pl.BlockSpec((q, gw), lambda g, c: (cc(c), g)),
        b=pl.BlockSpec((q, n), lambda g, c: (cc(c), boff + g)),
        c=pl.BlockSpec((q, n), lambda g, c: (cc(c), boff + SSD_GROUPS + g)),
        bc_out=pl.BlockSpec((q, n), lambda g, c: (cc(c), g)),
        dt=pl.BlockSpec((1, q, hpg), lambda g, c: (g, cc(c), 0)),
        dtt=pl.BlockSpec((1, hpg, q), lambda g, c: (g, 0, cc(c))),
        ar=pl.BlockSpec((1, 1, hpg), lambda g, c: (g, 0, 0)),
        ac=pl.BlockSpec((1, hpg, 1), lambda g, c: (g, 0, 0)),
        h=pl.BlockSpec((1, n, gw), lambda g, c: (cc(c), 0, g)))


def _ssd_fwd(xbc, dt_g, dtt_g, a_row, a_col, cfg):
    q, n, gw, hpg, lp = SSD_BLOCK, SSD_STATE, cfg.gw, cfg.hpg, cfg.lp
    nc = lp // q
    hd = SSD_HEAD_DIM
    sp = _ssd_specs(cfg, False)

    def body(x_ref, b_ref, c_ref, dt_ref, dtt_ref, ar_ref, ac_ref, y_ref, hp_ref, h_s):
        @pl.when(pl.program_id(1) == 0)
        def _():
            h_s[...] = jnp.zeros((n, gw), F32)
        h = h_s[...]
        hp_ref[0] = h
        x, bm, cm, dt, tri, cs, cst, g, ch, spread, _ = _ssd_common(x_ref, b_ref, c_ref, dt_ref, dtt_ref, ar_ref,
                                                                    ac_ref, h)
        last = cs[q - 1:q, :]
        xdt = x * spread(dt)
        y_off = spread(jnp.exp(cs)) * ch
        xw = xdt * spread(jnp.exp(last - cs))
        for r in range(hpg):
            sl = slice(r * hd, (r + 1) * hd)
            lm = jnp.exp(jnp.where(tri, cs[:, r:r + 1] - cst[r:r + 1, :], NEG))
            y_ref[:, sl] = _bdot(g * lm, xdt[:, sl]) + y_off[:, sl]
        h_s[...] = h * spread(jnp.exp(last)) + _bdot(bm, xw, "tn")

    return pl.pallas_call(
        body, out_shape=[jax.ShapeDtypeStruct((lp, cfg.ssd_inner), F32),
                         jax.ShapeDtypeStruct((nc, n, cfg.ssd_inner), F32)],
        grid=(SSD_GROUPS, nc),
        in_specs=[sp["x"], sp["b"], sp["c"], sp["dt"], sp["dtt"], sp["ar"], sp["ac"]],
        out_specs=[sp["x"], sp["h"]],
        scratch_shapes=[pltpu.VMEM((n, gw), F32)],
        name="ssd_scan", compiler_params=_cparams(2))(xbc, xbc, xbc, dt_g, dtt_g, a_row, a_col)


def _ssd_bwd(xbc, dt_g, dtt_g, a_row, a_col, hprev, dy, dx_gate, cfg):
    q, n, gw, hpg, lp = SSD_BLOCK, SSD_STATE, cfg.gw, cfg.hpg, cfg.lp
    nc = lp // q
    hd = SSD_HEAD_DIM
    sp = _ssd_specs(cfg, True)

    def body(x_ref, b_ref, c_ref, dt_ref, dtt_ref, ar_ref, ac_ref, hp_ref, dy_ref, dxg_ref,
             dx_ref, db_ref, dc_ref, ddt_ref, da_ref, dh_s, dxdt_s):
        @pl.when(pl.program_id(1) == 0)
        def _():
            dh_s[...] = jnp.zeros((n, gw), F32)
            da_ref[...] = jnp.zeros((1, 1, hpg), F32)
        h = hp_ref[0]
        dhn = dh_s[...]
        dy = dy_ref[...]
        x, bm, cm, dt, tri, cs, cst, g, ch, spread, gather = _ssd_common(x_ref, b_ref, c_ref, dt_ref, dtt_ref,
                                                                         ar_ref, ac_ref, h)
        last = cs[q - 1:q, :]
        e, wv, elast = jnp.exp(cs), jnp.exp(last - cs), jnp.exp(last)
        dt_x, w_x = spread(dt), spread(wv)
        xdt = x * dt_x
        dye = dy * spread(e)
        xw = xdt * w_x
        bd = _bdot(bm, dhn)
        de = gather(dy * ch)
        dw = gather(xdt * bd)
        hsum = gather(jnp.sum(dhn * h, axis=0, keepdims=True))
        head_lane = lax.broadcasted_iota(jnp.int32, (q, hpg), 1)
        head_row = lax.broadcasted_iota(jnp.int32, (hpg, q), 0)
        z_rows = jnp.zeros((q, hpg), F32)
        z_cols = jnp.zeros((hpg, q), F32)
        dg = jnp.zeros((q, q), F32)
        for r in range(hpg):
            sl = slice(r * hd, (r + 1) * hd)
            lm = jnp.exp(jnp.where(tri, cs[:, r:r + 1] - cst[r:r + 1, :], NEG))
            m = g * lm
            dyr = dy[:, sl]
            dxdt_s[:, sl] = _bdot(m, dyr, "tn")
            dm = _bdot(dyr, xdt[:, sl], "nt")
            dg = dg + dm * lm
            z = dm * m
            z_rows = jnp.where(head_lane == r, jnp.sum(z, axis=1, keepdims=True), z_rows)
            z_cols = jnp.where(head_row == r, jnp.sum(z, axis=0, keepdims=True), z_cols)
        dxdt = dxdt_s[...] + w_x * bd
        is_last = lax.broadcasted_iota(jnp.int32, (q, 1), 0) == q - 1
        extra = jnp.sum(dw * wv, axis=0, keepdims=True) + elast * hsum
        eye = (lax.broadcasted_iota(jnp.int32, (hpg, hpg), 0)
               == lax.broadcasted_iota(jnp.int32, (hpg, hpg), 1)).astype(F32)
        dcs = (z_rows + de * e - dw * wv + jnp.where(is_last, extra, 0.0)
               - _dot(z_cols, eye, "tn", precision=HI))
        row = lax.broadcasted_iota(jnp.int32, (q, q), 0)
        col = lax.broadcasted_iota(jnp.int32, (q, q), 1)
        dda = _dot((row <= col).astype(F32), dcs, precision=HI)
        ddt_ref[0] = dda * ar_ref[0] + gather(dxdt * x)
        da_ref[0] += jnp.sum(dda * dt, axis=0, keepdims=True)
        dx_ref[...] = dxdt * dt_x + dxg_ref[...]
        dc_ref[...] = _bdot(dg, bm) + _bdot(dye, h, "nt")
        db_ref[...] = _bdot(dg, cm, "tn") + _bdot(xw, dhn, "nt")
        dh_s[...] = dhn * spread(elast) + _bdot(cm, dye, "tn")

    return pl.pallas_call(
        body, out_shape=[jax.ShapeDtypeStruct((lp, cfg.ssd_inner), F32), jax.ShapeDtypeStruct((lp, cfg.gn), F32),
                         jax.ShapeDtypeStruct((lp, cfg.gn), F32), jax.ShapeDtypeStruct((SSD_GROUPS, lp, hpg), F32),
                         jax.ShapeDtypeStruct((SSD_GROUPS, 1, hpg), F32)],
        grid=(SSD_GROUPS, nc),
        in_specs=[sp["x"], sp["b"], sp["c"], sp["dt"], sp["dtt"], sp["ar"], sp["ac"], sp["h"], sp["x"], sp["x"]],
        out_specs=[sp["x"], sp["bc_out"], sp["bc_out"], sp["dt"], sp["ar"]],
        scratch_shapes=[pltpu.VMEM((n, gw), F32), pltpu.VMEM((q, gw), F32)],
        name="ssd_scan_bwd", compiler_params=_cparams(2))(xbc, xbc, xbc, dt_g, dtt_g, a_row, a_col, hprev, dy, dx_gate)


def _gate_fn(y, xs, z, dexp, ng):
    return _rms((y + dexp * xs) * _silu(z), ng)


def _gate_specs(cfg):
    tm, gw = cfg.tm, cfg.gw
    blk = pl.BlockSpec((tm, gw), lambda g, i: (i, g))
    par = pl.BlockSpec((1, gw), lambda g, i: (0, g))
    return blk, par


def _mamba_fwd(h, w, w_in_t, w_out, conv_w, cfg):
    lp, tm, nh, hpg, inner = cfg.lp, cfg.tm, cfg.ssd_heads, cfg.hpg, cfg.ssd_inner
    zx = _mm(h, w_in_t, "nt", "l1_in")
    conv_b = w["l1_conv_b"].reshape(1, -1)
    xbc = _conv_fwd(zx, conv_w, conv_b, cfg)
    dt_raw = zx[:, inner + cfg.conv_dim:inner + cfg.conv_dim + nh]
    dt_bias = w["l1_dt_bias"].reshape(1, nh)
    a_log = w["l1_a_log"].reshape(1, nh)
    dt = _rowwise(lambda r, b: _softplus(r + b), [dt_raw], [dt_bias], [(nh, F32)], [], tm, "ssd_dt")[0]
    a = _small_call(lambda al: -jnp.exp(al), [a_log], [(1, nh)], "ssd_a")[0]
    dt_g = dt.reshape(lp, SSD_GROUPS, hpg).transpose(1, 0, 2)
    dtt_g = dt_g.transpose(0, 2, 1)
    a_row, a_col = a.reshape(SSD_GROUPS, 1, hpg), a.reshape(SSD_GROUPS, hpg, 1)
    y, hprev = _ssd_fwd(xbc, dt_g, dtt_g, a_row, a_col, cfg)
    dexp = jnp.repeat(w["l1_d"], SSD_HEAD_DIM).reshape(1, inner)
    ng = w["l1_norm_g"].reshape(1, inner)
    blk, par = _gate_specs(cfg)
    yn = _tile_call(_gate_fn, (SSD_GROUPS, lp // tm), [y, xbc, zx, dexp, ng], [blk, blk, blk, par, par],
                    [((lp, inner), F32)], [blk], "ssd_gate")[0]
    mo = _mm(yn, w_out, "nn", "l1_out")
    return mo, (zx, xbc, dt_raw, dt_g, dtt_g, a, a_row, a_col, y, hprev, dexp, ng, yn)


def _mamba_bwd(h, saved, dmo, w, w_in_t, w_out, conv_w, cfg):
    zx, xbc, dt_raw, dt_g, dtt_g, a, a_row, a_col, y, hprev, dexp, ng, yn = saved
    lp, tm, nh, hpg, inner = cfg.lp, cfg.tm, cfg.ssd_heads, cfg.hpg, cfg.ssd_inner
    d_wout = _mm(yn, dmo, "tn", "l1_dwout", out_dtype=BF16)
    dyn = _mm(dmo, w_out, "nt", "l1_dyn")
    blk, par = _gate_specs(cfg)

    def gate_bwd(y, xs, z, dexp, ng, dyn):
        _, vjp = jax.vjp(_gate_fn, y, xs, z, dexp, ng)
        return vjp(dyn)
    dy, dxs_gate, dzx, ddexp, dng = _tile_call(
        gate_bwd, (SSD_GROUPS, lp // tm), [y, xbc, zx, dexp, ng, dyn], [blk, blk, blk, par, par, blk],
        [((lp, inner), F32)] * 2 + [((lp, cfg.l1_inp), BF16)] + [((1, inner), F32)] * 2, [blk, blk, blk, par, par],
        "ssd_gate_bwd", acc=(3, 4), acc_axis=1)
    dxs, dbm, dcm, ddt_g, da_g = _ssd_bwd(xbc, dt_g, dtt_g, a_row, a_col, hprev, dy, dxs_gate, cfg)
    conv_b = w["l1_conv_b"].reshape(1, -1)
    dzx, dconv_w, dconv_b = _conv_bwd(zx, conv_w, conv_b, dxs, dbm, dcm, dzx, cfg)
    assert cfg.l1_inp - inner - cfg.conv_dim == LANES
    ddt = jnp.pad(ddt_g.transpose(1, 0, 2).reshape(lp, nh), ((0, 0), (0, LANES - nh)))
    dt_bias = jnp.pad(w["l1_dt_bias"].reshape(1, nh), ((0, 0), (0, LANES - nh)))
    last = (inner + cfg.conv_dim) // LANES
    tail = pl.BlockSpec((tm, LANES), lambda i: (i, last))

    def dt_bwd(ddt, r, b):
        lane = lax.broadcasted_iota(jnp.int32, ddt.shape, 1)
        d = jnp.where(lane < nh, ddt * _sigmoid(r + b), 0.0)
        return d, jnp.sum(d, axis=0, keepdims=True)
    dzx, ddt_bias = _tile_call(dt_bwd, (lp // tm,), [ddt, zx, dt_bias], [_rows(tm, LANES), tail, _whole((1, LANES))],
                               [(dzx.shape, BF16), ((1, LANES), F32)], [tail, _whole((1, LANES))], "ssd_dt_bwd",
                               acc=(1,), fill=(dzx, 0))
    ddt_bias = ddt_bias[:, :nh]
    da_log, dd = _small_call(lambda da, a, dde: (da * a, jnp.sum(dde, axis=1, keepdims=True)),
                             [da_g.reshape(1, nh), a, ddexp.reshape(nh, SSD_HEAD_DIM)], [(1, nh), (nh, 1)],
                             "ssd_small_bwd")
    d_win_t = _mm(dzx, h, "tn", "l1_dwin", out_dtype=BF16, tm_t=1152)
    dh = _mm(dzx, w_in_t, "nn", "l1_dh", tn_t=2048)
    grads = {"l1_w_in": d_win_t, "l1_conv_w": dconv_w[:SSD_CONV], "l1_conv_b": dconv_b.reshape(-1),
             "l1_dt_bias": ddt_bias.reshape(-1), "l1_a_log": da_log.reshape(-1), "l1_d": dd.reshape(-1),
             "l1_norm_g": dng.reshape(-1), "l1_w_out": d_wout}
    return dh, grads


def _local_step(x, target, w, cfg, late_weights=None, early_grads=None):
    lp, n, d, tm, sw = cfg.lp, cfg.n, cfg.d, cfg.tr, cfg.s5_w
    row = lambda name: w[name].reshape(1, -1)
    h0 = jnp.concatenate([w["meta_tokens"], x, jnp.zeros((lp - n, d), F32)], axis=0)
    proj = _mm(h0, w["l0_w_in"], "nn", "l0_in")
    o1, o2, o3 = sw, sw + cfg.q_rank, sw + cfg.q_rank + cfg.kv_rank
    prm = _s5_prep(w, cfg)
    a_out, s5_saved = _s5_fwd(proj[:, :o1], prm, w["l0_s5_w_glu"], cfg)
    b_out, mla_saved, arrived = _mla_fwd(proj[:, o1:o2], proj[:, o2:o3], proj[:, o3:], w, w["l0_mla_w_uq_p"],
                                         w["l0_mla_w_ukv"], cfg, late_weights[0] if late_weights else None)
    if late_weights:
        w = dict(w, **late_weights[1](arrived))
    mix = jnp.concatenate([a_out, b_out], axis=1).astype(BF16)
    mo0 = _mm(mix, w["l0_w_out"], "nn", "l0_out")
    h1, h1b = _ln_fwd(h0, mo0, row("l0_ln1_g"), row("l0_ln1_b"), cfg, "l0_ln1")
    fo0, ffn0 = _ffn_fwd(h1b, w["l0_w_gu"], w["l0_ffn_w_down"], cfg, "l0_ffn")
    h2, h2b = _ln_fwd(h1, fo0, row("l0_ln2_g"), row("l0_ln2_b"), cfg, "l0_ln2")
    mo1, mam = _mamba_fwd(h2b, w, w["l1_w_in_t"], w["l1_w_out"], w["l1_conv_w"], cfg)
    h3, h3b = _ln_fwd(h2, mo1, row("l1_ln1_g"), row("l1_ln1_b"), cfg, "l1_ln1")
    fo1, ffn1 = _ffn_fwd(h3b, w["l1_w_gu"], w["l1_ffn_w_down"], cfg, "l1_ffn")
    h4, _ = _ln_fwd(h3, fo1, row("l1_ln2_g"), row("l1_ln2_b"), cfg, "l1_ln2")
    tgt = jnp.concatenate([jnp.zeros((N_META, d), F32), target, jnp.zeros((lp - n, d), F32)], axis=0)

    def loss_fn(y, t):
        r = pl.program_id(0) * tm + lax.broadcasted_iota(jnp.int32, (tm, 1), 0)
        diff = jnp.where(jnp.logical_and(r >= N_META, r < n), y - t, 0.0)
        return diff * (1.0 / d), jnp.sum(diff * diff, axis=0, keepdims=True) * (0.5 / d)
    dh4, loss_lanes = _rowwise(loss_fn, [h4, tgt], [], [(d, F32)], [(1, d)], tm, "loss")
    grads = {}
    dr4, dr4b, dg, db = _ln_bwd(h3, fo1, row("l1_ln2_g"), row("l1_ln2_b"), [dh4], [1.0], cfg, "l1_ln2_bwd")
    grads["l1_ln2_g"], grads["l1_ln2_b"] = dg.reshape(-1), db.reshape(-1)
    dh3, grads["l1_w_gu"], grads["l1_ffn_w_down"] = _ffn_bwd(h3b, ffn1, dr4b, w["l1_w_gu"], w["l1_ffn_w_down"], cfg,
                                                             "l1_ffn")
    dr3, dr3b, dg, db = _ln_bwd(h2, mo1, row("l1_ln1_g"), row("l1_ln1_b"), [dr4, dh3], [DN_ALPHA, 1.0], cfg,
                                "l1_ln1_bwd")
    grads["l1_ln1_g"], grads["l1_ln1_b"] = dg.reshape(-1), db.reshape(-1)
    dh2, mg = _mamba_bwd(h2b, mam, dr3b, w, w["l1_w_in_t"], w["l1_w_out"], w["l1_conv_w"], cfg)
    grads.update(mg)
    dr2, dr2b, dg, db = _ln_bwd(h1, fo0, row("l0_ln2_g"), row("l0_ln2_b"), [dr3, dh2], [DN_ALPHA, 1.0], cfg,
                                "l0_ln2_bwd")
    grads["l0_ln2_g"], grads["l0_ln2_b"] = dg.reshape(-1), db.reshape(-1)
    dh1, grads["l0_w_gu"], grads["l0_ffn_w_down"] = _ffn_bwd(h1b, ffn0, dr2b, w["l0_w_gu"], w["l0_ffn_w_down"], cfg,
                                                             "l0_ffn")
    dr1, dr1b, dg, db = _ln_bwd(h0, mo0, row("l0_ln1_g"), row("l0_ln1_b"), [dr2, dh1], [DN_ALPHA, 1.0], cfg,
                                "l0_ln1_bwd")
    grads["l0_ln1_g"], grads["l0_ln1_b"] = dg.reshape(-1), db.reshape(-1)
    grads["l0_w_out"] = _mm(mix, dr1b, "tn", "l0_dwout", out_dtype=BF16)
    dmix = _mm(dr1b, w["l0_w_out"], "nt", "l0_dmix")
    du, sg = _s5_bwd(dmix[:, :sw], s5_saved, prm, w, w["l0_s5_w_glu"], cfg)
    dql, dkl, dkr, ag, exchanged = _mla_bwd(dmix[:, sw:], mla_saved, w, w["l0_mla_w_uq_p"], w["l0_mla_w_ukv"], cfg,
                                            early_grads(grads) if early_grads else None)
    grads.update(sg)
    grads.update(ag)
    dproj = jnp.concatenate([du, dql, dkl, dkr], axis=1)
    grads["l0_w_in"] = _mm(h0, dproj, "tn", "l0_dwin", out_dtype=BF16)
    dh0m = _mm(dproj, w["l0_w_in"], "nt", "l0_dh")
    dh0 = _rowwise(lambda a, b: DN_ALPHA * a + b, [dr1, dh0m], [], [(d, F32)], [], tm, "l0_dh0")[0]
    grads["meta_tokens"] = dh0[:N_META]
    return loss_lanes, dh0[N_META:n], grads, exchanged


FLAT_W = 1024
N_SLOTS = 4
HBM_SPEC = pl.BlockSpec(memory_space=pltpu.HBM)


def _place():
    x, y, c = lax.axis_index("x"), lax.axis_index("y"), lax.axis_index("c")
    chips = [(1 - x, y), (x, 1 - y), (1 - x, 1 - y)]
    return x, y, c, chips


def _remote(src, dst, ssem, rsem, dev):
    return pltpu.make_async_remote_copy(src_ref=src, dst_ref=dst, send_sem=ssem, recv_sem=rsem, device_id=dev,
                                        device_id_type=MESH)


class _GatherPlan:
    def __init__(self, shards, groups):
        self.arrays = list(shards)
        self.n = n = len(shards)
        self.rows = [s.shape[0] for s in shards]
        self.place = {t: (g, row0) for g, members in enumerate(groups) for t, row0 in members}
        self.n_out = len(groups)
        self.out_shapes = []
        for members in groups:
            t0 = members[0][0]
            rows = max(row0 + N_SLOTS * shards[t].shape[0] for t, row0 in members)
            self.out_shapes.append(jax.ShapeDtypeStruct((rows, shards[t0].shape[1]), shards[t0].dtype))
        sems = pltpu.SemaphoreType.DMA((3 * n,))
        self.scratch = [sems, sems, sems, sems, pltpu.SemaphoreType.DMA((n,))]

    def _copies(self, refs):
        n = self.n
        srcs, outs = refs[:n], refs[n:n + self.n_out]
        send_sems, recv_sems, fsend, frecv, lsems = refs[n + self.n_out:]
        x, y, c, chips = _place()
        me = 2 * x + y
        sib = (x, y, 1 - c)

        def rows_of(t, slot, half):
            r = self.rows[t]
            g, row0 = self.place[t]
            return outs[g].at[pl.ds(row0 + slot * r + half * (r // 2), r // 2)]
        local, send, arrive, relay, arrive_sib = [], [], [], [], []
        for t in range(n):
            r = self.rows[t]
            g, row0 = self.place[t]
            local.append(pltpu.make_async_copy(srcs[t], outs[g].at[pl.ds(row0 + me * r, r)], lsems.at[t]))
            mine = srcs[t].at[pl.ds(c * (r // 2), r // 2)]
            for j, (cx, cy) in enumerate(chips):
                k = 3 * t + j
                send.append(_remote(mine, rows_of(t, me, c), send_sems.at[k], recv_sems.at[k], (cx, cy, c)))
                got = rows_of(t, 2 * cx + cy, c)
                arrive.append(_remote(got, got, send_sems.at[k], recv_sems.at[k], (cx, cy, c)))
                relay.append(_remote(got, got, fsend.at[k], frecv.at[k], sib))
                got_sib = rows_of(t, 2 * cx + cy, 1 - c)
                arrive_sib.append(_remote(got_sib, got_sib, fsend.at[k], frecv.at[k], sib))
        return local, send, arrive, relay, arrive_sib

    def start(self, refs):
        local, send, _, _, _ = self._copies(refs)
        for cp in local + send:
            cp.start()

    def relay(self, refs):
        _, _, arrive, relay, _ = self._copies(refs)
        for a, r in zip(arrive, relay):
            a.wait_recv()
            r.start()

    def finish(self, refs):
        local, send, _, relay, arrive_sib = self._copies(refs)
        for cp in arrive_sib:
            cp.wait_recv()
        for cp in send + relay:
            cp.wait_send()
        for cp in local:
            cp.wait()


class _ExchangePlan:
    def __init__(self, items):
        self.items = items
        self.arrays = [a for a, _, _ in items]
        self.n = self.n_out = n = len(items)
        self.out_shapes = [jax.ShapeDtypeStruct((3, rps, a.shape[1]), a.dtype) for a, _, rps in items]
        sems = pltpu.SemaphoreType.DMA((3 * n,))
        self.scratch = [sems, sems]

    def _copies(self, refs):
        n = self.n
        srcs, outs = refs[:n], refs[n:2 * n]
        send_sems, recv_sems = refs[2 * n:]
        x, y, c, chips = _place()
        cps = []
        for t, (_, row0, rps) in enumerate(self.items):
            for j, (cx, cy) in enumerate(chips):
                cps.append(_remote(srcs[t].at[pl.ds(row0 + (2 * cx + cy) * rps, rps)], outs[t].at[j],
                                   send_sems.at[3 * t + j], recv_sems.at[3 * t + j], (cx, cy, c)))
        return cps

    def start(self, refs):
        for cp in self._copies(refs):
            cp.start()

    def relay(self, refs):
        pass

    def finish(self, refs):
        for cp in self._copies(refs):
            cp.wait()


def _run_plan(plan, name):
    def body(*refs):
        plan.start(refs)
        plan.relay(refs)
        plan.finish(refs)
    return pl.pallas_call(body, out_shape=plan.out_shapes, in_specs=[HBM_SPEC] * plan.n,
                          out_specs=[HBM_SPEC] * plan.n_out, scratch_shapes=plan.scratch, name=name)(*plan.arrays)


def _exchange_sibling(arrays):
    n = len(arrays)

    def body(*refs):
        srcs, outs = refs[:n], refs[n:2 * n]
        ssems, rsems = refs[2 * n:]
        x, y, c, _ = _place()
        cps = []
        for t in range(n):
            cp = _remote(srcs[t], outs[t], ssems.at[t], rsems.at[t], (x, y, 1 - c))
            cp.start()
            cps.append(cp)
        for cp in cps:
            cp.wait()

    sems = pltpu.SemaphoreType.DMA((n,))
    return pl.pallas_call(
        body, out_shape=[jax.ShapeDtypeStruct(a.shape, a.dtype) for a in arrays], in_specs=[HBM_SPEC] * n,
        out_specs=[HBM_SPEC] * n, scratch_shapes=[sems, sems], name="exchange_sibling")(*arrays)


def _gather_all(v):
    flips = [(fx, fy, fc) for fx in (0, 1) for fy in (0, 1) for fc in (0, 1)][1:]

    def body(src, out, send_sems, recv_sems, lsem):
        x, y, c, _ = _place()
        local = pltpu.make_async_copy(src, out.at[4 * x + 2 * y + c], lsem)
        local.start()
        cps = []
        for k, (fx, fy, fc) in enumerate(flips):
            px, py, pc = (1 - x if fx else x), (1 - y if fy else y), (1 - c if fc else c)
            cp = _remote(src, out.at[4 * x + 2 * y + c], send_sems.at[k], recv_sems.at[k], (px, py, pc))
            cp.start()
            cps.append(cp)
        for cp in cps:
            cp.wait()
        local.wait()

    return pl.pallas_call(
        body, out_shape=jax.ShapeDtypeStruct((8,) + v.shape, v.dtype), in_specs=[HBM_SPEC], out_specs=HBM_SPEC,
        scratch_shapes=[pltpu.SemaphoreType.DMA((7,)), pltpu.SemaphoreType.DMA((7,)), pltpu.SemaphoreType.DMA],
        name="gather_all")(v)


def _flat_rows(n_elems, row_unit):
    return -(-n_elems // (FLAT_W * row_unit)) * row_unit


def _to_flat(pieces, rows):
    flat = jnp.concatenate([p.reshape(-1) for p in pieces])
    return jnp.pad(flat, (0, rows * FLAT_W - flat.shape[0])).reshape(rows, FLAT_W)


def _sum_parts(parts, name, tm=512):
    rows = parts[0].shape[1]
    tm = _div_tile(rows, tm, SUBLANES)

    def fn(*ps):
        acc = None
        for p in ps:
            for k in range(p.shape[0]):
                acc = p[k].astype(F32) if acc is None else acc + p[k].astype(F32)
        return acc
    return _tile_call(fn, (rows // tm,), parts,
                      [pl.BlockSpec((p.shape[0], tm, FLAT_W), lambda i: (0, i, 0)) for p in parts],
                      [((rows, FLAT_W), F32)], [_rows(tm, FLAT_W)], name)[0]


ELEMENTWISE_BLOCK = 1 << 19


def _row_tile(rows, cols, unit):
    return _div_tile(rows, max(unit, ELEMENTWISE_BLOCK // cols), unit)


def _sum_slot(g, row0, rps, others, me, name):
    c = g.shape[1]
    tm = _row_tile(rps, c, 2 * SUBLANES)
    nrb = rps // tm
    assert row0 % tm == 0

    def body(me_ref, g_ref, o_ref, out_ref):
        out_ref[...] = ((g_ref[...].astype(F32) + o_ref[0].astype(F32)) + o_ref[1].astype(F32)) + o_ref[2].astype(F32)

    grid_spec = pltpu.PrefetchScalarGridSpec(
        num_scalar_prefetch=1, grid=(nrb,),
        in_specs=[pl.BlockSpec((tm, c), lambda i, me_ref: (row0 // tm + me_ref[0] * nrb + i, 0)),
                  pl.BlockSpec((3, tm, c), lambda i, me_ref: (0, i, 0))],
        out_specs=pl.BlockSpec((tm, c), lambda i, me_ref: (i, 0)))
    return pl.pallas_call(body, out_shape=jax.ShapeDtypeStruct((rps, c), F32), grid_spec=grid_spec, name=name,
                          compiler_params=_cparams(1))(me, g, others)


def _adamw(gparts, w, m, v, name, tm=None):
    rows, cols = w.shape
    tm = _row_tile(rows, cols, SUBLANES) if tm is None else _div_tile(rows, tm, SUBLANES)
    ng = len(gparts)

    def fn(*a):
        g = a[0]
        for t in a[1:ng]:
            g = g + t
        w, m, v = a[ng:]
        m = ADAM_B1 * m + (1.0 - ADAM_B1) * g
        v = ADAM_B2 * v + (1.0 - ADAM_B2) * (g * g)
        m_hat = m / (1.0 - ADAM_B1 ** ADAM_STEP)
        v_hat = v / (1.0 - ADAM_B2 ** ADAM_STEP)
        delta = -ADAM_LR * (m_hat / (jnp.sqrt(v_hat) + ADAM_EPS) + ADAM_WD * w)
        return g, delta, m, v
    ins = list(gparts) + [w, m, v]
    return _tile_call(fn, (rows // tm,), ins, [_rows(tm, cols)] * len(ins), [((rows, cols), F32)] * 4,
                      [_rows(tm, cols)] * 4, name)


FIRST = ("l0_w_in", "l0_s5_w_glu", "l0_mla_w_uq", "l0_mla_w_ukv")
REST = ("l0_w_out", "l0_ffn_w_gate", "l0_ffn_w_up", "l0_ffn_w_down", "l1_w_in", "l1_w_out", "l1_ffn_w_gate",
        "l1_ffn_w_up", "l1_ffn_w_down")
BIG = FIRST + REST
TINY = ("meta_tokens", "l1_conv_w")
REPLICATED = ("l0_s5_log_dt", "l0_s5_a_re", "l0_s5_a_im", "l0_s5_b_re", "l0_s5_b_im", "l0_s5_c_re", "l0_s5_c_im",
              "l0_s5_d", "l0_mla_q_norm", "l0_mla_kv_norm", "l0_ln1_g", "l0_ln1_b", "l0_ln2_g", "l0_ln2_b",
              "l1_conv_b", "l1_dt_bias", "l1_a_log", "l1_d", "l1_norm_g", "l1_ln1_g", "l1_ln1_b", "l1_ln2_g",
              "l1_ln2_b")
WEIGHTS = ("meta_tokens", "l0_w_in", "l0_s5_log_dt", "l0_s5_a_re", "l0_s5_a_im", "l0_s5_b_re", "l0_s5_b_im",
           "l0_s5_c_re", "l0_s5_c_im", "l0_s5_d", "l0_s5_w_glu", "l0_mla_q_norm", "l0_mla_w_uq", "l0_mla_kv_norm",
           "l0_mla_w_ukv", "l0_w_out", "l0_ln1_g", "l0_ln1_b", "l0_ffn_w_gate", "l0_ffn_w_up", "l0_ffn_w_down",
           "l0_ln2_g", "l0_ln2_b", "l1_w_in", "l1_conv_w", "l1_conv_b", "l1_dt_bias", "l1_a_log", "l1_d",
           "l1_norm_g", "l1_w_out", "l1_ln1_g", "l1_ln1_b", "l1_ffn_w_gate", "l1_ffn_w_up", "l1_ffn_w_down",
           "l1_ln2_g", "l1_ln2_b")
def _split_flat(flat2d, shapes):
    flat = flat2d.reshape(-1)
    out, off = [], 0
    for s in shapes:
        n = math.prod(s)
        out.append(flat[off:off + n].reshape(s))
        off += n
    return out


def _cols_to_slots(full):
    r, c = full.shape
    return full.reshape(r, N_SLOTS, c // N_SLOTS).transpose(1, 0, 2).reshape(N_SLOTS * r, c // N_SLOTS)


def _slots_to_cols(slabs):
    r4, c = slabs.shape
    return slabs.reshape(N_SLOTS, r4 // N_SLOTS, c).transpose(1, 0, 2).reshape(r4 // N_SLOTS, N_SLOTS * c)


def _step(cfg, x, loss_target, ws, ms, vs):
    d, f = cfg.d, cfg.ffn
    me = 2 * lax.axis_index("x") + lax.axis_index("y")
    kinds = ("grad", "delta", "new_m", "new_v")
    def gather_plan(names):
        groups = []
        for t, name in enumerate(names):
            if name.endswith("_ffn_w_up"):
                groups[-1].append((t, N_SLOTS * d))
            else:
                groups.append([(t, 0)])
        heads = [names[members[0][0]] for members in groups]
        return _GatherPlan([ws[name].astype(BF16) for name in names], groups), heads
    plan, heads = gather_plan(FIRST)
    got = dict(zip(heads, _run_plan(plan, "gather_first")))
    w = {"l0_w_in": got["l0_w_in"], "l0_s5_w_glu": got["l0_s5_w_glu"],
         "l0_mla_w_uq_p": _pad_heads(_slots_to_cols(got["l0_mla_w_uq"]), cfg.heads, MLA_NOPE + MLA_ROPE),
         "l0_mla_w_ukv": _slots_to_cols(got["l0_mla_w_ukv"])}
    late_plan, late_heads = gather_plan(REST)

    def late_weights(arrived):
        got = dict(zip(late_heads, arrived))
        lw = {name: got[name] for name in ("l0_w_out", "l0_ffn_w_down", "l1_w_out", "l1_ffn_w_down")}
        for l in ("l0", "l1"):
            lw[l + "_w_gu"] = got[l + "_ffn_w_gate"].reshape(2 * N_SLOTS, d, f // N_SLOTS)
        w_in_t = got["l1_w_in"].reshape(N_SLOTS, d, cfg.l1_in // N_SLOTS).transpose(0, 2, 1).reshape(cfg.l1_in, d)
        lw["l1_w_in_t"] = jnp.pad(w_in_t, ((0, cfg.l1_inp - cfg.l1_in), (0, 0)))
        return lw
    tiny_shapes = [ws[name].shape for name in TINY]
    trows = _flat_rows(sum(math.prod(s) for s in tiny_shapes), SUBLANES)
    tiny = _gather_all(_to_flat([ws[name] for name in TINY], trows))[0::2]
    for k, name in enumerate(TINY):
        blocks = jnp.stack([_split_flat(tiny[s], tiny_shapes)[k] for s in range(N_SLOTS)])
        w[name] = blocks.transpose(1, 0, 2).reshape(blocks.shape[1], -1)
    for name in REPLICATED:
        w[name] = ws[name]
    rps = {name: ws[name].shape[1 if name == "l1_w_in" else 0] for name in BIG}

    def triples(grads, names):
        out = []
        for name in names:
            if name.endswith(("_ffn_w_gate", "_ffn_w_up")):
                g = grads[name[:3] + "w_gu"].reshape(2 * N_SLOTS * d, f // N_SLOTS)
                out.append((g, N_SLOTS * d if name.endswith("_up") else 0, rps[name]))
            elif name in ("l0_mla_w_uq", "l0_mla_w_ukv"):
                out.append((_cols_to_slots(grads[name]).astype(BF16), 0, rps[name]))
            else:
                out.append((grads[name], 0, rps[name]))
        return out
    held = {}

    def early_grads(grads):
        held["rest"] = triples(grads, REST)
        return _ExchangePlan(held["rest"])
    loss_lanes, grad_x, grads, others_rest = _local_step(x[0], loss_target[0], w, cfg, (late_plan, late_weights),
                                                         early_grads)
    first = triples(grads, FIRST)
    others = list(_run_plan(_ExchangePlan(first), "exchange_first")) + list(others_rest)
    me1 = me.reshape(1).astype(jnp.int32)
    parts = [_sum_slot(a, row0, r, o, me1, "sum_" + name)
             for (a, row0, r), o, name in zip(first + held["rest"], others, BIG)]
    sibs = _exchange_sibling(parts)
    res = {}
    for name, p, q in zip(BIG, parts, sibs):
        if name == "l1_w_in":
            p, q = p.T, q.T
        for kind, arr in zip(kinds, _adamw([p, q], ws[name], ms[name], vs[name], "adamw_" + name)):
            res[kind + "_" + name] = arr
    rep_shapes = [(1, d)] + [ws[name].shape for name in REPLICATED]
    all_shapes = rep_shapes + [grads[name].shape for name in TINY]
    srows = _flat_rows(sum(math.prod(s) for s in all_shapes), SUBLANES)
    small = _to_flat([loss_lanes] + [grads[name] for name in REPLICATED + TINY], srows)
    total = _sum_parts([_gather_all(small)], "sum_small", tm=srows)
    zero = jnp.zeros((1, d), F32)
    flat = lambda dct: _to_flat([zero] + [dct[name] for name in REPLICATED], srows)
    outs = _adamw([total], flat(ws), flat(ms), flat(vs), "adamw_replicated", tm=srows)
    for kind, arr in zip(kinds, outs):
        vals = _split_flat(arr, rep_shapes)
        for name, val in zip(REPLICATED, vals[1:]):
            res[kind + "_" + name] = val
    for name, g in zip(TINY, _split_flat(total, all_shapes)[len(rep_shapes):]):
        cols = ws[name].shape[1]
        mine = lax.dynamic_slice_in_dim(g, me * cols, cols, axis=1)
        for kind, arr in zip(kinds, _adamw([mine], ws[name], ms[name], vs[name], "adamw_" + name)):
            res[kind + "_" + name] = arr
    loss = _small_call(lambda t: jnp.sum(t, axis=1, keepdims=True), [_split_flat(total, rep_shapes)[0]], [(1, 1)],
                       "loss_sum")[0].reshape(())
    ordered = [res[kind + "_" + name] for kind in ("grad", "delta", "new_m", "new_v") for name in WEIGHTS]
    return (loss, grad_x[None]) + tuple(ordered)


def kernel(x, meta_tokens, l0_w_in, l0_s5_log_dt, l0_s5_a_re, l0_s5_a_im, l0_s5_b_re, l0_s5_b_im, l0_s5_c_re,
           l0_s5_c_im, l0_s5_d, l0_s5_w_glu, l0_mla_q_norm, l0_mla_w_uq, l0_mla_kv_norm, l0_mla_w_ukv, l0_w_out,
           l0_ln1_g, l0_ln1_b, l0_ffn_w_gate, l0_ffn_w_up, l0_ffn_w_down, l0_ln2_g, l0_ln2_b, l1_w_in, l1_conv_w,
           l1_conv_b, l1_dt_bias, l1_a_log, l1_d, l1_norm_g, l1_w_out, l1_ln1_g, l1_ln1_b, l1_ffn_w_gate,
           l1_ffn_w_up, l1_ffn_w_down, l1_ln2_g, l1_ln2_b, loss_target, m_meta_tokens, m_l0_w_in, m_l0_s5_log_dt,
           m_l0_s5_a_re, m_l0_s5_a_im, m_l0_s5_b_re, m_l0_s5_b_im, m_l0_s5_c_re, m_l0_s5_c_im, m_l0_s5_d,
           m_l0_s5_w_glu, m_l0_mla_q_norm, m_l0_mla_w_uq, m_l0_mla_kv_norm, m_l0_mla_w_ukv, m_l0_w_out, m_l0_ln1_g,
           m_l0_ln1_b, m_l0_ffn_w_gate, m_l0_ffn_w_up, m_l0_ffn_w_down, m_l0_ln2_g, m_l0_ln2_b, m_l1_w_in,
           m_l1_conv_w, m_l1_conv_b, m_l1_dt_bias, m_l1_a_log, m_l1_d, m_l1_norm_g, m_l1_w_out, m_l1_ln1_g,
           m_l1_ln1_b, m_l1_ffn_w_gate, m_l1_ffn_w_up, m_l1_ffn_w_down, m_l1_ln2_g, m_l1_ln2_b, v_meta_tokens,
           v_l0_w_in, v_l0_s5_log_dt, v_l0_s5_a_re, v_l0_s5_a_im, v_l0_s5_b_re, v_l0_s5_b_im, v_l0_s5_c_re,
           v_l0_s5_c_im, v_l0_s5_d, v_l0_s5_w_glu, v_l0_mla_q_norm, v_l0_mla_w_uq, v_l0_mla_kv_norm,
           v_l0_mla_w_ukv, v_l0_w_out, v_l0_ln1_g, v_l0_ln1_b, v_l0_ffn_w_gate, v_l0_ffn_w_up, v_l0_ffn_w_down,
           v_l0_ln2_g, v_l0_ln2_b, v_l1_w_in, v_l1_conv_w, v_l1_conv_b, v_l1_dt_bias, v_l1_a_log, v_l1_d,
           v_l1_norm_g, v_l1_w_out, v_l1_ln1_g, v_l1_ln1_b, v_l1_ffn_w_gate, v_l1_ffn_w_up, v_l1_ffn_w_down,
           v_l1_ln2_g, v_l1_ln2_b):
    given = dict(locals())
    ws = {name: given[name] for name in WEIGHTS}
    ms = {name: given["m_" + name] for name in WEIGHTS}
    vs = {name: given["v_" + name] for name in WEIGHTS}
    return _step(FULL, x, loss_target, ws, ms, vs)
```
